```python
import jax, jax.numpy as jnp
from jax import lax
import numpy as np

D_MODEL = 1024
BATCH = 8
SEQ = 2048
DEPTH = 1
DEC_BATCH = 32
DEC_SEQ = 1
PAST_LEN = 16384
PAGE_SIZE = 128

HEAD_DIM = 64
ATTN_GROUPS = ((128, 1), (512, 4), (2048, 16))
N_GROUPS = 3
HG = 4
N_ATTN_HEADS = N_GROUPS * HG
ATTN_WIDTH = N_ATTN_HEADS * HEAD_DIM
ATTN_OUT_WIDTH = HG * HEAD_DIM
BAND = 128
RWKV_N = 64
RWKV_WIDTH = D_MODEL // 2
RWKV_HEADS = RWKV_WIDTH // RWKV_N
DECAY_LORA = 64
AAA_LORA = 64
GATE_LORA = 128
SHIFT_WIDTH = 3 * RWKV_WIDTH + DECAY_LORA + AAA_LORA + GATE_LORA
D_FF = 4 * D_MODEL
IN_WIDTH = 3 * ATTN_WIDTH + SHIFT_WIDTH + 2 * D_MODEL
NORM_EPS = 1e-6
GN_EPS = 64e-5
NEG_INF = -1e30

kernel_name = "hybrid_dilated_attn_rwkv7_step"


def rms_norm(x, g):
    xf = x.astype(jnp.float32)
    y = xf * lax.rsqrt(jnp.mean(xf * xf, axis=-1, keepdims=True) + NORM_EPS)
    return (y * g.astype(jnp.float32)).astype(x.dtype)


def alibi_slopes():
    h = jnp.arange(1, N_ATTN_HEADS + 1, dtype=jnp.float32)
    return jnp.exp2(-8.0 * h / N_ATTN_HEADS)


def dilated_attn_prompt(q, k, v, dil, slopes):
    B, S, H, Dh = q.shape
    f32 = jnp.float32
    span = dil * BAND
    Sp = -(-S // span) * span
    L = Sp // dil
    nb = L // BAND

    def to_blocks(t):
        t = jnp.pad(t.astype(f32), ((0, 0), (0, Sp - S), (0, 0), (0, 0)))
        t = t.reshape(B, L, dil, H, Dh).transpose(0, 2, 1, 3, 4)
        return t.reshape(B, dil, nb, BAND, H, Dh)

    def with_prev(t):
        prev = jnp.concatenate([jnp.zeros_like(t[:, :, :1]), t[:, :, :-1]], axis=2)
        return jnp.concatenate([prev, t], axis=3)

    qb, kb, vb = to_blocks(q), to_blocks(k), to_blocks(v)
    k2, v2 = with_prev(kb), with_prev(vb)
    s = jnp.einsum('brnqhd,brnkhd->brnhqk', qb, k2) * (HEAD_DIM ** -0.5)
    qi = jnp.arange(BAND)[:, None]
    kj = jnp.arange(2 * BAND)[None, :]
    delta = BAND + qi - kj
    band = (delta >= 0) & (delta <= BAND)
    valid = band[None] & ((jnp.arange(nb)[:, None, None] > 0) | (kj[None] >= BAND))
    bias = -slopes[:, None, None] * (delta * dil).astype(f32)[None]
    s = jnp.where(valid[None, None, :, None], s + bias, NEG_INF)
    lse = jax.nn.logsumexp(s, axis=-1)
    p = jnp.exp(s - lse[..., None])
    o = jnp.einsum('brnhqk,brnkhd->brnqhd', p, v2)
    o = o.reshape(B, dil, L, H, Dh).transpose(0, 2, 1, 3, 4).reshape(B, Sp, H, Dh)[:, :S]
    lse = lse.transpose(0, 1, 2, 4, 3).reshape(B, dil, L, H).transpose(0, 2, 1, 3).reshape(B, Sp, H)[:, :S]
    return o, lse


def dilated_attn_sample(q, k_new, v_new, kv_buf, dil, slopes):
    B, T, H, Dh = q.shape
    f32 = jnp.float32
    Lb = kv_buf.shape[1]
    k_all = jnp.concatenate([kv_buf[:, :, 0].astype(f32), k_new.astype(f32)], axis=1)
    v_all = jnp.concatenate([kv_buf[:, :, 1].astype(f32), v_new.astype(f32)], axis=1)
    m = jnp.arange(BAND + 1)
    idx = Lb + jnp.arange(T)[:, None] - dil * m[None, :]
    valid = idx >= 0
    idx = jnp.maximum(idx, 0)
    kg = k_all[:, idx]
    vg = v_all[:, idx]
    s = jnp.einsum('bthd,btmhd->bthm', q.astype(f32), kg) * (HEAD_DIM ** -0.5)
    s = s - slopes[None, None, :, None] * (m * dil).astype(f32)[None, None, None, :]
    s = jnp.where(valid[None, :, None, :], s, NEG_INF)
    lse = jax.nn.logsumexp(s, axis=-1)
    p = jnp.exp(s - lse[..., None])
    o = jnp.einsum('bthm,btmhd->bthd', p, vg)
    return o, lse


def wkv_scan(r, decay, k, v, a_vec, b_vec, S0):
    def step(S, inp):
        r_t, w_t, k_t, v_t, a_t, b_t = inp
        sa = jnp.einsum('bhvk,bhk->bhv', S, a_t)
        S = S * w_t[:, :, None, :] + sa[..., None] * b_t[:, :, None, :] + v_t[..., None] * k_t[:, :, None, :]
        y = jnp.einsum('bhvk,bhk->bhv', S, r_t)
        return S, y
    xs = tuple(t.transpose(1, 0, 2, 3) for t in (r, decay, k, v, a_vec, b_vec))
    S, ys = lax.scan(step, S0.astype(jnp.float32), xs)
    return ys.transpose(1, 0, 2, 3), S


def rwkv_mix(zb, S0, w):
    B, T, _ = zb.shape
    H, N, R = RWKV_HEADS, RWKV_N, RWKV_WIDTH
    zf = zb.astype(jnp.float32)
    r, k, v, zw, za, zg = jnp.split(zf, [R, 2 * R, 3 * R, 3 * R + DECAY_LORA, 3 * R + DECAY_LORA + AAA_LORA], axis=-1)
    w_log = -jax.nn.softplus(-(w['w0'] + jnp.tanh(zw) @ w['w_lora_up'])) - 0.5
    decay = jnp.exp(-jnp.exp(w_log))
    a = jax.nn.sigmoid(w['a0'] + za @ w['a_lora_up'])
    g = jax.nn.sigmoid(zg) @ w['g_lora_up']
    kk = (k * w['k_k']).reshape(B, T, H, N)
    kk = kk / jnp.maximum(jnp.sqrt(jnp.sum(kk * kk, axis=-1, keepdims=True)), 1e-12)
    k = k * (1.0 + (a - 1.0) * w['k_a'])
    hd = lambda t: t.reshape(B, T, H, N)
    r_h, k_h, v_h, a_h, d_h = hd(r), hd(k), hd(v), hd(a), hd(decay)
    y, S = wkv_scan(r_h, d_h, k_h, v_h, -kk, kk * a_h, S0)
    mu = jnp.mean(y, axis=-1, keepdims=True)
    var = jnp.mean(jnp.square(y - mu), axis=-1, keepdims=True)
    yn = ((y - mu) * lax.rsqrt(var + GN_EPS)).reshape(B, T, R) * w['gn_g'] + w['gn_b']
    bonus = (jnp.sum(r_h * k_h * w['r_k'], axis=-1, keepdims=True) * v_h).reshape(B, T, R)
    return ((yn + bonus) * g).astype(zb.dtype), S


def block(x, kv_bufs, S0, shift_prev, w):
    B, T, _ = x.shape
    h = rms_norm(x, w['norm1_g'])
    z = h @ w['w_in']
    za, zb, zg = jnp.split(z, [3 * ATTN_WIDTH, 3 * ATTN_WIDTH + SHIFT_WIDTH], axis=-1)
    q, k, v = [t.reshape(B, T, N_ATTN_HEADS, HEAD_DIM) for t in jnp.split(za, 3, axis=-1)]
    slopes = alibi_slopes()
    outs, lses, kv_new = [], [], []
    for gi, (win, dil) in enumerate(ATTN_GROUPS):
        sl = slice(gi * HG, (gi + 1) * HG)
        qg, kg, vg = q[:, :, sl], k[:, :, sl], v[:, :, sl]
        if kv_bufs is None:
            o, l = dilated_attn_prompt(qg, kg, vg, dil, slopes[sl])
            rows = min(win, T)
            kv_new.append(jnp.stack([kg[:, T - rows:], vg[:, T - rows:]], axis=2))
        else:
            o, l = dilated_attn_sample(qg, kg, vg, kv_bufs[gi], dil, slopes[sl])
            kv_new.append(jnp.stack([kg, vg], axis=2))
        outs.append(o)
        lses.append(l)
    wts = jax.nn.softmax(jnp.stack(lses), axis=0)
    attn = jnp.sum(wts[..., None] * jnp.stack(outs), axis=0).reshape(B, T, ATTN_OUT_WIDTH).astype(x.dtype)
    z_prev = jnp.concatenate([shift_prev[:, None].astype(zb.dtype), zb[:, :-1]], axis=1)
    zb_mix = zb + (z_prev - zb) * w['mu_shift']
    rwkv, S = rwkv_mix(zb_mix, S0, w)
    gate = jax.nn.sigmoid((zg + w['b_gate']).astype(jnp.float32))
    gate_a, gate_b = gate[..., :D_MODEL], gate[..., D_MODEL:]
    merged = gate_a * (attn @ w['w_proj_a']) + gate_b * (rwkv @ w['w_proj_b'])
    x = x + merged.astype(x.dtype) @ w['w_out']
    hm = rms_norm(x, w['norm2_g'])
    x = x + jnp.square(jax.nn.relu(hm @ w['w_up'])) @ w['w_down']
    return x, kv_new[0], kv_new[1], kv_new[2], S, zb[:, -1]


def setup_inputs(seed: int = 0) -> dict:
    key = jax.random.key(seed)
    ks = list(jax.random.split(key, 40))
    f32 = jnp.float32

    def nrm(i, shape, scale):
        return scale * jax.random.normal(ks[i], shape, f32)

    def unif(i, shape, lo, hi):
        return jax.random.uniform(ks[i], shape, f32, lo, hi)

    Dd = DEPTH
    buf = lambda win: min(win, PAST_LEN)
    return {
        "x_prompt": nrm(0, (BATCH, SEQ, D_MODEL), 1.0),
        "x_sample": nrm(1, (DEC_BATCH, DEC_SEQ, D_MODEL), 1.0),
        "cache_kv_w128": nrm(2, (Dd, DEC_BATCH, buf(128), 2, HG, HEAD_DIM), 1.0),
        "cache_kv_w512": nrm(3, (Dd, DEC_BATCH, buf(512), 2, HG, HEAD_DIM), 1.0),
        "cache_kv_w2048": nrm(4, (Dd, DEC_BATCH, buf(2048), 2, HG, HEAD_DIM), 1.0),
        "state_wkv": nrm(5, (Dd, DEC_BATCH, RWKV_HEADS, RWKV_N, RWKV_N), 0.3),
        "state_shift": nrm(6, (Dd, DEC_BATCH, SHIFT_WIDTH), 1.0),
        "norm1_g": 1.0 + nrm(7, (Dd, D_MODEL), 0.05),
        "w_in": nrm(8, (Dd, D_MODEL, IN_WIDTH), D_MODEL ** -0.5),
        "b_gate": nrm(9, (Dd, 2 * D_MODEL), 0.1),
        "mu_shift": unif(10, (Dd, SHIFT_WIDTH), 0.0, 1.0),
        "w0": unif(11, (Dd, RWKV_WIDTH), -6.0, -0.5),
        "w_lora_up": nrm(12, (Dd, DECAY_LORA, RWKV_WIDTH), DECAY_LORA ** -0.5),
        "a0": nrm(13, (Dd, RWKV_WIDTH), 0.5),
        "a_lora_up": nrm(14, (Dd, AAA_LORA, RWKV_WIDTH), AAA_LORA ** -0.5),
        "g_lora_up": nrm(15, (Dd, GATE_LORA, RWKV_WIDTH), GATE_LORA ** -0.5),
        "k_k": 0.85 + nrm(16, (Dd, RWKV_WIDTH), 0.05),
        "k_a": 1.0 + nrm(17, (Dd, RWKV_WIDTH), 0.05),
        "r_k": nrm(18, (Dd, RWKV_HEADS, RWKV_N), 0.3),
        "gn_g": 1.0 + nrm(19, (Dd, RWKV_WIDTH), 0.05),
        "gn_b": nrm(20, (Dd, RWKV_WIDTH), 0.02),
        "w_proj_a": nrm(21, (Dd, ATTN_OUT_WIDTH, D_MODEL), ATTN_OUT_WIDTH ** -0.5),
        "w_proj_b": nrm(22, (Dd, RWKV_WIDTH, D_MODEL), RWKV_WIDTH ** -0.5),
        "w_out": nrm(23, (Dd, D_MODEL, D_MODEL), D_MODEL ** -0.5),
        "norm2_g": 1.0 + nrm(24, (Dd, D_MODEL), 0.05),
        "w_up": nrm(25, (Dd, D_MODEL, D_FF), D_MODEL ** -0.5),
        "w_down": nrm(26, (Dd, D_FF, D_MODEL), D_FF ** -0.5),
        "normf_g": 1.0 + nrm(27, (D_MODEL,), 0.05),
    }


def reference(x_prompt, x_sample, cache_kv_w128, cache_kv_w512, cache_kv_w2048, state_wkv, state_shift,
              norm1_g, w_in, b_gate, mu_shift, w0, w_lora_up, a0, a_lora_up, g_lora_up, k_k, k_a, r_k,
              gn_g, gn_b, w_proj_a, w_proj_b, w_out, norm2_g, w_up, w_down, normf_g):
    B = x_prompt.shape[0]
    xp, xs = x_prompt, x_sample
    per_p, per_s = [], []
    for l in range(DEPTH):
        w = dict(norm1_g=norm1_g[l], w_in=w_in[l], b_gate=b_gate[l], mu_shift=mu_shift[l], w0=w0[l],
                 w_lora_up=w_lora_up[l], a0=a0[l], a_lora_up=a_lora_up[l], g_lora_up=g_lora_up[l],
                 k_k=k_k[l], k_a=k_a[l], r_k=r_k[l], gn_g=gn_g[l], gn_b=gn_b[l], w_proj_a=w_proj_a[l],
                 w_proj_b=w_proj_b[l], w_out=w_out[l], norm2_g=norm2_g[l], w_up=w_up[l], w_down=w_down[l])
        S0p = jnp.zeros((B, RWKV_HEADS, RWKV_N, RWKV_N), jnp.float32)
        sh0p = jnp.zeros((B, SHIFT_WIDTH), xp.dtype)
        out_p = block(xp, None, S0p, sh0p, w)
        xp = out_p[0]
        per_p.append(out_p[1:])
        out_s = block(xs, (cache_kv_w128[l], cache_kv_w512[l], cache_kv_w2048[l]), state_wkv[l], state_shift[l], w)
        xs = out_s[0]
        per_s.append(out_s[1:])
    y_prompt = rms_norm(xp, normf_g)
    y_sample = rms_norm(xs, normf_g)
    kv128_p = jnp.stack([e[0] for e in per_p])
    kv512_p = jnp.stack([e[1] for e in per_p])
    kv2048_p = jnp.stack([e[2] for e in per_p])
    wkv_p = jnp.stack([e[3] for e in per_p])
    shift_p = jnp.stack([e[4] for e in per_p])
    kv128_s = jnp.stack([e[0] for e in per_s])
    kv512_s = jnp.stack([e[1] for e in per_s])
    kv2048_s = jnp.stack([e[2] for e in per_s])
    wkv_s = jnp.stack([e[3] for e in per_s])
    shift_s = jnp.stack([e[4] for e in per_s])
    return (y_prompt, y_sample, kv128_p, kv512_p, kv2048_p, wkv_p, shift_p, kv128_s, kv512_s, kv2048_s, wkv_s, shift_s)
```

```python
import functools

import numpy as np
import jax
import jax.numpy as jnp
from jax import lax
from jax.experimental import pallas as pl
from jax.experimental.pallas import tpu as pltpu

F32 = jnp.float32
BF16 = jnp.bfloat16

D_MODEL = 1024
HEAD_DIM = 64
HG = 4
DILATIONS = (1, 4, 16)
BAND = 128
N_ATTN_HEADS = HG * len(DILATIONS)
GROUP_W = HG * HEAD_DIM
ATTN_W = N_ATTN_HEADS * HEAD_DIM
QKV_W = 3 * ATTN_W
RWKV_N = 64
RWKV_W = 512
RWKV_HEADS = RWKV_W // RWKV_N
PAIR_W = 2 * RWKV_N
N_PAIRS = RWKV_HEADS // 2
LORA_W = 256
SHIFT_W = 3 * RWKV_W + LORA_W
GATE_W = 2 * D_MODEL
IN_W = QKV_W + SHIFT_W + GATE_W
D_FF = 4 * D_MODEL
NORM_EPS = 1e-6
GN_EPS = 64e-5
NEG_INF = -1e30
ATTN_SCALE = HEAD_DIM ** -0.5
CHUNK = 64
DECAY_SCALE = float(np.exp(-0.5))
SLOPES = [float(s) for s in np.exp2(-8.0 * np.arange(1, N_ATTN_HEADS + 1, dtype=np.float32) / N_ATTN_HEADS)]

V7X_VMEM_LIMIT = 56 * 1024 * 1024

NN = (((1,), (0,)), ((), ()))
NT = (((1,), (1,)), ((), ()))
TN = (((0,), (0,)), ((), ()))


def _dg(a, b, dims=NN):
    return lax.dot_general(a, b, dims, preferred_element_type=F32)


def _split2(a):
    hi = a.astype(BF16)
    lo = (a - hi.astype(F32)).astype(BF16)
    return hi, lo


def _dot3(a, b, dims=NN):
    ah, al = _split2(a)
    bh, bl = _split2(b)
    return _dg(ah, bh, dims) + _dg(al, bh, dims) + _dg(ah, bl, dims)


def _dot_exact_rhs(a, b_bf16, passes):
    out = None
    rem = a
    for _ in range(passes):
        part = rem.astype(BF16)
        term = _dg(part, b_bf16)
        out = term if out is None else out + term
        rem = rem - part.astype(F32)
    return out


def _sigmoid(x):
    return 1.0 / (1.0 + jnp.exp(-x))


def _rms(x, g):
    return x * lax.rsqrt(jnp.mean(x * x, axis=-1, keepdims=True) + NORM_EPS) * g


def _params(sem):
    return pltpu.CompilerParams(dimension_semantics=sem, vmem_limit_bytes=V7X_VMEM_LIMIT)


def _in_proj_body(x_ref, g_ref, w_ref, bg_ref, qkv_ref, zb_ref, gate_ref):
    h = _rms(x_ref[...], g_ref[...]).astype(BF16)
    for c in range(0, QKV_W, 768):
        qkv_ref[:, c:c + 768] = _dg(h, w_ref[:, c:c + 768])
    for c in range(0, SHIFT_W, 896):
        zb_ref[:, c:c + 896] = _dg(h, w_ref[:, QKV_W + c:QKV_W + c + 896])
    for c in range(0, GATE_W, 1024):
        zg = _dg(h, w_ref[:, QKV_W + SHIFT_W + c:QKV_W + SHIFT_W + c + 1024])
        gate_ref[:, c:c + 1024] = _sigmoid(zg + bg_ref[:, c:c + 1024])


def _in_proj(x, norm_g, w_in_bf16, b_gate, tm):
    m = x.shape[0]
    row = lambda w: pl.BlockSpec((tm, w), lambda i: (i, 0))
    full = lambda a: pl.BlockSpec(a.shape, lambda i: (0,) * a.ndim)
    return pl.pallas_call(
        _in_proj_body,
        grid=(m // tm,),
        in_specs=[row(D_MODEL), full(norm_g), full(w_in_bf16), full(b_gate)],
        out_specs=[row(QKV_W), row(SHIFT_W), row(GATE_W)],
        out_shape=[jax.ShapeDtypeStruct((m, QKV_W), F32),
                   jax.ShapeDtypeStruct((m, SHIFT_W), F32),
                   jax.ShapeDtypeStruct((m, GATE_W), F32)],
        compiler_params=_params(("parallel",)),
        name="in_proj",
    )(x, norm_g, w_in_bf16, b_gate)


def _merge(o_a, lse_a, o_b, lse_b):
    m = jnp.maximum(lse_a, lse_b)
    wa = jnp.exp(lse_a - m)
    wb = jnp.exp(lse_b - m)
    den = wa + wb
    return (wa * o_a + wb * o_b) / den, m + jnp.log(den)


def _attn_body(*refs, group, has_prev, last):
    q_ref, kc_ref, kp_ref, vc_ref, vp_ref = refs[:5]
    refs = refs[5:]
    if has_prev:
        op_ref, lp_ref = refs[:2]
        refs = refs[2:]
    o_ref = refs[0]
    dil = DILATIONS[group]
    n = pl.program_id(2)
    q = q_ref[...]
    k2 = jnp.concatenate([kp_ref[...], kc_ref[...]], axis=0).astype(BF16)
    v2 = jnp.concatenate([vp_ref[...], vc_ref[...]], axis=0).astype(BF16)
    qi = lax.broadcasted_iota(jnp.int32, (BAND, 2 * BAND), 0)
    kj = lax.broadcasted_iota(jnp.int32, (BAND, 2 * BAND), 1)
    delta = BAND + qi - kj
    first_key = jnp.where(n > 0, 0, BAND)
    valid = (delta >= 0) & (delta <= BAND) & (kj >= first_key)
    dist = (delta * dil).astype(F32)
    head = lax.broadcasted_iota(jnp.int32, (1, GROUP_W), 1) // HEAD_DIM
    o_acc = jnp.zeros((BAND, GROUP_W), F32)
    lse_acc = jnp.zeros((BAND, GROUP_W), F32)
    for h in range(HG):
        hm = head == h
        qh = jnp.where(hm, q, 0.0).astype(BF16)
        s = _dg(qh, k2, NT) * ATTN_SCALE - SLOPES[group * HG + h] * dist
        s = jnp.where(valid, s, NEG_INF)
        m = jnp.max(s, axis=-1, keepdims=True)
        p = jnp.exp(s - m)
        l = jnp.sum(p, axis=-1, keepdims=True)
        pv = _dg(p.astype(BF16), v2)
        o_acc = jnp.where(hm, pv / l, o_acc)
        lse_acc = jnp.where(hm, m + jnp.log(l), lse_acc)
    if has_prev:
        o_acc, lse_acc = _merge(op_ref[...], lp_ref[...], o_acc, lse_acc)
    if last:
        o_ref[...] = o_acc.astype(o_ref.dtype)
    else:
        o_ref[...] = o_acc
        refs[1][...] = lse_acc


def _attn_group(qkv, prev, group, last):
    b, s, _ = qkv.shape
    dil = DILATIONS[group]
    l = s // dil
    nb = l // BAND
    cb = QKV_W // GROUP_W
    qkv_r = qkv.reshape(b, l, dil * QKV_W)
    blk = (None, BAND, GROUP_W)

    def col(off):
        return pl.BlockSpec(blk, lambda bi, r, n: (bi, n, r * cb + off + group))

    def col_prev(off):
        return pl.BlockSpec(blk, lambda bi, r, n: (bi, jnp.maximum(n - 1, 0), r * cb + off + group))

    pos = pl.BlockSpec(blk, lambda bi, r, n: (bi, n, r))
    hg = ATTN_W // GROUP_W
    in_specs = [col(0), col(hg), col_prev(hg), col(2 * hg), col_prev(2 * hg)]
    args = [qkv_r] * 5
    if prev is not None:
        in_specs += [pos, pos]
        args += [prev[0].reshape(b, l, dil * GROUP_W), prev[1].reshape(b, l, dil * GROUP_W)]
    if last:
        out_specs = [pos]
        out_shape = [jax.ShapeDtypeStruct((b, l, dil * GROUP_W), BF16)]
    else:
        out_specs = [pos, pos]
        out_shape = [jax.ShapeDtypeStruct((b, l, dil * GROUP_W), F32)] * 2
    outs = pl.pallas_call(
        functools.partial(_attn_body, group=group, has_prev=prev is not None, last=last),
        grid=(b, dil, nb),
        in_specs=in_specs,
        out_specs=out_specs,
        out_shape=out_shape,
        compiler_params=_params(("parallel", "parallel", "arbitrary")),
        name=f"attn_g{group}",
    )(*args)
    return [o.reshape(b, s, GROUP_W) for o in outs]


def _attn_prompt(qkv):
    assert qkv.shape[1] % (DILATIONS[-1] * BAND) == 0
    prev = None
    for group in range(len(DILATIONS)):
        last = group == len(DILATIONS) - 1
        prev = _attn_group(qkv, prev, group, last)
    return prev[0]


def _attn_sample_body(qkv_ref, c0_ref, c1_ref, c2_ref, o_ref):
    bt = qkv_ref.shape[0]
    lane_r = lax.broadcasted_iota(jnp.int32, (GROUP_W, 128), 0)
    lane_c = lax.broadcasted_iota(jnp.int32, (GROUP_W, 128), 1)
    seg = (lane_r // HEAD_DIM == lane_c).astype(BF16)
    ex_r = lax.broadcasted_iota(jnp.int32, (128, GROUP_W), 0)
    ex_c = lax.broadcasted_iota(jnp.int32, (128, GROUP_W), 1)
    expand = (ex_c // HEAD_DIM == ex_r).astype(BF16)
    hl = lax.broadcasted_iota(jnp.int32, (1, 128), 1)
    j = lax.broadcasted_iota(jnp.int32, (1, BAND, 1), 1)
    accs, ms, ls = [], [], []
    for g, c_ref in enumerate((c0_ref, c1_ref, c2_ref)):
        dil = DILATIONS[g]
        qg = qkv_ref[:, g * GROUP_W:(g + 1) * GROUP_W]
        kn = qkv_ref[:, ATTN_W + g * GROUP_W:ATTN_W + (g + 1) * GROUP_W]
        vn = qkv_ref[:, 2 * ATTN_W + g * GROUP_W:2 * ATTN_W + (g + 1) * GROUP_W]
        kb = c_ref[:, :, 0:GROUP_W]
        vb = c_ref[:, :, GROUP_W:2 * GROUP_W]
        slope = jnp.zeros((1, 128), F32)
        for h in range(HG):
            slope = jnp.where(hl == h, SLOPES[g * HG + h], slope)
        prod = (kb * qg[:, None, :]).reshape(bt * BAND, GROUP_W)
        s_b = _dot_exact_rhs(prod, seg, 2).reshape(bt, BAND, 128) * ATTN_SCALE
        s_b = s_b - slope[None] * ((BAND - j) * dil).astype(F32)
        s_n = _dot_exact_rhs(qg * kn, seg, 2) * ATTN_SCALE
        m = jnp.maximum(jnp.max(s_b, axis=1), s_n)
        p_b = jnp.exp(s_b - m[:, None, :])
        p_n = jnp.exp(s_n - m)
        l = jnp.sum(p_b, axis=1) + p_n
        pe = _dot_exact_rhs(p_b.reshape(bt * BAND, 128), expand, 2).reshape(bt, BAND, GROUP_W)
        acc = jnp.sum(pe * vb, axis=1) + _dot_exact_rhs(p_n, expand, 2) * vn
        accs.append(acc)
        ms.append(m)
        ls.append(l)
    m_all = jnp.maximum(jnp.maximum(ms[0], ms[1]), ms[2])
    sc = [jnp.exp(m - m_all) for m in ms]
    den = sc[0] * ls[0] + sc[1] * ls[1] + sc[2] * ls[2]
    den = jnp.where(hl < HG, den, 1.0)
    out = jnp.zeros((bt, GROUP_W), F32)
    for g in range(len(DILATIONS)):
        out = out + _dot_exact_rhs(sc[g] / den, expand, 2) * accs[g]
    o_ref[...] = out.astype(o_ref.dtype)


def _attn_sample(qkv, caches, bt):
    b = qkv.shape[0]
    kv_w = 2 * GROUP_W
    c_args, c_specs = [], []
    for g, c in enumerate(caches):
        dil = DILATIONS[g]
        assert c.shape[1] == BAND * dil
        c_args.append(c.reshape(b, BAND, dil * kv_w))
        c_specs.append(pl.BlockSpec((bt, BAND, kv_w), lambda i: (i, 0, 0)))
    return pl.pallas_call(
        _attn_sample_body,
        grid=(b // bt,),
        in_specs=[pl.BlockSpec((bt, QKV_W), lambda i: (i, 0))] + c_specs,
        out_specs=pl.BlockSpec((bt, GROUP_W), lambda i: (i, 0)),
        out_shape=jax.ShapeDtypeStruct((b, GROUP_W), BF16),
        compiler_params=_params(("parallel",)),
        name="attn_sample",
    )(qkv, *c_args)


def _wkv_chunk(r, kp, v, logw, av, bv, s0):
    c = CHUNK
    ti = lax.broadcasted_iota(jnp.int32, (c, c), 0)
    tj = lax.broadcasted_iota(jnp.int32, (c, c), 1)
    tri = (ti >= tj).astype(BF16)
    lc = _dot_exact_rhs_left(tri, logw)
    e_in = jnp.exp(lc)
    e_ex = jnp.exp(lc - logw)
    e_neg = jnp.exp(-lc)
    lane = lax.broadcasted_iota(jnp.int32, (1, PAIR_W), 1)
    first = lane < RWKV_N

    def stack(x):
        return jnp.concatenate([jnp.where(first, x, 0.0), jnp.where(first, 0.0, x)], axis=0)

    a_s = stack(av * e_ex)
    r_s = stack(r * e_in)
    b_s = stack(bv * e_neg)
    k_s = stack(kp * e_neg)
    v_s = stack(v)
    g = _dot3(jnp.concatenate([a_s, r_s], axis=0), jnp.concatenate([b_s, k_s], axis=0), NT)
    n2 = 2 * c
    ri = lax.broadcasted_iota(jnp.int32, (n2, n2), 0)
    ci = lax.broadcasted_iota(jnp.int32, (n2, n2), 1)
    strict = ri > ci
    incl = ri >= ci
    n_ab = jnp.where(strict, g[0:n2, 0:n2], 0.0)
    n_ak = jnp.where(strict, g[0:n2, n2:2 * n2], 0.0)
    m_rb = jnp.where(incl, g[n2:2 * n2, 0:n2], 0.0)
    m_rk = jnp.where(incl, g[n2:2 * n2, n2:2 * n2], 0.0)
    t = jnp.where(ri == ci, 1.0, 0.0) + n_ab
    p = n_ab
    for _ in range(int(np.log2(c)) - 1):
        p = _dot3(p, p)
        t = t + _dot3(t, p)
    x = _dot3(a_s, s0, NT) + _dot3(n_ak, v_s)
    u = _dot3(t, x)
    y_s = _dot3(r_s, s0, NT) + _dot3(m_rb, u) + _dot3(m_rk, v_s)
    y = y_s[0:c] + y_s[c:n2]
    s1 = (s0 + _dot3(u, b_s, TN) + _dot3(v_s, k_s, TN)) * e_in[c - 1:c, :]
    return y, s1


def _dot_exact_rhs_left(lhs_bf16, b):
    out = None
    rem = b
    for _ in range(3):
        part = rem.astype(BF16)
        term = _dg(lhs_bf16, part)
        out = term if out is None else out + term
        rem = rem - part.astype(F32)
    return out


def _rwkv_body(zr_ref, zk_ref, zv_ref, zl_ref, sr_ref, sk_ref, sv_ref, sl_ref,
               mr_ref, mk_ref, mv_ref, ml_ref, s0_ref,
               w0_ref, a0_ref, kk_ref, ka_ref, rk_ref, gg_ref, gb_ref,
               ww_ref, wa_ref, wg_ref,
               out_ref, s_out_ref,
               s_scr, cr_scr, ck_scr, cv_scr, cl_scr,
               r_scr, kp_scr, v_scr, lw_scr, av_scr, bv_scr, y_scr,
               *, t_valid):
    t = pl.program_id(2)
    tc = zr_ref.shape[0]

    @pl.when(t == 0)
    def _():
        s_scr[...] = s0_ref[...]
        cr_scr[...] = sr_ref[...]
        ck_scr[...] = sk_ref[...]
        cv_scr[...] = sv_ref[...]
        cl_scr[...] = sl_ref[...]

    row = lax.broadcasted_iota(jnp.int32, (tc, 1), 0)

    def mix(z_ref, carry, mu_ref):
        z = z_ref[...]
        zp = jnp.where(row == 0, carry[...], pltpu.roll(z, 1, 0))
        carry[...] = z[tc - 1:tc, :]
        return z + (zp - z) * mu_ref[...]

    r = mix(zr_ref, cr_scr, mr_ref)
    k = mix(zk_ref, ck_scr, mk_ref)
    v = mix(zv_ref, cv_scr, mv_ref)
    zl = mix(zl_ref, cl_scr, ml_ref)

    ll = lax.broadcasted_iota(jnp.int32, (1, LORA_W), 1)
    zl = jnp.where(ll < 64, jnp.tanh(zl), jnp.where(ll < 128, zl, _sigmoid(zl)))
    x_wa = zl[:, 0:128].astype(BF16)
    x_g = zl[:, 128:256].astype(BF16)
    u = w0_ref[...] + _dg(x_wa, ww_ref[...])
    logw = -DECAY_SCALE * _sigmoid(u)
    a = _sigmoid(a0_ref[...] + _dg(x_wa, wa_ref[...]))
    gate = _dg(x_g, wg_ref[...])

    lr = lax.broadcasted_iota(jnp.int32, (PAIR_W, PAIR_W), 0)
    lc = lax.broadcasted_iota(jnp.int32, (PAIR_W, PAIR_W), 1)
    seg = (lr // RWKV_N == lc // RWKV_N).astype(BF16)

    kk = k * kk_ref[...]
    kk = kk / jnp.maximum(jnp.sqrt(_dot_exact_rhs(kk * kk, seg, 2)), 1e-12)
    kp = k * (1.0 + (a - 1.0) * ka_ref[...])
    av = -kk
    bv = kk * a
    if t_valid is not None:
        live = row < t_valid
        zero = lambda x_: jnp.where(live, x_, 0.0)
        r, kp, v, logw, av, bv = zero(r), zero(kp), zero(v), zero(logw), zero(av), zero(bv)
    r_scr[...] = r
    kp_scr[...] = kp
    v_scr[...] = v
    lw_scr[...] = logw
    av_scr[...] = av
    bv_scr[...] = bv

    def chunk(ci, s):
        sl = pl.ds(pl.multiple_of(ci * CHUNK, CHUNK), CHUNK)
        y, s = _wkv_chunk(r_scr[sl, :], kp_scr[sl, :], v_scr[sl, :], lw_scr[sl, :], av_scr[sl, :], bv_scr[sl, :], s)
        y_scr[sl, :] = y
        return s

    s_new = lax.fori_loop(0, tc // CHUNK, chunk, s_scr[...])
    s_scr[...] = s_new
    s_out_ref[...] = s_new

    y = y_scr[...]
    inv_n = 1.0 / RWKV_N
    mu = _dot_exact_rhs(y, seg, 2) * inv_n
    d = y - mu
    var = _dot_exact_rhs(d * d, seg, 2) * inv_n
    yn = d * lax.rsqrt(var + GN_EPS) * gg_ref[...] + gb_ref[...]
    bonus = _dot_exact_rhs(r * kp * rk_ref[...], seg, 2) * v
    out_ref[...] = ((yn + bonus) * gate).astype(out_ref.dtype)


def _rwkv(zb, shift0, s0_bd, w, tc, t_valid=None):
    b, t_len, _ = zb.shape
    nr = RWKV_W // PAIR_W
    lb = 3 * RWKV_W // LORA_W

    def zspec(width, off):
        return pl.BlockSpec((None, tc, width), lambda bi, p, t: (bi, t, off(p)))

    def sspec(width, off):
        return pl.BlockSpec((None, 1, width), lambda bi, p, t: (bi, 0, off(p)))

    def vspec(width, off):
        return pl.BlockSpec((1, width), lambda bi, p, t: (0, off(p)))

    offs = [lambda p: p, lambda p: nr + p, lambda p: 2 * nr + p, lambda p: lb]
    widths = [PAIR_W, PAIR_W, PAIR_W, LORA_W]
    pvec = vspec(PAIR_W, lambda p: p)
    wspec = lambda off: pl.BlockSpec((PAIR_W, PAIR_W), lambda bi, p, t: (0, off(p)))
    state = pl.BlockSpec((None, None, PAIR_W, PAIR_W), lambda bi, p, t: (bi, p, 0, 0))
    in_specs = ([zspec(wd, o) for wd, o in zip(widths, offs)]
                + [sspec(wd, o) for wd, o in zip(widths, offs)]
                + [vspec(wd, o) for wd, o in zip(widths, offs)]
                + [state] + [pvec] * 7
                + [wspec(lambda p: p), wspec(lambda p: nr + p), wspec(lambda p: p)])
    args = ([zb] * 4 + [shift0] * 4 + [w["mu_shift"]] * 4 + [s0_bd]
            + [w["w0"], w["a0"], w["k_k"], w["k_a"], w["r_k"], w["gn_g"], w["gn_b"]]
            + [w["wa_lora"], w["wa_lora"], w["g_lora"]])
    vm = lambda shape: pltpu.VMEM(shape, F32)
    scratch = ([vm((PAIR_W, PAIR_W)), vm((1, PAIR_W)), vm((1, PAIR_W)), vm((1, PAIR_W)), vm((1, LORA_W))]
               + [vm((tc, PAIR_W))] * 7)
    return pl.pallas_call(
        functools.partial(_rwkv_body, t_valid=t_valid),
        grid=(b, N_PAIRS, t_len // tc),
        in_specs=in_specs,
        out_specs=[pl.BlockSpec((None, tc, PAIR_W), lambda bi, p, t: (bi, t, p)), state],
        out_shape=[jax.ShapeDtypeStruct((b, t_len, RWKV_W), BF16),
                   jax.ShapeDtypeStruct((b, N_PAIRS, PAIR_W, PAIR_W), F32)],
        scratch_shapes=scratch,
        compiler_params=_params(("parallel", "parallel", "arbitrary")),
        name="rwkv",
    )(*args)


def _state_to_pairs(s):
    b = s.shape[0]
    s = s.reshape(b, N_PAIRS, 2, RWKV_N, RWKV_N)
    z = jnp.zeros_like(s[:, :, 0])
    top = jnp.concatenate([s[:, :, 0], z], axis=-1)
    bot = jnp.concatenate([z, s[:, :, 1]], axis=-1)
    return jnp.concatenate([top, bot], axis=-2)


def _pairs_to_state(s_bd):
    b = s_bd.shape[0]
    h0 = s_bd[:, :, :RWKV_N, :RWKV_N]
    h1 = s_bd[:, :, RWKV_N:, RWKV_N:]
    return jnp.stack([h0, h1], axis=2).reshape(b, RWKV_HEADS, RWKV_N, RWKV_N)


def _out_proj_body(x_ref, attn_ref, rwkv_ref, gate_ref, wpa_ref, wpb_ref, wout_ref, n2_ref,
                   wup_ref, wdn_ref, nf_ref, y_ref):
    pa = _dg(attn_ref[...], wpa_ref[...])
    pb = _dg(rwkv_ref[...], wpb_ref[...])
    merged = gate_ref[:, 0:D_MODEL] * pa + gate_ref[:, D_MODEL:GATE_W] * pb
    x1 = x_ref[...] + _dg(merged.astype(BF16), wout_ref[...])
    hm = _rms(x1, n2_ref[...]).astype(BF16)
    acc = x1
    for c in range(0, D_FF, 1024):
        up = jnp.maximum(_dg(hm, wup_ref[:, c:c + 1024]), 0.0)
        acc = acc + _dg((up * up).astype(BF16), wdn_ref[c:c + 1024, :])
    y_ref[...] = _rms(acc, nf_ref[...])


def _out_proj(x, attn, rwkv, gate, w, tm):
    m = x.shape[0]
    row = lambda wd: pl.BlockSpec((tm, wd), lambda i: (i, 0))
    full = lambda a: pl.BlockSpec(a.shape, lambda i: (0,) * a.ndim)
    consts = [w["w_proj_a"], w["w_proj_b"], w["w_out"], w["norm2_g"], w["w_up"], w["w_down"], w["normf_g"]]
    return pl.pallas_call(
        _out_proj_body,
        grid=(m // tm,),
        in_specs=[row(D_MODEL), row(GROUP_W), row(RWKV_W), row(GATE_W)] + [full(c) for c in consts],
        out_specs=row(D_MODEL),
        out_shape=jax.ShapeDtypeStruct((m, D_MODEL), F32),
        compiler_params=_params(("parallel",)),
        name="out_proj",
    )(x, attn, rwkv, gate, *consts)


def _kv_rows(qkv, group, rows):
    b, s, _ = qkv.shape
    k0 = ATTN_W + group * GROUP_W
    v0 = 2 * ATTN_W + group * GROUP_W
    k = qkv[:, s - rows:, k0:k0 + GROUP_W].reshape(b, rows, HG, HEAD_DIM)
    v = qkv[:, s - rows:, v0:v0 + GROUP_W].reshape(b, rows, HG, HEAD_DIM)
    return jnp.stack([k, v], axis=2)


def _layer_weights(l, norm1_g, w_in, b_gate, mu_shift, w0, w_lora_up, a0, a_lora_up, g_lora_up, k_k, k_a,
                   r_k, gn_g, gn_b, w_proj_a, w_proj_b, w_out, norm2_g, w_up, w_down, normf_g):
    row = lambda a: a.reshape(1, -1)
    zero = jnp.zeros_like(w_lora_up[l])
    wa_lora = jnp.concatenate([jnp.concatenate([w_lora_up[l], zero], axis=1),
                               jnp.concatenate([zero, a_lora_up[l]], axis=1)], axis=0)
    return dict(
        norm1_g=row(norm1_g[l]), w_in=w_in[l].astype(BF16), b_gate=row(b_gate[l]), mu_shift=row(mu_shift[l]),
        w0=row(w0[l]), a0=row(a0[l]), k_k=row(k_k[l]), k_a=row(k_a[l]), r_k=row(r_k[l]),
        gn_g=row(gn_g[l]), gn_b=row(gn_b[l]),
        wa_lora=wa_lora.astype(BF16), g_lora=g_lora_up[l].astype(BF16),
        w_proj_a=w_proj_a[l].astype(BF16), w_proj_b=w_proj_b[l].astype(BF16), w_out=w_out[l].astype(BF16),
        norm2_g=row(norm2_g[l]), w_up=w_up[l].astype(BF16), w_down=w_down[l].astype(BF16),
        normf_g=row(normf_g))


def _prompt_layer(x, w, tm, tc):
    b, s, _ = x.shape
    x2 = x.reshape(b * s, D_MODEL)
    qkv, zb, gate = _in_proj(x2, w["norm1_g"], w["w_in"], w["b_gate"], tm)
    qkv = qkv.reshape(b, s, QKV_W)
    zb = zb.reshape(b, s, SHIFT_W)
    attn = _attn_prompt(qkv)
    rwkv, s_bd = _rwkv(zb, jnp.zeros((b, 1, SHIFT_W), F32), jnp.zeros((b, N_PAIRS, PAIR_W, PAIR_W), F32), w, tc)
    y = _out_proj(x2, attn.reshape(b * s, GROUP_W), rwkv.reshape(b * s, RWKV_W), gate, w, tm)
    kvs = [_kv_rows(qkv, g, min(BAND * d, s)) for g, d in enumerate(DILATIONS)]
    return y.reshape(b, s, D_MODEL), kvs, _pairs_to_state(s_bd), zb[:, -1]


def _sample_layer(x, caches, s0, shift0, w):
    b, t_len, _ = x.shape
    assert t_len == 1
    x2 = x.reshape(b, D_MODEL)
    qkv, zb, gate = _in_proj(x2, w["norm1_g"], w["w_in"], w["b_gate"], b)
    attn = _attn_sample(qkv, caches, 8)
    zb_pad = jnp.pad(zb[:, None, :], ((0, 0), (0, CHUNK - 1), (0, 0)))
    rwkv, s_bd = _rwkv(zb_pad, shift0[:, None, :], _state_to_pairs(s0), w, CHUNK, t_valid=1)
    y = _out_proj(x2, attn, rwkv[:, 0], gate, w, b)
    kvs = [_kv_rows(qkv[:, None, :], g, 1) for g in range(len(DILATIONS))]
    return y.reshape(b, 1, D_MODEL), kvs, _pairs_to_state(s_bd), zb


def kernel(x_prompt, x_sample, cache_kv_w128, cache_kv_w512, cache_kv_w2048, state_wkv, state_shift, norm1_g, w_in, b_gate, mu_shift, w0, w_lora_up, a0, a_lora_up, g_lora_up, k_k, k_a, r_k, gn_g, gn_b, w_proj_a, w_proj_b, w_out, norm2_g, w_up, w_down, normf_g):
    depth = w_in.shape[0]
    assert depth == 1, "the final norm is fused into the layer's output stage"
    w = _layer_weights(0, norm1_g, w_in, b_gate, mu_shift, w0, w_lora_up, a0, a_lora_up, g_lora_up, k_k, k_a,
                       r_k, gn_g, gn_b, w_proj_a, w_proj_b, w_out, norm2_g, w_up, w_down, normf_g)
    y_p, kv_p, wkv_p, shift_p = _prompt_layer(x_prompt, w, 256, 512)
    y_s, kv_s, wkv_s, shift_s = _sample_layer(
        x_sample, (cache_kv_w128[0], cache_kv_w512[0], cache_kv_w2048[0]), state_wkv[0], state_shift[0], w)
    lead = lambda a: a[None]
    return (y_p, y_s, lead(kv_p[0]), lead(kv_p[1]), lead(kv_p[2]), lead(wkv_p), lead(shift_p),
            lead(kv_s[0]), lead(kv_s[1]), lead(kv_s[2]), lead(wkv_s), lead(shift_s))
```

```python
import functools

import numpy as np
import jax
import jax.numpy as jnp
from jax import lax
from jax.experimental import pallas as pl
from jax.experimental.pallas import tpu as pltpu

F32 = jnp.float32
BF16 = jnp.bfloat16

D_MODEL = 1024
HEAD_DIM = 64
HG = 4
DILATIONS = (1, 4, 16)
BAND = 128
N_ATTN_HEADS = HG * len(DILATIONS)
GROUP_W = HG * HEAD_DIM
ATTN_W = N_ATTN_HEADS * HEAD_DIM
QKV_W = 3 * ATTN_W
RWKV_N = 64
RWKV_W = 512
RWKV_HEADS = RWKV_W // RWKV_N
PAIR_W = 2 * RWKV_N
N_PAIRS = RWKV_HEADS // 2
LORA_W = 256
SHIFT_W = 3 * RWKV_W + LORA_W
GATE_W = 2 * D_MODEL
IN_W = QKV_W + SHIFT_W + GATE_W
D_FF = 4 * D_MODEL
NORM_EPS = 1e-6
GN_EPS = 64e-5
NEG_INF = -1e30
ATTN_SCALE = HEAD_DIM ** -0.5
CHUNK = 64
DECAY_SCALE = float(np.exp(-0.5))
SLOPES = [float(s) for s in np.exp2(-8.0 * np.arange(1, N_ATTN_HEADS + 1, dtype=np.float32) / N_ATTN_HEADS)]

V7X_VMEM_LIMIT = 56 * 1024 * 1024

NN = (((1,), (0,)), ((), ()))
NT = (((1,), (1,)), ((), ()))
TN = (((0,), (0,)), ((), ()))


def _dg(a, b, dims=NN):
    return lax.dot_general(a, b, dims, preferred_element_type=F32)


def _split2(a):
    hi = a.astype(BF16)
    lo = (a - hi.astype(F32)).astype(BF16)
    return hi, lo


def _dot3(a, b, dims=NN):
    ah, al = _split2(a)
    bh, bl = _split2(b)
    return _dg(ah, bh, dims) + _dg(al, bh, dims) + _dg(ah, bl, dims)


def _dot_exact_rhs(a, b_bf16, passes):
    out = None
    rem = a
    for _ in range(passes):
        part = rem.astype(BF16)
        term = _dg(part, b_bf16)
        out = term if out is None else out + term
        rem = rem - part.astype(F32)
    return out


def _sigmoid(x):
    return 1.0 / (1.0 + jnp.exp(-x))


def _rms(x, g):
    return x * lax.rsqrt(jnp.mean(x * x, axis=-1, keepdims=True) + NORM_EPS) * g


def _params(sem):
    return pltpu.CompilerParams(dimension_semantics=sem, vmem_limit_bytes=V7X_VMEM_LIMIT)


def _in_proj_body(x_ref, g_ref, w_ref, bg_ref, qkv_ref, zb_ref, gate_ref):
    h = _rms(x_ref[...], g_ref[...]).astype(BF16)
    for c in range(0, QKV_W, 768):
        qkv_ref[:, c:c + 768] = _dg(h, w_ref[:, c:c + 768])
    for c in range(0, SHIFT_W, 896):
        zb_ref[:, c:c + 896] = _dg(h, w_ref[:, QKV_W + c:QKV_W + c + 896])
    for c in range(0, GATE_W, 1024):
        zg = _dg(h, w_ref[:, QKV_W + SHIFT_W + c:QKV_W + SHIFT_W + c + 1024])
        gate_ref[:, c:c + 1024] = _sigmoid(zg + bg_ref[:, c:c + 1024])


def _in_proj(x, norm_g, w_in_bf16, b_gate, tm):
    m = x.shape[0]
    row = lambda w: pl.BlockSpec((tm, w), lambda i: (i, 0))
    full = lambda a: pl.BlockSpec(a.shape, lambda i: (0,) * a.ndim)
    return pl.pallas_call(
        _in_proj_body,
        grid=(m // tm,),
        in_specs=[row(D_MODEL), full(norm_g), full(w_in_bf16), full(b_gate)],
        out_specs=[row(QKV_W), row(SHIFT_W), row(GATE_W)],
        out_shape=[jax.ShapeDtypeStruct((m, QKV_W), F32),
                   jax.ShapeDtypeStruct((m, SHIFT_W), F32),
                   jax.ShapeDtypeStruct((m, GATE_W), F32)],
        compiler_params=_params(("parallel",)),
        name="in_proj",
    )(x, norm_g, w_in_bf16, b_gate)


def _merge(o_a, lse_a, o_b, lse_b):
    m = jnp.maximum(lse_a, lse_b)
    wa = jnp.exp(lse_a - m)
    wb = jnp.exp(lse_b - m)
    den = wa + wb
    return (wa * o_a + wb * o_b) / den, m + jnp.log(den)


def _attn_body(*refs, group, has_prev, last):
    q_ref, kc_ref, kp_ref, vc_ref, vp_ref = refs[:5]
    refs = refs[5:]
    if has_prev:
        op_ref, lp_ref = refs[:2]
        refs = refs[2:]
    o_ref = refs[0]
    dil = DILATIONS[group]
    n = pl.program_id(2)
    q = q_ref[...]
    k2 = jnp.concatenate([kp_ref[...], kc_ref[...]], axis=0).astype(BF16)
    v2 = jnp.concatenate([vp_ref[...], vc_ref[...]], axis=0).astype(BF16)
    qi = lax.broadcasted_iota(jnp.int32, (BAND, 2 * BAND), 0)
    kj = lax.broadcasted_iota(jnp.int32, (BAND, 2 * BAND), 1)
    delta = BAND + qi - kj
    first_key = jnp.where(n > 0, 0, BAND)
    valid = (delta >= 0) & (delta <= BAND) & (kj >= first_key)
    dist = (delta * dil).astype(F32)
    head = lax.broadcasted_iota(jnp.int32, (1, GROUP_W), 1) // HEAD_DIM
    o_acc = jnp.zeros((BAND, GROUP_W), F32)
    lse_acc = jnp.zeros((BAND, GROUP_W), F32)
    for h in range(HG):
        hm = head == h
        qh = jnp.where(hm, q, 0.0).astype(BF16)
        s = _dg(qh, k2, NT) * ATTN_SCALE - SLOPES[group * HG + h] * dist
        s = jnp.where(valid, s, NEG_INF)
        m = jnp.max(s, axis=-1, keepdims=True)
        p = jnp.exp(s - m)
        l = jnp.sum(p, axis=-1, keepdims=True)
        pv = _dg(p.astype(BF16), v2)
        o_acc = jnp.where(hm, pv / l, o_acc)
        lse_acc = jnp.where(hm, m + jnp.log(l), lse_acc)
    if has_prev:
        o_acc, lse_acc = _merge(op_ref[...], lp_ref[...], o_acc, lse_acc)
    if last:
        o_ref[...] = o_acc.astype(o_ref.dtype)
    else:
        o_ref[...] = o_acc
        refs[1][...] = lse_acc


def _attn_group(qkv, prev, group, last):
    b, s, _ = qkv.shape
    dil = DILATIONS[group]
    l = s // dil
    nb = l // BAND
    cb = QKV_W // GROUP_W
    qkv_r = qkv.reshape(b, l, dil * QKV_W)
    blk = (None, BAND, GROUP_W)

    def col(off):
        return pl.BlockSpec(blk, lambda bi, r, n: (bi, n, r * cb + off + group))

    def col_prev(off):
        return pl.BlockSpec(blk, lambda bi, r, n: (bi, jnp.maximum(n - 1, 0), r * cb + off + group))

    pos = pl.BlockSpec(blk, lambda bi, r, n: (bi, n, r))
    hg = ATTN_W // GROUP_W
    in_specs = [col(0), col(hg), col_prev(hg), col(2 * hg), col_prev(2 * hg)]
    args = [qkv_r] * 5
    if prev is not None:
        in_specs += [pos, pos]
        args += [prev[0].reshape(b, l, dil * GROUP_W), prev[1].reshape(b, l, dil * GROUP_W)]
    if last:
        out_specs = [pos]
        out_shape = [jax.ShapeDtypeStruct((b, l, dil * GROUP_W), BF16)]
    else:
        out_specs = [pos, pos]
        out_shape = [jax.ShapeDtypeStruct((b, l, dil * GROUP_W), F32)] * 2
    outs = pl.pallas_call(
        functools.partial(_attn_body, group=group, has_prev=prev is not None, last=last),
        grid=(b, dil, nb),
        in_specs=in_specs,
        out_specs=out_specs,
        out_shape=out_shape,
        compiler_params=_params(("parallel", "parallel", "arbitrary")),
        name=f"attn_g{group}",
    )(*args)
    return [o.reshape(b, s, GROUP_W) for o in outs]


def _attn_prompt(qkv):
    assert qkv.shape[1] % (DILATIONS[-1] * BAND) == 0
    prev = None
    for group in range(len(DILATIONS)):
        last = group == len(DILATIONS) - 1
        prev = _attn_group(qkv, prev, group, last)
    return prev[0]


def _attn_sample_body(qkv_ref, c0_ref, c1_ref, c2_ref, o_ref):
    bt = qkv_ref.shape[0]
    lane_r = lax.broadcasted_iota(jnp.int32, (GROUP_W, 128), 0)
    lane_c = lax.broadcasted_iota(jnp.int32, (GROUP_W, 128), 1)
    seg = (lane_r // HEAD_DIM == lane_c).astype(BF16)
    ex_r = lax.broadcasted_iota(jnp.int32, (128, GROUP_W), 0)
    ex_c = lax.broadcasted_iota(jnp.int32, (128, GROUP_W), 1)
    expand = (ex_c // HEAD_DIM == ex_r).astype(BF16)
    hl = lax.broadcasted_iota(jnp.int32, (1, 128), 1)
    j = lax.broadcasted_iota(jnp.int32, (1, BAND, 1), 1)
    accs, ms, ls = [], [], []
    for g, c_ref in enumerate((c0_ref, c1_ref, c2_ref)):
        dil = DILATIONS[g]
        qg = qkv_ref[:, g * GROUP_W:(g + 1) * GROUP_W]
        kn = qkv_ref[:, ATTN_W + g * GROUP_W:ATTN_W + (g + 1) * GROUP_W]
        vn = qkv_ref[:, 2 * ATTN_W + g * GROUP_W:2 * ATTN_W + (g + 1) * GROUP_W]
        kb = c_ref[:, :, 0:GROUP_W]
        vb = c_ref[:, :, GROUP_W:2 * GROUP_W]
        slope = jnp.zeros((1, 128), F32)
        for h in range(HG):
            slope = jnp.where(hl == h, SLOPES[g * HG + h], slope)
        prod = (kb * qg[:, None, :]).reshape(bt * BAND, GROUP_W)
        s_b = _dot_exact_rhs(prod, seg, 2).reshape(bt, BAND, 128) * ATTN_SCALE
        s_b = s_b - slope[None] * ((BAND - j) * dil).astype(F32)
        s_n = _dot_exact_rhs(qg * kn, seg, 2) * ATTN_SCALE
        m = jnp.maximum(jnp.max(s_b, axis=1), s_n)
        p_b = jnp.exp(s_b - m[:, None, :])
        p_n = jnp.exp(s_n - m)
        l = jnp.sum(p_b, axis=1) + p_n
        pe = _dot_exact_rhs(p_b.reshape(bt * BAND, 128), expand, 2).reshape(bt, BAND, GROUP_W)
        acc = jnp.sum(pe * vb, axis=1) + _dot_exact_rhs(p_n, expand, 2) * vn
        accs.append(acc)
        ms.append(m)
        ls.append(l)
    m_all = jnp.maximum(jnp.maximum(ms[0], ms[1]), ms[2])
    sc = [jnp.exp(m - m_all) for m in ms]
    den = sc[0] * ls[0] + sc[1] * ls[1] + sc[2] * ls[2]
    den = jnp.where(hl < HG, den, 1.0)
    out = jnp.zeros((bt, GROUP_W), F32)
    for g in range(len(DILATIONS)):
        out = out + _dot_exact_rhs(sc[g] / den, expand, 2) * accs[g]
    o_ref[...] = out.astype(o_ref.dtype)


def _attn_sample(qkv, caches, bt):
    b = qkv.shape[0]
    kv_w = 2 * GROUP_W
    c_args, c_specs = [], []
    for g, c in enumerate(caches):
        dil = DILATIONS[g]
        assert c.shape[1] == BAND * dil
        c_args.append(c.reshape(b, BAND, dil * kv_w))
        c_specs.append(pl.BlockSpec((bt, BAND, kv_w), lambda i: (i, 0, 0)))
    return pl.pallas_call(
        _attn_sample_body,
        grid=(b // bt,),
        in_specs=[pl.BlockSpec((bt, QKV_W), lambda i: (i, 0))] + c_specs,
        out_specs=pl.BlockSpec((bt, GROUP_W), lambda i: (i, 0)),
        out_shape=jax.ShapeDtypeStruct((b, GROUP_W), BF16),
        compiler_params=_params(("parallel",)),
        name="attn_sample",
    )(qkv, *c_args)


WKV_PASSES = dict(g=1, s0=1, x=1, inv=1, y=1, s1=3)


def _mm(a, b, dims, passes):
    if passes == 1:
        return _dg(a.astype(BF16), b.astype(BF16), dims)
    ah, al = _split2(a)
    if passes == 2:
        bh = b.astype(BF16)
        return _dg(ah, bh, dims) + _dg(al, bh, dims)
    bh, bl = _split2(b)
    return _dg(ah, bh, dims) + _dg(al, bh, dims) + _dg(ah, bl, dims)


def _wkv_levels(live_rows):
    return int(np.ceil(np.log2(live_rows))) if live_rows > 1 else 0


def _wkv_chunk(r, kp, v, logw, av, bv, states, levels):
    c = CHUNK
    n2 = 2 * c
    pairs = range(N_PAIRS)
    ti = lax.broadcasted_iota(jnp.int32, (c, c), 0)
    tj = lax.broadcasted_iota(jnp.int32, (c, c), 1)
    tri = (ti >= tj).astype(BF16)
    lc = _dot_exact_rhs_left(tri, logw)
    e_in = jnp.exp(lc)
    e_neg = jnp.exp(-lc)
    a_t = av * jnp.exp(lc - logw)
    r_t = r * e_in
    b_t = bv * e_neg
    k_t = kp * e_neg
    first = lax.broadcasted_iota(jnp.int32, (1, PAIR_W), 1) < RWKV_N

    def stack(x, p):
        x = x[:, p * PAIR_W:(p + 1) * PAIR_W]
        return jnp.concatenate([jnp.where(first, x, 0.0), jnp.where(first, 0.0, x)], axis=0)

    ar = [jnp.concatenate([stack(a_t, p), stack(r_t, p)], axis=0) for p in pairs]
    bk = [jnp.concatenate([stack(b_t, p), stack(k_t, p)], axis=0) for p in pairs]
    v_s = [stack(v, p) for p in pairs]
    g = [_mm(ar[p], bk[p], NT, WKV_PASSES["g"]) for p in pairs]
    ar_s0 = [_mm(ar[p], states[p], NT, WKV_PASSES["s0"]) for p in pairs]
    ri = lax.broadcasted_iota(jnp.int32, (n2, n2), 0)
    ci = lax.broadcasted_iota(jnp.int32, (n2, n2), 1)
    strict = ri > ci
    incl = ri >= ci
    n_ab = [jnp.where(strict, g[p][0:n2, 0:n2], 0.0) for p in pairs]
    n_ak = [jnp.where(strict, g[p][0:n2, n2:2 * n2], 0.0) for p in pairs]
    m_r = [jnp.where(jnp.concatenate([incl, incl], axis=1), g[p][n2:2 * n2, :], 0.0) for p in pairs]
    z = [ar_s0[p][0:n2] + _mm(n_ak[p], v_s[p], NN, WKV_PASSES["x"]) for p in pairs]
    pw = n_ab
    for i in range(levels):
        if i < levels - 1:
            pz = [_mm(pw[p], jnp.concatenate([pw[p], z[p]], axis=1), NN, WKV_PASSES["inv"]) for p in pairs]
            pw = [pz[p][:, 0:n2] for p in pairs]
            z = [z[p] + pz[p][:, n2:2 * n2] for p in pairs]
        else:
            z = [z[p] + _mm(pw[p], z[p], NN, WKV_PASSES["inv"]) for p in pairs]
    uv = [jnp.concatenate([z[p], v_s[p]], axis=0) for p in pairs]
    y_s = [ar_s0[p][n2:2 * n2] + _mm(m_r[p], uv[p], NN, WKV_PASSES["y"]) for p in pairs]
    y = jnp.concatenate([y_s[p][0:c] + y_s[p][c:n2] for p in pairs], axis=1)
    w_end = e_in[c - 1:c, :]
    s1 = [(states[p] + _mm(uv[p], bk[p], TN, WKV_PASSES["s1"])) * w_end[:, p * PAIR_W:(p + 1) * PAIR_W] for p in pairs]
    return y, tuple(s1)


def _dot_exact_rhs_left(lhs_bf16, b):
    out = None
    rem = b
    for _ in range(3):
        part = rem.astype(BF16)
        term = _dg(lhs_bf16, part)
        out = term if out is None else out + term
        rem = rem - part.astype(F32)
    return out


def _rwkv_body(zb_ref, sh_ref, mu_ref, s0_ref,
               w0_ref, a0_ref, kk_ref, ka_ref, rk_ref, gg_ref, gb_ref, wwa_ref, wg_ref,
               out_ref, s_out_ref,
               s_scr, carry_scr, r_scr, kp_scr, v_scr, lw_scr, av_scr, bv_scr, y_scr, g_scr,
               *, t_valid):
    t = pl.program_id(1)
    tc = zb_ref.shape[0]

    @pl.when(t == 0)
    def _():
        s_scr[...] = s0_ref[...]
        carry_scr[...] = sh_ref[...]

    row = lax.broadcasted_iota(jnp.int32, (tc, 1), 0)
    z = zb_ref[...]
    zp = jnp.where(row == 0, carry_scr[...], pltpu.roll(z, 1, 0))
    carry_scr[...] = z[tc - 1:tc, :]
    zm = z + (zp - z) * mu_ref[...]
    r = zm[:, 0:RWKV_W]
    k = zm[:, RWKV_W:2 * RWKV_W]
    v = zm[:, 2 * RWKV_W:3 * RWKV_W]
    z_wa = zm[:, 3 * RWKV_W:3 * RWKV_W + 128]
    z_g = zm[:, 3 * RWKV_W + 128:SHIFT_W]
    half = lax.broadcasted_iota(jnp.int32, (1, 128), 1) < 64
    x_wa = jnp.where(half, jnp.tanh(z_wa), z_wa).astype(BF16)
    x_g = _sigmoid(z_g).astype(BF16)
    u = w0_ref[...] + _dg(x_wa, wwa_ref[:, 0:RWKV_W])
    logw = -DECAY_SCALE * _sigmoid(u)
    a = _sigmoid(a0_ref[...] + _dg(x_wa, wwa_ref[:, RWKV_W:2 * RWKV_W]))
    g_scr[...] = _dg(x_g, wg_ref[...])

    lr = lax.broadcasted_iota(jnp.int32, (PAIR_W, PAIR_W), 0)
    lc = lax.broadcasted_iota(jnp.int32, (PAIR_W, PAIR_W), 1)
    seg = (lr // RWKV_N == lc // RWKV_N).astype(BF16)

    def head_sum(x):
        return jnp.concatenate(
            [_dot_exact_rhs(x[:, p * PAIR_W:(p + 1) * PAIR_W], seg, 2) for p in range(N_PAIRS)], axis=1)

    kk = k * kk_ref[...]
    kk = kk / jnp.maximum(jnp.sqrt(head_sum(kk * kk)), 1e-12)
    kp = k * (1.0 + (a - 1.0) * ka_ref[...])
    av = -kk
    bv = kk * a
    if t_valid is not None:
        live = row < t_valid
        zero = lambda x_: jnp.where(live, x_, 0.0)
        r, kp, v, logw, av, bv = zero(r), zero(kp), zero(v), zero(logw), zero(av), zero(bv)
    r_scr[...] = r
    kp_scr[...] = kp
    v_scr[...] = v
    lw_scr[...] = logw
    av_scr[...] = av
    bv_scr[...] = bv
    levels = _wkv_levels(CHUNK if t_valid is None else min(t_valid, CHUNK))

    def chunk(ci, states):
        rows = pl.ds(pl.multiple_of(ci * CHUNK, CHUNK), CHUNK)
        y, states = _wkv_chunk(r_scr[rows, :], kp_scr[rows, :], v_scr[rows, :], lw_scr[rows, :],
                               av_scr[rows, :], bv_scr[rows, :], states, levels)
        y_scr[rows, :] = y
        return states

    states = lax.fori_loop(0, tc // CHUNK, chunk, tuple(s_scr[p] for p in range(N_PAIRS)))
    for p in range(N_PAIRS):
        s_scr[p] = states[p]
        s_out_ref[p] = states[p]

    y = y_scr[...]
    inv_n = 1.0 / RWKV_N
    mean = head_sum(y) * inv_n
    d = y - mean
    var = head_sum(d * d) * inv_n
    yn = d * lax.rsqrt(var + GN_EPS) * gg_ref[...] + gb_ref[...]
    bonus = head_sum(r_scr[...] * kp_scr[...] * rk_ref[...]) * v_scr[...]
    out_ref[...] = ((yn + bonus) * g_scr[...]).astype(out_ref.dtype)


def _rwkv(zb, shift0, s0_bd, w, tc, t_valid=None):
    b, t_len, _ = zb.shape
    full = lambda a: pl.BlockSpec(a.shape, lambda bi, t: (0,) * a.ndim)
    state = pl.BlockSpec((None, N_PAIRS, PAIR_W, PAIR_W), lambda bi, t: (bi, 0, 0, 0))
    consts = [w["mu_shift"]]
    vecs = [w["w0"], w["a0"], w["k_k"], w["k_a"], w["r_k"], w["gn_g"], w["gn_b"], w["wa_lora"], w["g_lora"]]
    in_specs = ([pl.BlockSpec((None, tc, SHIFT_W), lambda bi, t: (bi, t, 0)),
                 pl.BlockSpec((None, 1, SHIFT_W), lambda bi, t: (bi, 0, 0))]
                + [full(c) for c in consts] + [state] + [full(c) for c in vecs])
    vm = lambda shape: pltpu.VMEM(shape, F32)
    scratch = [vm((N_PAIRS, PAIR_W, PAIR_W)), vm((1, SHIFT_W))] + [vm((tc, RWKV_W))] * 8
    return pl.pallas_call(
        functools.partial(_rwkv_body, t_valid=t_valid),
        grid=(b, t_len // tc),
        in_specs=in_specs,
        out_specs=[pl.BlockSpec((None, tc, RWKV_W), lambda bi, t: (bi, t, 0)), state],
        out_shape=[jax.ShapeDtypeStruct((b, t_len, RWKV_W), BF16),
                   jax.ShapeDtypeStruct((b, N_PAIRS, PAIR_W, PAIR_W), F32)],
        scratch_shapes=scratch,
        compiler_params=_params(("parallel", "arbitrary")),
        name="rwkv",
    )(zb, shift0, *consts, s0_bd, *vecs)


def _state_to_pairs(s):
    b = s.shape[0]
    s = s.reshape(b, N_PAIRS, 2, RWKV_N, RWKV_N)
    z = jnp.zeros_like(s[:, :, 0])
    top = jnp.concatenate([s[:, :, 0], z], axis=-1)
    bot = jnp.concatenate([z, s[:, :, 1]], axis=-1)
    return jnp.concatenate([top, bot], axis=-2)


def _pairs_to_state(s_bd):
    b = s_bd.shape[0]
    h0 = s_bd[:, :, :RWKV_N, :RWKV_N]
    h1 = s_bd[:, :, RWKV_N:, RWKV_N:]
    return jnp.stack([h0, h1], axis=2).reshape(b, RWKV_HEADS, RWKV_N, RWKV_N)


def _out_proj_body(x_ref, attn_ref, rwkv_ref, gate_ref, wpa_ref, wpb_ref, wout_ref, n2_ref,
                   wup_ref, wdn_ref, nf_ref, y_ref):
    pa = _dg(attn_ref[...], wpa_ref[...])
    pb = _dg(rwkv_ref[...], wpb_ref[...])
    merged = gate_ref[:, 0:D_MODEL] * pa + gate_ref[:, D_MODEL:GATE_W] * pb
    x1 = x_ref[...] + _dg(merged.astype(BF16), wout_ref[...])
    hm = _rms(x1, n2_ref[...]).astype(BF16)
    acc = x1
    for c in range(0, D_FF, 1024):
        up = jnp.maximum(_dg(hm, wup_ref[:, c:c + 1024]), 0.0)
        acc = acc + _dg((up * up).astype(BF16), wdn_ref[c:c + 1024, :])
    y_ref[...] = _rms(acc, nf_ref[...])


def _out_proj(x, attn, rwkv, gate, w, tm):
    m = x.shape[0]
    row = lambda wd: pl.BlockSpec((tm, wd), lambda i: (i, 0))
    full = lambda a: pl.BlockSpec(a.shape, lambda i: (0,) * a.ndim)
    consts = [w["w_proj_a"], w["w_proj_b"], w["w_out"], w["norm2_g"], w["w_up"], w["w_down"], w["normf_g"]]
    return pl.pallas_call(
        _out_proj_body,
        grid=(m // tm,),
        in_specs=[row(D_MODEL), row(GROUP_W), row(RWKV_W), row(GATE_W)] + [full(c) for c in consts],
        out_specs=row(D_MODEL),
        out_shape=jax.ShapeDtypeStruct((m, D_MODEL), F32),
        compiler_params=_params(("parallel",)),
        name="out_proj",
    )(x, attn, rwkv, gate, *consts)


def _kv_rows(qkv, group, rows):
    b, s, _ = qkv.shape
    k0 = ATTN_W + group * GROUP_W
    v0 = 2 * ATTN_W + group * GROUP_W
    k = qkv[:, s - rows:, k0:k0 + GROUP_W].reshape(b, rows, HG, HEAD_DIM)
    v = qkv[:, s - rows:, v0:v0 + GROUP_W].reshape(b, rows, HG, HEAD_DIM)
    return jnp.stack([k, v], axis=2)


def _layer_weights(l, norm1_g, w_in, b_gate, mu_shift, w0, w_lora_up, a0, a_lora_up, g_lora_up, k_k, k_a,
                   r_k, gn_g, gn_b, w_proj_a, w_proj_b, w_out, norm2_g, w_up, w_down, normf_g):
    row = lambda a: a.reshape(1, -1)
    zero = jnp.zeros_like(w_lora_up[l])
    wa_lora = jnp.concatenate([jnp.concatenate([w_lora_up[l], zero], axis=1),
                               jnp.concatenate([zero, a_lora_up[l]], axis=1)], axis=0)
    return dict(
        norm1_g=row(norm1_g[l]), w_in=w_in[l].astype(BF16), b_gate=row(b_gate[l]), mu_shift=row(mu_shift[l]),
        w0=row(w0[l]), a0=row(a0[l]), k_k=row(k_k[l]), k_a=row(k_a[l]), r_k=row(r_k[l]),
        gn_g=row(gn_g[l]), gn_b=row(gn_b[l]),
        wa_lora=wa_lora.astype(BF16), g_lora=g_lora_up[l].astype(BF16),
        w_proj_a=w_proj_a[l].astype(BF16), w_proj_b=w_proj_b[l].astype(BF16), w_out=w_out[l].astype(BF16),
        norm2_g=row(norm2_g[l]), w_up=w_up[l].astype(BF16), w_down=w_down[l].astype(BF16),
        normf_g=row(normf_g))


def _prompt_layer(x, w, tm, tc):
    b, s, _ = x.shape
    x2 = x.reshape(b * s, D_MODEL)
    qkv, zb, gate = _in_proj(x2, w["norm1_g"], w["w_in"], w["b_gate"], tm)
    qkv = qkv.reshape(b, s, QKV_W)
    zb = zb.reshape(b, s, SHIFT_W)
    attn = _attn_prompt(qkv)
    rwkv, s_bd = _rwkv(zb, jnp.zeros((b, 1, SHIFT_W), F32), jnp.zeros((b, N_PAIRS, PAIR_W, PAIR_W), F32), w, tc)
    y = _out_proj(x2, attn.reshape(b * s, GROUP_W), rwkv.reshape(b * s, RWKV_W), gate, w, tm)
    kvs = [_kv_rows(qkv, g, min(BAND * d, s)) for g, d in enumerate(DILATIONS)]
    return y.reshape(b, s, D_MODEL), kvs, _pairs_to_state(s_bd), zb[:, -1]


def _sample_layer(x, caches, s0, shift0, w):
    b, t_len, _ = x.shape
    assert t_len == 1
    x2 = x.reshape(b, D_MODEL)
    qkv, zb, gate = _in_proj(x2, w["norm1_g"], w["w_in"], w["b_gate"], b)
    attn = _attn_sample(qkv, caches, 8)
    zb_pad = jnp.pad(zb[:, None, :], ((0, 0), (0, CHUNK - 1), (0, 0)))
    rwkv, s_bd = _rwkv(zb_pad, shift0[:, None, :], _state_to_pairs(s0), w, CHUNK, t_valid=1)
    y = _out_proj(x2, attn, rwkv[:, 0], gate, w, b)
    kvs = [_kv_rows(qkv[:, None, :], g, 1) for g in range(len(DILATIONS))]
    return y.reshape(b, 1, D_MODEL), kvs, _pairs_to_state(s_bd), zb


def kernel(x_prompt, x_sample, cache_kv_w128, cache_kv_w512, cache_kv_w2048, state_wkv, state_shift, norm1_g, w_in, b_gate, mu_shift, w0, w_lora_up, a0, a_lora_up, g_lora_up, k_k, k_a, r_k, gn_g, gn_b, w_proj_a, w_proj_b, w_out, norm2_g, w_up, w_down, normf_g):
    depth = w_in.shape[0]
    assert depth == 1, "the final norm is fused into the layer's output stage"
    w = _layer_weights(0, norm1_g, w_in, b_gate, mu_shift, w0, w_lora_up, a0, a_lora_up, g_lora_up, k_k, k_a,
                       r_k, gn_g, gn_b, w_proj_a, w_proj_b, w_out, norm2_g, w_up, w_down, normf_g)
    y_p, kv_p, wkv_p, shift_p = _prompt_layer(x_prompt, w, 256, 512)
    y_s, kv_s, wkv_s, shift_s = _sample_layer(
        x_sample, (cache_kv_w128[0], cache_kv_w512[0], cache_kv_w2048[0]), state_wkv[0], state_shift[0], w)
    lead = lambda a: a[None]
    return (y_p, y_s, lead(kv_p[0]), lead(kv_p[1]), lead(kv_p[2]), lead(wkv_p), lead(shift_p),
            lead(kv_s[0]), lead(kv_s[1]), lead(kv_s[2]), lead(wkv_s), lead(shift_s))
```

```python
import functools

import numpy as np
import jax
import jax.numpy as jnp
from jax import lax
from jax.experimental import pallas as pl
from jax.experimental.pallas import tpu as pltpu

F32 = jnp.float32
BF16 = jnp.bfloat16

D_MODEL = 1024
HEAD_DIM = 64
HG = 4
DILATIONS = (1, 4, 16)
BAND = 128
N_ATTN_HEADS = HG * len(DILATIONS)
GROUP_W = HG * HEAD_DIM
LANE_HALVES = GROUP_W // 128
ATTN_W = N_ATTN_HEADS * HEAD_DIM
QKV_W = 3 * ATTN_W
RWKV_N = 64
RWKV_W = 512
RWKV_HEADS = RWKV_W // RWKV_N
PAIR_W = 2 * RWKV_N
N_PAIRS = RWKV_HEADS // 2
LORA_W = 256
SHIFT_W = 3 * RWKV_W + LORA_W
GATE_W = 2 * D_MODEL
IN_W = QKV_W + SHIFT_W + GATE_W
D_FF = 4 * D_MODEL
NORM_EPS = 1e-6
GN_EPS = 64e-5
NEG_INF = -1e30
ATTN_SCALE = HEAD_DIM ** -0.5
CHUNK = 64
DECAY_SCALE = float(np.exp(-0.5))
SLOPES = [float(s) for s in np.exp2(-8.0 * np.arange(1, N_ATTN_HEADS + 1, dtype=np.float32) / N_ATTN_HEADS)]

V7X_VMEM_LIMIT = 56 * 1024 * 1024

NN = (((1,), (0,)), ((), ()))
NT = (((1,), (1,)), ((), ()))
TN = (((0,), (0,)), ((), ()))


def _dg(a, b, dims=NN):
    return lax.dot_general(a, b, dims, preferred_element_type=F32)


def _split2(a):
    hi = a.astype(BF16)
    lo = (a - hi.astype(F32)).astype(BF16)
    return hi, lo


def _dot3(a, b, dims=NN):
    ah, al = _split2(a)
    bh, bl = _split2(b)
    return _dg(ah, bh, dims) + _dg(al, bh, dims) + _dg(ah, bl, dims)


def _dot_exact_rhs(a, b_bf16, passes):
    out = None
    rem = a
    for _ in range(passes):
        part = rem.astype(BF16)
        term = _dg(part, b_bf16)
        out = term if out is None else out + term
        rem = rem - part.astype(F32)
    return out


def _sigmoid(x):
    return 1.0 / (1.0 + jnp.exp(-x))


def _rms(x, g):
    return x * lax.rsqrt(jnp.mean(x * x, axis=-1, keepdims=True) + NORM_EPS) * g


def _params(sem):
    return pltpu.CompilerParams(dimension_semantics=sem, vmem_limit_bytes=V7X_VMEM_LIMIT)


def _in_proj_body(x_ref, g_ref, w_ref, bg_ref, qkv_ref, zb_ref, gate_ref):
    h = _rms(x_ref[...], g_ref[...]).astype(BF16)
    for c in range(0, QKV_W, 768):
        qkv_ref[:, c:c + 768] = _dg(h, w_ref[:, c:c + 768])
    for c in range(0, SHIFT_W, 896):
        zb_ref[:, c:c + 896] = _dg(h, w_ref[:, QKV_W + c:QKV_W + c + 896])
    for c in range(0, GATE_W, 1024):
        zg = _dg(h, w_ref[:, QKV_W + SHIFT_W + c:QKV_W + SHIFT_W + c + 1024])
        gate_ref[:, c:c + 1024] = _sigmoid(zg + bg_ref[:, c:c + 1024])


def _in_proj(x, norm_g, w_in_bf16, b_gate, tm):
    m = x.shape[0]
    row = lambda w: pl.BlockSpec((tm, w), lambda i: (i, 0))
    full = lambda a: pl.BlockSpec(a.shape, lambda i: (0,) * a.ndim)
    return pl.pallas_call(
        _in_proj_body,
        grid=(m // tm,),
        in_specs=[row(D_MODEL), full(norm_g), full(w_in_bf16), full(b_gate)],
        out_specs=[row(QKV_W), row(SHIFT_W), row(GATE_W)],
        out_shape=[jax.ShapeDtypeStruct((m, QKV_W), F32),
                   jax.ShapeDtypeStruct((m, SHIFT_W), F32),
                   jax.ShapeDtypeStruct((m, GATE_W), F32)],
        compiler_params=_params(("parallel",)),
        name="in_proj",
    )(x, norm_g, w_in_bf16, b_gate)


def _residue_rows(r, count, dil):
    return pl.ds(r, count) if dil == 1 else pl.ds(r, count, stride=dil)


def _attn_group_blocks(q_ref, k_ref, v_ref, o_scr, lse_scr, qs, ks, vs, os_, ls, group):
    s_len = qs.shape[0]
    dil = DILATIONS[group]
    l_res = s_len // dil
    nb = l_res // BAND
    for r in range(dil):
        dst = pl.ds(r * l_res, l_res)
        src = _residue_rows(r, l_res, dil)
        for staged, halves in ((qs, q_ref), (ks, k_ref), (vs, v_ref)):
            staged[dst, :] = jnp.concatenate([h[src, :] for h in halves], axis=1).astype(BF16)
    has_prev = nb > 1
    nk = 2 * BAND if has_prev else BAND
    qi = lax.broadcasted_iota(jnp.int32, (BAND, nk), 0)
    kj = lax.broadcasted_iota(jnp.int32, (BAND, nk), 1)
    delta = (nk - BAND) + qi - kj
    band = (delta >= 0) & (delta <= BAND)
    dist = (delta * dil).astype(F32)
    head = lax.broadcasted_iota(jnp.int32, (1, GROUP_W), 1) // HEAD_DIM

    def block(idx, carry):
        cur = pl.ds(pl.multiple_of(idx * BAND, BAND), BAND)
        q = qs[cur, :]
        k2 = ks[cur, :]
        v2 = vs[cur, :]
        valid = band
        if has_prev:
            n = idx % nb
            prev = pl.ds(pl.multiple_of(jnp.maximum(idx - 1, 0) * BAND, BAND), BAND)
            k2 = jnp.concatenate([ks[prev, :], k2], axis=0)
            v2 = jnp.concatenate([vs[prev, :], v2], axis=0)
            valid = band & (kj >= jnp.where(n > 0, 0, BAND))
        heads = range(HG)
        hm = [head == h for h in heads]
        s = [_dg(jnp.where(hm[h], q, jnp.zeros_like(q)), k2, NT) for h in heads]
        s = [jnp.where(valid, s[h] * ATTN_SCALE - SLOPES[group * HG + h] * dist, NEG_INF) for h in heads]
        m = [jnp.max(s[h], axis=-1, keepdims=True) for h in heads]
        p = [jnp.exp(s[h] - m[h]) for h in heads]
        l = [jnp.sum(p[h], axis=-1, keepdims=True) for h in heads]
        pv = [_dg(p[h].astype(BF16), v2) for h in heads]
        o_acc = jnp.zeros((BAND, GROUP_W), F32)
        lse_acc = jnp.zeros((BAND, GROUP_W), F32)
        for h in heads:
            o_acc = jnp.where(hm[h], pv[h] / l[h], o_acc)
            lse_acc = jnp.where(hm[h], m[h] + jnp.log(l[h]), lse_acc)
        os_[cur, :] = o_acc
        ls[cur, :] = lse_acc
        return carry

    lax.fori_loop(0, dil * nb, block, 0)
    for r in range(dil):
        src = pl.ds(r * l_res, l_res)
        dst = _residue_rows(r, l_res, dil)
        for half in range(LANE_HALVES):
            cols = slice(half * 128, (half + 1) * 128)
            o_scr[group * LANE_HALVES + half, dst, :] = os_[src, cols]
            lse_scr[group * LANE_HALVES + half, dst, :] = ls[src, cols]


def _attn_prompt_body(q0_ref, q1_ref, k0_ref, k1_ref, v0_ref, v1_ref, o_ref, o_scr, lse_scr, qs, ks, vs, os_, ls):
    q_ref, k_ref, v_ref = (q0_ref, q1_ref), (k0_ref, k1_ref), (v0_ref, v1_ref)
    gid = pl.program_id(1)
    n_groups = len(DILATIONS)
    for g in range(n_groups):
        @pl.when(gid == g)
        def _(g=g):
            _attn_group_blocks(q_ref, k_ref, v_ref, o_scr, lse_scr, qs, ks, vs, os_, ls, g)

    @pl.when(gid == n_groups - 1)
    def _():
        def merge(i, carry):
            rows = pl.ds(pl.multiple_of(i * BAND, BAND), BAND)
            both = lambda ref, g: jnp.concatenate(
                [ref[g * LANE_HALVES + half, rows, :] for half in range(LANE_HALVES)], axis=1)
            lse = [both(lse_scr, g) for g in range(n_groups)]
            m = functools.reduce(jnp.maximum, lse)
            wts = [jnp.exp(x - m) for x in lse]
            num = sum(wts[g] * both(o_scr, g) for g in range(n_groups))
            o_ref[rows, :] = (num / sum(wts)).astype(o_ref.dtype)
            return carry

        lax.fori_loop(0, o_ref.shape[0] // BAND, merge, 0)


def _attn_prompt(qkv):
    b, s, _ = qkv.shape
    n_groups = len(DILATIONS)
    assert s % (DILATIONS[-1] * BAND) == 0
    sec = ATTN_W // 128

    def col(section, half):
        return pl.BlockSpec((None, s, 128), lambda bi, g: (bi, 0, section * sec + g * LANE_HALVES + half))

    return pl.pallas_call(
        _attn_prompt_body,
        grid=(b, n_groups),
        in_specs=[col(section, half) for section in range(3) for half in range(LANE_HALVES)],
        out_specs=pl.BlockSpec((None, s, GROUP_W), lambda bi, g: (bi, 0, 0)),
        out_shape=jax.ShapeDtypeStruct((b, s, GROUP_W), BF16),
        scratch_shapes=([pltpu.VMEM((n_groups * LANE_HALVES, s, 128), F32)] * 2
                        + [pltpu.VMEM((s, GROUP_W), BF16)] * 3 + [pltpu.VMEM((s, GROUP_W), F32)] * 2),
        compiler_params=_params(("parallel", "arbitrary")),
        name="attn_prompt",
    )(*([qkv] * (3 * LANE_HALVES)))


def _attn_sample_body(qkv_ref, c0_ref, c1_ref, c2_ref, o_ref):
    j = lax.broadcasted_iota(jnp.int32, (1, BAND, 1, 1), 1)
    hidx = lax.broadcasted_iota(jnp.int32, (1, HG, 1), 1)
    accs, ms, ls = [], [], []
    for g, c_ref in enumerate((c0_ref, c1_ref, c2_ref)):
        dil = DILATIONS[g]
        q = qkv_ref[:, 0, g]
        kn = qkv_ref[:, 1, g]
        vn = qkv_ref[:, 2, g]
        kb = c_ref[:, :, 0]
        vb = c_ref[:, :, 1]
        slope = jnp.zeros((1, HG, 1), F32)
        for h in range(HG):
            slope = jnp.where(hidx == h, SLOPES[g * HG + h], slope)
        s_b = jnp.sum(kb * q[:, None], axis=-1, keepdims=True) * ATTN_SCALE
        s_b = s_b - slope[:, None] * ((BAND - j) * dil).astype(F32)
        s_n = jnp.sum(kn * q, axis=-1, keepdims=True) * ATTN_SCALE
        m = jnp.maximum(jnp.max(s_b, axis=1), s_n)
        p_b = jnp.exp(s_b - m[:, None])
        p_n = jnp.exp(s_n - m)
        ls.append(jnp.sum(p_b, axis=1) + p_n)
        accs.append(jnp.sum(p_b * vb, axis=1) + p_n * vn)
        ms.append(m)
    m_all = functools.reduce(jnp.maximum, ms)
    sc = [jnp.exp(m - m_all) for m in ms]
    den = sum(s * l for s, l in zip(sc, ls))
    o_ref[...] = sum((s / den) * a for s, a in zip(sc, accs))


def _attn_sample(qkv, caches, bt):
    b = qkv.shape[0]
    n_groups = len(DILATIONS)
    c_args, c_specs = [], []
    for g, c in enumerate(caches):
        dil = DILATIONS[g]
        assert c.shape[1] == BAND * dil
        c_args.append(c.reshape(b, BAND, dil, 2, HG, HEAD_DIM))
        c_specs.append(pl.BlockSpec((bt, BAND, None, 2, HG, HEAD_DIM), lambda i: (i, 0, 0, 0, 0, 0)))
    out = pl.pallas_call(
        _attn_sample_body,
        grid=(b // bt,),
        in_specs=[pl.BlockSpec((bt, 3, n_groups, HG, HEAD_DIM), lambda i: (i, 0, 0, 0, 0))] + c_specs,
        out_specs=pl.BlockSpec((bt, HG, HEAD_DIM), lambda i: (i, 0, 0)),
        out_shape=jax.ShapeDtypeStruct((b, HG, HEAD_DIM), F32),
        compiler_params=_params(("parallel",)),
        name="attn_sample",
    )(qkv.reshape(b, 3, n_groups, HG, HEAD_DIM), *c_args)
    return out.reshape(b, GROUP_W).astype(BF16)


WKV_PASSES = dict(g=1, s0=1, x=1, inv=1, y=1, s1=3)


def _mm(a, b, dims, passes):
    if passes == 1:
        return _dg(a.astype(BF16), b.astype(BF16), dims)
    ah, al = _split2(a)
    if passes == 2:
        bh = b.astype(BF16)
        return _dg(ah, bh, dims) + _dg(al, bh, dims)
    bh, bl = _split2(b)
    return _dg(ah, bh, dims) + _dg(al, bh, dims) + _dg(ah, bl, dims)


def _wkv_levels(live_rows):
    return int(np.ceil(np.log2(live_rows))) if live_rows > 1 else 0


def _wkv_chunk(r, kp, v, logw, av, bv, states, levels):
    c = CHUNK
    n2 = 2 * c
    pairs = range(N_PAIRS)
    ti = lax.broadcasted_iota(jnp.int32, (c, c), 0)
    tj = lax.broadcasted_iota(jnp.int32, (c, c), 1)
    tri = (ti >= tj).astype(BF16)
    lc = _dot_exact_rhs_left(tri, logw)
    e_in = jnp.exp(lc)
    e_neg = jnp.exp(-lc)
    a_t = av * jnp.exp(lc - logw)
    r_t = r * e_in
    b_t = bv * e_neg
    k_t = kp * e_neg
    first = lax.broadcasted_iota(jnp.int32, (1, PAIR_W), 1) < RWKV_N

    def stack(x, p):
        x = x[:, p * PAIR_W:(p + 1) * PAIR_W]
        return jnp.concatenate([jnp.where(first, x, 0.0), jnp.where(first, 0.0, x)], axis=0)

    ar = [jnp.concatenate([stack(a_t, p), stack(r_t, p)], axis=0) for p in pairs]
    bk = [jnp.concatenate([stack(b_t, p), stack(k_t, p)], axis=0) for p in pairs]
    v_s = [stack(v, p) for p in pairs]
    g = [_mm(ar[p], bk[p], NT, WKV_PASSES["g"]) for p in pairs]
    ar_s0 = [_mm(ar[p], states[p], NT, WKV_PASSES["s0"]) for p in pairs]
    ri = lax.broadcasted_iota(jnp.int32, (n2, n2), 0)
    ci = lax.broadcasted_iota(jnp.int32, (n2, n2), 1)
    strict = ri > ci
    incl = ri >= ci
    n_ab = [jnp.where(strict, g[p][0:n2, 0:n2], 0.0) for p in pairs]
    n_ak = [jnp.where(strict, g[p][0:n2, n2:2 * n2], 0.0) for p in pairs]
    m_r = [jnp.where(jnp.concatenate([incl, incl], axis=1), g[p][n2:2 * n2, :], 0.0) for p in pairs]
    z = [ar_s0[p][0:n2] + _mm(n_ak[p], v_s[p], NN, WKV_PASSES["x"]) for p in pairs]
    pw = n_ab
    for i in range(levels):
        if i < levels - 1:
            pz = [_mm(pw[p], jnp.concatenate([pw[p], z[p]], axis=1), NN, WKV_PASSES["inv"]) for p in pairs]
            pw = [pz[p][:, 0:n2] for p in pairs]
            z = [z[p] + pz[p][:, n2:2 * n2] for p in pairs]
        else:
            z = [z[p] + _mm(pw[p], z[p], NN, WKV_PASSES["inv"]) for p in pairs]
    uv = [jnp.concatenate([z[p], v_s[p]], axis=0) for p in pairs]
    y_s = [ar_s0[p][n2:2 * n2] + _mm(m_r[p], uv[p], NN, WKV_PASSES["y"]) for p in pairs]
    y = jnp.concatenate([y_s[p][0:c] + y_s[p][c:n2] for p in pairs], axis=1)
    w_end = e_in[c - 1:c, :]
    s1 = [(states[p] + _mm(uv[p], bk[p], TN, WKV_PASSES["s1"])) * w_end[:, p * PAIR_W:(p + 1) * PAIR_W] for p in pairs]
    return y, tuple(s1)


def _dot_exact_rhs_left(lhs_bf16, b):
    out = None
    rem = b
    for _ in range(3):
        part = rem.astype(BF16)
        term = _dg(lhs_bf16, part)
        out = term if out is None else out + term
        rem = rem - part.astype(F32)
    return out


def _rwkv_body(zb_ref, sh_ref, mu_ref, s0_ref,
               w0_ref, a0_ref, kk_ref, ka_ref, rk_ref, gg_ref, gb_ref, wwa_ref, wg_ref,
               out_ref, s_out_ref,
               s_scr, carry_scr, r_scr, kp_scr, v_scr, lw_scr, av_scr, bv_scr, y_scr, g_scr,
               *, t_valid):
    t = pl.program_id(1)
    tc = zb_ref.shape[0]

    @pl.when(t == 0)
    def _():
        s_scr[...] = s0_ref[...]
        carry_scr[...] = sh_ref[...]

    row = lax.broadcasted_iota(jnp.int32, (tc, 1), 0)
    z = zb_ref[...]
    zp = jnp.where(row == 0, carry_scr[...], pltpu.roll(z, 1, 0))
    carry_scr[...] = z[tc - 1:tc, :]
    zm = z + (zp - z) * mu_ref[...]
    r = zm[:, 0:RWKV_W]
    k = zm[:, RWKV_W:2 * RWKV_W]
    v = zm[:, 2 * RWKV_W:3 * RWKV_W]
    z_wa = zm[:, 3 * RWKV_W:3 * RWKV_W + 128]
    z_g = zm[:, 3 * RWKV_W + 128:SHIFT_W]
    half = lax.broadcasted_iota(jnp.int32, (1, 128), 1) < 64
    x_wa = jnp.where(half, jnp.tanh(z_wa), z_wa).astype(BF16)
    x_g = _sigmoid(z_g).astype(BF16)
    u = w0_ref[...] + _dg(x_wa, wwa_ref[:, 0:RWKV_W])
    logw = -DECAY_SCALE * _sigmoid(u)
    a = _sigmoid(a0_ref[...] + _dg(x_wa, wwa_ref[:, RWKV_W:2 * RWKV_W]))
    g_scr[...] = _dg(x_g, wg_ref[...])

    lr = lax.broadcasted_iota(jnp.int32, (PAIR_W, PAIR_W), 0)
    lc = lax.broadcasted_iota(jnp.int32, (PAIR_W, PAIR_W), 1)
    seg = (lr // RWKV_N == lc // RWKV_N).astype(BF16)

    def head_sum(x):
        return jnp.concatenate(
            [_dot_exact_rhs(x[:, p * PAIR_W:(p + 1) * PAIR_W], seg, 2) for p in range(N_PAIRS)], axis=1)

    kk = k * kk_ref[...]
    kk = kk / jnp.maximum(jnp.sqrt(head_sum(kk * kk)), 1e-12)
    kp = k * (1.0 + (a - 1.0) * ka_ref[...])
    av = -kk
    bv = kk * a
    if t_valid is not None:
        live = row < t_valid
        zero = lambda x_: jnp.where(live, x_, 0.0)
        r, kp, v, logw, av, bv = zero(r), zero(kp), zero(v), zero(logw), zero(av), zero(bv)
    r_scr[...] = r
    kp_scr[...] = kp
    v_scr[...] = v
    lw_scr[...] = logw
    av_scr[...] = av
    bv_scr[...] = bv
    levels = _wkv_levels(CHUNK if t_valid is None else min(t_valid, CHUNK))

    def chunk(ci, states):
        rows = pl.ds(pl.multiple_of(ci * CHUNK, CHUNK), CHUNK)
        y, states = _wkv_chunk(r_scr[rows, :], kp_scr[rows, :], v_scr[rows, :], lw_scr[rows, :],
                               av_scr[rows, :], bv_scr[rows, :], states, levels)
        y_scr[rows, :] = y
        return states

    states = lax.fori_loop(0, tc // CHUNK, chunk, tuple(s_scr[p] for p in range(N_PAIRS)))
    for p in range(N_PAIRS):
        s_scr[p] = states[p]
        s_out_ref[p] = states[p]

    y = y_scr[...]
    inv_n = 1.0 / RWKV_N
    mean = head_sum(y) * inv_n
    d = y - mean
    var = head_sum(d * d) * inv_n
    yn = d * lax.rsqrt(var + GN_EPS) * gg_ref[...] + gb_ref[...]
    bonus = head_sum(r_scr[...] * kp_scr[...] * rk_ref[...]) * v_scr[...]
    out_ref[...] = ((yn + bonus) * g_scr[...]).astype(out_ref.dtype)


def _rwkv(zb, shift0, s0_bd, w, tc, t_valid=None):
    b, t_len, _ = zb.shape
    full = lambda a: pl.BlockSpec(a.shape, lambda bi, t: (0,) * a.ndim)
    state = pl.BlockSpec((None, N_PAIRS, PAIR_W, PAIR_W), lambda bi, t: (bi, 0, 0, 0))
    consts = [w["mu_shift"]]
    vecs = [w["w0"], w["a0"], w["k_k"], w["k_a"], w["r_k"], w["gn_g"], w["gn_b"], w["wa_lora"], w["g_lora"]]
    in_specs = ([pl.BlockSpec((None, tc, SHIFT_W), lambda bi, t: (bi, t, 0)),
                 pl.BlockSpec((None, 1, SHIFT_W), lambda bi, t: (bi, 0, 0))]
                + [full(c) for c in consts] + [state] + [full(c) for c in vecs])
    vm = lambda shape: pltpu.VMEM(shape, F32)
    scratch = [vm((N_PAIRS, PAIR_W, PAIR_W)), vm((1, SHIFT_W))] + [vm((tc, RWKV_W))] * 8
    return pl.pallas_call(
        functools.partial(_rwkv_body, t_valid=t_valid),
        grid=(b, t_len // tc),
        in_specs=in_specs,
        out_specs=[pl.BlockSpec((None, tc, RWKV_W), lambda bi, t: (bi, t, 0)), state],
        out_shape=[jax.ShapeDtypeStruct((b, t_len, RWKV_W), BF16),
                   jax.ShapeDtypeStruct((b, N_PAIRS, PAIR_W, PAIR_W), F32)],
        scratch_shapes=scratch,
        compiler_params=_params(("parallel", "arbitrary")),
        name="rwkv",
    )(zb, shift0, *consts, s0_bd, *vecs)


def _state_to_pairs(s):
    b = s.shape[0]
    s = s.reshape(b, N_PAIRS, 2, RWKV_N, RWKV_N)
    z = jnp.zeros_like(s[:, :, 0])
    top = jnp.concatenate([s[:, :, 0], z], axis=-1)
    bot = jnp.concatenate([z, s[:, :, 1]], axis=-1)
    return jnp.concatenate([top, bot], axis=-2)


def _pairs_to_state(s_bd):
    b = s_bd.shape[0]
    h0 = s_bd[:, :, :RWKV_N, :RWKV_N]
    h1 = s_bd[:, :, RWKV_N:, RWKV_N:]
    return jnp.stack([h0, h1], axis=2).reshape(b, RWKV_HEADS, RWKV_N, RWKV_N)


def _out_proj_body(x_ref, attn_ref, rwkv_ref, gate_ref, wpa_ref, wpb_ref, wout_ref, n2_ref,
                   wup_ref, wdn_ref, nf_ref, y_ref):
    pa = _dg(attn_ref[...], wpa_ref[...])
    pb = _dg(rwkv_ref[...], wpb_ref[...])
    merged = gate_ref[:, 0:D_MODEL] * pa + gate_ref[:, D_MODEL:GATE_W] * pb
    x1 = x_ref[...] + _dg(merged.astype(BF16), wout_ref[...])
    hm = _rms(x1, n2_ref[...]).astype(BF16)
    acc = x1
    for c in range(0, D_FF, 1024):
        up = jnp.maximum(_dg(hm, wup_ref[:, c:c + 1024]), 0.0)
        acc = acc + _dg((up * up).astype(BF16), wdn_ref[c:c + 1024, :])
    y_ref[...] = _rms(acc, nf_ref[...])


def _out_proj(x, attn, rwkv, gate, w, tm):
    m = x.shape[0]
    row = lambda wd: pl.BlockSpec((tm, wd), lambda i: (i, 0))
    full = lambda a: pl.BlockSpec(a.shape, lambda i: (0,) * a.ndim)
    consts = [w["w_proj_a"], w["w_proj_b"], w["w_out"], w["norm2_g"], w["w_up"], w["w_down"], w["normf_g"]]
    return pl.pallas_call(
        _out_proj_body,
        grid=(m // tm,),
        in_specs=[row(D_MODEL), row(GROUP_W), row(RWKV_W), row(GATE_W)] + [full(c) for c in consts],
        out_specs=row(D_MODEL),
        out_shape=jax.ShapeDtypeStruct((m, D_MODEL), F32),
        compiler_params=_params(("parallel",)),
        name="out_proj",
    )(x, attn, rwkv, gate, *consts)


def _kv_rows(qkv, group, rows):
    b, s, _ = qkv.shape
    k0 = ATTN_W + group * GROUP_W
    v0 = 2 * ATTN_W + group * GROUP_W
    k = qkv[:, s - rows:, k0:k0 + GROUP_W].reshape(b, rows, HG, HEAD_DIM)
    v = qkv[:, s - rows:, v0:v0 + GROUP_W].reshape(b, rows, HG, HEAD_DIM)
    return jnp.stack([k, v], axis=2)


def _layer_weights(l, norm1_g, w_in, b_gate, mu_shift, w0, w_lora_up, a0, a_lora_up, g_lora_up, k_k, k_a,
                   r_k, gn_g, gn_b, w_proj_a, w_proj_b, w_out, norm2_g, w_up, w_down, normf_g):
    row = lambda a: a.reshape(1, -1)
    zero = jnp.zeros_like(w_lora_up[l])
    wa_lora = jnp.concatenate([jnp.concatenate([w_lora_up[l], zero], axis=1),
                               jnp.concatenate([zero, a_lora_up[l]], axis=1)], axis=0)
    return dict(
        norm1_g=row(norm1_g[l]), w_in=w_in[l].astype(BF16), b_gate=row(b_gate[l]), mu_shift=row(mu_shift[l]),
        w0=row(w0[l]), a0=row(a0[l]), k_k=row(k_k[l]), k_a=row(k_a[l]), r_k=row(r_k[l]),
        gn_g=row(gn_g[l]), gn_b=row(gn_b[l]),
        wa_lora=wa_lora.astype(BF16), g_lora=g_lora_up[l].astype(BF16),
        w_proj_a=w_proj_a[l].astype(BF16), w_proj_b=w_proj_b[l].astype(BF16), w_out=w_out[l].astype(BF16),
        norm2_g=row(norm2_g[l]), w_up=w_up[l].astype(BF16), w_down=w_down[l].astype(BF16),
        normf_g=row(normf_g))


def _prompt_layer(x, w, tm, tc):
    b, s, _ = x.shape
    x2 = x.reshape(b * s, D_MODEL)
    qkv, zb, gate = _in_proj(x2, w["norm1_g"], w["w_in"], w["b_gate"], tm)
    qkv = qkv.reshape(b, s, QKV_W)
    zb = zb.reshape(b, s, SHIFT_W)
    attn = _attn_prompt(qkv)
    rwkv, s_bd = _rwkv(zb, jnp.zeros((b, 1, SHIFT_W), F32), jnp.zeros((b, N_PAIRS, PAIR_W, PAIR_W), F32), w, tc)
    y = _out_proj(x2, attn.reshape(b * s, GROUP_W), rwkv.reshape(b * s, RWKV_W), gate, w, tm)
    kvs = [_kv_rows(qkv, g, min(BAND * d, s)) for g, d in enumerate(DILATIONS)]
    return y.reshape(b, s, D_MODEL), kvs, _pairs_to_state(s_bd), zb[:, -1]


def _sample_layer(x, caches, s0, shift0, w):
    b, t_len, _ = x.shape
    assert t_len == 1
    x2 = x.reshape(b, D_MODEL)
    qkv, zb, gate = _in_proj(x2, w["norm1_g"], w["w_in"], w["b_gate"], b)
    attn = _attn_sample(qkv, caches, 8)
    zb_pad = jnp.pad(zb[:, None, :], ((0, 0), (0, CHUNK - 1), (0, 0)))
    rwkv, s_bd = _rwkv(zb_pad, shift0[:, None, :], _state_to_pairs(s0), w, CHUNK, t_valid=1)
    y = _out_proj(x2, attn, rwkv[:, 0], gate, w, b)
    kvs = [_kv_rows(qkv[:, None, :], g, 1) for g in range(len(DILATIONS))]
    return y.reshape(b, 1, D_MODEL), kvs, _pairs_to_state(s_bd), zb


def kernel(x_prompt, x_sample, cache_kv_w128, cache_kv_w512, cache_kv_w2048, state_wkv, state_shift, norm1_g, w_in, b_gate, mu_shift, w0, w_lora_up, a0, a_lora_up, g_lora_up, k_k, k_a, r_k, gn_g, gn_b, w_proj_a, w_proj_b, w_out, norm2_g, w_up, w_down, normf_g):
    depth = w_in.shape[0]
    assert depth == 1, "the final norm is fused into the layer's output stage"
    w = _layer_weights(0, norm1_g, w_in, b_gate, mu_shift, w0, w_lora_up, a0, a_lora_up, g_lora_up, k_k, k_a,
                       r_k, gn_g, gn_b, w_proj_a, w_proj_b, w_out, norm2_g, w_up, w_down, normf_g)
    y_p, kv_p, wkv_p, shift_p = _prompt_layer(x_prompt, w, 256, 512)
    y_s, kv_s, wkv_s, shift_s = _sample_layer(
        x_sample, (cache_kv_w128[0], cache_kv_w512[0], cache_kv_w2048[0]), state_wkv[0], state_shift[0], w)
    lead = lambda a: a[None]
    return (y_p, y_s, lead(kv_p[0]), lead(kv_p[1]), lead(kv_p[2]), lead(wkv_p), lead(shift_p),
            lead(kv_s[0]), lead(kv_s[1]), lead(kv_s[2]), lead(wkv_s), lead(shift_s))
```

```python
import functools

import numpy as np
import jax
import jax.numpy as jnp
from jax import lax
from jax.experimental import pallas as pl
from jax.experimental.pallas import tpu as pltpu

F32 = jnp.float32
BF16 = jnp.bfloat16

D_MODEL = 1024
HEAD_DIM = 64
HG = 4
DILATIONS = (1, 4, 16)
BAND = 128
N_ATTN_HEADS = HG * len(DILATIONS)
GROUP_W = HG * HEAD_DIM
LANE_HALVES = GROUP_W // 128
ATTN_W = N_ATTN_HEADS * HEAD_DIM
QKV_W = 3 * ATTN_W
RWKV_N = 64
RWKV_W = 512
RWKV_HEADS = RWKV_W // RWKV_N
PAIR_W = 2 * RWKV_N
N_PAIRS = RWKV_HEADS // 2
LORA_W = 256
SHIFT_W = 3 * RWKV_W + LORA_W
GATE_W = 2 * D_MODEL
IN_W = QKV_W + SHIFT_W + GATE_W
D_FF = 4 * D_MODEL
NORM_EPS = 1e-6
GN_EPS = 64e-5
NEG_INF = -1e30
ATTN_SCALE = HEAD_DIM ** -0.5
CHUNK = 64
RWKV_SEQS_PER_STEP = 2
DECAY_SCALE = float(np.exp(-0.5))
SLOPES = [float(s) for s in np.exp2(-8.0 * np.arange(1, N_ATTN_HEADS + 1, dtype=np.float32) / N_ATTN_HEADS)]

V7X_VMEM_LIMIT = 56 * 1024 * 1024

NN = (((1,), (0,)), ((), ()))
NT = (((1,), (1,)), ((), ()))
TN = (((0,), (0,)), ((), ()))


def _dg(a, b, dims=NN):
    return lax.dot_general(a, b, dims, preferred_element_type=F32)


def _split2(a):
    hi = a.astype(BF16)
    lo = (a - hi.astype(F32)).astype(BF16)
    return hi, lo


def _dot3(a, b, dims=NN):
    ah, al = _split2(a)
    bh, bl = _split2(b)
    return _dg(ah, bh, dims) + _dg(al, bh, dims) + _dg(ah, bl, dims)


def _dot_exact_rhs(a, b_bf16, passes):
    out = None
    rem = a
    for _ in range(passes):
        part = rem.astype(BF16)
        term = _dg(part, b_bf16)
        out = term if out is None else out + term
        rem = rem - part.astype(F32)
    return out


def _sigmoid(x):
    return 1.0 / (1.0 + jnp.exp(-x))


def _rms(x, g):
    return x * lax.rsqrt(jnp.mean(x * x, axis=-1, keepdims=True) + NORM_EPS) * g


def _params(sem):
    return pltpu.CompilerParams(dimension_semantics=sem, vmem_limit_bytes=V7X_VMEM_LIMIT)


def _kv_tail_plan(seq_len, tm):
    plan = []
    for dil in DILATIONS:
        rows = min(BAND * dil, seq_len)
        width = min(rows, tm)
        plan.append((rows, width, (seq_len - rows) // tm))
    return plan


def _in_proj_body(x_ref, g_ref, w_ref, bg_ref, qkv_ref, zb_ref, gate_ref, *kv_refs, seq_len):
    tm = x_ref.shape[0]
    h = _rms(x_ref[...], g_ref[...]).astype(BF16)
    for c in range(0, QKV_W, 768):
        qkv_ref[:, c:c + 768] = _dg(h, w_ref[:, c:c + 768])
    for c in range(0, SHIFT_W, 896):
        zb_ref[:, c:c + 896] = _dg(h, w_ref[:, QKV_W + c:QKV_W + c + 896])
    for c in range(0, GATE_W, 1024):
        zg = _dg(h, w_ref[:, QKV_W + SHIFT_W + c:QKV_W + SHIFT_W + c + 1024])
        gate_ref[:, c:c + 1024] = _sigmoid(zg + bg_ref[:, c:c + 1024])
    if kv_refs:
        tile = pl.program_id(0) % (seq_len // tm)
        for g, (kv_ref, (_, width, first)) in enumerate(zip(kv_refs, _kv_tail_plan(seq_len, tm))):
            @pl.when(tile >= first)
            def _(g=g, kv_ref=kv_ref, width=width):
                for sec in (1, 2):
                    c0 = sec * ATTN_W + g * GROUP_W
                    kv_ref[sec - 1] = qkv_ref[tm - width:tm, c0:c0 + GROUP_W].T


def _in_proj(x, norm_g, w_in_bf16, b_gate, tm, seq_len=None):
    m = x.shape[0]
    row = lambda w: pl.BlockSpec((tm, w), lambda i: (i, 0))
    full = lambda a: pl.BlockSpec(a.shape, lambda i: (0,) * a.ndim)
    out_specs = [row(QKV_W), row(SHIFT_W), row(GATE_W)]
    out_shape = [jax.ShapeDtypeStruct((m, QKV_W), F32),
                 jax.ShapeDtypeStruct((m, SHIFT_W), F32),
                 jax.ShapeDtypeStruct((m, GATE_W), F32)]
    if seq_len is not None:
        tiles = seq_len // tm
        for rows, width, first in _kv_tail_plan(seq_len, tm):
            out_specs.append(pl.BlockSpec(
                (None, 2, GROUP_W, width),
                lambda i, first=first: (i // tiles, 0, 0, jnp.maximum(i % tiles - first, 0))))
            out_shape.append(jax.ShapeDtypeStruct((m // seq_len, 2, GROUP_W, rows), F32))
    return pl.pallas_call(
        functools.partial(_in_proj_body, seq_len=seq_len),
        grid=(m // tm,),
        in_specs=[row(D_MODEL), full(norm_g), full(w_in_bf16), full(b_gate)],
        out_specs=out_specs,
        out_shape=out_shape,
        compiler_params=_params(("arbitrary",)),
        name="in_proj",
    )(x, norm_g, w_in_bf16, b_gate)


def _residue_rows(r, count, dil):
    return pl.ds(r, count) if dil == 1 else pl.ds(r, count, stride=dil)


def _attn_group_blocks(q_ref, k_ref, v_ref, o_scr, lse_scr, qs, ks, vs, os_, ls, group):
    s_len = qs.shape[0]
    dil = DILATIONS[group]
    l_res = s_len // dil
    nb = l_res // BAND
    for r in range(dil):
        dst = pl.ds(r * l_res, l_res)
        src = _residue_rows(r, l_res, dil)
        for staged, halves in ((qs, q_ref), (ks, k_ref), (vs, v_ref)):
            staged[dst, :] = jnp.concatenate([h[src, :] for h in halves], axis=1).astype(BF16)
    has_prev = nb > 1
    nk = 2 * BAND if has_prev else BAND
    qi = lax.broadcasted_iota(jnp.int32, (BAND, nk), 0)
    kj = lax.broadcasted_iota(jnp.int32, (BAND, nk), 1)
    delta = (nk - BAND) + qi - kj
    band = (delta >= 0) & (delta <= BAND)
    dist = (delta * dil).astype(F32)
    head = lax.broadcasted_iota(jnp.int32, (1, GROUP_W), 1) // HEAD_DIM

    def block(idx, carry):
        cur = pl.ds(pl.multiple_of(idx * BAND, BAND), BAND)
        q = qs[cur, :]
        k2 = ks[cur, :]
        v2 = vs[cur, :]
        valid = band
        if has_prev:
            n = idx % nb
            prev = pl.ds(pl.multiple_of(jnp.maximum(idx - 1, 0) * BAND, BAND), BAND)
            k2 = jnp.concatenate([ks[prev, :], k2], axis=0)
            v2 = jnp.concatenate([vs[prev, :], v2], axis=0)
            valid = band & (kj >= jnp.where(n > 0, 0, BAND))
        heads = range(HG)
        hm = [head == h for h in heads]
        s = [_dg(jnp.where(hm[h], q, jnp.zeros_like(q)), k2, NT) for h in heads]
        s = [jnp.where(valid, s[h] * ATTN_SCALE - SLOPES[group * HG + h] * dist, NEG_INF) for h in heads]
        m = [jnp.max(s[h], axis=-1, keepdims=True) for h in heads]
        p = [jnp.exp(s[h] - m[h]) for h in heads]
        l = [jnp.sum(p[h], axis=-1, keepdims=True) for h in heads]
        pv = [_dg(p[h].astype(BF16), v2) for h in heads]
        o_acc = jnp.zeros((BAND, GROUP_W), F32)
        lse_acc = jnp.zeros((BAND, GROUP_W), F32)
        for h in heads:
            o_acc = jnp.where(hm[h], pv[h] / l[h], o_acc)
            lse_acc = jnp.where(hm[h], m[h] + jnp.log(l[h]), lse_acc)
        os_[cur, :] = o_acc
        ls[cur, :] = lse_acc
        return carry

    lax.fori_loop(0, dil * nb, block, 0)
    for r in range(dil):
        src = pl.ds(r * l_res, l_res)
        dst = _residue_rows(r, l_res, dil)
        for half in range(LANE_HALVES):
            cols = slice(half * 128, (half + 1) * 128)
            o_scr[group * LANE_HALVES + half, dst, :] = os_[src, cols]
            lse_scr[group * LANE_HALVES + half, dst, :] = ls[src, cols]


def _attn_prompt_body(q0_ref, q1_ref, k0_ref, k1_ref, v0_ref, v1_ref, o_ref, o_scr, lse_scr, qs, ks, vs, os_, ls):
    q_ref, k_ref, v_ref = (q0_ref, q1_ref), (k0_ref, k1_ref), (v0_ref, v1_ref)
    gid = pl.program_id(1)
    n_groups = len(DILATIONS)
    for g in range(n_groups):
        @pl.when(gid == g)
        def _(g=g):
            _attn_group_blocks(q_ref, k_ref, v_ref, o_scr, lse_scr, qs, ks, vs, os_, ls, g)

    @pl.when(gid == n_groups - 1)
    def _():
        def merge(i, carry):
            rows = pl.ds(pl.multiple_of(i * BAND, BAND), BAND)
            both = lambda ref, g: jnp.concatenate(
                [ref[g * LANE_HALVES + half, rows, :] for half in range(LANE_HALVES)], axis=1)
            lse = [both(lse_scr, g) for g in range(n_groups)]
            m = functools.reduce(jnp.maximum, lse)
            wts = [jnp.exp(x - m) for x in lse]
            num = sum(wts[g] * both(o_scr, g) for g in range(n_groups))
            o_ref[rows, :] = (num / sum(wts)).astype(o_ref.dtype)
            return carry

        lax.fori_loop(0, o_ref.shape[0] // BAND, merge, 0)


def _attn_prompt(qkv):
    b, s, _ = qkv.shape
    n_groups = len(DILATIONS)
    assert s % (DILATIONS[-1] * BAND) == 0
    sec = ATTN_W // 128

    def col(section, half):
        return pl.BlockSpec((None, s, 128), lambda bi, g: (bi, 0, section * sec + g * LANE_HALVES + half))

    return pl.pallas_call(
        _attn_prompt_body,
        grid=(b, n_groups),
        in_specs=[col(section, half) for section in range(3) for half in range(LANE_HALVES)],
        out_specs=pl.BlockSpec((None, s, GROUP_W), lambda bi, g: (bi, 0, 0)),
        out_shape=jax.ShapeDtypeStruct((b, s, GROUP_W), BF16),
        scratch_shapes=([pltpu.VMEM((n_groups * LANE_HALVES, s, 128), F32)] * 2
                        + [pltpu.VMEM((s, GROUP_W), BF16)] * 3 + [pltpu.VMEM((s, GROUP_W), F32)] * 2),
        compiler_params=_params(("parallel", "arbitrary")),
        name="attn_prompt",
    )(*([qkv] * (3 * LANE_HALVES)))


def _attn_sample_body(qkv_ref, c0_ref, c1_ref, c2_ref, o_ref):
    rows = 8
    row = lax.broadcasted_iota(jnp.int32, (rows, GROUP_W), 0)
    own = lax.broadcasted_iota(jnp.int32, (rows, GROUP_W), 1) // HEAD_DIM == row
    hrow = lax.broadcasted_iota(jnp.int32, (rows, 1), 0)
    accs, ms, ls = [], [], []
    for g, c_ref in enumerate((c0_ref, c1_ref, c2_ref)):
        dil = DILATIONS[g]
        lb = c_ref.shape[2]
        col = lambda sec: qkv_ref[:, sec * ATTN_W + g * GROUP_W:sec * ATTN_W + (g + 1) * GROUP_W]
        q, kn, vn = col(0), col(1), col(2)
        qb = jnp.where(own, q, 0.0)
        slope = jnp.zeros((rows, 1), F32)
        for h in range(HG):
            slope = jnp.where(hrow == h, SLOPES[g * HG + h], slope)
        t = lax.broadcasted_iota(jnp.int32, (rows, lb), 1)
        s_b = _dg(qb.astype(BF16), c_ref[0].astype(BF16)) * ATTN_SCALE - slope * (lb - t).astype(F32)
        s_b = jnp.where(t % dil == 0, s_b, NEG_INF)
        s_n = jnp.sum(qb * kn, axis=-1, keepdims=True) * ATTN_SCALE
        m = jnp.maximum(jnp.max(s_b, axis=-1, keepdims=True), s_n)
        p_b = jnp.exp(s_b - m)
        p_n = jnp.exp(s_n - m)
        ls.append(jnp.sum(p_b, axis=-1, keepdims=True) + p_n)
        pv = _dg(p_b.astype(BF16), c_ref[1].astype(BF16), NT) + p_n * vn
        accs.append(jnp.where(own, pv, 0.0))
        ms.append(m)
    m_all = functools.reduce(jnp.maximum, ms)
    sc = [jnp.exp(m - m_all) for m in ms]
    den = sum(s * l for s, l in zip(sc, ls))
    out = sum((s / den) * a for s, a in zip(sc, accs))
    o_ref[...] = jnp.sum(out, axis=0, keepdims=True).astype(o_ref.dtype)


def _attn_sample(qkv, caches):
    b = qkv.shape[0]
    c_args, c_specs = [], []
    for g, c in enumerate(caches):
        lb = c.shape[1]
        assert lb == BAND * DILATIONS[g]
        c_args.append(jnp.transpose(c, (0, 2, 3, 4, 1)).reshape(b, 2, GROUP_W, lb))
        c_specs.append(pl.BlockSpec((None, 2, GROUP_W, lb), lambda i: (i, 0, 0, 0)))
    out = pl.pallas_call(
        _attn_sample_body,
        grid=(b,),
        in_specs=[pl.BlockSpec((None, 1, QKV_W), lambda i: (i, 0, 0))] + c_specs,
        out_specs=pl.BlockSpec((None, 1, GROUP_W), lambda i: (i, 0, 0)),
        out_shape=jax.ShapeDtypeStruct((b, 1, GROUP_W), BF16),
        compiler_params=_params(("parallel",)),
        name="attn_sample",
    )(qkv.reshape(b, 1, QKV_W), *c_args)
    return out.reshape(b, GROUP_W)


WKV_PASSES = dict(g=1, s0=1, x=1, inv=1, y=1, s1=3)


def _mm(a, b, dims, passes):
    if passes == 1:
        return _dg(a.astype(BF16), b.astype(BF16), dims)
    ah, al = _split2(a)
    if passes == 2:
        bh = b.astype(BF16)
        return _dg(ah, bh, dims) + _dg(al, bh, dims)
    bh, bl = _split2(b)
    return _dg(ah, bh, dims) + _dg(al, bh, dims) + _dg(ah, bl, dims)


def _wkv_levels(live_rows):
    return int(np.ceil(np.log2(live_rows))) if live_rows > 1 else 0


def _wkv_chunk(operands, states, levels):
    c = CHUNK
    n2 = 2 * c
    n_seq = len(operands)
    ti = lax.broadcasted_iota(jnp.int32, (c, c), 0)
    tj = lax.broadcasted_iota(jnp.int32, (c, c), 1)
    tri = (ti >= tj).astype(BF16)
    first = lax.broadcasted_iota(jnp.int32, (1, PAIR_W), 1) < RWKV_N

    def stack(x, p):
        x = x[:, p * PAIR_W:(p + 1) * PAIR_W]
        return jnp.concatenate([jnp.where(first, x, 0.0), jnp.where(first, 0.0, x)], axis=0)

    ar, bk, v_s, s0, w_end = [], [], [], [], []
    for (r, kp, v, logw, av, bv), seq_states in zip(operands, states):
        lc = _dot_exact_rhs_left(tri, logw)
        e_in = jnp.exp(lc)
        e_neg = jnp.exp(-lc)
        a_t = av * jnp.exp(lc - logw)
        r_t = r * e_in
        b_t = bv * e_neg
        k_t = kp * e_neg
        for p in range(N_PAIRS):
            ar.append(jnp.concatenate([stack(a_t, p), stack(r_t, p)], axis=0))
            bk.append(jnp.concatenate([stack(b_t, p), stack(k_t, p)], axis=0))
            v_s.append(stack(v, p))
            s0.append(seq_states[p])
            w_end.append(e_in[c - 1:c, p * PAIR_W:(p + 1) * PAIR_W])
    chains = range(len(ar))
    g = [_mm(ar[i], bk[i], NT, WKV_PASSES["g"]) for i in chains]
    ar_s0 = [_mm(ar[i], s0[i], NT, WKV_PASSES["s0"]) for i in chains]
    ri = lax.broadcasted_iota(jnp.int32, (n2, n2), 0)
    ci = lax.broadcasted_iota(jnp.int32, (n2, n2), 1)
    strict = ri > ci
    incl = ri >= ci
    n_ab = [jnp.where(strict, g[i][0:n2, 0:n2], 0.0) for i in chains]
    n_ak = [jnp.where(strict, g[i][0:n2, n2:2 * n2], 0.0) for i in chains]
    m_r = [jnp.where(jnp.concatenate([incl, incl], axis=1), g[i][n2:2 * n2, :], 0.0) for i in chains]
    z = [ar_s0[i][0:n2] + _mm(n_ak[i], v_s[i], NN, WKV_PASSES["x"]) for i in chains]
    pw = n_ab
    for lvl in range(levels):
        if lvl < levels - 1:
            pz = [_mm(pw[i], jnp.concatenate([pw[i], z[i]], axis=1), NN, WKV_PASSES["inv"]) for i in chains]
            pw = [pz[i][:, 0:n2] for i in chains]
            z = [z[i] + pz[i][:, n2:2 * n2] for i in chains]
        else:
            z = [z[i] + _mm(pw[i], z[i], NN, WKV_PASSES["inv"]) for i in chains]
    uv = [jnp.concatenate([z[i], v_s[i]], axis=0) for i in chains]
    y_s = [ar_s0[i][n2:2 * n2] + _mm(m_r[i], uv[i], NN, WKV_PASSES["y"]) for i in chains]
    s1 = [(s0[i] + _mm(uv[i], bk[i], TN, WKV_PASSES["s1"])) * w_end[i] for i in chains]
    ys = [jnp.concatenate([y_s[j * N_PAIRS + p][0:c] + y_s[j * N_PAIRS + p][c:n2] for p in range(N_PAIRS)], axis=1)
          for j in range(n_seq)]
    new_states = tuple(tuple(s1[j * N_PAIRS:(j + 1) * N_PAIRS]) for j in range(n_seq))
    return ys, new_states


def _dot_exact_rhs_left(lhs_bf16, b):
    out = None
    rem = b
    for _ in range(3):
        part = rem.astype(BF16)
        term = _dg(lhs_bf16, part)
        out = term if out is None else out + term
        rem = rem - part.astype(F32)
    return out


def _rwkv_body(zb_ref, sh_ref, mu_ref, s0_ref,
               w0_ref, a0_ref, kk_ref, ka_ref, rk_ref, gg_ref, gb_ref, wwa_ref, wg_ref,
               out_ref, s_out_ref,
               s_scr, carry_scr, r_scr, kp_scr, v_scr, lw_scr, av_scr, bv_scr, y_scr, g_scr,
               *, t_valid):
    t = pl.program_id(1)
    n_seq, tc, _ = zb_ref.shape
    rows_all = n_seq * tc

    @pl.when(t == 0)
    def _():
        s_scr[...] = s0_ref[...]
        carry_scr[...] = sh_ref[...]

    row = lax.broadcasted_iota(jnp.int32, (rows_all, 1), 0)
    z = zb_ref[...].reshape(rows_all, SHIFT_W)
    zp = pltpu.roll(z, 1, 0)
    for j in range(n_seq):
        zp = jnp.where(row == j * tc, carry_scr[j], zp)
        carry_scr[j] = z[(j + 1) * tc - 1:(j + 1) * tc, :]
    zm = z + (zp - z) * mu_ref[...]
    r = zm[:, 0:RWKV_W]
    k = zm[:, RWKV_W:2 * RWKV_W]
    v = zm[:, 2 * RWKV_W:3 * RWKV_W]
    z_wa = zm[:, 3 * RWKV_W:3 * RWKV_W + 128]
    z_g = zm[:, 3 * RWKV_W + 128:SHIFT_W]
    half = lax.broadcasted_iota(jnp.int32, (1, 128), 1) < 64
    x_wa = jnp.where(half, jnp.tanh(z_wa), z_wa).astype(BF16)
    x_g = _sigmoid(z_g).astype(BF16)
    u = w0_ref[...] + _dg(x_wa, wwa_ref[:, 0:RWKV_W])
    logw = -DECAY_SCALE * _sigmoid(u)
    a = _sigmoid(a0_ref[...] + _dg(x_wa, wwa_ref[:, RWKV_W:2 * RWKV_W]))
    g_scr[...] = _dg(x_g, wg_ref[...])

    lr = lax.broadcasted_iota(jnp.int32, (PAIR_W, PAIR_W), 0)
    lc = lax.broadcasted_iota(jnp.int32, (PAIR_W, PAIR_W), 1)
    seg = (lr // RWKV_N == lc // RWKV_N).astype(BF16)

    def head_sum(x):
        return jnp.concatenate(
            [_dg(x[:, p * PAIR_W:(p + 1) * PAIR_W].astype(BF16), seg) for p in range(N_PAIRS)], axis=1)

    kk = k * kk_ref[...]
    kk = kk * lax.rsqrt(jnp.maximum(head_sum(kk * kk), 1e-24))
    kp = k * (1.0 + (a - 1.0) * ka_ref[...])
    av = -kk
    bv = kk * a
    if t_valid is not None:
        live = row % tc < t_valid
        zero = lambda x_: jnp.where(live, x_, 0.0)
        r, kp, v, logw, av, bv = zero(r), zero(kp), zero(v), zero(logw), zero(av), zero(bv)
    r_scr[...] = r
    kp_scr[...] = kp
    v_scr[...] = v
    lw_scr[...] = logw
    av_scr[...] = av
    bv_scr[...] = bv
    levels = _wkv_levels(CHUNK if t_valid is None else min(t_valid, CHUNK))
    vectors = (r_scr, kp_scr, v_scr, lw_scr, av_scr, bv_scr)

    def chunk(ci, states):
        rows = [pl.ds(pl.multiple_of(j * tc + ci * CHUNK, CHUNK), CHUNK) for j in range(n_seq)]
        ys, states = _wkv_chunk([tuple(ref[rows[j], :] for ref in vectors) for j in range(n_seq)], states, levels)
        for j in range(n_seq):
            y_scr[rows[j], :] = ys[j]
        return states

    states = lax.fori_loop(0, tc // CHUNK, chunk,
                           tuple(tuple(s_scr[j, p] for p in range(N_PAIRS)) for j in range(n_seq)))
    for j in range(n_seq):
        for p in range(N_PAIRS):
            s_scr[j, p] = states[j][p]
            s_out_ref[j, p] = states[j][p]

    y = y_scr[...]
    inv_n = 1.0 / RWKV_N
    mean = head_sum(y) * inv_n
    d = y - mean
    var = head_sum(d * d) * inv_n
    yn = d * lax.rsqrt(var + GN_EPS) * gg_ref[...] + gb_ref[...]
    bonus = head_sum(r_scr[...] * kp_scr[...] * rk_ref[...]) * v_scr[...]
    out_ref[...] = ((yn + bonus) * g_scr[...]).astype(out_ref.dtype).reshape(n_seq, tc, RWKV_W)


def _rwkv(zb, shift0, s0_bd, w, tc, n_seq, t_valid=None):
    b, t_len, _ = zb.shape
    full = lambda a: pl.BlockSpec(a.shape, lambda bi, t: (0,) * a.ndim)
    state = pl.BlockSpec((n_seq, N_PAIRS, PAIR_W, PAIR_W), lambda bi, t: (bi, 0, 0, 0))
    consts = [w["mu_shift"]]
    vecs = [w["w0"], w["a0"], w["k_k"], w["k_a"], w["r_k"], w["gn_g"], w["gn_b"], w["wa_lora"], w["g_lora"]]
    in_specs = ([pl.BlockSpec((n_seq, tc, SHIFT_W), lambda bi, t: (bi, t, 0)),
                 pl.BlockSpec((n_seq, 1, SHIFT_W), lambda bi, t: (bi, 0, 0))]
                + [full(c) for c in consts] + [state] + [full(c) for c in vecs])
    vm = lambda shape: pltpu.VMEM(shape, F32)
    scratch = [vm((n_seq, N_PAIRS, PAIR_W, PAIR_W)), vm((n_seq, 1, SHIFT_W))] + [vm((n_seq * tc, RWKV_W))] * 8
    return pl.pallas_call(
        functools.partial(_rwkv_body, t_valid=t_valid),
        grid=(b // n_seq, t_len // tc),
        in_specs=in_specs,
        out_specs=[pl.BlockSpec((n_seq, tc, RWKV_W), lambda bi, t: (bi, t, 0)), state],
        out_shape=[jax.ShapeDtypeStruct((b, t_len, RWKV_W), BF16),
                   jax.ShapeDtypeStruct((b, N_PAIRS, PAIR_W, PAIR_W), F32)],
        scratch_shapes=scratch,
        compiler_params=_params(("parallel", "arbitrary")),
        name="rwkv",
    )(zb, shift0, *consts, s0_bd, *vecs)


def _state_to_pairs(s):
    b = s.shape[0]
    s = s.reshape(b, N_PAIRS, 2, RWKV_N, RWKV_N)
    z = jnp.zeros_like(s[:, :, 0])
    top = jnp.concatenate([s[:, :, 0], z], axis=-1)
    bot = jnp.concatenate([z, s[:, :, 1]], axis=-1)
    return jnp.concatenate([top, bot], axis=-2)


def _pairs_to_state(s_bd):
    b = s_bd.shape[0]
    h0 = s_bd[:, :, :RWKV_N, :RWKV_N]
    h1 = s_bd[:, :, RWKV_N:, RWKV_N:]
    return jnp.stack([h0, h1], axis=2).reshape(b, RWKV_HEADS, RWKV_N, RWKV_N)


def _out_proj_body(x_ref, attn_ref, rwkv_ref, gate_ref, wpa_ref, wpb_ref, wout_ref, n2_ref,
                   wup_ref, wdn_ref, nf_ref, y_ref):
    pa = _dg(attn_ref[...], wpa_ref[...])
    pb = _dg(rwkv_ref[...], wpb_ref[...])
    merged = gate_ref[:, 0:D_MODEL] * pa + gate_ref[:, D_MODEL:GATE_W] * pb
    x1 = x_ref[...] + _dg(merged.astype(BF16), wout_ref[...])
    hm = _rms(x1, n2_ref[...]).astype(BF16)
    acc = x1
    for c in range(0, D_FF, 1024):
        up = jnp.maximum(_dg(hm, wup_ref[:, c:c + 1024]), 0.0)
        acc = acc + _dg((up * up).astype(BF16), wdn_ref[c:c + 1024, :])
    y_ref[...] = _rms(acc, nf_ref[...])


def _out_proj(x, attn, rwkv, gate, w, tm):
    m = x.shape[0]
    row = lambda wd: pl.BlockSpec((tm, wd), lambda i: (i, 0))
    full = lambda a: pl.BlockSpec(a.shape, lambda i: (0,) * a.ndim)
    consts = [w["w_proj_a"], w["w_proj_b"], w["w_out"], w["norm2_g"], w["w_up"], w["w_down"], w["normf_g"]]
    return pl.pallas_call(
        _out_proj_body,
        grid=(m // tm,),
        in_specs=[row(D_MODEL), row(GROUP_W), row(RWKV_W), row(GATE_W)] + [full(c) for c in consts],
        out_specs=row(D_MODEL),
        out_shape=jax.ShapeDtypeStruct((m, D_MODEL), F32),
        compiler_params=_params(("parallel",)),
        name="out_proj",
    )(x, attn, rwkv, gate, *consts)


def _kv_rows(qkv, group, rows):
    b, s, _ = qkv.shape
    k0 = ATTN_W + group * GROUP_W
    v0 = 2 * ATTN_W + group * GROUP_W
    k = qkv[:, s - rows:, k0:k0 + GROUP_W].reshape(b, rows, HG, HEAD_DIM)
    v = qkv[:, s - rows:, v0:v0 + GROUP_W].reshape(b, rows, HG, HEAD_DIM)
    return jnp.stack([k, v], axis=2)


def _layer_weights(l, norm1_g, w_in, b_gate, mu_shift, w0, w_lora_up, a0, a_lora_up, g_lora_up, k_k, k_a,
                   r_k, gn_g, gn_b, w_proj_a, w_proj_b, w_out, norm2_g, w_up, w_down, normf_g):
    row = lambda a: a.reshape(1, -1)
    zero = jnp.zeros_like(w_lora_up[l])
    wa_lora = jnp.concatenate([jnp.concatenate([w_lora_up[l], zero], axis=1),
                               jnp.concatenate([zero, a_lora_up[l]], axis=1)], axis=0)
    return dict(
        norm1_g=row(norm1_g[l]), w_in=w_in[l].astype(BF16), b_gate=row(b_gate[l]), mu_shift=row(mu_shift[l]),
        w0=row(w0[l]), a0=row(a0[l]), k_k=row(k_k[l]), k_a=row(k_a[l]), r_k=row(r_k[l]),
        gn_g=row(gn_g[l]), gn_b=row(gn_b[l]),
        wa_lora=wa_lora.astype(BF16), g_lora=g_lora_up[l].astype(BF16),
        w_proj_a=w_proj_a[l].astype(BF16), w_proj_b=w_proj_b[l].astype(BF16), w_out=w_out[l].astype(BF16),
        norm2_g=row(norm2_g[l]), w_up=w_up[l].astype(BF16), w_down=w_down[l].astype(BF16),
        normf_g=row(normf_g))


def _prompt_layer(x, w, tm, tc):
    b, s, _ = x.shape
    x2 = x.reshape(b * s, D_MODEL)
    qkv, zb, gate, *kv_t = _in_proj(x2, w["norm1_g"], w["w_in"], w["b_gate"], tm, seq_len=s)
    qkv = qkv.reshape(b, s, QKV_W)
    zb = zb.reshape(b, s, SHIFT_W)
    attn = _attn_prompt(qkv)
    rwkv, s_bd = _rwkv(zb, jnp.zeros((b, 1, SHIFT_W), F32), jnp.zeros((b, N_PAIRS, PAIR_W, PAIR_W), F32), w, tc,
                       RWKV_SEQS_PER_STEP)
    y = _out_proj(x2, attn.reshape(b * s, GROUP_W), rwkv.reshape(b * s, RWKV_W), gate, w, tm)
    kvs = [jnp.transpose(t.reshape(b, 2, HG, HEAD_DIM, t.shape[-1]), (0, 4, 1, 2, 3)) for t in kv_t]
    return y.reshape(b, s, D_MODEL), kvs, _pairs_to_state(s_bd), zb[:, -1]


def _sample_layer(x, caches, s0, shift0, w):
    b, t_len, _ = x.shape
    assert t_len == 1
    x2 = x.reshape(b, D_MODEL)
    qkv, zb, gate = _in_proj(x2, w["norm1_g"], w["w_in"], w["b_gate"], b)
    attn = _attn_sample(qkv, caches)
    zb_pad = jnp.pad(zb[:, None, :], ((0, 0), (0, CHUNK - 1), (0, 0)))
    rwkv, s_bd = _rwkv(zb_pad, shift0[:, None, :], _state_to_pairs(s0), w, CHUNK, RWKV_SEQS_PER_STEP, t_valid=1)
    y = _out_proj(x2, attn, rwkv[:, 0], gate, w, b)
    kvs = [_kv_rows(qkv[:, None, :], g, 1) for g in range(len(DILATIONS))]
    return y.reshape(b, 1, D_MODEL), kvs, _pairs_to_state(s_bd), zb


def kernel(x_prompt, x_sample, cache_kv_w128, cache_kv_w512, cache_kv_w2048, state_wkv, state_shift, norm1_g, w_in, b_gate, mu_shift, w0, w_lora_up, a0, a_lora_up, g_lora_up, k_k, k_a, r_k, gn_g, gn_b, w_proj_a, w_proj_b, w_out, norm2_g, w_up, w_down, normf_g):
    depth = w_in.shape[0]
    assert depth == 1, "the final norm is fused into the layer's output stage"
    w = _layer_weights(0, norm1_g, w_in, b_gate, mu_shift, w0, w_lora_up, a0, a_lora_up, g_lora_up, k_k, k_a,
                       r_k, gn_g, gn_b, w_proj_a, w_proj_b, w_out, norm2_g, w_up, w_down, normf_g)
    y_p, kv_p, wkv_p, shift_p = _prompt_layer(x_prompt, w, 256, 512)
    y_s, kv_s, wkv_s, shift_s = _sample_layer(
        x_sample, (cache_kv_w128[0], cache_kv_w512[0], cache_kv_w2048[0]), state_wkv[0], state_shift[0], w)
    lead = lambda a: a[None]
    return (y_p, y_s, lead(kv_p[0]), lead(kv_p[1]), lead(kv_p[2]), lead(wkv_p), lead(shift_p),
            lead(kv_s[0]), lead(kv_s[1]), lead(kv_s[2]), lead(wkv_s), lead(shift_s))
```

```python
import functools

import numpy as np
import jax
import jax.numpy as jnp
from jax import lax
from jax.experimental import pallas as pl
from jax.experimental.pallas import tpu as pltpu

F32 = jnp.float32
BF16 = jnp.bfloat16

D_MODEL = 1024
HEAD_DIM = 64
HG = 4
DILATIONS = (1, 4, 16)
BAND = 128
N_ATTN_HEADS = HG * len(DILATIONS)
GROUP_W = HG * HEAD_DIM
LANE_HALVES = GROUP_W // 128
ATTN_W = N_ATTN_HEADS * HEAD_DIM
QKV_W = 3 * ATTN_W
RWKV_N = 64
RWKV_W = 512
RWKV_HEADS = RWKV_W // RWKV_N
PAIR_W = 2 * RWKV_N
N_PAIRS = RWKV_HEADS // 2
LORA_W = 256
SHIFT_W = 3 * RWKV_W + LORA_W
GATE_W = 2 * D_MODEL
IN_W = QKV_W + SHIFT_W + GATE_W
D_FF = 4 * D_MODEL
NORM_EPS = 1e-6
GN_EPS = 64e-5
NEG_INF = -1e30
ATTN_SCALE = HEAD_DIM ** -0.5
CHUNK = 64
RWKV_SEQS_PER_STEP = 2
DECAY_SCALE = float(np.exp(-0.5))
SLOPES = [float(s) for s in np.exp2(-8.0 * np.arange(1, N_ATTN_HEADS + 1, dtype=np.float32) / N_ATTN_HEADS)]

V7X_VMEM_LIMIT = 56 * 1024 * 1024

NN = (((1,), (0,)), ((), ()))
NT = (((1,), (1,)), ((), ()))
TN = (((0,), (0,)), ((), ()))


def _dg(a, b, dims=NN):
    return lax.dot_general(a, b, dims, preferred_element_type=F32)


def _split2(a):
    hi = a.astype(BF16)
    lo = (a - hi.astype(F32)).astype(BF16)
    return hi, lo


def _dot3(a, b, dims=NN):
    ah, al = _split2(a)
    bh, bl = _split2(b)
    return _dg(ah, bh, dims) + _dg(al, bh, dims) + _dg(ah, bl, dims)


def _dot_exact_rhs(a, b_bf16, passes):
    out = None
    rem = a
    for _ in range(passes):
        part = rem.astype(BF16)
        term = _dg(part, b_bf16)
        out = term if out is None else out + term
        rem = rem - part.astype(F32)
    return out


def _sigmoid(x):
    return 0.5 * jnp.tanh(0.5 * x) + 0.5


def _rms(x, g):
    return x * lax.rsqrt(jnp.mean(x * x, axis=-1, keepdims=True) + NORM_EPS) * g


def _params(sem):
    return pltpu.CompilerParams(dimension_semantics=sem, vmem_limit_bytes=V7X_VMEM_LIMIT)


def _kv_tail_plan(seq_len, tm):
    plan = []
    for dil in DILATIONS:
        rows = min(BAND * dil, seq_len)
        width = min(rows, tm)
        plan.append((rows, width, (seq_len - rows) // tm))
    return plan


def _in_proj_body(x_ref, g_ref, w_ref, bg_ref, qkv_ref, zb_ref, gate_ref, *kv_refs, seq_len):
    tm = x_ref.shape[0]
    h = _rms(x_ref[...], g_ref[...]).astype(BF16)
    for c in range(0, GATE_W, 1024):
        zg = _dg(h, w_ref[:, QKV_W + SHIFT_W + c:QKV_W + SHIFT_W + c + 1024])
        gate_ref[:, c:c + 1024] = _sigmoid(zg + bg_ref[:, c:c + 1024])
    for c in range(0, QKV_W, 768):
        qkv_ref[:, c:c + 768] = _dg(h, w_ref[:, c:c + 768])

    def write_tail(g, kv_ref, width):
        for sec in (1, 2):
            c0 = sec * ATTN_W + g * GROUP_W
            kv_ref[sec - 1] = qkv_ref[tm - width:tm, c0:c0 + GROUP_W].T

    plan = _kv_tail_plan(seq_len, tm) if kv_refs else []
    for g, (kv_ref, (_, width, first)) in enumerate(zip(kv_refs, plan)):
        if first == 0:
            write_tail(g, kv_ref, width)
    for c in range(0, SHIFT_W, 896):
        zb_ref[:, c:c + 896] = _dg(h, w_ref[:, QKV_W + c:QKV_W + c + 896])
    for g, (kv_ref, (_, width, first)) in enumerate(zip(kv_refs, plan)):
        if first > 0:
            tile = pl.program_id(0) % (seq_len // tm)
            pl.when(tile >= first)(functools.partial(write_tail, g, kv_ref, width))


def _in_proj(x, norm_g, w_in_bf16, b_gate, tm, seq_len=None):
    m = x.shape[0]
    row = lambda w: pl.BlockSpec((tm, w), lambda i: (i, 0))
    full = lambda a: pl.BlockSpec(a.shape, lambda i: (0,) * a.ndim)
    out_specs = [row(QKV_W), row(SHIFT_W), row(GATE_W)]
    out_shape = [jax.ShapeDtypeStruct((m, QKV_W), F32),
                 jax.ShapeDtypeStruct((m, SHIFT_W), F32),
                 jax.ShapeDtypeStruct((m, GATE_W), F32)]
    if seq_len is not None:
        tiles = seq_len // tm
        for rows, width, first in _kv_tail_plan(seq_len, tm):
            out_specs.append(pl.BlockSpec(
                (None, 2, GROUP_W, width),
                lambda i, first=first: (i // tiles, 0, 0, jnp.maximum(i % tiles - first, 0))))
            out_shape.append(jax.ShapeDtypeStruct((m // seq_len, 2, GROUP_W, rows), F32))
    return pl.pallas_call(
        functools.partial(_in_proj_body, seq_len=seq_len),
        grid=(m // tm,),
        in_specs=[row(D_MODEL), full(norm_g), full(w_in_bf16), full(b_gate)],
        out_specs=out_specs,
        out_shape=out_shape,
        compiler_params=_params(("arbitrary",)),
        name="in_proj",
    )(x, norm_g, w_in_bf16, b_gate)


def _residue_rows(r, count, dil):
    return pl.ds(r, count) if dil == 1 else pl.ds(r, count, stride=dil)


def _attn_group_blocks(q_ref, k_ref, v_ref, o_scr, lse_scr, qs, ks, vs, os_, ls, bias_scr, group):
    s_len = qs.shape[0]
    dil = DILATIONS[group]
    l_res = s_len // dil
    nb = l_res // BAND
    for r in range(dil):
        dst = pl.ds(r * l_res, l_res)
        src = _residue_rows(r, l_res, dil)
        stage = lambda halves: jnp.concatenate([h[src, :] for h in halves], axis=1)
        qs[dst, :] = (stage(q_ref) * ATTN_SCALE).astype(BF16)
        ks[dst, :] = stage(k_ref).astype(BF16)
        vs[dst, :] = stage(v_ref).astype(BF16)
    has_prev = nb > 1
    nk = 2 * BAND if has_prev else BAND
    qi = lax.broadcasted_iota(jnp.int32, (BAND, nk), 0)
    kj = lax.broadcasted_iota(jnp.int32, (BAND, nk), 1)
    delta = (nk - BAND) + qi - kj
    band = (delta >= 0) & (delta <= BAND)
    dist = (delta * dil).astype(F32)
    for h in range(HG):
        alibi = -SLOPES[group * HG + h] * dist
        bias_scr[h, :, 0:nk] = jnp.where(band, alibi, NEG_INF)
        bias_scr[HG + h, :, 0:nk] = jnp.where(band & (kj >= nk - BAND), alibi, NEG_INF)
    head = lax.broadcasted_iota(jnp.int32, (1, GROUP_W), 1) // HEAD_DIM
    hm = [head == h for h in range(HG)]
    blocks_per_iter = 2

    def block_pair(it, carry):
        cur, q, k2, v2, first = [], [], [], [], []
        for u in range(blocks_per_iter):
            idx = it * blocks_per_iter + u
            cur.append(pl.ds(pl.multiple_of(idx * BAND, BAND), BAND))
            q.append(qs[cur[u], :])
            k2.append(ks[cur[u], :])
            v2.append(vs[cur[u], :])
            first.append(jnp.where(idx % nb == 0, HG, 0))
            if has_prev:
                prev = pl.ds(pl.multiple_of(jnp.maximum(idx - 1, 0) * BAND, BAND), BAND)
                k2[u] = jnp.concatenate([ks[prev, :], k2[u]], axis=0)
                v2[u] = jnp.concatenate([vs[prev, :], v2[u]], axis=0)
        chains = [(u, h) for u in range(blocks_per_iter) for h in range(HG)]
        s = [_dg(jnp.where(hm[h], q[u], jnp.zeros_like(q[u])), k2[u], NT) + bias_scr[first[u] + h, :, 0:nk]
             for u, h in chains]
        m = [jnp.max(x, axis=-1, keepdims=True) for x in s]
        p = [jnp.exp(x - mx) for x, mx in zip(s, m)]
        l = [jnp.sum(x, axis=-1, keepdims=True) for x in p]
        pv = [_dg(p[i].astype(BF16), v2[u]) for i, (u, h) in enumerate(chains)]
        for u in range(blocks_per_iter):
            o_acc = jnp.zeros((BAND, GROUP_W), F32)
            lse_acc = jnp.zeros((BAND, GROUP_W), F32)
            for h in range(HG):
                i = u * HG + h
                o_acc = jnp.where(hm[h], pv[i] / l[i], o_acc)
                lse_acc = jnp.where(hm[h], m[i] + jnp.log(l[i]), lse_acc)
            os_[cur[u], :] = o_acc
            ls[cur[u], :] = lse_acc
        return carry

    assert (dil * nb) % blocks_per_iter == 0
    lax.fori_loop(0, dil * nb // blocks_per_iter, block_pair, 0)
    for r in range(dil):
        src = pl.ds(r * l_res, l_res)
        dst = _residue_rows(r, l_res, dil)
        for half in range(LANE_HALVES):
            cols = slice(half * 128, (half + 1) * 128)
            o_scr[group * LANE_HALVES + half, dst, :] = os_[src, cols]
            lse_scr[group * LANE_HALVES + half, dst, :] = ls[src, cols]


def _attn_prompt_body(q0_ref, q1_ref, k0_ref, k1_ref, v0_ref, v1_ref, o_ref, o_scr, lse_scr, qs, ks, vs, os_, ls, bias_scr):
    q_ref, k_ref, v_ref = (q0_ref, q1_ref), (k0_ref, k1_ref), (v0_ref, v1_ref)
    gid = pl.program_id(1)
    n_groups = len(DILATIONS)
    for g in range(n_groups):
        @pl.when(gid == g)
        def _(g=g):
            _attn_group_blocks(q_ref, k_ref, v_ref, o_scr, lse_scr, qs, ks, vs, os_, ls, bias_scr, g)

    @pl.when(gid == n_groups - 1)
    def _():
        def merge(i, carry):
            rows = pl.ds(pl.multiple_of(i * BAND, BAND), BAND)
            both = lambda ref, g: jnp.concatenate(
                [ref[g * LANE_HALVES + half, rows, :] for half in range(LANE_HALVES)], axis=1)
            lse = [both(lse_scr, g) for g in range(n_groups)]
            m = functools.reduce(jnp.maximum, lse)
            wts = [jnp.exp(x - m) for x in lse]
            num = sum(wts[g] * both(o_scr, g) for g in range(n_groups))
            o_ref[rows, :] = (num / sum(wts)).astype(o_ref.dtype)
            return carry

        lax.fori_loop(0, o_ref.shape[0] // BAND, merge, 0)


def _attn_prompt(qkv):
    b, s, _ = qkv.shape
    n_groups = len(DILATIONS)
    assert s % (DILATIONS[-1] * BAND) == 0
    sec = ATTN_W // 128

    def col(section, half):
        return pl.BlockSpec((None, s, 128), lambda bi, g: (bi, 0, section * sec + g * LANE_HALVES + half))

    return pl.pallas_call(
        _attn_prompt_body,
        grid=(b, n_groups),
        in_specs=[col(section, half) for section in range(3) for half in range(LANE_HALVES)],
        out_specs=pl.BlockSpec((None, s, GROUP_W), lambda bi, g: (bi, 0, 0)),
        out_shape=jax.ShapeDtypeStruct((b, s, GROUP_W), BF16),
        scratch_shapes=([pltpu.VMEM((n_groups * LANE_HALVES, s, 128), F32)] * 2
                        + [pltpu.VMEM((s, GROUP_W), BF16)] * 3 + [pltpu.VMEM((s, GROUP_W), F32)] * 2
                        + [pltpu.VMEM((2 * HG, BAND, 2 * BAND), F32)]),
        compiler_params=_params(("parallel", "arbitrary")),
        name="attn_prompt",
    )(*([qkv] * (3 * LANE_HALVES)))


def _attn_sample_body(qkv_ref, c0_ref, c1_ref, c2_ref, o_ref):
    rows = 8
    row = lax.broadcasted_iota(jnp.int32, (rows, GROUP_W), 0)
    own = lax.broadcasted_iota(jnp.int32, (rows, GROUP_W), 1) // HEAD_DIM == row
    hrow = lax.broadcasted_iota(jnp.int32, (rows, 1), 0)
    accs, ms, ls = [], [], []
    for g, c_ref in enumerate((c0_ref, c1_ref, c2_ref)):
        dil = DILATIONS[g]
        lb = c_ref.shape[2]
        col = lambda sec: qkv_ref[:, sec * ATTN_W + g * GROUP_W:sec * ATTN_W + (g + 1) * GROUP_W]
        q, kn, vn = col(0), col(1), col(2)
        qb = jnp.where(own, q, 0.0)
        slope = jnp.zeros((rows, 1), F32)
        for h in range(HG):
            slope = jnp.where(hrow == h, SLOPES[g * HG + h], slope)
        t = lax.broadcasted_iota(jnp.int32, (rows, lb), 1)
        s_b = _dg(qb.astype(BF16), c_ref[0].astype(BF16)) * ATTN_SCALE - slope * (lb - t).astype(F32)
        s_b = jnp.where(t % dil == 0, s_b, NEG_INF)
        s_n = jnp.sum(qb * kn, axis=-1, keepdims=True) * ATTN_SCALE
        m = jnp.maximum(jnp.max(s_b, axis=-1, keepdims=True), s_n)
        p_b = jnp.exp(s_b - m)
        p_n = jnp.exp(s_n - m)
        ls.append(jnp.sum(p_b, axis=-1, keepdims=True) + p_n)
        pv = _dg(p_b.astype(BF16), c_ref[1].astype(BF16), NT) + p_n * vn
        accs.append(jnp.where(own, pv, 0.0))
        ms.append(m)
    m_all = functools.reduce(jnp.maximum, ms)
    sc = [jnp.exp(m - m_all) for m in ms]
    den = sum(s * l for s, l in zip(sc, ls))
    out = sum((s / den) * a for s, a in zip(sc, accs))
    o_ref[...] = jnp.sum(out, axis=0, keepdims=True).astype(o_ref.dtype)


def _attn_sample(qkv, caches):
    b = qkv.shape[0]
    c_args, c_specs = [], []
    for g, c in enumerate(caches):
        lb = c.shape[1]
        assert lb == BAND * DILATIONS[g]
        c_args.append(jnp.transpose(c, (0, 2, 3, 4, 1)).reshape(b, 2, GROUP_W, lb))
        c_specs.append(pl.BlockSpec((None, 2, GROUP_W, lb), lambda i: (i, 0, 0, 0)))
    out = pl.pallas_call(
        _attn_sample_body,
        grid=(b,),
        in_specs=[pl.BlockSpec((None, 1, QKV_W), lambda i: (i, 0, 0))] + c_specs,
        out_specs=pl.BlockSpec((None, 1, GROUP_W), lambda i: (i, 0, 0)),
        out_shape=jax.ShapeDtypeStruct((b, 1, GROUP_W), BF16),
        compiler_params=_params(("parallel",)),
        name="attn_sample",
    )(qkv.reshape(b, 1, QKV_W), *c_args)
    return out.reshape(b, GROUP_W)


WKV_PASSES = dict(g=1, s0=1, x=1, inv=1, y=1, s1=1)


def _mm(a, b, dims, passes):
    if passes == 1:
        return _dg(a.astype(BF16), b.astype(BF16), dims)
    ah, al = _split2(a)
    if passes == 2:
        bh = b.astype(BF16)
        return _dg(ah, bh, dims) + _dg(al, bh, dims)
    bh, bl = _split2(b)
    return _dg(ah, bh, dims) + _dg(al, bh, dims) + _dg(ah, bl, dims)


def _wkv_levels(live_rows):
    return int(np.ceil(np.log2(live_rows))) if live_rows > 1 else 0


def _wkv_chunk(operands, states, levels):
    c = CHUNK
    n2 = 2 * c
    n_seq = len(operands)
    ti = lax.broadcasted_iota(jnp.int32, (c, c), 0)
    tj = lax.broadcasted_iota(jnp.int32, (c, c), 1)
    tri = (ti >= tj).astype(BF16)
    first = lax.broadcasted_iota(jnp.int32, (1, PAIR_W), 1) < RWKV_N

    def stack(x, p):
        x = x[:, p * PAIR_W:(p + 1) * PAIR_W]
        return jnp.concatenate([jnp.where(first, x, 0.0), jnp.where(first, 0.0, x)], axis=0)

    ar, bk, v_s, s0, w_end = [], [], [], [], []
    for (r, kp, v, logw, av, bv), seq_states in zip(operands, states):
        lc = _dot_exact_rhs_left(tri, logw)
        e_in = jnp.exp(lc)
        e_neg = jnp.exp(-lc)
        a_t = av * jnp.exp(lc - logw)
        r_t = r * e_in
        b_t = bv * e_neg
        k_t = kp * e_neg
        for p in range(N_PAIRS):
            ar.append(jnp.concatenate([stack(a_t, p), stack(r_t, p)], axis=0))
            bk.append(jnp.concatenate([stack(b_t, p), stack(k_t, p)], axis=0))
            v_s.append(stack(v, p))
            s0.append(seq_states[p])
            w_end.append(e_in[c - 1:c, p * PAIR_W:(p + 1) * PAIR_W])
    chains = range(len(ar))
    g = [_mm(ar[i], bk[i], NT, WKV_PASSES["g"]) for i in chains]
    ar_s0 = [_mm(ar[i], s0[i], NT, WKV_PASSES["s0"]) for i in chains]
    ri = lax.broadcasted_iota(jnp.int32, (n2, n2), 0)
    ci = lax.broadcasted_iota(jnp.int32, (n2, n2), 1)
    strict = ri > ci
    incl = ri >= ci
    n_ab = [jnp.where(strict, g[i][0:n2, 0:n2], 0.0) for i in chains]
    n_ak = [jnp.where(strict, g[i][0:n2, n2:2 * n2], 0.0) for i in chains]
    m_r = [jnp.where(jnp.concatenate([incl, incl], axis=1), g[i][n2:2 * n2, :], 0.0) for i in chains]
    z = [ar_s0[i][0:n2] + _mm(n_ak[i], v_s[i], NN, WKV_PASSES["x"]) for i in chains]
    pw = n_ab
    for lvl in range(levels):
        if lvl < levels - 1:
            pz = [_mm(pw[i], jnp.concatenate([pw[i], z[i]], axis=1), NN, WKV_PASSES["inv"]) for i in chains]
            pw = [pz[i][:, 0:n2] for i in chains]
            z = [z[i] + pz[i][:, n2:2 * n2] for i in chains]
        else:
            z = [z[i] + _mm(pw[i], z[i], NN, WKV_PASSES["inv"]) for i in chains]
    uv = [jnp.concatenate([z[i], v_s[i]], axis=0) for i in chains]
    y_s = [ar_s0[i][n2:2 * n2] + _mm(m_r[i], uv[i], NN, WKV_PASSES["y"]) for i in chains]
    s1 = [(s0[i] + _mm(uv[i], bk[i], TN, WKV_PASSES["s1"])) * w_end[i] for i in chains]
    ys = [jnp.concatenate([y_s[j * N_PAIRS + p][0:c] + y_s[j * N_PAIRS + p][c:n2] for p in range(N_PAIRS)], axis=1)
          for j in range(n_seq)]
    new_states = tuple(tuple(s1[j * N_PAIRS:(j + 1) * N_PAIRS]) for j in range(n_seq))
    return ys, new_states


def _dot_exact_rhs_left(lhs_bf16, b):
    out = None
    rem = b
    for _ in range(3):
        part = rem.astype(BF16)
        term = _dg(lhs_bf16, part)
        out = term if out is None else out + term
        rem = rem - part.astype(F32)
    return out


def _rwkv_body(zb_ref, sh_ref, mu_ref, s0_ref,
               w0_ref, a0_ref, kk_ref, ka_ref, rk_ref, gg_ref, gb_ref, wwa_ref, wg_ref,
               out_ref, s_out_ref,
               s_scr, carry_scr, r_scr, kp_scr, v_scr, lw_scr, av_scr, bv_scr, y_scr, g_scr,
               *, t_valid):
    t = pl.program_id(1)
    n_seq, tc, _ = zb_ref.shape
    rows_all = n_seq * tc

    @pl.when(t == 0)
    def _():
        s_scr[...] = s0_ref[...]
        carry_scr[...] = sh_ref[...]

    row = lax.broadcasted_iota(jnp.int32, (rows_all, 1), 0)
    z = zb_ref[...].reshape(rows_all, SHIFT_W)
    zp = pltpu.roll(z, 1, 0)
    for j in range(n_seq):
        zp = jnp.where(row == j * tc, carry_scr[j], zp)
        carry_scr[j] = z[(j + 1) * tc - 1:(j + 1) * tc, :]
    zm = z + (zp - z) * mu_ref[...]
    r = zm[:, 0:RWKV_W]
    k = zm[:, RWKV_W:2 * RWKV_W]
    v = zm[:, 2 * RWKV_W:3 * RWKV_W]
    z_wa = zm[:, 3 * RWKV_W:3 * RWKV_W + 128]
    z_g = zm[:, 3 * RWKV_W + 128:SHIFT_W]
    half = lax.broadcasted_iota(jnp.int32, (1, 128), 1) < 64
    x_wa = jnp.where(half, jnp.tanh(z_wa), z_wa).astype(BF16)
    x_g = _sigmoid(z_g).astype(BF16)
    u = w0_ref[...] + _dg(x_wa, wwa_ref[:, 0:RWKV_W])
    logw = -DECAY_SCALE * _sigmoid(u)
    a = _sigmoid(a0_ref[...] + _dg(x_wa, wwa_ref[:, RWKV_W:2 * RWKV_W]))
    g_scr[...] = _dg(x_g, wg_ref[...])

    lr = lax.broadcasted_iota(jnp.int32, (PAIR_W, PAIR_W), 0)
    lc = lax.broadcasted_iota(jnp.int32, (PAIR_W, PAIR_W), 1)
    seg = (lr // RWKV_N == lc // RWKV_N).astype(BF16)

    def head_sum(x):
        return jnp.concatenate(
            [_dg(x[:, p * PAIR_W:(p + 1) * PAIR_W].astype(BF16), seg) for p in range(N_PAIRS)], axis=1)

    kk = k * kk_ref[...]
    kk = kk * lax.rsqrt(jnp.maximum(head_sum(kk * kk), 1e-24))
    kp = k * (1.0 + (a - 1.0) * ka_ref[...])
    av = -kk
    bv = kk * a
    if t_valid is not None:
        live = row % tc < t_valid
        zero = lambda x_: jnp.where(live, x_, 0.0)
        r, kp, v, logw, av, bv = zero(r), zero(kp), zero(v), zero(logw), zero(av), zero(bv)
    r_scr[...] = r
    kp_scr[...] = kp
    v_scr[...] = v
    lw_scr[...] = logw
    av_scr[...] = av
    bv_scr[...] = bv
    levels = _wkv_levels(CHUNK if t_valid is None else min(t_valid, CHUNK))
    vectors = (r_scr, kp_scr, v_scr, lw_scr, av_scr, bv_scr)

    def chunk(ci, states):
        rows = [pl.ds(pl.multiple_of(j * tc + ci * CHUNK, CHUNK), CHUNK) for j in range(n_seq)]
        ys, states = _wkv_chunk([tuple(ref[rows[j], :] for ref in vectors) for j in range(n_seq)], states, levels)
        for j in range(n_seq):
            y_scr[rows[j], :] = ys[j]
        return states

    states = lax.fori_loop(0, tc // CHUNK, chunk,
                           tuple(tuple(s_scr[j, p] for p in range(N_PAIRS)) for j in range(n_seq)))
    for j in range(n_seq):
        for p in range(N_PAIRS):
            s_scr[j, p] = states[j][p]
            s_out_ref[j, p] = states[j][p]

    y = y_scr[...]
    inv_n = 1.0 / RWKV_N
    mean = head_sum(y) * inv_n
    d = y - mean
    var = head_sum(d * d) * inv_n
    yn = d * lax.rsqrt(var + GN_EPS) * gg_ref[...] + gb_ref[...]
    bonus = head_sum(r_scr[...] * kp_scr[...] * rk_ref[...]) * v_scr[...]
    out_ref[...] = ((yn + bonus) * g_scr[...]).astype(out_ref.dtype).reshape(n_seq, tc, RWKV_W)


def _rwkv(zb, shift0, s0_bd, w, tc, n_seq, t_valid=None):
    b, t_len, _ = zb.shape
    full = lambda a: pl.BlockSpec(a.shape, lambda bi, t: (0,) * a.ndim)
    state = pl.BlockSpec((n_seq, N_PAIRS, PAIR_W, PAIR_W), lambda bi, t: (bi, 0, 0, 0))
    consts = [w["mu_shift"]]
    vecs = [w["w0"], w["a0"], w["k_k"], w["k_a"], w["r_k"], w["gn_g"], w["gn_b"], w["wa_lora"], w["g_lora"]]
    in_specs = ([pl.BlockSpec((n_seq, tc, SHIFT_W), lambda bi, t: (bi, t, 0)),
                 pl.BlockSpec((n_seq, 1, SHIFT_W), lambda bi, t: (bi, 0, 0))]
                + [full(c) for c in consts] + [state] + [full(c) for c in vecs])
    vm = lambda shape: pltpu.VMEM(shape, F32)
    scratch = [vm((n_seq, N_PAIRS, PAIR_W, PAIR_W)), vm((n_seq, 1, SHIFT_W))] + [vm((n_seq * tc, RWKV_W))] * 8
    return pl.pallas_call(
        functools.partial(_rwkv_body, t_valid=t_valid),
        grid=(b // n_seq, t_len // tc),
        in_specs=in_specs,
        out_specs=[pl.BlockSpec((n_seq, tc, RWKV_W), lambda bi, t: (bi, t, 0)), state],
        out_shape=[jax.ShapeDtypeStruct((b, t_len, RWKV_W), BF16),
                   jax.ShapeDtypeStruct((b, N_PAIRS, PAIR_W, PAIR_W), F32)],
        scratch_shapes=scratch,
        compiler_params=_params(("parallel", "arbitrary")),
        name="rwkv",
    )(zb, shift0, *consts, s0_bd, *vecs)


def _state_to_pairs(s):
    b = s.shape[0]
    s = s.reshape(b, N_PAIRS, 2, RWKV_N, RWKV_N)
    z = jnp.zeros_like(s[:, :, 0])
    top = jnp.concatenate([s[:, :, 0], z], axis=-1)
    bot = jnp.concatenate([z, s[:, :, 1]], axis=-1)
    return jnp.concatenate([top, bot], axis=-2)


def _pairs_to_state(s_bd):
    b = s_bd.shape[0]
    h0 = s_bd[:, :, :RWKV_N, :RWKV_N]
    h1 = s_bd[:, :, RWKV_N:, RWKV_N:]
    return jnp.stack([h0, h1], axis=2).reshape(b, RWKV_HEADS, RWKV_N, RWKV_N)


def _out_proj_body(x_ref, attn_ref, rwkv_ref, gate_ref, wpa_ref, wpb_ref, wout_ref, n2_ref,
                   wup_ref, wdn_ref, nf_ref, y_ref):
    pa = _dg(attn_ref[...], wpa_ref[...])
    pb = _dg(rwkv_ref[...], wpb_ref[...])
    merged = gate_ref[:, 0:D_MODEL] * pa + gate_ref[:, D_MODEL:GATE_W] * pb
    x1 = x_ref[...] + _dg(merged.astype(BF16), wout_ref[...])
    hm = _rms(x1, n2_ref[...]).astype(BF16)
    acc = x1
    for c in range(0, D_FF, 1024):
        up = jnp.maximum(_dg(hm, wup_ref[:, c:c + 1024]), 0.0)
        acc = acc + _dg((up * up).astype(BF16), wdn_ref[c:c + 1024, :])
    y_ref[...] = _rms(acc, nf_ref[...])


def _out_proj(x, attn, rwkv, gate, w, tm):
    m = x.shape[0]
    row = lambda wd: pl.BlockSpec((tm, wd), lambda i: (i, 0))
    full = lambda a: pl.BlockSpec(a.shape, lambda i: (0,) * a.ndim)
    consts = [w["w_proj_a"], w["w_proj_b"], w["w_out"], w["norm2_g"], w["w_up"], w["w_down"], w["normf_g"]]
    return pl.pallas_call(
        _out_proj_body,
        grid=(m // tm,),
        in_specs=[row(D_MODEL), row(GROUP_W), row(RWKV_W), row(GATE_W)] + [full(c) for c in consts],
        out_specs=row(D_MODEL),
        out_shape=jax.ShapeDtypeStruct((m, D_MODEL), F32),
        compiler_params=_params(("parallel",)),
        name="out_proj",
    )(x, attn, rwkv, gate, *consts)


def _kv_rows(qkv, group, rows):
    b, s, _ = qkv.shape
    k0 = ATTN_W + group * GROUP_W
    v0 = 2 * ATTN_W + group * GROUP_W
    k = qkv[:, s - rows:, k0:k0 + GROUP_W].reshape(b, rows, HG, HEAD_DIM)
    v = qkv[:, s - rows:, v0:v0 + GROUP_W].reshape(b, rows, HG, HEAD_DIM)
    return jnp.stack([k, v], axis=2)


def _layer_weights(l, norm1_g, w_in, b_gate, mu_shift, w0, w_lora_up, a0, a_lora_up, g_lora_up, k_k, k_a,
                   r_k, gn_g, gn_b, w_proj_a, w_proj_b, w_out, norm2_g, w_up, w_down, normf_g):
    row = lambda a: a.reshape(1, -1)
    zero = jnp.zeros_like(w_lora_up[l])
    wa_lora = jnp.concatenate([jnp.concatenate([w_lora_up[l], zero], axis=1),
                               jnp.concatenate([zero, a_lora_up[l]], axis=1)], axis=0)
    return dict(
        norm1_g=row(norm1_g[l]), w_in=w_in[l].astype(BF16), b_gate=row(b_gate[l]), mu_shift=row(mu_shift[l]),
        w0=row(w0[l]), a0=row(a0[l]), k_k=row(k_k[l]), k_a=row(k_a[l]), r_k=row(r_k[l]),
        gn_g=row(gn_g[l]), gn_b=row(gn_b[l]),
        wa_lora=wa_lora.astype(BF16), g_lora=g_lora_up[l].astype(BF16),
        w_proj_a=w_proj_a[l].astype(BF16), w_proj_b=w_proj_b[l].astype(BF16), w_out=w_out[l].astype(BF16),
        norm2_g=row(norm2_g[l]), w_up=w_up[l].astype(BF16), w_down=w_down[l].astype(BF16),
        normf_g=row(normf_g))


def _prompt_layer(x, w, tm_in, tm_out, tc):
    b, s, _ = x.shape
    x2 = x.reshape(b * s, D_MODEL)
    qkv, zb, gate, *kv_t = _in_proj(x2, w["norm1_g"], w["w_in"], w["b_gate"], tm_in, seq_len=s)
    qkv = qkv.reshape(b, s, QKV_W)
    zb = zb.reshape(b, s, SHIFT_W)
    attn = _attn_prompt(qkv)
    rwkv, s_bd = _rwkv(zb, jnp.zeros((b, 1, SHIFT_W), F32), jnp.zeros((b, N_PAIRS, PAIR_W, PAIR_W), F32), w, tc,
                       RWKV_SEQS_PER_STEP)
    y = _out_proj(x2, attn.reshape(b * s, GROUP_W), rwkv.reshape(b * s, RWKV_W), gate, w, tm_out)
    kvs = [jnp.transpose(t.reshape(b, 2, HG, HEAD_DIM, t.shape[-1]), (0, 4, 1, 2, 3)) for t in kv_t]
    return y.reshape(b, s, D_MODEL), kvs, _pairs_to_state(s_bd), zb[:, -1]


def _sample_layer(x, caches, s0, shift0, w):
    b, t_len, _ = x.shape
    assert t_len == 1
    x2 = x.reshape(b, D_MODEL)
    qkv, zb, gate = _in_proj(x2, w["norm1_g"], w["w_in"], w["b_gate"], b)
    attn = _attn_sample(qkv, caches)
    zb_pad = jnp.pad(zb[:, None, :], ((0, 0), (0, CHUNK - 1), (0, 0)))
    rwkv, s_bd = _rwkv(zb_pad, shift0[:, None, :], _state_to_pairs(s0), w, CHUNK, RWKV_SEQS_PER_STEP, t_valid=1)
    y = _out_proj(x2, attn, rwkv[:, 0], gate, w, b)
    kvs = [_kv_rows(qkv[:, None, :], g, 1) for g in range(len(DILATIONS))]
    return y.reshape(b, 1, D_MODEL), kvs, _pairs_to_state(s_bd), zb


def kernel(x_prompt, x_sample, cache_kv_w128, cache_kv_w512, cache_kv_w2048, state_wkv, state_shift, norm1_g, w_in, b_gate, mu_shift, w0, w_lora_up, a0, a_lora_up, g_lora_up, k_k, k_a, r_k, gn_g, gn_b, w_proj_a, w_proj_b, w_out, norm2_g, w_up, w_down, normf_g):
    depth = w_in.shape[0]
    assert depth == 1, "the final norm is fused into the layer's output stage"
    w = _layer_weights(0, norm1_g, w_in, b_gate, mu_shift, w0, w_lora_up, a0, a_lora_up, g_lora_up, k_k, k_a,
                       r_k, gn_g, gn_b, w_proj_a, w_proj_b, w_out, norm2_g, w_up, w_down, normf_g)
    y_p, kv_p, wkv_p, shift_p = _prompt_layer(x_prompt, w, 512, 512, 512)
    y_s, kv_s, wkv_s, shift_s = _sample_layer(
        x_sample, (cache_kv_w128[0], cache_kv_w512[0], cache_kv_w2048[0]), state_wkv[0], state_shift[0], w)
    lead = lambda a: a[None]
    return (y_p, y_s, lead(kv_p[0]), lead(kv_p[1]), lead(kv_p[2]), lead(wkv_p), lead(shift_p),
            lead(kv_s[0]), lead(kv_s[1]), lead(kv_s[2]), lead(wkv_s), lead(shift_s))
```

```python
import functools

import numpy as np
import jax
import jax.numpy as jnp
from jax import lax
from jax.experimental import pallas as pl
from jax.experimental.pallas import tpu as pltpu

F32 = jnp.float32
BF16 = jnp.bfloat16

D_MODEL = 1024
HEAD_DIM = 64
HG = 4
DILATIONS = (1, 4, 16)
BAND = 128
N_ATTN_HEADS = HG * len(DILATIONS)
GROUP_W = HG * HEAD_DIM
LANE_HALVES = GROUP_W // 128
ATTN_W = N_ATTN_HEADS * HEAD_DIM
QKV_W = 3 * ATTN_W
RWKV_N = 64
RWKV_W = 512
RWKV_HEADS = RWKV_W // RWKV_N
PAIR_W = 2 * RWKV_N
N_PAIRS = RWKV_HEADS // 2
LORA_W = 256
SHIFT_W = 3 * RWKV_W + LORA_W
GATE_W = 2 * D_MODEL
IN_W = QKV_W + SHIFT_W + GATE_W
D_FF = 4 * D_MODEL
NORM_EPS = 1e-6
GN_EPS = 64e-5
NEG_INF = -1e30
ATTN_SCALE = HEAD_DIM ** -0.5
CHUNK = 64
RWKV_SEQS_PER_STEP = 2
DECAY_SCALE = float(np.exp(-0.5))
SLOPES = [float(s) for s in np.exp2(-8.0 * np.arange(1, N_ATTN_HEADS + 1, dtype=np.float32) / N_ATTN_HEADS)]

V7X_VMEM_LIMIT = 56 * 1024 * 1024

NN = (((1,), (0,)), ((), ()))
NT = (((1,), (1,)), ((), ()))
TN = (((0,), (0,)), ((), ()))


def _dg(a, b, dims=NN):
    return lax.dot_general(a, b, dims, preferred_element_type=F32)


def _split2(a):
    hi = a.astype(BF16)
    lo = (a - hi.astype(F32)).astype(BF16)
    return hi, lo


def _dot3(a, b, dims=NN):
    ah, al = _split2(a)
    bh, bl = _split2(b)
    return _dg(ah, bh, dims) + _dg(al, bh, dims) + _dg(ah, bl, dims)


def _dot_exact_rhs(a, b_bf16, passes):
    out = None
    rem = a
    for _ in range(passes):
        part = rem.astype(BF16)
        term = _dg(part, b_bf16)
        out = term if out is None else out + term
        rem = rem - part.astype(F32)
    return out


def _sigmoid(x):
    return 0.5 * jnp.tanh(0.5 * x) + 0.5


def _rms(x, g):
    return x * lax.rsqrt(jnp.mean(x * x, axis=-1, keepdims=True) + NORM_EPS) * g


def _params(sem):
    return pltpu.CompilerParams(dimension_semantics=sem, vmem_limit_bytes=V7X_VMEM_LIMIT)


def _kv_tail_plan(seq_len, tm):
    plan = []
    for dil in DILATIONS:
        rows = min(BAND * dil, seq_len)
        width = min(rows, tm)
        plan.append((rows, width, (seq_len - rows) // tm))
    return plan


def _in_proj_body(x_ref, g_ref, w_ref, bg_ref, qkv_ref, zb_ref, gate_ref, *kv_refs, seq_len):
    tm = x_ref.shape[0]
    h = _rms(x_ref[...], g_ref[...]).astype(BF16)
    for c in range(0, GATE_W, 1024):
        zg = _dg(h, w_ref[:, QKV_W + SHIFT_W + c:QKV_W + SHIFT_W + c + 1024])
        gate_ref[:, c:c + 1024] = _sigmoid(zg + bg_ref[:, c:c + 1024])
    for c in range(0, QKV_W, 768):
        qkv_ref[:, c:c + 768] = _dg(h, w_ref[:, c:c + 768])

    def write_tail(g, kv_ref, width):
        for sec in (1, 2):
            c0 = sec * ATTN_W + g * GROUP_W
            kv_ref[sec - 1] = qkv_ref[tm - width:tm, c0:c0 + GROUP_W].T

    plan = _kv_tail_plan(seq_len, tm) if kv_refs else []
    for g, (kv_ref, (_, width, first)) in enumerate(zip(kv_refs, plan)):
        if first == 0:
            write_tail(g, kv_ref, width)
    for c in range(0, SHIFT_W, 896):
        zb_ref[:, c:c + 896] = _dg(h, w_ref[:, QKV_W + c:QKV_W + c + 896])
    for g, (kv_ref, (_, width, first)) in enumerate(zip(kv_refs, plan)):
        if first > 0:
            tile = pl.program_id(0) % (seq_len // tm)
            pl.when(tile >= first)(functools.partial(write_tail, g, kv_ref, width))


def _in_proj(x, norm_g, w_in_bf16, b_gate, tm, seq_len=None):
    m = x.shape[0]
    row = lambda w: pl.BlockSpec((tm, w), lambda i: (i, 0))
    full = lambda a: pl.BlockSpec(a.shape, lambda i: (0,) * a.ndim)
    out_specs = [row(QKV_W), row(SHIFT_W), row(GATE_W)]
    out_shape = [jax.ShapeDtypeStruct((m, QKV_W), F32),
                 jax.ShapeDtypeStruct((m, SHIFT_W), F32),
                 jax.ShapeDtypeStruct((m, GATE_W), F32)]
    if seq_len is not None:
        tiles = seq_len // tm
        for rows, width, first in _kv_tail_plan(seq_len, tm):
            out_specs.append(pl.BlockSpec(
                (None, 2, GROUP_W, width),
                lambda i, first=first: (i // tiles, 0, 0, jnp.maximum(i % tiles - first, 0))))
            out_shape.append(jax.ShapeDtypeStruct((m // seq_len, 2, GROUP_W, rows), F32))
    return pl.pallas_call(
        functools.partial(_in_proj_body, seq_len=seq_len),
        grid=(m // tm,),
        in_specs=[row(D_MODEL), full(norm_g), full(w_in_bf16), full(b_gate)],
        out_specs=out_specs,
        out_shape=out_shape,
        compiler_params=_params(("arbitrary",)),
        name="in_proj",
    )(x, norm_g, w_in_bf16, b_gate)


def _residue_rows(r, count, dil):
    return pl.ds(r, count) if dil == 1 else pl.ds(r, count, stride=dil)


def _attn_group_blocks(q_ref, k_ref, v_ref, o_scr, lse_scr, qs, ks, vs, os_, ls, bias_scr, group):
    s_len = qs.shape[0]
    dil = DILATIONS[group]
    l_res = s_len // dil
    nb = l_res // BAND
    for r in range(dil):
        dst = pl.ds(r * l_res, l_res)
        src = _residue_rows(r, l_res, dil)
        stage = lambda halves: jnp.concatenate([h[src, :] for h in halves], axis=1)
        qs[dst, :] = (stage(q_ref) * ATTN_SCALE).astype(BF16)
        ks[dst, :] = stage(k_ref).astype(BF16)
        vs[dst, :] = stage(v_ref).astype(BF16)
    has_prev = nb > 1
    nk = 2 * BAND if has_prev else BAND
    qi = lax.broadcasted_iota(jnp.int32, (BAND, nk), 0)
    kj = lax.broadcasted_iota(jnp.int32, (BAND, nk), 1)
    delta = (nk - BAND) + qi - kj
    band = (delta >= 0) & (delta <= BAND)
    dist = (delta * dil).astype(F32)
    for h in range(HG):
        alibi = -SLOPES[group * HG + h] * dist
        bias_scr[h, :, 0:nk] = jnp.where(band, alibi, NEG_INF)
        bias_scr[HG + h, :, 0:nk] = jnp.where(band & (kj >= nk - BAND), alibi, NEG_INF)
    head = lax.broadcasted_iota(jnp.int32, (1, GROUP_W), 1) // HEAD_DIM
    hm = [head == h for h in range(HG)]
    blocks_per_iter = 2

    def block_pair(it, carry):
        cur, q, k2, v2, first = [], [], [], [], []
        for u in range(blocks_per_iter):
            idx = it * blocks_per_iter + u
            cur.append(pl.ds(pl.multiple_of(idx * BAND, BAND), BAND))
            q.append(qs[cur[u], :])
            k2.append(ks[cur[u], :])
            v2.append(vs[cur[u], :])
            first.append(jnp.where(idx % nb == 0, HG, 0))
            if has_prev:
                prev = pl.ds(pl.multiple_of(jnp.maximum(idx - 1, 0) * BAND, BAND), BAND)
                k2[u] = jnp.concatenate([ks[prev, :], k2[u]], axis=0)
                v2[u] = jnp.concatenate([vs[prev, :], v2[u]], axis=0)
        chains = [(u, h) for u in range(blocks_per_iter) for h in range(HG)]
        s = [_dg(jnp.where(hm[h], q[u], jnp.zeros_like(q[u])), k2[u], NT) + bias_scr[first[u] + h, :, 0:nk]
             for u, h in chains]
        m = [jnp.max(x, axis=-1, keepdims=True) for x in s]
        p = [jnp.exp(x - mx) for x, mx in zip(s, m)]
        l = [jnp.sum(x, axis=-1, keepdims=True) for x in p]
        pv = [_dg(p[i].astype(BF16), v2[u]) for i, (u, h) in enumerate(chains)]
        for u in range(blocks_per_iter):
            o_acc = jnp.zeros((BAND, GROUP_W), F32)
            lse_acc = jnp.zeros((BAND, GROUP_W), F32)
            for h in range(HG):
                i = u * HG + h
                o_acc = jnp.where(hm[h], pv[i] / l[i], o_acc)
                lse_acc = jnp.where(hm[h], m[i] + jnp.log(l[i]), lse_acc)
            os_[cur[u], :] = o_acc
            ls[cur[u], :] = lse_acc
        return carry

    assert (dil * nb) % blocks_per_iter == 0
    lax.fori_loop(0, dil * nb // blocks_per_iter, block_pair, 0)
    for r in range(dil):
        src = pl.ds(r * l_res, l_res)
        dst = _residue_rows(r, l_res, dil)
        for half in range(LANE_HALVES):
            cols = slice(half * 128, (half + 1) * 128)
            o_scr[group * LANE_HALVES + half, dst, :] = os_[src, cols]
            lse_scr[group * LANE_HALVES + half, dst, :] = ls[src, cols]


def _attn_prompt_body(q0_ref, q1_ref, k0_ref, k1_ref, v0_ref, v1_ref, o_ref, o_scr, lse_scr, qs, ks, vs, os_, ls, bias_scr):
    q_ref, k_ref, v_ref = (q0_ref, q1_ref), (k0_ref, k1_ref), (v0_ref, v1_ref)
    gid = pl.program_id(1)
    n_groups = len(DILATIONS)
    for g in range(n_groups):
        @pl.when(gid == g)
        def _(g=g):
            _attn_group_blocks(q_ref, k_ref, v_ref, o_scr, lse_scr, qs, ks, vs, os_, ls, bias_scr, g)

    @pl.when(gid == n_groups - 1)
    def _():
        def merge(i, carry):
            rows = pl.ds(pl.multiple_of(i * BAND, BAND), BAND)
            both = lambda ref, g: jnp.concatenate(
                [ref[g * LANE_HALVES + half, rows, :] for half in range(LANE_HALVES)], axis=1)
            lse = [both(lse_scr, g) for g in range(n_groups)]
            m = functools.reduce(jnp.maximum, lse)
            wts = [jnp.exp(x - m) for x in lse]
            num = sum(wts[g] * both(o_scr, g) for g in range(n_groups))
            o_ref[rows, :] = (num / sum(wts)).astype(o_ref.dtype)
            return carry

        lax.fori_loop(0, o_ref.shape[0] // BAND, merge, 0)


def _attn_prompt(qkv):
    b, s, _ = qkv.shape
    n_groups = len(DILATIONS)
    assert s % (DILATIONS[-1] * BAND) == 0
    sec = ATTN_W // 128

    def col(section, half):
        return pl.BlockSpec((None, s, 128), lambda bi, g: (bi, 0, section * sec + g * LANE_HALVES + half))

    return pl.pallas_call(
        _attn_prompt_body,
        grid=(b, n_groups),
        in_specs=[col(section, half) for section in range(3) for half in range(LANE_HALVES)],
        out_specs=pl.BlockSpec((None, s, GROUP_W), lambda bi, g: (bi, 0, 0)),
        out_shape=jax.ShapeDtypeStruct((b, s, GROUP_W), BF16),
        scratch_shapes=([pltpu.VMEM((n_groups * LANE_HALVES, s, 128), F32)] * 2
                        + [pltpu.VMEM((s, GROUP_W), BF16)] * 3 + [pltpu.VMEM((s, GROUP_W), F32)] * 2
                        + [pltpu.VMEM((2 * HG, BAND, 2 * BAND), F32)]),
        compiler_params=_params(("parallel", "arbitrary")),
        name="attn_prompt",
    )(*([qkv] * (3 * LANE_HALVES)))


def _attn_sample_body(qkv_ref, c0_ref, c1_ref, c2_ref, o_ref):
    rows = 8
    row = lax.broadcasted_iota(jnp.int32, (rows, GROUP_W), 0)
    own = lax.broadcasted_iota(jnp.int32, (rows, GROUP_W), 1) // HEAD_DIM == row
    hrow = lax.broadcasted_iota(jnp.int32, (rows, 1), 0)
    accs, ms, ls = [], [], []
    for g, c_ref in enumerate((c0_ref, c1_ref, c2_ref)):
        dil = DILATIONS[g]
        lb = c_ref.shape[2]
        col = lambda sec: qkv_ref[:, sec * ATTN_W + g * GROUP_W:sec * ATTN_W + (g + 1) * GROUP_W]
        q, kn, vn = col(0), col(1), col(2)
        qb = jnp.where(own, q, 0.0)
        slope = jnp.zeros((rows, 1), F32)
        for h in range(HG):
            slope = jnp.where(hrow == h, SLOPES[g * HG + h], slope)
        t = lax.broadcasted_iota(jnp.int32, (rows, lb), 1)
        s_b = _dg(qb.astype(BF16), c_ref[0].astype(BF16)) * ATTN_SCALE - slope * (lb - t).astype(F32)
        s_b = jnp.where(t % dil == 0, s_b, NEG_INF)
        s_n = jnp.sum(qb * kn, axis=-1, keepdims=True) * ATTN_SCALE
        m = jnp.maximum(jnp.max(s_b, axis=-1, keepdims=True), s_n)
        p_b = jnp.exp(s_b - m)
        p_n = jnp.exp(s_n - m)
        ls.append(jnp.sum(p_b, axis=-1, keepdims=True) + p_n)
        pv = _dg(p_b.astype(BF16), c_ref[1].astype(BF16), NT) + p_n * vn
        accs.append(jnp.where(own, pv, 0.0))
        ms.append(m)
    m_all = functools.reduce(jnp.maximum, ms)
    sc = [jnp.exp(m - m_all) for m in ms]
    den = sum(s * l for s, l in zip(sc, ls))
    out = sum((s / den) * a for s, a in zip(sc, accs))
    o_ref[...] = jnp.sum(out, axis=0, keepdims=True).astype(o_ref.dtype)


def _attn_sample(qkv, caches):
    b = qkv.shape[0]
    c_args, c_specs = [], []
    for g, c in enumerate(caches):
        lb = c.shape[1]
        assert lb == BAND * DILATIONS[g]
        c_args.append(jnp.transpose(c, (0, 2, 3, 4, 1)).reshape(b, 2, GROUP_W, lb))
        c_specs.append(pl.BlockSpec((None, 2, GROUP_W, lb), lambda i: (i, 0, 0, 0)))
    out = pl.pallas_call(
        _attn_sample_body,
        grid=(b,),
        in_specs=[pl.BlockSpec((None, 1, QKV_W), lambda i: (i, 0, 0))] + c_specs,
        out_specs=pl.BlockSpec((None, 1, GROUP_W), lambda i: (i, 0, 0)),
        out_shape=jax.ShapeDtypeStruct((b, 1, GROUP_W), BF16),
        compiler_params=_params(("parallel",)),
        name="attn_sample",
    )(qkv.reshape(b, 1, QKV_W), *c_args)
    return out.reshape(b, GROUP_W)


WKV_PASSES = dict(g=1, s0=1, x=1, inv=1, y=1, s1=1)


def _mm(a, b, dims, passes):
    if passes == 1:
        return _dg(a.astype(BF16), b.astype(BF16), dims)
    ah, al = _split2(a)
    if passes == 2:
        bh = b.astype(BF16)
        return _dg(ah, bh, dims) + _dg(al, bh, dims)
    bh, bl = _split2(b)
    return _dg(ah, bh, dims) + _dg(al, bh, dims) + _dg(ah, bl, dims)


def _wkv_levels(live_rows):
    return int(np.ceil(np.log2(live_rows))) if live_rows > 1 else 0


def _wkv_chunk(operands, states, levels, side=()):
    side = list(side)

    def run_side():
        if side:
            side.pop(0)()
    c = CHUNK
    n2 = 2 * c
    n_seq = len(operands)
    ti = lax.broadcasted_iota(jnp.int32, (c, c), 0)
    tj = lax.broadcasted_iota(jnp.int32, (c, c), 1)
    tri = (ti >= tj).astype(BF16)
    first = lax.broadcasted_iota(jnp.int32, (1, PAIR_W), 1) < RWKV_N

    def stack(x, p):
        x = x[:, p * PAIR_W:(p + 1) * PAIR_W]
        return jnp.concatenate([jnp.where(first, x, 0.0), jnp.where(first, 0.0, x)], axis=0)

    ar, bk, v_s, s0, w_end = [], [], [], [], []
    for (r, kp, v, logw, av, bv), seq_states in zip(operands, states):
        lc = _dot_exact_rhs_left(tri, logw)
        e_in = jnp.exp(lc)
        e_neg = jnp.exp(-lc)
        a_t = av * jnp.exp(lc - logw)
        r_t = r * e_in
        b_t = bv * e_neg
        k_t = kp * e_neg
        for p in range(N_PAIRS):
            ar.append(jnp.concatenate([stack(a_t, p), stack(r_t, p)], axis=0))
            bk.append(jnp.concatenate([stack(b_t, p), stack(k_t, p)], axis=0))
            v_s.append(stack(v, p))
            s0.append(seq_states[p])
            w_end.append(e_in[c - 1:c, p * PAIR_W:(p + 1) * PAIR_W])
    chains = range(len(ar))
    g = [_mm(ar[i], bk[i], NT, WKV_PASSES["g"]) for i in chains]
    ar_s0 = [_mm(ar[i], s0[i], NT, WKV_PASSES["s0"]) for i in chains]
    run_side()
    ri = lax.broadcasted_iota(jnp.int32, (n2, n2), 0)
    ci = lax.broadcasted_iota(jnp.int32, (n2, n2), 1)
    strict = ri > ci
    incl = ri >= ci
    n_ab = [jnp.where(strict, g[i][0:n2, 0:n2], 0.0) for i in chains]
    n_ak = [jnp.where(strict, g[i][0:n2, n2:2 * n2], 0.0) for i in chains]
    m_r = [jnp.where(jnp.concatenate([incl, incl], axis=1), g[i][n2:2 * n2, :], 0.0) for i in chains]
    z = [ar_s0[i][0:n2] + _mm(n_ak[i], v_s[i], NN, WKV_PASSES["x"]) for i in chains]
    pw = n_ab
    for lvl in range(levels):
        if lvl < levels - 1:
            pz = [_mm(pw[i], jnp.concatenate([pw[i], z[i]], axis=1), NN, WKV_PASSES["inv"]) for i in chains]
            pw = [pz[i][:, 0:n2] for i in chains]
            z = [z[i] + pz[i][:, n2:2 * n2] for i in chains]
        else:
            z = [z[i] + _mm(pw[i], z[i], NN, WKV_PASSES["inv"]) for i in chains]
        if lvl % 2 == 1:
            run_side()
    while side:
        run_side()
    uv = [jnp.concatenate([z[i], v_s[i]], axis=0) for i in chains]
    y_s = [ar_s0[i][n2:2 * n2] + _mm(m_r[i], uv[i], NN, WKV_PASSES["y"]) for i in chains]
    s1 = [(s0[i] + _mm(uv[i], bk[i], TN, WKV_PASSES["s1"])) * w_end[i] for i in chains]
    ys = [jnp.concatenate([y_s[j * N_PAIRS + p][0:c] + y_s[j * N_PAIRS + p][c:n2] for p in range(N_PAIRS)], axis=1)
          for j in range(n_seq)]
    new_states = tuple(tuple(s1[j * N_PAIRS:(j + 1) * N_PAIRS]) for j in range(n_seq))
    return ys, new_states


def _dot_exact_rhs_left(lhs_bf16, b):
    out = None
    rem = b
    for _ in range(3):
        part = rem.astype(BF16)
        term = _dg(lhs_bf16, part)
        out = term if out is None else out + term
        rem = rem - part.astype(F32)
    return out


def _rwkv_body(zb_ref, sh_ref, mu_ref, s0_ref,
               w0_ref, a0_ref, kk_ref, ka_ref, rk_ref, gg_ref, gb_ref, wwa_ref, wg_ref,
               out_ref, s_out_ref,
               s_scr, carry_scr, ring0_scr, ring1_scr,
               *, t_valid):
    t = pl.program_id(1)
    n_seq, tc, _ = zb_ref.shape

    @pl.when(t == 0)
    def _():
        s_scr[...] = s0_ref[...]
        carry_scr[...] = sh_ref[...]

    lr = lax.broadcasted_iota(jnp.int32, (2 * PAIR_W, 2 * PAIR_W), 0)
    lc = lax.broadcasted_iota(jnp.int32, (2 * PAIR_W, 2 * PAIR_W), 1)
    seg = (lr // RWKV_N == lc // RWKV_N).astype(BF16)
    half = lax.broadcasted_iota(jnp.int32, (1, 128), 1) < 64
    row = lax.broadcasted_iota(jnp.int32, (CHUNK, 1), 0)
    inv_n = 1.0 / RWKV_N
    levels = _wkv_levels(CHUNK if t_valid is None else min(t_valid, CHUNK))

    def head_sum(x):
        wide = 2 * PAIR_W
        return jnp.concatenate(
            [_dg(x[:, c:c + wide].astype(BF16), seg) for c in range(0, RWKV_W, wide)], axis=1)

    n_chunks = tc // CHUNK
    rings = (ring0_scr, ring1_scr)

    def staging_jobs(ci, ring):
        rows = pl.ds(pl.multiple_of(ci * CHUNK, CHUNK), CHUNK)
        st = [dict() for _ in range(n_seq)]

        def mix_and_lora():
            for j in range(n_seq):
                z = zb_ref[j, rows, :]
                before = zb_ref[j, pl.ds(jnp.maximum(ci * CHUNK - 1, 0), 1), :]
                before = jnp.where(ci == 0, carry_scr[j], before)
                zp = jnp.where(row == 0, before, pltpu.roll(z, 1, 0))
                zm = z + (zp - z) * mu_ref[...]
                z_wa = zm[:, 3 * RWKV_W:3 * RWKV_W + 128]
                z_g = zm[:, 3 * RWKV_W + 128:SHIFT_W]
                x_wa = jnp.where(half, jnp.tanh(z_wa), z_wa).astype(BF16)
                st[j].update(r=zm[:, 0:RWKV_W], k=zm[:, RWKV_W:2 * RWKV_W], v=zm[:, 2 * RWKV_W:3 * RWKV_W],
                             u=_dg(x_wa, wwa_ref[:, 0:RWKV_W]), la=_dg(x_wa, wwa_ref[:, RWKV_W:2 * RWKV_W]),
                             gate=_dg(_sigmoid(z_g).astype(BF16), wg_ref[...]))

        def head_sums():
            for j in range(n_seq):
                s = st[j]
                a = _sigmoid(a0_ref[...] + s["la"])
                kk = s["k"] * kk_ref[...]
                kp = s["k"] * (1.0 + (a - 1.0) * ka_ref[...])
                s.update(a=a, kk=kk, kp=kp, kk2=head_sum(kk * kk), rk=head_sum(s["r"] * kp * rk_ref[...]))

        def finish():
            for j in range(n_seq):
                s = st[j]
                logw = -DECAY_SCALE * _sigmoid(w0_ref[...] + s["u"])
                kk = s["kk"] * lax.rsqrt(jnp.maximum(s["kk2"], 1e-24))
                ops = (s["r"], s["kp"], s["v"], logw, -kk, kk * s["a"], s["rk"] * s["v"])
                if t_valid is not None:
                    live = row + ci * CHUNK < t_valid
                    ops = tuple(jnp.where(live, x, 0.0) for x in ops)
                for i, x in enumerate(ops + (s["gate"],)):
                    ring[j, i] = x

        return mix_and_lora, head_sums, finish

    def chunk(ci, states, ring, other):
        rows = pl.ds(pl.multiple_of(ci * CHUNK, CHUNK), CHUNK)
        ys, states = _wkv_chunk([tuple(ring[j, i] for i in range(6)) for j in range(n_seq)], states, levels,
                                side=staging_jobs(jnp.minimum(ci + 1, n_chunks - 1), other))
        for j in range(n_seq):
            mean = head_sum(ys[j]) * inv_n
            d = ys[j] - mean
            var = head_sum(d * d) * inv_n
            yn = d * lax.rsqrt(var + GN_EPS) * gg_ref[...] + gb_ref[...]
            out_ref[j, rows, :] = ((yn + ring[j, 6]) * ring[j, 7]).astype(out_ref.dtype)
        return states

    for job in staging_jobs(0, rings[0]):
        job()
    states = tuple(tuple(s_scr[j, p] for p in range(N_PAIRS)) for j in range(n_seq))
    if n_chunks == 1:
        states = chunk(0, states, rings[0], rings[1])
    else:
        assert n_chunks % 2 == 0

        def chunk_pair(i, states):
            states = chunk(2 * i, states, rings[0], rings[1])
            return chunk(2 * i + 1, states, rings[1], rings[0])

        states = lax.fori_loop(0, n_chunks // 2, chunk_pair, states)
    for j in range(n_seq):
        carry_scr[j] = zb_ref[j, tc - 1:tc, :]
        for p in range(N_PAIRS):
            s_scr[j, p] = states[j][p]
            s_out_ref[j, p] = states[j][p]


def _rwkv(zb, shift0, s0_bd, w, tc, n_seq, t_valid=None):
    b, t_len, _ = zb.shape
    full = lambda a: pl.BlockSpec(a.shape, lambda bi, t: (0,) * a.ndim)
    state = pl.BlockSpec((n_seq, N_PAIRS, PAIR_W, PAIR_W), lambda bi, t: (bi, 0, 0, 0))
    consts = [w["mu_shift"]]
    vecs = [w["w0"], w["a0"], w["k_k"], w["k_a"], w["r_k"], w["gn_g"], w["gn_b"], w["wa_lora"], w["g_lora"]]
    in_specs = ([pl.BlockSpec((n_seq, tc, SHIFT_W), lambda bi, t: (bi, t, 0)),
                 pl.BlockSpec((n_seq, 1, SHIFT_W), lambda bi, t: (bi, 0, 0))]
                + [full(c) for c in consts] + [state] + [full(c) for c in vecs])
    vm = lambda shape: pltpu.VMEM(shape, F32)
    assert b % n_seq == 0 and t_len % tc == 0 and tc % CHUNK == 0
    scratch = [vm((n_seq, N_PAIRS, PAIR_W, PAIR_W)), vm((n_seq, 1, SHIFT_W))] + [vm((n_seq, 8, CHUNK, RWKV_W))] * 2
    return pl.pallas_call(
        functools.partial(_rwkv_body, t_valid=t_valid),
        grid=(b // n_seq, t_len // tc),
        in_specs=in_specs,
        out_specs=[pl.BlockSpec((n_seq, tc, RWKV_W), lambda bi, t: (bi, t, 0)), state],
        out_shape=[jax.ShapeDtypeStruct((b, t_len, RWKV_W), BF16),
                   jax.ShapeDtypeStruct((b, N_PAIRS, PAIR_W, PAIR_W), F32)],
        scratch_shapes=scratch,
        compiler_params=_params(("parallel", "arbitrary")),
        name="rwkv",
    )(zb, shift0, *consts, s0_bd, *vecs)


def _state_to_pairs(s):
    b = s.shape[0]
    s = s.reshape(b, N_PAIRS, 2, RWKV_N, RWKV_N)
    z = jnp.zeros_like(s[:, :, 0])
    top = jnp.concatenate([s[:, :, 0], z], axis=-1)
    bot = jnp.concatenate([z, s[:, :, 1]], axis=-1)
    return jnp.concatenate([top, bot], axis=-2)


def _pairs_to_state(s_bd):
    b = s_bd.shape[0]
    h0 = s_bd[:, :, :RWKV_N, :RWKV_N]
    h1 = s_bd[:, :, RWKV_N:, RWKV_N:]
    return jnp.stack([h0, h1], axis=2).reshape(b, RWKV_HEADS, RWKV_N, RWKV_N)


def _out_proj_body(x_ref, attn_ref, rwkv_ref, gate_ref, wpa_ref, wpb_ref, wout_ref, n2_ref,
                   wup_ref, wdn_ref, nf_ref, y_ref):
    pa = _dg(attn_ref[...], wpa_ref[...])
    pb = _dg(rwkv_ref[...], wpb_ref[...])
    merged = gate_ref[:, 0:D_MODEL] * pa + gate_ref[:, D_MODEL:GATE_W] * pb
    x1 = x_ref[...] + _dg(merged.astype(BF16), wout_ref[...])
    hm = _rms(x1, n2_ref[...]).astype(BF16)
    acc = x1
    for c in range(0, D_FF, 1024):
        up = jnp.maximum(_dg(hm, wup_ref[:, c:c + 1024]), 0.0)
        acc = acc + _dg((up * up).astype(BF16), wdn_ref[c:c + 1024, :])
    y_ref[...] = _rms(acc, nf_ref[...])


def _out_proj(x, attn, rwkv, gate, w, tm):
    m = x.shape[0]
    row = lambda wd: pl.BlockSpec((tm, wd), lambda i: (i, 0))
    full = lambda a: pl.BlockSpec(a.shape, lambda i: (0,) * a.ndim)
    consts = [w["w_proj_a"], w["w_proj_b"], w["w_out"], w["norm2_g"], w["w_up"], w["w_down"], w["normf_g"]]
    return pl.pallas_call(
        _out_proj_body,
        grid=(m // tm,),
        in_specs=[row(D_MODEL), row(GROUP_W), row(RWKV_W), row(GATE_W)] + [full(c) for c in consts],
        out_specs=row(D_MODEL),
        out_shape=jax.ShapeDtypeStruct((m, D_MODEL), F32),
        compiler_params=_params(("parallel",)),
        name="out_proj",
    )(x, attn, rwkv, gate, *consts)


def _kv_rows(qkv, group, rows):
    b, s, _ = qkv.shape
    k0 = ATTN_W + group * GROUP_W
    v0 = 2 * ATTN_W + group * GROUP_W
    k = qkv[:, s - rows:, k0:k0 + GROUP_W].reshape(b, rows, HG, HEAD_DIM)
    v = qkv[:, s - rows:, v0:v0 + GROUP_W].reshape(b, rows, HG, HEAD_DIM)
    return jnp.stack([k, v], axis=2)


def _layer_weights(l, norm1_g, w_in, b_gate, mu_shift, w0, w_lora_up, a0, a_lora_up, g_lora_up, k_k, k_a,
                   r_k, gn_g, gn_b, w_proj_a, w_proj_b, w_out, norm2_g, w_up, w_down, normf_g):
    row = lambda a: a.reshape(1, -1)
    zero = jnp.zeros_like(w_lora_up[l])
    wa_lora = jnp.concatenate([jnp.concatenate([w_lora_up[l], zero], axis=1),
                               jnp.concatenate([zero, a_lora_up[l]], axis=1)], axis=0)
    return dict(
        norm1_g=row(norm1_g[l]), w_in=w_in[l].astype(BF16), b_gate=row(b_gate[l]), mu_shift=row(mu_shift[l]),
        w0=row(w0[l]), a0=row(a0[l]), k_k=row(k_k[l]), k_a=row(k_a[l]), r_k=row(r_k[l]),
        gn_g=row(gn_g[l]), gn_b=row(gn_b[l]),
        wa_lora=wa_lora.astype(BF16), g_lora=g_lora_up[l].astype(BF16),
        w_proj_a=w_proj_a[l].astype(BF16), w_proj_b=w_proj_b[l].astype(BF16), w_out=w_out[l].astype(BF16),
        norm2_g=row(norm2_g[l]), w_up=w_up[l].astype(BF16), w_down=w_down[l].astype(BF16),
        normf_g=row(normf_g))


def _prompt_layer(x, w, tm_in, tm_out, tc):
    b, s, _ = x.shape
    x2 = x.reshape(b * s, D_MODEL)
    qkv, zb, gate, *kv_t = _in_proj(x2, w["norm1_g"], w["w_in"], w["b_gate"], tm_in, seq_len=s)
    qkv = qkv.reshape(b, s, QKV_W)
    zb = zb.reshape(b, s, SHIFT_W)
    attn = _attn_prompt(qkv)
    rwkv, s_bd = _rwkv(zb, jnp.zeros((b, 1, SHIFT_W), F32), jnp.zeros((b, N_PAIRS, PAIR_W, PAIR_W), F32), w, tc,
                       RWKV_SEQS_PER_STEP)
    y = _out_proj(x2, attn.reshape(b * s, GROUP_W), rwkv.reshape(b * s, RWKV_W), gate, w, tm_out)
    kvs = [jnp.transpose(t.reshape(b, 2, HG, HEAD_DIM, t.shape[-1]), (0, 4, 1, 2, 3)) for t in kv_t]
    return y.reshape(b, s, D_MODEL), kvs, _pairs_to_state(s_bd), zb[:, -1]


def _sample_layer(x, caches, s0, shift0, w):
    b, t_len, _ = x.shape
    assert t_len == 1
    x2 = x.reshape(b, D_MODEL)
    qkv, zb, gate = _in_proj(x2, w["norm1_g"], w["w_in"], w["b_gate"], b)
    attn = _attn_sample(qkv, caches)
    zb_pad = jnp.pad(zb[:, None, :], ((0, 0), (0, CHUNK - 1), (0, 0)))
    rwkv, s_bd = _rwkv(zb_pad, shift0[:, None, :], _state_to_pairs(s0), w, CHUNK, RWKV_SEQS_PER_STEP, t_valid=1)
    y = _out_proj(x2, attn, rwkv[:, 0], gate, w, b)
    kvs = [_kv_rows(qkv[:, None, :], g, 1) for g in range(len(DILATIONS))]
    return y.reshape(b, 1, D_MODEL), kvs, _pairs_to_state(s_bd), zb


def kernel(x_prompt, x_sample, cache_kv_w128, cache_kv_w512, cache_kv_w2048, state_wkv, state_shift, norm1_g, w_in, b_gate, mu_shift, w0, w_lora_up, a0, a_lora_up, g_lora_up, k_k, k_a, r_k, gn_g, gn_b, w_proj_a, w_proj_b, w_out, norm2_g, w_up, w_down, normf_g):
    depth = w_in.shape[0]
    assert depth == 1, "the final norm is fused into the layer's output stage"
    w = _layer_weights(0, norm1_g, w_in, b_gate, mu_shift, w0, w_lora_up, a0, a_lora_up, g_lora_up, k_k, k_a,
                       r_k, gn_g, gn_b, w_proj_a, w_proj_b, w_out, norm2_g, w_up, w_down, normf_g)
    y_p, kv_p, wkv_p, shift_p = _prompt_layer(x_prompt, w, 512, 512, 512)
    y_s, kv_s, wkv_s, shift_s = _sample_layer(
        x_sample, (cache_kv_w128[0], cache_kv_w512[0], cache_kv_w2048[0]), state_wkv[0], state_shift[0], w)
    lead = lambda a: a[None]
    return (y_p, y_s, lead(kv_p[0]), lead(kv_p[1]), lead(kv_p[2]), lead(wkv_p), lead(shift_p),
            lead(kv_s[0]), lead(kv_s[1]), lead(kv_s[2]), lead(wkv_s), lead(shift_s))
```

```python
import functools

import numpy as np
import jax
import jax.numpy as jnp
from jax import lax
from jax.experimental import pallas as pl
from jax.experimental.pallas import tpu as pltpu

F32 = jnp.float32
BF16 = jnp.bfloat16

D_MODEL = 1024
HEAD_DIM = 64
HG = 4
DILATIONS = (1, 4, 16)
BAND = 128
N_ATTN_HEADS = HG * len(DILATIONS)
GROUP_W = HG * HEAD_DIM
LANE_HALVES = GROUP_W // 128
ATTN_W = N_ATTN_HEADS * HEAD_DIM
QKV_W = 3 * ATTN_W
RWKV_N = 64
RWKV_W = 512
RWKV_HEADS = RWKV_W // RWKV_N
PAIR_W = 2 * RWKV_N
N_PAIRS = RWKV_HEADS // 2
LORA_W = 256
SHIFT_W = 3 * RWKV_W + LORA_W
GATE_W = 2 * D_MODEL
IN_W = QKV_W + SHIFT_W + GATE_W
D_FF = 4 * D_MODEL
NORM_EPS = 1e-6
GN_EPS = 64e-5
NEG_INF = -1e30
ATTN_SCALE = HEAD_DIM ** -0.5
CHUNK = 64
RWKV_SEQS_PER_STEP = 2
RWKV_TILE = 512
PROMPT_TILE = 512
DECAY_SCALE = float(np.exp(-0.5))
SLOPES = [float(s) for s in np.exp2(-8.0 * np.arange(1, N_ATTN_HEADS + 1, dtype=np.float32) / N_ATTN_HEADS)]

V7X_VMEM_LIMIT = 56 * 1024 * 1024

NN = (((1,), (0,)), ((), ()))
NT = (((1,), (1,)), ((), ()))
TN = (((0,), (0,)), ((), ()))


def _dg(a, b, dims=NN):
    return lax.dot_general(a, b, dims, preferred_element_type=F32)


def _split2(a):
    hi = a.astype(BF16)
    lo = (a - hi.astype(F32)).astype(BF16)
    return hi, lo


def _dot3(a, b, dims=NN):
    ah, al = _split2(a)
    bh, bl = _split2(b)
    return _dg(ah, bh, dims) + _dg(al, bh, dims) + _dg(ah, bl, dims)


def _dot_exact_rhs(a, b_bf16, passes):
    out = None
    rem = a
    for _ in range(passes):
        part = rem.astype(BF16)
        term = _dg(part, b_bf16)
        out = term if out is None else out + term
        rem = rem - part.astype(F32)
    return out


def _sigmoid(x):
    return 0.5 * jnp.tanh(0.5 * x) + 0.5


def _rms(x, g):
    return x * lax.rsqrt(jnp.mean(x * x, axis=-1, keepdims=True) + NORM_EPS) * g


def _params(sem):
    return pltpu.CompilerParams(dimension_semantics=sem, vmem_limit_bytes=V7X_VMEM_LIMIT)


def _kv_tail_plan(seq_len, tm):
    plan = []
    for dil in DILATIONS:
        rows = min(BAND * dil, seq_len)
        width = min(rows, tm)
        plan.append((rows, width, (seq_len - rows) // tm))
    return plan


def _in_proj_body(x_ref, g_ref, w_ref, bg_ref, qkv_ref, zb_ref, gate_ref, *kv_refs, seq_len):
    tm = x_ref.shape[0]
    h = _rms(x_ref[...], g_ref[...]).astype(BF16)
    for c in range(0, GATE_W, 1024):
        zg = _dg(h, w_ref[:, QKV_W + SHIFT_W + c:QKV_W + SHIFT_W + c + 1024])
        gate_ref[:, c:c + 1024] = _sigmoid(zg + bg_ref[:, c:c + 1024])
    for c in range(0, QKV_W, 768):
        qkv_ref[:, c:c + 768] = _dg(h, w_ref[:, c:c + 768])

    def write_tail(g, kv_ref, width):
        for sec in (1, 2):
            c0 = sec * ATTN_W + g * GROUP_W
            kv_ref[sec - 1] = qkv_ref[tm - width:tm, c0:c0 + GROUP_W].T

    plan = _kv_tail_plan(seq_len, tm) if kv_refs else []
    for g, (kv_ref, (_, width, first)) in enumerate(zip(kv_refs, plan)):
        if first == 0:
            write_tail(g, kv_ref, width)
    for c in range(0, SHIFT_W, 896):
        zb_ref[:, c:c + 896] = _dg(h, w_ref[:, QKV_W + c:QKV_W + c + 896])
    for g, (kv_ref, (_, width, first)) in enumerate(zip(kv_refs, plan)):
        if first > 0:
            tile = pl.program_id(0) % (seq_len // tm)
            pl.when(tile >= first)(functools.partial(write_tail, g, kv_ref, width))


def _in_proj(x, norm_g, w_in_bf16, b_gate, tm, seq_len=None):
    m = x.shape[0]
    row = lambda w: pl.BlockSpec((tm, w), lambda i: (i, 0))
    full = lambda a: pl.BlockSpec(a.shape, lambda i: (0,) * a.ndim)
    out_specs = [row(QKV_W), row(SHIFT_W), row(GATE_W)]
    out_shape = [jax.ShapeDtypeStruct((m, QKV_W), F32),
                 jax.ShapeDtypeStruct((m, SHIFT_W), F32),
                 jax.ShapeDtypeStruct((m, GATE_W), F32)]
    if seq_len is not None:
        tiles = seq_len // tm
        for rows, width, first in _kv_tail_plan(seq_len, tm):
            out_specs.append(pl.BlockSpec(
                (None, 2, GROUP_W, width),
                lambda i, first=first: (i // tiles, 0, 0, jnp.maximum(i % tiles - first, 0))))
            out_shape.append(jax.ShapeDtypeStruct((m // seq_len, 2, GROUP_W, rows), F32))
    return pl.pallas_call(
        functools.partial(_in_proj_body, seq_len=seq_len),
        grid=(m // tm,),
        in_specs=[row(D_MODEL), full(norm_g), full(w_in_bf16), full(b_gate)],
        out_specs=out_specs,
        out_shape=out_shape,
        compiler_params=_params(("arbitrary",)),
        name="in_proj",
    )(x, norm_g, w_in_bf16, b_gate)


def _residue_rows(r, count, dil):
    return pl.ds(r, count) if dil == 1 else pl.ds(r, count, stride=dil)


def _attn_group_blocks(q_ref, k_ref, v_ref, o_scr, lse_scr, qs, ks, vs, os_, ls, bias_scr, group):
    s_len = qs.shape[0]
    dil = DILATIONS[group]
    l_res = s_len // dil
    nb = l_res // BAND
    for r in range(dil):
        dst = pl.ds(r * l_res, l_res)
        src = _residue_rows(r, l_res, dil)
        stage = lambda halves: jnp.concatenate([h[src, :] for h in halves], axis=1)
        qs[dst, :] = (stage(q_ref) * ATTN_SCALE).astype(BF16)
        ks[dst, :] = stage(k_ref).astype(BF16)
        vs[dst, :] = stage(v_ref).astype(BF16)
    has_prev = nb > 1
    nk = 2 * BAND if has_prev else BAND
    qi = lax.broadcasted_iota(jnp.int32, (BAND, nk), 0)
    kj = lax.broadcasted_iota(jnp.int32, (BAND, nk), 1)
    delta = (nk - BAND) + qi - kj
    band = (delta >= 0) & (delta <= BAND)
    dist = (delta * dil).astype(F32)
    for h in range(HG):
        alibi = -SLOPES[group * HG + h] * dist
        bias_scr[h, :, 0:nk] = jnp.where(band, alibi, NEG_INF)
        bias_scr[HG + h, :, 0:nk] = jnp.where(band & (kj >= nk - BAND), alibi, NEG_INF)
    head = lax.broadcasted_iota(jnp.int32, (1, GROUP_W), 1) // HEAD_DIM
    hm = [head == h for h in range(HG)]
    blocks_per_iter = 2

    def block_pair(it, carry):
        cur, q, k2, v2, first = [], [], [], [], []
        for u in range(blocks_per_iter):
            idx = it * blocks_per_iter + u
            cur.append(pl.ds(pl.multiple_of(idx * BAND, BAND), BAND))
            q.append(qs[cur[u], :])
            k2.append(ks[cur[u], :])
            v2.append(vs[cur[u], :])
            first.append(jnp.where(idx % nb == 0, HG, 0))
            if has_prev:
                prev = pl.ds(pl.multiple_of(jnp.maximum(idx - 1, 0) * BAND, BAND), BAND)
                k2[u] = jnp.concatenate([ks[prev, :], k2[u]], axis=0)
                v2[u] = jnp.concatenate([vs[prev, :], v2[u]], axis=0)
        chains = [(u, h) for u in range(blocks_per_iter) for h in range(HG)]
        s = [_dg(jnp.where(hm[h], q[u], jnp.zeros_like(q[u])), k2[u], NT) + bias_scr[first[u] + h, :, 0:nk]
             for u, h in chains]
        m = [jnp.max(x, axis=-1, keepdims=True) for x in s]
        p = [jnp.exp(x - mx) for x, mx in zip(s, m)]
        l = [jnp.sum(x, axis=-1, keepdims=True) for x in p]
        pv = [_dg(p[i].astype(BF16), v2[u]) for i, (u, h) in enumerate(chains)]
        for u in range(blocks_per_iter):
            o_acc = jnp.zeros((BAND, GROUP_W), F32)
            lse_acc = jnp.zeros((BAND, GROUP_W), F32)
            for h in range(HG):
                i = u * HG + h
                o_acc = jnp.where(hm[h], pv[i] / l[i], o_acc)
                lse_acc = jnp.where(hm[h], m[i] + jnp.log(l[i]), lse_acc)
            os_[cur[u], :] = o_acc
            ls[cur[u], :] = lse_acc
        return carry

    assert (dil * nb) % blocks_per_iter == 0
    lax.fori_loop(0, dil * nb // blocks_per_iter, block_pair, 0)
    for r in range(dil):
        src = pl.ds(r * l_res, l_res)
        dst = _residue_rows(r, l_res, dil)
        for half in range(LANE_HALVES):
            cols = slice(half * 128, (half + 1) * 128)
            o_scr[group * LANE_HALVES + half, dst, :] = os_[src, cols]
            lse_scr[group * LANE_HALVES + half, dst, :] = ls[src, cols]


def _attn_prompt_body(q0_ref, q1_ref, k0_ref, k1_ref, v0_ref, v1_ref, o_ref, o_scr, lse_scr, qs, ks, vs, os_, ls, bias_scr):
    q_ref, k_ref, v_ref = (q0_ref, q1_ref), (k0_ref, k1_ref), (v0_ref, v1_ref)
    gid = pl.program_id(1)
    n_groups = len(DILATIONS)
    for g in range(n_groups):
        @pl.when(gid == g)
        def _(g=g):
            _attn_group_blocks(q_ref, k_ref, v_ref, o_scr, lse_scr, qs, ks, vs, os_, ls, bias_scr, g)

    @pl.when(gid == n_groups - 1)
    def _():
        def merge(i, carry):
            rows = pl.ds(pl.multiple_of(i * BAND, BAND), BAND)
            both = lambda ref, g: jnp.concatenate(
                [ref[g * LANE_HALVES + half, rows, :] for half in range(LANE_HALVES)], axis=1)
            lse = [both(lse_scr, g) for g in range(n_groups)]
            m = functools.reduce(jnp.maximum, lse)
            wts = [jnp.exp(x - m) for x in lse]
            num = sum(wts[g] * both(o_scr, g) for g in range(n_groups))
            o_ref[rows, :] = (num / sum(wts)).astype(o_ref.dtype)
            return carry

        lax.fori_loop(0, o_ref.shape[0] // BAND, merge, 0)


def _attn_prompt(qkv):
    b, s, _ = qkv.shape
    n_groups = len(DILATIONS)
    assert s % (DILATIONS[-1] * BAND) == 0
    sec = ATTN_W // 128

    def col(section, half):
        return pl.BlockSpec((None, s, 128), lambda bi, g: (bi, 0, section * sec + g * LANE_HALVES + half))

    return pl.pallas_call(
        _attn_prompt_body,
        grid=(b, n_groups),
        in_specs=[col(section, half) for section in range(3) for half in range(LANE_HALVES)],
        out_specs=pl.BlockSpec((None, s, GROUP_W), lambda bi, g: (bi, 0, 0)),
        out_shape=jax.ShapeDtypeStruct((b, s, GROUP_W), BF16),
        scratch_shapes=([pltpu.VMEM((n_groups * LANE_HALVES, s, 128), F32)] * 2
                        + [pltpu.VMEM((s, GROUP_W), BF16)] * 3 + [pltpu.VMEM((s, GROUP_W), F32)] * 2
                        + [pltpu.VMEM((2 * HG, BAND, 2 * BAND), F32)]),
        compiler_params=_params(("parallel", "arbitrary")),
        name="attn_prompt",
    )(*([qkv] * (3 * LANE_HALVES)))


def _attn_sample_body(qkv_ref, c0_ref, c1_ref, c2_ref, o_ref):
    for job in _attn_sample_jobs(qkv_ref, c0_ref, c1_ref, c2_ref, o_ref):
        job()


def _attn_sample_jobs(qkv_ref, c0_ref, c1_ref, c2_ref, o_ref):
    rows = 8
    row = lax.broadcasted_iota(jnp.int32, (rows, GROUP_W), 0)
    own = lax.broadcasted_iota(jnp.int32, (rows, GROUP_W), 1) // HEAD_DIM == row
    hrow = lax.broadcasted_iota(jnp.int32, (HG, 1), 0)
    lane_head = lax.broadcasted_iota(jnp.int32, (1, GROUP_W), 1) // HEAD_DIM
    c_refs = (c0_ref, c1_ref, c2_ref)
    groups = range(len(DILATIONS))
    st = [dict() for _ in groups]

    def to_column(x_row):
        return jnp.sum(jnp.where(own, x_row, 0.0).T, axis=1, keepdims=True)

    def per_head(col):
        out = jnp.zeros((1, GROUP_W), F32)
        for h in range(HG):
            out = jnp.where(lane_head == h, col[h:h + 1, :], out)
        return out

    def scores():
        for g in groups:
            lb = c_refs[g].shape[2]
            col = lambda sec: qkv_ref[:, sec * ATTN_W + g * GROUP_W:sec * ATTN_W + (g + 1) * GROUP_W]
            q, kn, vn = col(0), col(1), col(2)
            prod = c_refs[g][0] * to_column(q)
            s_b = jnp.sum(prod.reshape(HG, HEAD_DIM, lb), axis=1) * ATTN_SCALE
            s_n = jnp.sum(jnp.where(own, q * kn, 0.0), axis=-1, keepdims=True)[0:HG] * ATTN_SCALE
            st[g].update(vn=vn, s_b=s_b, s_n=s_n)

    def values():
        for g in groups:
            dil = DILATIONS[g]
            lb = c_refs[g].shape[2]
            slope = jnp.zeros((HG, 1), F32)
            for h in range(HG):
                slope = jnp.where(hrow == h, SLOPES[g * HG + h], slope)
            t = lax.broadcasted_iota(jnp.int32, (HG, lb), 1)
            s_b = jnp.where(t % dil == 0, st[g]["s_b"] - slope * (lb - t).astype(F32), NEG_INF)
            m = jnp.maximum(jnp.max(s_b, axis=-1, keepdims=True), st[g]["s_n"])
            p_b = jnp.exp(s_b - m)
            p_n = jnp.exp(st[g]["s_n"] - m)
            p_rows = jnp.broadcast_to(p_b[:, None, :], (HG, HEAD_DIM, lb)).reshape(GROUP_W, lb)
            acc = jnp.sum(p_rows * c_refs[g][1], axis=1, keepdims=True)
            st[g].update(m=m, p_n=p_n, l=jnp.sum(p_b, axis=-1, keepdims=True) + p_n, acc=acc)

    def merge():
        m_all = functools.reduce(jnp.maximum, [s["m"] for s in st])
        sc = [jnp.exp(s["m"] - m_all) for s in st]
        den = sum(c * s["l"] for c, s in zip(sc, st))
        out = jnp.zeros((1, GROUP_W), F32)
        for c, s in zip(sc, st):
            acc_row = jnp.broadcast_to(s["acc"], (GROUP_W, rows)).T[0:1, :]
            out = out + per_head(c / den) * (acc_row + per_head(s["p_n"]) * s["vn"])
        o_ref[...] = out.astype(o_ref.dtype)

    return [scores, values, merge]


def _attn_sample_operands(qkv, caches):
    b = qkv.shape[0]
    args = [qkv.reshape(b, 1, QKV_W)]
    specs = [pl.BlockSpec((None, 1, QKV_W), lambda i: (i, 0, 0))]
    for g, c in enumerate(caches):
        lb = c.shape[1]
        assert lb == BAND * DILATIONS[g]
        args.append(jnp.transpose(c, (0, 2, 3, 4, 1)).reshape(b, 2, GROUP_W, lb))
        specs.append(pl.BlockSpec((None, 2, GROUP_W, lb), lambda i: (i, 0, 0, 0)))
    out_spec = pl.BlockSpec((None, 1, GROUP_W), lambda i: (i, 0, 0))
    return args, specs, out_spec, jax.ShapeDtypeStruct((b, 1, GROUP_W), BF16)


def _attn_sample(qkv, caches):
    args, specs, out_spec, out_shape = _attn_sample_operands(qkv, caches)
    out = pl.pallas_call(
        _attn_sample_body,
        grid=(qkv.shape[0],),
        in_specs=specs,
        out_specs=out_spec,
        out_shape=out_shape,
        compiler_params=_params(("parallel",)),
        name="attn_sample",
    )(*args)
    return out.reshape(-1, GROUP_W)


WKV_PASSES = dict(g=1, s0=1, x=1, inv=1, y=1, s1=1)


def _mm(a, b, dims, passes):
    if passes == 1:
        return _dg(a.astype(BF16), b.astype(BF16), dims)
    ah, al = _split2(a)
    if passes == 2:
        bh = b.astype(BF16)
        return _dg(ah, bh, dims) + _dg(al, bh, dims)
    bh, bl = _split2(b)
    return _dg(ah, bh, dims) + _dg(al, bh, dims) + _dg(ah, bl, dims)


def _wkv_levels(live_rows):
    return int(np.ceil(np.log2(live_rows))) if live_rows > 1 else 0


def _wkv_chunk(operands, states, levels, side=()):
    side = list(side)

    def run_side():
        if side:
            side.pop(0)()
    c = CHUNK
    n2 = 2 * c
    n_seq = len(operands)
    ti = lax.broadcasted_iota(jnp.int32, (c, c), 0)
    tj = lax.broadcasted_iota(jnp.int32, (c, c), 1)
    tri = (ti >= tj).astype(BF16)
    first = lax.broadcasted_iota(jnp.int32, (1, PAIR_W), 1) < RWKV_N

    def stack(x, p):
        x = x[:, p * PAIR_W:(p + 1) * PAIR_W]
        return jnp.concatenate([jnp.where(first, x, 0.0), jnp.where(first, 0.0, x)], axis=0)

    ar, bk, v_s, s0, w_end = [], [], [], [], []
    for (r, kp, v, logw, av, bv), seq_states in zip(operands, states):
        lc = _dot_exact_rhs_left(tri, logw)
        e_in = jnp.exp(lc)
        e_neg = jnp.exp(-lc)
        a_t = av * jnp.exp(lc - logw)
        r_t = r * e_in
        b_t = bv * e_neg
        k_t = kp * e_neg
        for p in range(N_PAIRS):
            ar.append(jnp.concatenate([stack(a_t, p), stack(r_t, p)], axis=0))
            bk.append(jnp.concatenate([stack(b_t, p), stack(k_t, p)], axis=0))
            v_s.append(stack(v, p))
            s0.append(seq_states[p])
            w_end.append(e_in[c - 1:c, p * PAIR_W:(p + 1) * PAIR_W])
    chains = range(len(ar))
    g = [_mm(ar[i], bk[i], NT, WKV_PASSES["g"]) for i in chains]
    ar_s0 = [_mm(ar[i], s0[i], NT, WKV_PASSES["s0"]) for i in chains]
    run_side()
    ri = lax.broadcasted_iota(jnp.int32, (n2, n2), 0)
    ci = lax.broadcasted_iota(jnp.int32, (n2, n2), 1)
    strict = ri > ci
    incl = ri >= ci
    n_ab = [jnp.where(strict, g[i][0:n2, 0:n2], 0.0) for i in chains]
    n_ak = [jnp.where(strict, g[i][0:n2, n2:2 * n2], 0.0) for i in chains]
    m_r = [jnp.where(jnp.concatenate([incl, incl], axis=1), g[i][n2:2 * n2, :], 0.0) for i in chains]
    z = [ar_s0[i][0:n2] + _mm(n_ak[i], v_s[i], NN, WKV_PASSES["x"]) for i in chains]
    pw = n_ab
    for lvl in range(levels):
        if lvl < levels - 1:
            pz = [_mm(pw[i], jnp.concatenate([pw[i], z[i]], axis=1), NN, WKV_PASSES["inv"]) for i in chains]
            pw = [pz[i][:, 0:n2] for i in chains]
            z = [z[i] + pz[i][:, n2:2 * n2] for i in chains]
        else:
            z = [z[i] + _mm(pw[i], z[i], NN, WKV_PASSES["inv"]) for i in chains]
        if lvl % 2 == 1:
            run_side()
    while side:
        run_side()
    uv = [jnp.concatenate([z[i], v_s[i]], axis=0) for i in chains]
    y_s = [ar_s0[i][n2:2 * n2] + _mm(m_r[i], uv[i], NN, WKV_PASSES["y"]) for i in chains]
    s1 = [(s0[i] + _mm(uv[i], bk[i], TN, WKV_PASSES["s1"])) * w_end[i] for i in chains]
    ys = [jnp.concatenate([y_s[j * N_PAIRS + p][0:c] + y_s[j * N_PAIRS + p][c:n2] for p in range(N_PAIRS)], axis=1)
          for j in range(n_seq)]
    new_states = tuple(tuple(s1[j * N_PAIRS:(j + 1) * N_PAIRS]) for j in range(n_seq))
    return ys, new_states


def _dot_exact_rhs_left(lhs_bf16, b):
    out = None
    rem = b
    for _ in range(3):
        part = rem.astype(BF16)
        term = _dg(lhs_bf16, part)
        out = term if out is None else out + term
        rem = rem - part.astype(F32)
    return out


def _rwkv_body(zb_ref, sh_ref, mu_ref, s0_ref,
               w0_ref, a0_ref, kk_ref, ka_ref, rk_ref, gg_ref, gb_ref, wwa_ref, wg_ref,
               out_ref, s_out_ref,
               s_scr, carry_scr, ring0_scr, ring1_scr,
               *, t_valid):
    t = pl.program_id(1)
    n_seq, tc, _ = zb_ref.shape

    @pl.when(t == 0)
    def _():
        s_scr[...] = s0_ref[...]
        carry_scr[...] = sh_ref[...]

    lr = lax.broadcasted_iota(jnp.int32, (2 * PAIR_W, 2 * PAIR_W), 0)
    lc = lax.broadcasted_iota(jnp.int32, (2 * PAIR_W, 2 * PAIR_W), 1)
    seg = (lr // RWKV_N == lc // RWKV_N).astype(BF16)
    half = lax.broadcasted_iota(jnp.int32, (1, 128), 1) < 64
    row = lax.broadcasted_iota(jnp.int32, (CHUNK, 1), 0)
    inv_n = 1.0 / RWKV_N
    levels = _wkv_levels(CHUNK if t_valid is None else min(t_valid, CHUNK))
    short = tc < CHUNK

    def head_sum(x):
        wide = 2 * PAIR_W
        return jnp.concatenate(
            [_dg(x[:, c:c + wide].astype(BF16), seg) for c in range(0, RWKV_W, wide)], axis=1)

    n_chunks = max(tc // CHUNK, 1)
    rings = (ring0_scr, ring1_scr)

    def staging_jobs(ci, ring):
        rows = pl.ds(pl.multiple_of(ci * CHUNK, CHUNK), CHUNK)
        st = [dict() for _ in range(n_seq)]

        def mix_and_lora():
            for j in range(n_seq):
                if short:
                    z = jnp.where(row == 0, zb_ref[j], 0.0)
                    before = carry_scr[j]
                else:
                    z = zb_ref[j, rows, :]
                    before = zb_ref[j, pl.ds(jnp.maximum(ci * CHUNK - 1, 0), 1), :]
                    before = jnp.where(ci == 0, carry_scr[j], before)
                zp = jnp.where(row == 0, before, pltpu.roll(z, 1, 0))
                zm = z + (zp - z) * mu_ref[...]
                z_wa = zm[:, 3 * RWKV_W:3 * RWKV_W + 128]
                z_g = zm[:, 3 * RWKV_W + 128:SHIFT_W]
                x_wa = jnp.where(half, jnp.tanh(z_wa), z_wa).astype(BF16)
                st[j].update(r=zm[:, 0:RWKV_W], k=zm[:, RWKV_W:2 * RWKV_W], v=zm[:, 2 * RWKV_W:3 * RWKV_W],
                             u=_dg(x_wa, wwa_ref[:, 0:RWKV_W]), la=_dg(x_wa, wwa_ref[:, RWKV_W:2 * RWKV_W]),
                             gate=_dg(_sigmoid(z_g).astype(BF16), wg_ref[...]))

        def head_sums():
            for j in range(n_seq):
                s = st[j]
                a = _sigmoid(a0_ref[...] + s["la"])
                kk = s["k"] * kk_ref[...]
                kp = s["k"] * (1.0 + (a - 1.0) * ka_ref[...])
                s.update(a=a, kk=kk, kp=kp, kk2=head_sum(kk * kk), rk=head_sum(s["r"] * kp * rk_ref[...]))

        def finish():
            for j in range(n_seq):
                s = st[j]
                logw = -DECAY_SCALE * _sigmoid(w0_ref[...] + s["u"])
                kk = s["kk"] * lax.rsqrt(jnp.maximum(s["kk2"], 1e-24))
                ops = (s["r"], s["kp"], s["v"], logw, -kk, kk * s["a"], s["rk"] * s["v"])
                if t_valid is not None:
                    live = row + ci * CHUNK < t_valid
                    ops = tuple(jnp.where(live, x, 0.0) for x in ops)
                for i, x in enumerate(ops + (s["gate"],)):
                    ring[j, i] = x

        return mix_and_lora, head_sums, finish

    def chunk(ci, states, ring, other):
        side = staging_jobs(jnp.minimum(ci + 1, n_chunks - 1), other) if n_chunks > 1 else ()
        ys, states = _wkv_chunk([tuple(ring[j, i] for i in range(6)) for j in range(n_seq)], states, levels, side)
        for j in range(n_seq):
            mean = head_sum(ys[j]) * inv_n
            d = ys[j] - mean
            var = head_sum(d * d) * inv_n
            yn = d * lax.rsqrt(var + GN_EPS) * gg_ref[...] + gb_ref[...]
            res = (yn + ring[j, 6]) * ring[j, 7]
            if short:
                out_ref[j] = res[0:tc, :].astype(out_ref.dtype)
            else:
                out_ref[j, pl.ds(pl.multiple_of(ci * CHUNK, CHUNK), CHUNK), :] = res.astype(out_ref.dtype)
        return states

    for job in staging_jobs(0, rings[0]):
        job()
    states = tuple(tuple(s_scr[j, p] for p in range(N_PAIRS)) for j in range(n_seq))
    if n_chunks == 1:
        states = chunk(0, states, rings[0], rings[1])
    else:
        assert n_chunks % 2 == 0

        def chunk_pair(i, states):
            states = chunk(2 * i, states, rings[0], rings[1])
            return chunk(2 * i + 1, states, rings[1], rings[0])

        states = lax.fori_loop(0, n_chunks // 2, chunk_pair, states)
    for j in range(n_seq):
        carry_scr[j] = zb_ref[j, tc - 1:tc, :]
        for p in range(N_PAIRS):
            s_scr[j, p] = states[j][p]
            s_out_ref[j, p] = states[j][p]


def _rwkv(zb, shift0, s0_bd, w, tc, n_seq, t_valid=None):
    b, t_len, _ = zb.shape
    full = lambda a: pl.BlockSpec(a.shape, lambda bi, t: (0,) * a.ndim)
    state = pl.BlockSpec((n_seq, N_PAIRS, PAIR_W, PAIR_W), lambda bi, t: (bi, 0, 0, 0))
    consts = [w["mu_shift"]]
    vecs = [w["w0"], w["a0"], w["k_k"], w["k_a"], w["r_k"], w["gn_g"], w["gn_b"], w["wa_lora"], w["g_lora"]]
    in_specs = ([pl.BlockSpec((n_seq, tc, SHIFT_W), lambda bi, t: (bi, t, 0)),
                 pl.BlockSpec((n_seq, 1, SHIFT_W), lambda bi, t: (bi, 0, 0))]
                + [full(c) for c in consts] + [state] + [full(c) for c in vecs])
    vm = lambda shape: pltpu.VMEM(shape, F32)
    assert b % n_seq == 0 and t_len % tc == 0
    assert tc % CHUNK == 0 or (tc == t_len == t_valid == 1), "whole chunks, or single-token sequences"
    scratch = [vm((n_seq, N_PAIRS, PAIR_W, PAIR_W)), vm((n_seq, 1, SHIFT_W))] + [vm((n_seq, 8, CHUNK, RWKV_W))] * 2
    return pl.pallas_call(
        functools.partial(_rwkv_body, t_valid=t_valid),
        grid=(b // n_seq, t_len // tc),
        in_specs=in_specs,
        out_specs=[pl.BlockSpec((n_seq, tc, RWKV_W), lambda bi, t: (bi, t, 0)), state],
        out_shape=[jax.ShapeDtypeStruct((b, t_len, RWKV_W), BF16),
                   jax.ShapeDtypeStruct((b, N_PAIRS, PAIR_W, PAIR_W), F32)],
        scratch_shapes=scratch,
        compiler_params=_params(("parallel", "arbitrary")),
        name="rwkv",
    )(zb, shift0, *consts, s0_bd, *vecs)


def _state_to_pairs(s):
    b = s.shape[0]
    s = s.reshape(b, N_PAIRS, 2, RWKV_N, RWKV_N)
    z = jnp.zeros_like(s[:, :, 0])
    top = jnp.concatenate([s[:, :, 0], z], axis=-1)
    bot = jnp.concatenate([z, s[:, :, 1]], axis=-1)
    return jnp.concatenate([top, bot], axis=-2)


def _pairs_to_state(s_bd):
    b = s_bd.shape[0]
    h0 = s_bd[:, :, :RWKV_N, :RWKV_N]
    h1 = s_bd[:, :, RWKV_N:, RWKV_N:]
    return jnp.stack([h0, h1], axis=2).reshape(b, RWKV_HEADS, RWKV_N, RWKV_N)


def _out_proj_body(x_ref, attn_ref, rwkv_ref, gate_ref, wpa_ref, wpb_ref, wout_ref, n2_ref,
                   wup_ref, wdn_ref, nf_ref, *refs):
    rider = []
    if len(refs) > 1:
        qkv_s_ref, c0_ref, c1_ref, c2_ref, y_ref, attn_s_ref = refs
        rider = _attn_sample_jobs(qkv_s_ref, c0_ref, c1_ref, c2_ref, attn_s_ref)
    else:
        y_ref, = refs
    pa = _dg(attn_ref[...], wpa_ref[...])
    pb = _dg(rwkv_ref[...], wpb_ref[...])
    merged = gate_ref[:, 0:D_MODEL] * pa + gate_ref[:, D_MODEL:GATE_W] * pb
    x1 = x_ref[...] + _dg(merged.astype(BF16), wout_ref[...])
    hm = _rms(x1, n2_ref[...]).astype(BF16)
    acc = x1
    for c in range(0, D_FF, 1024):
        up = jnp.maximum(_dg(hm, wup_ref[:, c:c + 1024]), 0.0)
        if rider:
            rider.pop(0)()
        acc = acc + _dg((up * up).astype(BF16), wdn_ref[c:c + 1024, :])
    y_ref[...] = _rms(acc, nf_ref[...])


def _out_proj(x, attn, rwkv, gate, w, tm, rider=None):
    m = x.shape[0]
    row = lambda wd: pl.BlockSpec((tm, wd), lambda i: (i, 0))
    full = lambda a: pl.BlockSpec(a.shape, lambda i: (0,) * a.ndim)
    consts = [w["w_proj_a"], w["w_proj_b"], w["w_out"], w["norm2_g"], w["w_up"], w["w_down"], w["normf_g"]]
    in_specs = [row(D_MODEL), row(GROUP_W), row(RWKV_W), row(GATE_W)] + [full(c) for c in consts]
    args = [x, attn, rwkv, gate, *consts]
    out_specs = [row(D_MODEL)]
    out_shape = [jax.ShapeDtypeStruct((m, D_MODEL), F32)]
    if rider is not None:
        r_args, r_specs, r_out_spec, r_out_shape = _attn_sample_operands(*rider)
        assert r_out_shape.shape[0] == m // tm
        in_specs += r_specs
        args += r_args
        out_specs.append(r_out_spec)
        out_shape.append(r_out_shape)
    outs = pl.pallas_call(
        _out_proj_body,
        grid=(m // tm,),
        in_specs=in_specs,
        out_specs=out_specs,
        out_shape=out_shape,
        compiler_params=_params(("parallel",)),
        name="out_proj",
    )(*args)
    if rider is None:
        return outs[0]
    return outs[0], outs[1].reshape(-1, GROUP_W)


def _kv_rows(qkv, group, rows):
    b, s, _ = qkv.shape
    k0 = ATTN_W + group * GROUP_W
    v0 = 2 * ATTN_W + group * GROUP_W
    k = qkv[:, s - rows:, k0:k0 + GROUP_W].reshape(b, rows, HG, HEAD_DIM)
    v = qkv[:, s - rows:, v0:v0 + GROUP_W].reshape(b, rows, HG, HEAD_DIM)
    return jnp.stack([k, v], axis=2)


def _layer_weights(l, norm1_g, w_in, b_gate, mu_shift, w0, w_lora_up, a0, a_lora_up, g_lora_up, k_k, k_a,
                   r_k, gn_g, gn_b, w_proj_a, w_proj_b, w_out, norm2_g, w_up, w_down, normf_g):
    row = lambda a: a.reshape(1, -1)
    zero = jnp.zeros_like(w_lora_up[l])
    wa_lora = jnp.concatenate([jnp.concatenate([w_lora_up[l], zero], axis=1),
                               jnp.concatenate([zero, a_lora_up[l]], axis=1)], axis=0)
    return dict(
        norm1_g=row(norm1_g[l]), w_in=w_in[l].astype(BF16), b_gate=row(b_gate[l]), mu_shift=row(mu_shift[l]),
        w0=row(w0[l]), a0=row(a0[l]), k_k=row(k_k[l]), k_a=row(k_a[l]), r_k=row(r_k[l]),
        gn_g=row(gn_g[l]), gn_b=row(gn_b[l]),
        wa_lora=wa_lora.astype(BF16), g_lora=g_lora_up[l].astype(BF16),
        w_proj_a=w_proj_a[l].astype(BF16), w_proj_b=w_proj_b[l].astype(BF16), w_out=w_out[l].astype(BF16),
        norm2_g=row(norm2_g[l]), w_up=w_up[l].astype(BF16), w_down=w_down[l].astype(BF16),
        normf_g=row(normf_g))


def _layer(xp, xs, caches, s0, shift0, w):
    b, s, _ = xp.shape
    bs, t_len, _ = xs.shape
    assert t_len == 1
    xp2 = xp.reshape(b * s, D_MODEL)
    xs2 = xs.reshape(bs, D_MODEL)
    qkv_s, zb_s, gate_s = _in_proj(xs2, w["norm1_g"], w["w_in"], w["b_gate"], bs)
    qkv, zb, gate, *kv_t = _in_proj(xp2, w["norm1_g"], w["w_in"], w["b_gate"], PROMPT_TILE, seq_len=s)
    zb = zb.reshape(b, s, SHIFT_W)
    attn = _attn_prompt(qkv.reshape(b, s, QKV_W))
    rwkv, s_bd = _rwkv(zb, jnp.zeros((b, 1, SHIFT_W), F32), jnp.zeros((b, N_PAIRS, PAIR_W, PAIR_W), F32), w,
                       RWKV_TILE, RWKV_SEQS_PER_STEP)
    attn2, rwkv2 = attn.reshape(b * s, GROUP_W), rwkv.reshape(b * s, RWKV_W)
    if b * s // PROMPT_TILE == bs:
        y_p, attn_s = _out_proj(xp2, attn2, rwkv2, gate, w, PROMPT_TILE, rider=(qkv_s, caches))
    else:
        y_p = _out_proj(xp2, attn2, rwkv2, gate, w, PROMPT_TILE)
        attn_s = _attn_sample(qkv_s, caches)
    rwkv_s, s_bd_s = _rwkv(zb_s[:, None, :], shift0[:, None, :], _state_to_pairs(s0), w, 1, RWKV_SEQS_PER_STEP,
                           t_valid=1)
    y_s = _out_proj(xs2, attn_s, rwkv_s[:, 0], gate_s, w, bs)
    kv_p = [jnp.transpose(t.reshape(b, 2, HG, HEAD_DIM, t.shape[-1]), (0, 4, 1, 2, 3)) for t in kv_t]
    kv_s = [_kv_rows(qkv_s[:, None, :], g, 1) for g in range(len(DILATIONS))]
    prompt = (y_p.reshape(b, s, D_MODEL), kv_p, _pairs_to_state(s_bd), zb[:, -1])
    sample = (y_s.reshape(bs, 1, D_MODEL), kv_s, _pairs_to_state(s_bd_s), zb_s)
    return prompt, sample


def kernel(x_prompt, x_sample, cache_kv_w128, cache_kv_w512, cache_kv_w2048, state_wkv, state_shift, norm1_g, w_in, b_gate, mu_shift, w0, w_lora_up, a0, a_lora_up, g_lora_up, k_k, k_a, r_k, gn_g, gn_b, w_proj_a, w_proj_b, w_out, norm2_g, w_up, w_down, normf_g):
    depth = w_in.shape[0]
    assert depth == 1, "the final norm is fused into the layer's output stage"
    w = _layer_weights(0, norm1_g, w_in, b_gate, mu_shift, w0, w_lora_up, a0, a_lora_up, g_lora_up, k_k, k_a,
                       r_k, gn_g, gn_b, w_proj_a, w_proj_b, w_out, norm2_g, w_up, w_down, normf_g)
    (y_p, kv_p, wkv_p, shift_p), (y_s, kv_s, wkv_s, shift_s) = _layer(
        x_prompt, x_sample, (cache_kv_w128[0], cache_kv_w512[0], cache_kv_w2048[0]), state_wkv[0], state_shift[0], w)
    lead = lambda a: a[None]
    return (y_p, y_s, lead(kv_p[0]), lead(kv_p[1]), lead(kv_p[2]), lead(wkv_p), lead(shift_p),
            lead(kv_s[0]), lead(kv_s[1]), lead(kv_s[2]), lead(wkv_s), lead(shift_s))
```

```python
import functools

import numpy as np
import jax
import jax.numpy as jnp
from jax import lax
from jax.experimental import pallas as pl
from jax.experimental.pallas import tpu as pltpu

F32 = jnp.float32
BF16 = jnp.bfloat16

D_MODEL = 1024
HEAD_DIM = 64
HG = 4
DILATIONS = (1, 4, 16)
BAND = 128
N_ATTN_HEADS = HG * len(DILATIONS)
GROUP_W = HG * HEAD_DIM
LANE_HALVES = GROUP_W // 128
ATTN_W = N_ATTN_HEADS * HEAD_DIM
QKV_W = 3 * ATTN_W
RWKV_N = 64
RWKV_W = 512
RWKV_HEADS = RWKV_W // RWKV_N
PAIR_W = 2 * RWKV_N
N_PAIRS = RWKV_HEADS // 2
LORA_W = 256
SHIFT_W = 3 * RWKV_W + LORA_W
GATE_W = 2 * D_MODEL
IN_W = QKV_W + SHIFT_W + GATE_W
D_FF = 4 * D_MODEL
NORM_EPS = 1e-6
GN_EPS = 64e-5
NEG_INF = -1e30
ATTN_SCALE = HEAD_DIM ** -0.5
CHUNK = 64
RWKV_SEQS_PER_STEP = 4
RWKV_TILE = 256
PROMPT_TILE = 512
DECAY_SCALE = float(np.exp(-0.5))
SLOPES = [float(s) for s in np.exp2(-8.0 * np.arange(1, N_ATTN_HEADS + 1, dtype=np.float32) / N_ATTN_HEADS)]

V7X_VMEM_LIMIT = 56 * 1024 * 1024

NN = (((1,), (0,)), ((), ()))
NT = (((1,), (1,)), ((), ()))
TN = (((0,), (0,)), ((), ()))


def _dg(a, b, dims=NN):
    return lax.dot_general(a, b, dims, preferred_element_type=F32)


def _split2(a):
    hi = a.astype(BF16)
    lo = (a - hi.astype(F32)).astype(BF16)
    return hi, lo


def _dot3(a, b, dims=NN):
    ah, al = _split2(a)
    bh, bl = _split2(b)
    return _dg(ah, bh, dims) + _dg(al, bh, dims) + _dg(ah, bl, dims)


def _dot_exact_rhs(a, b_bf16, passes):
    out = None
    rem = a
    for _ in range(passes):
        part = rem.astype(BF16)
        term = _dg(part, b_bf16)
        out = term if out is None else out + term
        rem = rem - part.astype(F32)
    return out


def _sigmoid(x):
    return 0.5 * jnp.tanh(0.5 * x) + 0.5


def _rms(x, g):
    return x * lax.rsqrt(jnp.mean(x * x, axis=-1, keepdims=True) + NORM_EPS) * g


def _params(sem):
    return pltpu.CompilerParams(dimension_semantics=sem, vmem_limit_bytes=V7X_VMEM_LIMIT)


def _kv_tail_plan(seq_len, tm):
    plan = []
    for dil in DILATIONS:
        rows = min(BAND * dil, seq_len)
        width = min(rows, tm)
        plan.append((rows, width, (seq_len - rows) // tm))
    return plan


def _in_proj_body(x_ref, g_ref, w_ref, bg_ref, qkv_ref, zb_ref, gate_ref, *kv_refs, seq_len):
    tm = x_ref.shape[0]

    def write_tail(g, kv_ref, width):
        for sec in (1, 2):
            c0 = sec * ATTN_W + g * GROUP_W
            kv_ref[sec - 1] = qkv_ref[tm - width:tm, c0:c0 + GROUP_W].T

    plan = _kv_tail_plan(seq_len, tm) if kv_refs else []
    h = _rms(x_ref[...], g_ref[...]).astype(BF16)
    for c in range(0, GATE_W, 1024):
        zg = _dg(h, w_ref[:, QKV_W + SHIFT_W + c:QKV_W + SHIFT_W + c + 1024])
        gate_ref[:, c:c + 1024] = _sigmoid(zg + bg_ref[:, c:c + 1024])
    for c in range(0, QKV_W, 768):
        qkv_ref[:, c:c + 768] = _dg(h, w_ref[:, c:c + 768])
    for g, (kv_ref, (_, width, first)) in enumerate(zip(kv_refs, plan)):
        if first == 0:
            write_tail(g, kv_ref, width)
    for c in range(0, SHIFT_W, 896):
        zb_ref[:, c:c + 896] = _dg(h, w_ref[:, QKV_W + c:QKV_W + c + 896])
    for g, (kv_ref, (_, width, first)) in enumerate(zip(kv_refs, plan)):
        if first > 0:
            tile = pl.program_id(0) % (seq_len // tm)
            pl.when(tile >= first)(functools.partial(write_tail, g, kv_ref, width))


def _in_proj(x, norm_g, w_in_bf16, b_gate, tm, seq_len=None):
    m = x.shape[0]
    row = lambda w: pl.BlockSpec((tm, w), lambda i: (i, 0))
    full = lambda a: pl.BlockSpec(a.shape, lambda i: (0,) * a.ndim)
    out_specs = [row(QKV_W), row(SHIFT_W), row(GATE_W)]
    out_shape = [jax.ShapeDtypeStruct((m, QKV_W), F32),
                 jax.ShapeDtypeStruct((m, SHIFT_W), F32),
                 jax.ShapeDtypeStruct((m, GATE_W), F32)]
    if seq_len is not None:
        tiles = seq_len // tm
        for rows, width, first in _kv_tail_plan(seq_len, tm):
            out_specs.append(pl.BlockSpec(
                (None, 2, GROUP_W, width),
                lambda i, first=first: (i // tiles, 0, 0, jnp.maximum(i % tiles - first, 0))))
            out_shape.append(jax.ShapeDtypeStruct((m // seq_len, 2, GROUP_W, rows), F32))
    return pl.pallas_call(
        functools.partial(_in_proj_body, seq_len=seq_len),
        grid=(m // tm,),
        in_specs=[row(D_MODEL), full(norm_g), full(w_in_bf16), full(b_gate)],
        out_specs=out_specs,
        out_shape=out_shape,
        compiler_params=_params(("arbitrary",)),
        name="in_proj",
    )(x, norm_g, w_in_bf16, b_gate)


def _residue_rows(r, count, dil):
    return pl.ds(r, count) if dil == 1 else pl.ds(r, count, stride=dil)


def _attn_group_blocks(q_ref, k_ref, v_ref, o_scr, lse_scr, qs, ks, vs, os_, ls, bias_scr, tmp, group):
    s_len = qs.shape[0]
    dil = DILATIONS[group]
    l_res = s_len // dil
    nb = l_res // BAND
    inner = 4 if dil > 4 else dil
    outer = dil // inner
    l_in = s_len // inner
    assert dil in (1, inner, inner * outer) and outer <= 4

    def gather(halves, r):
        if outer == 1:
            return jnp.concatenate([h[_residue_rows(r, l_res, dil), :] for h in halves], axis=1)
        c, q4 = r % inner, r // inner
        return jnp.concatenate([tmp[i, pl.ds(c * l_in + q4, l_res, stride=outer), :]
                                for i in range(LANE_HALVES)], axis=1)

    for staged, halves, scale in ((qs, q_ref, ATTN_SCALE), (ks, k_ref, None), (vs, v_ref, None)):
        if outer > 1:
            for i, h in enumerate(halves):
                for c in range(inner):
                    tmp[i, pl.ds(c * l_in, l_in), :] = h[_residue_rows(c, l_in, inner), :]
        for r in range(dil):
            x = gather(halves, r)
            if scale is not None:
                x = x * scale
            staged[pl.ds(r * l_res, l_res), :] = x.astype(BF16)
    has_prev = nb > 1
    nk = 2 * BAND if has_prev else BAND
    qi = lax.broadcasted_iota(jnp.int32, (BAND, nk), 0)
    kj = lax.broadcasted_iota(jnp.int32, (BAND, nk), 1)
    delta = (nk - BAND) + qi - kj
    band = (delta >= 0) & (delta <= BAND)
    dist = (delta * dil).astype(F32)
    for h in range(HG):
        alibi = -SLOPES[group * HG + h] * dist
        bias_scr[h, :, 0:nk] = jnp.where(band, alibi, NEG_INF)
        bias_scr[HG + h, :, 0:nk] = jnp.where(band & (kj >= nk - BAND), alibi, NEG_INF)
    head = lax.broadcasted_iota(jnp.int32, (1, GROUP_W), 1) // HEAD_DIM
    hm = [head == h for h in range(HG)]
    blocks_per_iter = 2

    def block_pair(it, carry):
        cur, q, k2, v2, first = [], [], [], [], []
        for u in range(blocks_per_iter):
            idx = it * blocks_per_iter + u
            cur.append(pl.ds(pl.multiple_of(idx * BAND, BAND), BAND))
            q.append(qs[cur[u], :])
            k2.append(ks[cur[u], :])
            v2.append(vs[cur[u], :])
            first.append(jnp.where(idx % nb == 0, HG, 0))
            if has_prev:
                prev = pl.ds(pl.multiple_of(jnp.maximum(idx - 1, 0) * BAND, BAND), BAND)
                k2[u] = jnp.concatenate([ks[prev, :], k2[u]], axis=0)
                v2[u] = jnp.concatenate([vs[prev, :], v2[u]], axis=0)
        chains = [(u, h) for u in range(blocks_per_iter) for h in range(HG)]
        s = [_dg(jnp.where(hm[h], q[u], jnp.zeros_like(q[u])), k2[u], NT) + bias_scr[first[u] + h, :, 0:nk]
             for u, h in chains]
        m = [jnp.max(x, axis=-1, keepdims=True) for x in s]
        p = [jnp.exp(x - mx) for x, mx in zip(s, m)]
        l = [jnp.sum(x, axis=-1, keepdims=True) for x in p]
        pv = [_dg(p[i].astype(BF16), v2[u]) for i, (u, h) in enumerate(chains)]
        for u in range(blocks_per_iter):
            o_acc = jnp.zeros((BAND, GROUP_W), F32)
            lse_acc = jnp.zeros((BAND, GROUP_W), F32)
            for h in range(HG):
                i = u * HG + h
                o_acc = jnp.where(hm[h], pv[i] / l[i], o_acc)
                lse_acc = jnp.where(hm[h], m[i] + jnp.log(l[i]), lse_acc)
            os_[cur[u], :] = o_acc
            ls[cur[u], :] = lse_acc
        return carry

    assert (dil * nb) % blocks_per_iter == 0
    lax.fori_loop(0, dil * nb // blocks_per_iter, block_pair, 0)
    for res, nat in ((os_, o_scr), (ls, lse_scr)):
        for half in range(LANE_HALVES):
            cols = slice(half * 128, (half + 1) * 128)
            slab = group * LANE_HALVES + half
            for r in range(dil):
                src = pl.ds(r * l_res, l_res)
                if outer == 1:
                    nat[slab, _residue_rows(r, l_res, dil), :] = res[src, cols]
                else:
                    tmp[half, pl.ds((r % inner) * l_in + r // inner, l_res, stride=outer), :] = res[src, cols]
            if outer > 1:
                for c in range(inner):
                    nat[slab, _residue_rows(c, l_in, inner), :] = tmp[half, pl.ds(c * l_in, l_in), :]


def _attn_prompt_body(q0_ref, q1_ref, k0_ref, k1_ref, v0_ref, v1_ref, o_ref, o_scr, lse_scr, qs, ks, vs, os_, ls, bias_scr, tmp):
    q_ref, k_ref, v_ref = (q0_ref, q1_ref), (k0_ref, k1_ref), (v0_ref, v1_ref)
    gid = pl.program_id(1)
    n_groups = len(DILATIONS)
    for g in range(n_groups):
        @pl.when(gid == g)
        def _(g=g):
            _attn_group_blocks(q_ref, k_ref, v_ref, o_scr, lse_scr, qs, ks, vs, os_, ls, bias_scr, tmp, g)

    @pl.when(gid == n_groups - 1)
    def _():
        def merge(i, carry):
            rows = pl.ds(pl.multiple_of(i * BAND, BAND), BAND)
            both = lambda ref, g: jnp.concatenate(
                [ref[g * LANE_HALVES + half, rows, :] for half in range(LANE_HALVES)], axis=1)
            lse = [both(lse_scr, g) for g in range(n_groups)]
            m = functools.reduce(jnp.maximum, lse)
            wts = [jnp.exp(x - m) for x in lse]
            num = sum(wts[g] * both(o_scr, g) for g in range(n_groups))
            o_ref[rows, :] = (num / sum(wts)).astype(o_ref.dtype)
            return carry

        lax.fori_loop(0, o_ref.shape[0] // BAND, merge, 0)


def _attn_prompt(qkv):
    b, s, _ = qkv.shape
    n_groups = len(DILATIONS)
    assert s % (DILATIONS[-1] * BAND) == 0
    sec = ATTN_W // 128

    def col(section, half):
        return pl.BlockSpec((None, s, 128), lambda bi, g: (bi, 0, section * sec + g * LANE_HALVES + half))

    return pl.pallas_call(
        _attn_prompt_body,
        grid=(b, n_groups),
        in_specs=[col(section, half) for section in range(3) for half in range(LANE_HALVES)],
        out_specs=pl.BlockSpec((None, s, GROUP_W), lambda bi, g: (bi, 0, 0)),
        out_shape=jax.ShapeDtypeStruct((b, s, GROUP_W), BF16),
        scratch_shapes=([pltpu.VMEM((n_groups * LANE_HALVES, s, 128), F32)] * 2
                        + [pltpu.VMEM((s, GROUP_W), BF16)] * 3 + [pltpu.VMEM((s, GROUP_W), F32)] * 2
                        + [pltpu.VMEM((2 * HG, BAND, 2 * BAND), F32), pltpu.VMEM((LANE_HALVES, s, 128), F32)]),
        compiler_params=_params(("parallel", "arbitrary")),
        name="attn_prompt",
    )(*([qkv] * (3 * LANE_HALVES)))


def _attn_sample_body(qkv_ref, c0_ref, c1_ref, c2_ref, o_ref):
    for job in _attn_sample_jobs(qkv_ref, c0_ref, c1_ref, c2_ref, o_ref):
        job()


def _attn_sample_jobs(qkv_ref, c0_ref, c1_ref, c2_ref, o_ref):
    rows = 8
    row = lax.broadcasted_iota(jnp.int32, (rows, GROUP_W), 0)
    own = lax.broadcasted_iota(jnp.int32, (rows, GROUP_W), 1) // HEAD_DIM == row
    hrow = lax.broadcasted_iota(jnp.int32, (HG, 1), 0)
    lane_head = lax.broadcasted_iota(jnp.int32, (1, GROUP_W), 1) // HEAD_DIM
    c_refs = (c0_ref, c1_ref, c2_ref)
    groups = range(len(DILATIONS))
    st = [dict() for _ in groups]

    def to_column(x_row):
        return jnp.sum(jnp.where(own, x_row, 0.0).T, axis=1, keepdims=True)

    def per_head(col):
        out = jnp.zeros((1, GROUP_W), F32)
        for h in range(HG):
            out = jnp.where(lane_head == h, col[h:h + 1, :], out)
        return out

    def scores():
        for g in groups:
            lb = c_refs[g].shape[2]
            col = lambda sec: qkv_ref[:, sec * ATTN_W + g * GROUP_W:sec * ATTN_W + (g + 1) * GROUP_W]
            q, kn, vn = col(0), col(1), col(2)
            prod = c_refs[g][0] * to_column(q)
            s_b = jnp.sum(prod.reshape(HG, HEAD_DIM, lb), axis=1) * ATTN_SCALE
            s_n = jnp.sum(jnp.where(own, q * kn, 0.0), axis=-1, keepdims=True)[0:HG] * ATTN_SCALE
            st[g].update(vn=vn, s_b=s_b, s_n=s_n)

    def values():
        for g in groups:
            dil = DILATIONS[g]
            lb = c_refs[g].shape[2]
            slope = jnp.zeros((HG, 1), F32)
            for h in range(HG):
                slope = jnp.where(hrow == h, SLOPES[g * HG + h], slope)
            t = lax.broadcasted_iota(jnp.int32, (HG, lb), 1)
            s_b = jnp.where(t % dil == 0, st[g]["s_b"] - slope * (lb - t).astype(F32), NEG_INF)
            m = jnp.maximum(jnp.max(s_b, axis=-1, keepdims=True), st[g]["s_n"])
            p_b = jnp.exp(s_b - m)
            p_n = jnp.exp(st[g]["s_n"] - m)
            p_rows = jnp.broadcast_to(p_b[:, None, :], (HG, HEAD_DIM, lb)).reshape(GROUP_W, lb)
            acc = jnp.sum(p_rows * c_refs[g][1], axis=1, keepdims=True)
            st[g].update(m=m, p_n=p_n, l=jnp.sum(p_b, axis=-1, keepdims=True) + p_n, acc=acc)

    def merge():
        m_all = functools.reduce(jnp.maximum, [s["m"] for s in st])
        sc = [jnp.exp(s["m"] - m_all) for s in st]
        den = sum(c * s["l"] for c, s in zip(sc, st))
        out = jnp.zeros((1, GROUP_W), F32)
        for c, s in zip(sc, st):
            acc_row = jnp.broadcast_to(s["acc"], (GROUP_W, rows)).T[0:1, :]
            out = out + per_head(c / den) * (acc_row + per_head(s["p_n"]) * s["vn"])
        o_ref[...] = out.astype(o_ref.dtype)

    return [scores, values, merge]


def _attn_sample_operands(qkv, caches):
    b = qkv.shape[0]
    args = [qkv.reshape(b, 1, QKV_W)]
    specs = [pl.BlockSpec((None, 1, QKV_W), lambda i: (i, 0, 0))]
    for g, c in enumerate(caches):
        lb = c.shape[1]
        assert lb == BAND * DILATIONS[g]
        args.append(jnp.transpose(c, (0, 2, 3, 4, 1)).reshape(b, 2, GROUP_W, lb))
        specs.append(pl.BlockSpec((None, 2, GROUP_W, lb), lambda i: (i, 0, 0, 0)))
    out_spec = pl.BlockSpec((None, 1, GROUP_W), lambda i: (i, 0, 0))
    return args, specs, out_spec, jax.ShapeDtypeStruct((b, 1, GROUP_W), BF16)


def _attn_sample(qkv, caches):
    args, specs, out_spec, out_shape = _attn_sample_operands(qkv, caches)
    out = pl.pallas_call(
        _attn_sample_body,
        grid=(qkv.shape[0],),
        in_specs=specs,
        out_specs=out_spec,
        out_shape=out_shape,
        compiler_params=_params(("parallel",)),
        name="attn_sample",
    )(*args)
    return out.reshape(-1, GROUP_W)


WKV_PASSES = dict(g=1, s0=1, x=1, inv=1, y=1, s1=1)


def _mm(a, b, dims, passes):
    if passes == 1:
        return _dg(a.astype(BF16), b.astype(BF16), dims)
    ah, al = _split2(a)
    if passes == 2:
        bh = b.astype(BF16)
        return _dg(ah, bh, dims) + _dg(al, bh, dims)
    bh, bl = _split2(b)
    return _dg(ah, bh, dims) + _dg(al, bh, dims) + _dg(ah, bl, dims)


def _wkv_levels(live_rows):
    return int(np.ceil(np.log2(live_rows))) if live_rows > 1 else 0


def _wkv_chunk(operands, states, levels, side=()):
    side = list(side)

    def run_side():
        if side:
            side.pop(0)()
    c = CHUNK
    n2 = 2 * c
    n_seq = len(operands)
    ti = lax.broadcasted_iota(jnp.int32, (c, c), 0)
    tj = lax.broadcasted_iota(jnp.int32, (c, c), 1)
    tri = (ti >= tj).astype(BF16)
    first = lax.broadcasted_iota(jnp.int32, (1, PAIR_W), 1) < RWKV_N

    def stack(x, p):
        x = x[:, p * PAIR_W:(p + 1) * PAIR_W]
        return jnp.concatenate([jnp.where(first, x, 0.0), jnp.where(first, 0.0, x)], axis=0)

    ar, bk, v_s, s0, w_end = [], [], [], [], []
    for (r, kp, v, logw, av, bv), seq_states in zip(operands, states):
        lc = _dot_exact_rhs_left(tri, logw)
        e_in = jnp.exp(lc)
        e_neg = jnp.exp(-lc)
        a_t = av * jnp.exp(lc - logw)
        r_t = r * e_in
        b_t = bv * e_neg
        k_t = kp * e_neg
        for p in range(N_PAIRS):
            ar.append(jnp.concatenate([stack(a_t, p), stack(r_t, p)], axis=0))
            bk.append(jnp.concatenate([stack(b_t, p), stack(k_t, p)], axis=0))
            v_s.append(stack(v, p))
            s0.append(seq_states[p])
            w_end.append(e_in[c - 1:c, p * PAIR_W:(p + 1) * PAIR_W])
    chains = range(len(ar))
    g = [_mm(ar[i], bk[i], NT, WKV_PASSES["g"]) for i in chains]
    ar_s0 = [_mm(ar[i], s0[i], NT, WKV_PASSES["s0"]) for i in chains]
    run_side()
    ri = lax.broadcasted_iota(jnp.int32, (n2, n2), 0)
    ci = lax.broadcasted_iota(jnp.int32, (n2, n2), 1)
    strict = ri > ci
    incl = ri >= ci
    n_ab = [jnp.where(strict, g[i][0:n2, 0:n2], 0.0) for i in chains]
    n_ak = [jnp.where(strict, g[i][0:n2, n2:2 * n2], 0.0) for i in chains]
    m_r = [jnp.where(jnp.concatenate([incl, incl], axis=1), g[i][n2:2 * n2, :], 0.0) for i in chains]
    z = [ar_s0[i][0:n2] + _mm(n_ak[i], v_s[i], NN, WKV_PASSES["x"]) for i in chains]
    pw = n_ab
    for lvl in range(levels):
        if lvl < levels - 1:
            pz = [_mm(pw[i], jnp.concatenate([pw[i], z[i]], axis=1), NN, WKV_PASSES["inv"]) for i in chains]
            pw = [pz[i][:, 0:n2] for i in chains]
            z = [z[i] + pz[i][:, n2:2 * n2] for i in chains]
        else:
            z = [z[i] + _mm(pw[i], z[i], NN, WKV_PASSES["inv"]) for i in chains]
        if lvl % 2 == 1:
            run_side()
    while side:
        run_side()
    uv = [jnp.concatenate([z[i], v_s[i]], axis=0) for i in chains]
    y_s = [ar_s0[i][n2:2 * n2] + _mm(m_r[i], uv[i], NN, WKV_PASSES["y"]) for i in chains]
    s1 = [(s0[i] + _mm(uv[i], bk[i], TN, WKV_PASSES["s1"])) * w_end[i] for i in chains]
    ys = [jnp.concatenate([y_s[j * N_PAIRS + p][0:c] + y_s[j * N_PAIRS + p][c:n2] for p in range(N_PAIRS)], axis=1)
          for j in range(n_seq)]
    new_states = tuple(tuple(s1[j * N_PAIRS:(j + 1) * N_PAIRS]) for j in range(n_seq))
    return ys, new_states


def _dot_exact_rhs_left(lhs_bf16, b):
    out = None
    rem = b
    for _ in range(3):
        part = rem.astype(BF16)
        term = _dg(lhs_bf16, part)
        out = term if out is None else out + term
        rem = rem - part.astype(F32)
    return out


def _rwkv_body(zb_ref, sh_ref, mu_ref, s0_ref,
               w0_ref, a0_ref, kk_ref, ka_ref, rk_ref, gg_ref, gb_ref, wwa_ref, wg_ref,
               out_ref, s_out_ref,
               s_scr, carry_scr, ring0_scr, ring1_scr,
               *, t_valid):
    t = pl.program_id(1)
    n_seq, tc, _ = zb_ref.shape

    @pl.when(t == 0)
    def _():
        s_scr[...] = s0_ref[...]
        carry_scr[...] = sh_ref[...]

    lr = lax.broadcasted_iota(jnp.int32, (2 * PAIR_W, 2 * PAIR_W), 0)
    lc = lax.broadcasted_iota(jnp.int32, (2 * PAIR_W, 2 * PAIR_W), 1)
    seg = (lr // RWKV_N == lc // RWKV_N).astype(BF16)
    half = lax.broadcasted_iota(jnp.int32, (1, 128), 1) < 64
    row = lax.broadcasted_iota(jnp.int32, (CHUNK, 1), 0)
    inv_n = 1.0 / RWKV_N
    levels = _wkv_levels(CHUNK if t_valid is None else min(t_valid, CHUNK))
    short = tc < CHUNK

    def head_sum(x):
        wide = 2 * PAIR_W
        return jnp.concatenate(
            [_dg(x[:, c:c + wide].astype(BF16), seg) for c in range(0, RWKV_W, wide)], axis=1)

    n_chunks = max(tc // CHUNK, 1)
    rings = (ring0_scr, ring1_scr)

    def staging_jobs(ci, ring):
        rows = pl.ds(pl.multiple_of(ci * CHUNK, CHUNK), CHUNK)
        st = [dict() for _ in range(n_seq)]

        def mix_and_lora():
            for j in range(n_seq):
                if short:
                    z = jnp.where(row == 0, zb_ref[j], 0.0)
                    before = carry_scr[j]
                else:
                    z = zb_ref[j, rows, :]
                    before = zb_ref[j, pl.ds(jnp.maximum(ci * CHUNK - 1, 0), 1), :]
                    before = jnp.where(ci == 0, carry_scr[j], before)
                zp = jnp.where(row == 0, before, pltpu.roll(z, 1, 0))
                zm = z + (zp - z) * mu_ref[...]
                z_wa = zm[:, 3 * RWKV_W:3 * RWKV_W + 128]
                z_g = zm[:, 3 * RWKV_W + 128:SHIFT_W]
                x_wa = jnp.where(half, jnp.tanh(z_wa), z_wa).astype(BF16)
                st[j].update(r=zm[:, 0:RWKV_W], k=zm[:, RWKV_W:2 * RWKV_W], v=zm[:, 2 * RWKV_W:3 * RWKV_W],
                             u=_dg(x_wa, wwa_ref[:, 0:RWKV_W]), la=_dg(x_wa, wwa_ref[:, RWKV_W:2 * RWKV_W]),
                             gate=_dg(_sigmoid(z_g).astype(BF16), wg_ref[...]))

        def head_sums():
            for j in range(n_seq):
                s = st[j]
                a = _sigmoid(a0_ref[...] + s["la"])
                kk = s["k"] * kk_ref[...]
                kp = s["k"] * (1.0 + (a - 1.0) * ka_ref[...])
                s.update(a=a, kk=kk, kp=kp, kk2=head_sum(kk * kk), rk=head_sum(s["r"] * kp * rk_ref[...]))

        def finish():
            for j in range(n_seq):
                s = st[j]
                logw = -DECAY_SCALE * _sigmoid(w0_ref[...] + s["u"])
                kk = s["kk"] * lax.rsqrt(jnp.maximum(s["kk2"], 1e-24))
                ops = (s["r"], s["kp"], s["v"], logw, -kk, kk * s["a"], s["rk"] * s["v"])
                if t_valid is not None:
                    live = row + ci * CHUNK < t_valid
                    ops = tuple(jnp.where(live, x, 0.0) for x in ops)
                for i, x in enumerate(ops + (s["gate"],)):
                    ring[j, i] = x

        return mix_and_lora, head_sums, finish

    def chunk(ci, states, ring, other):
        side = staging_jobs(jnp.minimum(ci + 1, n_chunks - 1), other) if n_chunks > 1 else ()
        ys, states = _wkv_chunk([tuple(ring[j, i] for i in range(6)) for j in range(n_seq)], states, levels, side)
        for j in range(n_seq):
            mean = head_sum(ys[j]) * inv_n
            d = ys[j] - mean
            var = head_sum(d * d) * inv_n
            yn = d * lax.rsqrt(var + GN_EPS) * gg_ref[...] + gb_ref[...]
            res = (yn + ring[j, 6]) * ring[j, 7]
            if short:
                out_ref[j] = res[0:tc, :].astype(out_ref.dtype)
            else:
                out_ref[j, pl.ds(pl.multiple_of(ci * CHUNK, CHUNK), CHUNK), :] = res.astype(out_ref.dtype)
        return states

    for job in staging_jobs(0, rings[0]):
        job()
    states = tuple(tuple(s_scr[j, p] for p in range(N_PAIRS)) for j in range(n_seq))
    if n_chunks == 1:
        states = chunk(0, states, rings[0], rings[1])
    else:
        assert n_chunks % 2 == 0

        def chunk_pair(i, states):
            states = chunk(2 * i, states, rings[0], rings[1])
            return chunk(2 * i + 1, states, rings[1], rings[0])

        states = lax.fori_loop(0, n_chunks // 2, chunk_pair, states)
    for j in range(n_seq):
        carry_scr[j] = zb_ref[j, tc - 1:tc, :]
        for p in range(N_PAIRS):
            s_scr[j, p] = states[j][p]
            s_out_ref[j, p] = states[j][p]


def _rwkv(zb, shift0, s0_bd, w, tc, n_seq, t_valid=None):
    b, t_len, _ = zb.shape
    full = lambda a: pl.BlockSpec(a.shape, lambda bi, t: (0,) * a.ndim)
    state = pl.BlockSpec((n_seq, N_PAIRS, PAIR_W, PAIR_W), lambda bi, t: (bi, 0, 0, 0))
    consts = [w["mu_shift"]]
    vecs = [w["w0"], w["a0"], w["k_k"], w["k_a"], w["r_k"], w["gn_g"], w["gn_b"], w["wa_lora"], w["g_lora"]]
    in_specs = ([pl.BlockSpec((n_seq, tc, SHIFT_W), lambda bi, t: (bi, t, 0)),
                 pl.BlockSpec((n_seq, 1, SHIFT_W), lambda bi, t: (bi, 0, 0))]
                + [full(c) for c in consts] + [state] + [full(c) for c in vecs])
    vm = lambda shape: pltpu.VMEM(shape, F32)
    assert b % n_seq == 0 and t_len % tc == 0
    assert tc % CHUNK == 0 or (tc == t_len == t_valid == 1), "whole chunks, or single-token sequences"
    scratch = [vm((n_seq, N_PAIRS, PAIR_W, PAIR_W)), vm((n_seq, 1, SHIFT_W))] + [vm((n_seq, 8, CHUNK, RWKV_W))] * 2
    return pl.pallas_call(
        functools.partial(_rwkv_body, t_valid=t_valid),
        grid=(b // n_seq, t_len // tc),
        in_specs=in_specs,
        out_specs=[pl.BlockSpec((n_seq, tc, RWKV_W), lambda bi, t: (bi, t, 0)), state],
        out_shape=[jax.ShapeDtypeStruct((b, t_len, RWKV_W), BF16),
                   jax.ShapeDtypeStruct((b, N_PAIRS, PAIR_W, PAIR_W), F32)],
        scratch_shapes=scratch,
        compiler_params=_params(("parallel", "arbitrary")),
        name="rwkv",
    )(zb, shift0, *consts, s0_bd, *vecs)


def _state_to_pairs(s):
    b = s.shape[0]
    s = s.reshape(b, N_PAIRS, 2, RWKV_N, RWKV_N)
    z = jnp.zeros_like(s[:, :, 0])
    top = jnp.concatenate([s[:, :, 0], z], axis=-1)
    bot = jnp.concatenate([z, s[:, :, 1]], axis=-1)
    return jnp.concatenate([top, bot], axis=-2)


def _pairs_to_state(s_bd):
    b = s_bd.shape[0]
    h0 = s_bd[:, :, :RWKV_N, :RWKV_N]
    h1 = s_bd[:, :, RWKV_N:, RWKV_N:]
    return jnp.stack([h0, h1], axis=2).reshape(b, RWKV_HEADS, RWKV_N, RWKV_N)


def _out_proj_body(x_ref, attn_ref, rwkv_ref, gate_ref, wpa_ref, wpb_ref, wout_ref, n2_ref,
                   wup_ref, wdn_ref, nf_ref, *refs):
    rider = []
    if len(refs) > 1:
        qkv_s_ref, c0_ref, c1_ref, c2_ref, y_ref, attn_s_ref = refs
        rider = _attn_sample_jobs(qkv_s_ref, c0_ref, c1_ref, c2_ref, attn_s_ref)
    else:
        y_ref, = refs
    pa = _dg(attn_ref[...], wpa_ref[...])
    pb = _dg(rwkv_ref[...], wpb_ref[...])
    merged = gate_ref[:, 0:D_MODEL] * pa + gate_ref[:, D_MODEL:GATE_W] * pb
    x1 = x_ref[...] + _dg(merged.astype(BF16), wout_ref[...])
    hm = _rms(x1, n2_ref[...]).astype(BF16)
    acc = x1
    for c in range(0, D_FF, 1024):
        up = jnp.maximum(_dg(hm, wup_ref[:, c:c + 1024]), 0.0)
        if rider:
            rider.pop(0)()
        acc = acc + _dg((up * up).astype(BF16), wdn_ref[c:c + 1024, :])
    y_ref[...] = _rms(acc, nf_ref[...])


def _out_proj(x, attn, rwkv, gate, w, tm, rider=None):
    m = x.shape[0]
    row = lambda wd: pl.BlockSpec((tm, wd), lambda i: (i, 0))
    full = lambda a: pl.BlockSpec(a.shape, lambda i: (0,) * a.ndim)
    consts = [w["w_proj_a"], w["w_proj_b"], w["w_out"], w["norm2_g"], w["w_up"], w["w_down"], w["normf_g"]]
    in_specs = [row(D_MODEL), row(GROUP_W), row(RWKV_W), row(GATE_W)] + [full(c) for c in consts]
    args = [x, attn, rwkv, gate, *consts]
    out_specs = [row(D_MODEL)]
    out_shape = [jax.ShapeDtypeStruct((m, D_MODEL), F32)]
    if rider is not None:
        r_args, r_specs, r_out_spec, r_out_shape = _attn_sample_operands(*rider)
        assert r_out_shape.shape[0] == m // tm
        in_specs += r_specs
        args += r_args
        out_specs.append(r_out_spec)
        out_shape.append(r_out_shape)
    outs = pl.pallas_call(
        _out_proj_body,
        grid=(m // tm,),
        in_specs=in_specs,
        out_specs=out_specs,
        out_shape=out_shape,
        compiler_params=_params(("parallel",)),
        name="out_proj",
    )(*args)
    if rider is None:
        return outs[0]
    return outs[0], outs[1].reshape(-1, GROUP_W)


def _kv_rows(qkv, group, rows):
    b, s, _ = qkv.shape
    k0 = ATTN_W + group * GROUP_W
    v0 = 2 * ATTN_W + group * GROUP_W
    k = qkv[:, s - rows:, k0:k0 + GROUP_W].reshape(b, rows, HG, HEAD_DIM)
    v = qkv[:, s - rows:, v0:v0 + GROUP_W].reshape(b, rows, HG, HEAD_DIM)
    return jnp.stack([k, v], axis=2)


def _layer_weights(l, norm1_g, w_in, b_gate, mu_shift, w0, w_lora_up, a0, a_lora_up, g_lora_up, k_k, k_a,
                   r_k, gn_g, gn_b, w_proj_a, w_proj_b, w_out, norm2_g, w_up, w_down, normf_g):
    row = lambda a: a.reshape(1, -1)
    zero = jnp.zeros_like(w_lora_up[l])
    wa_lora = jnp.concatenate([jnp.concatenate([w_lora_up[l], zero], axis=1),
                               jnp.concatenate([zero, a_lora_up[l]], axis=1)], axis=0)
    return dict(
        norm1_g=row(norm1_g[l]), w_in=w_in[l].astype(BF16), b_gate=row(b_gate[l]), mu_shift=row(mu_shift[l]),
        w0=row(w0[l]), a0=row(a0[l]), k_k=row(k_k[l]), k_a=row(k_a[l]), r_k=row(r_k[l]),
        gn_g=row(gn_g[l]), gn_b=row(gn_b[l]),
        wa_lora=wa_lora.astype(BF16), g_lora=g_lora_up[l].astype(BF16),
        w_proj_a=w_proj_a[l].astype(BF16), w_proj_b=w_proj_b[l].astype(BF16), w_out=w_out[l].astype(BF16),
        norm2_g=row(norm2_g[l]), w_up=w_up[l].astype(BF16), w_down=w_down[l].astype(BF16),
        normf_g=row(normf_g))


def _layer(xp, xs, caches, s0, shift0, w):
    b, s, _ = xp.shape
    bs, t_len, _ = xs.shape
    assert t_len == 1
    xp2 = xp.reshape(b * s, D_MODEL)
    xs2 = xs.reshape(bs, D_MODEL)
    qkv_s, zb_s, gate_s = _in_proj(xs2, w["norm1_g"], w["w_in"], w["b_gate"], bs)
    qkv, zb, gate, *kv_t = _in_proj(xp2, w["norm1_g"], w["w_in"], w["b_gate"], PROMPT_TILE, seq_len=s)
    zb = zb.reshape(b, s, SHIFT_W)
    attn = _attn_prompt(qkv.reshape(b, s, QKV_W))
    rwkv, s_bd = _rwkv(zb, jnp.zeros((b, 1, SHIFT_W), F32), jnp.zeros((b, N_PAIRS, PAIR_W, PAIR_W), F32), w,
                       RWKV_TILE, RWKV_SEQS_PER_STEP)
    attn2, rwkv2 = attn.reshape(b * s, GROUP_W), rwkv.reshape(b * s, RWKV_W)
    if b * s // PROMPT_TILE == bs:
        y_p, attn_s = _out_proj(xp2, attn2, rwkv2, gate, w, PROMPT_TILE, rider=(qkv_s, caches))
    else:
        y_p = _out_proj(xp2, attn2, rwkv2, gate, w, PROMPT_TILE)
        attn_s = _attn_sample(qkv_s, caches)
    rwkv_s, s_bd_s = _rwkv(zb_s[:, None, :], shift0[:, None, :], _state_to_pairs(s0), w, 1, RWKV_SEQS_PER_STEP,
                           t_valid=1)
    y_s = _out_proj(xs2, attn_s, rwkv_s[:, 0], gate_s, w, bs)
    kv_p = [jnp.transpose(t.reshape(b, 2, HG, HEAD_DIM, t.shape[-1]), (0, 4, 1, 2, 3)) for t in kv_t]
    kv_s = [_kv_rows(qkv_s[:, None, :], g, 1) for g in range(len(DILATIONS))]
    prompt = (y_p.reshape(b, s, D_MODEL), kv_p, _pairs_to_state(s_bd), zb[:, -1])
    sample = (y_s.reshape(bs, 1, D_MODEL), kv_s, _pairs_to_state(s_bd_s), zb_s)
    return prompt, sample


def kernel(x_prompt, x_sample, cache_kv_w128, cache_kv_w512, cache_kv_w2048, state_wkv, state_shift, norm1_g, w_in, b_gate, mu_shift, w0, w_lora_up, a0, a_lora_up, g_lora_up, k_k, k_a, r_k, gn_g, gn_b, w_proj_a, w_proj_b, w_out, norm2_g, w_up, w_down, normf_g):
    depth = w_in.shape[0]
    assert depth == 1, "the final norm is fused into the layer's output stage"
    w = _layer_weights(0, norm1_g, w_in, b_gate, mu_shift, w0, w_lora_up, a0, a_lora_up, g_lora_up, k_k, k_a,
                       r_k, gn_g, gn_b, w_proj_a, w_proj_b, w_out, norm2_g, w_up, w_down, normf_g)
    (y_p, kv_p, wkv_p, shift_p), (y_s, kv_s, wkv_s, shift_s) = _layer(
        x_prompt, x_sample, (cache_kv_w128[0], cache_kv_w512[0], cache_kv_w2048[0]), state_wkv[0], state_shift[0], w)
    lead = lambda a: a[None]
    return (y_p, y_s, lead(kv_p[0]), lead(kv_p[1]), lead(kv_p[2]), lead(wkv_p), lead(shift_p),
            lead(kv_s[0]), lead(kv_s[1]), lead(kv_s[2]), lead(wkv_s), lead(shift_s))
```

```python
import functools

import numpy as np
import jax
import jax.numpy as jnp
from jax import lax
from jax.experimental import pallas as pl
from jax.experimental.pallas import tpu as pltpu

F32 = jnp.float32
BF16 = jnp.bfloat16

D_MODEL = 1024
HEAD_DIM = 64
HG = 4
DILATIONS = (1, 4, 16)
BAND = 128
N_ATTN_HEADS = HG * len(DILATIONS)
GROUP_W = HG * HEAD_DIM
LANE_HALVES = GROUP_W // 128
ATTN_W = N_ATTN_HEADS * HEAD_DIM
QKV_W = 3 * ATTN_W
RWKV_N = 64
RWKV_W = 512
RWKV_HEADS = RWKV_W // RWKV_N
PAIR_W = 2 * RWKV_N
N_PAIRS = RWKV_HEADS // 2
LORA_W = 256
SHIFT_W = 3 * RWKV_W + LORA_W
GATE_W = 2 * D_MODEL
IN_W = QKV_W + SHIFT_W + GATE_W
D_FF = 4 * D_MODEL
NORM_EPS = 1e-6
GN_EPS = 64e-5
NEG_INF = -1e30
ATTN_SCALE = HEAD_DIM ** -0.5
CHUNK = 64
RWKV_SEQS_PER_STEP = 4
RWKV_TILE = 256
PROMPT_TILE = 512
DECAY_SCALE = float(np.exp(-0.5))
SLOPES = [float(s) for s in np.exp2(-8.0 * np.arange(1, N_ATTN_HEADS + 1, dtype=np.float32) / N_ATTN_HEADS)]

V7X_VMEM_LIMIT = 56 * 1024 * 1024

NN = (((1,), (0,)), ((), ()))
NT = (((1,), (1,)), ((), ()))
TN = (((0,), (0,)), ((), ()))


def _dg(a, b, dims=NN):
    return lax.dot_general(a, b, dims, preferred_element_type=F32)


def _split2(a):
    hi = a.astype(BF16)
    lo = (a - hi.astype(F32)).astype(BF16)
    return hi, lo


def _dot3(a, b, dims=NN):
    ah, al = _split2(a)
    bh, bl = _split2(b)
    return _dg(ah, bh, dims) + _dg(al, bh, dims) + _dg(ah, bl, dims)


def _dot_exact_rhs(a, b_bf16, passes):
    out = None
    rem = a
    for _ in range(passes):
        part = rem.astype(BF16)
        term = _dg(part, b_bf16)
        out = term if out is None else out + term
        rem = rem - part.astype(F32)
    return out


def _sigmoid(x):
    return 0.5 * jnp.tanh(0.5 * x) + 0.5


def _rms(x, g):
    return x * lax.rsqrt(jnp.mean(x * x, axis=-1, keepdims=True) + NORM_EPS) * g


def _params(sem):
    return pltpu.CompilerParams(dimension_semantics=sem, vmem_limit_bytes=V7X_VMEM_LIMIT)


def _kv_tail_plan(seq_len, tm):
    plan = []
    for dil in DILATIONS:
        rows = min(BAND * dil, seq_len)
        width = min(rows, tm)
        plan.append((rows, width, (seq_len - rows) // tm))
    return plan


def _in_proj_body(x_ref, g_ref, w_ref, bg_ref, qkv_ref, zb_ref, gate_ref, *kv_refs, seq_len):
    tm = x_ref.shape[0]

    def write_tail(g, kv_ref, width):
        for sec in (1, 2):
            c0 = sec * ATTN_W + g * GROUP_W
            kv_ref[sec - 1] = qkv_ref[tm - width:tm, c0:c0 + GROUP_W].T

    plan = _kv_tail_plan(seq_len, tm) if kv_refs else []
    h = _rms(x_ref[...], g_ref[...]).astype(BF16)
    for c in range(0, GATE_W, 1024):
        zg = _dg(h, w_ref[:, QKV_W + SHIFT_W + c:QKV_W + SHIFT_W + c + 1024])
        gate_ref[:, c:c + 1024] = _sigmoid(zg + bg_ref[:, c:c + 1024])
    for c in range(0, QKV_W, 768):
        qkv_ref[:, c:c + 768] = _dg(h, w_ref[:, c:c + 768])
    for g, (kv_ref, (_, width, first)) in enumerate(zip(kv_refs, plan)):
        if first == 0:
            write_tail(g, kv_ref, width)
    for c in range(0, SHIFT_W, 896):
        zb_ref[:, c:c + 896] = _dg(h, w_ref[:, QKV_W + c:QKV_W + c + 896])
    for g, (kv_ref, (_, width, first)) in enumerate(zip(kv_refs, plan)):
        if first > 0:
            tile = pl.program_id(0) % (seq_len // tm)
            pl.when(tile >= first)(functools.partial(write_tail, g, kv_ref, width))


def _in_proj(x, norm_g, w_in_bf16, b_gate, tm, seq_len=None):
    m = x.shape[0]
    row = lambda w: pl.BlockSpec((tm, w), lambda i: (i, 0))
    full = lambda a: pl.BlockSpec(a.shape, lambda i: (0,) * a.ndim)
    out_specs = [row(QKV_W), row(SHIFT_W), row(GATE_W)]
    out_shape = [jax.ShapeDtypeStruct((m, QKV_W), F32),
                 jax.ShapeDtypeStruct((m, SHIFT_W), F32),
                 jax.ShapeDtypeStruct((m, GATE_W), F32)]
    if seq_len is not None:
        tiles = seq_len // tm
        for rows, width, first in _kv_tail_plan(seq_len, tm):
            out_specs.append(pl.BlockSpec(
                (None, 2, GROUP_W, width),
                lambda i, first=first: (i // tiles, 0, 0, jnp.maximum(i % tiles - first, 0))))
            out_shape.append(jax.ShapeDtypeStruct((m // seq_len, 2, GROUP_W, rows), F32))
    return pl.pallas_call(
        functools.partial(_in_proj_body, seq_len=seq_len),
        grid=(m // tm,),
        in_specs=[row(D_MODEL), full(norm_g), full(w_in_bf16), full(b_gate)],
        out_specs=out_specs,
        out_shape=out_shape,
        compiler_params=_params(("arbitrary",)),
        name="in_proj",
    )(x, norm_g, w_in_bf16, b_gate)


def _residue_rows(r, count, dil):
    return pl.ds(r, count) if dil == 1 else pl.ds(r, count, stride=dil)


def _attn_group_blocks(q_ref, k_ref, v_ref, o_scr, lse_scr, qs, ks, vs, os_, ls, bias_scr, tmp, group):
    s_len = qs.shape[0]
    dil = DILATIONS[group]
    l_res = s_len // dil
    nb = l_res // BAND
    inner = 4 if dil > 4 else dil
    outer = dil // inner
    l_in = s_len // inner
    assert dil in (1, inner, inner * outer) and outer <= 4

    def gather(halves, r):
        if outer == 1:
            return jnp.concatenate([h[_residue_rows(r, l_res, dil), :] for h in halves], axis=1)
        c, q4 = r % inner, r // inner
        return jnp.concatenate([tmp[i, pl.ds(c * l_in + q4, l_res, stride=outer), :]
                                for i in range(LANE_HALVES)], axis=1)

    head = lax.broadcasted_iota(jnp.int32, (1, GROUP_W), 1) // HEAD_DIM
    hm = [head == h for h in range(HG)]
    for name, halves in (("q", q_ref), ("k", k_ref), ("v", v_ref)):
        if outer > 1:
            for i, h in enumerate(halves):
                for c in range(inner):
                    tmp[i, pl.ds(c * l_in, l_in), :] = h[_residue_rows(c, l_in, inner), :]
        for r in range(dil):
            dst = pl.ds(r * l_res, l_res)
            x = gather(halves, r)
            if name == "q":
                qs[dst, :] = (x * ATTN_SCALE).astype(BF16)
            elif name == "k":
                ks[dst, :] = x.astype(BF16)
            else:
                vs[dst, :] = x.astype(BF16)
    has_prev = nb > 1
    nk = 2 * BAND if has_prev else BAND
    qi = lax.broadcasted_iota(jnp.int32, (BAND, nk), 0)
    kj = lax.broadcasted_iota(jnp.int32, (BAND, nk), 1)
    delta = (nk - BAND) + qi - kj
    band = (delta >= 0) & (delta <= BAND)
    dist = (delta * dil).astype(F32)
    for h in range(HG):
        alibi = -SLOPES[group * HG + h] * dist
        bias_scr[h, :, 0:nk] = jnp.where(band, alibi, NEG_INF)
        bias_scr[HG + h, :, 0:nk] = jnp.where(band & (kj >= nk - BAND), alibi, NEG_INF)
    blocks_per_iter = 2

    def block_pair(it, carry):
        cur, q, k2, v2, first = [], [], [], [], []
        for u in range(blocks_per_iter):
            idx = it * blocks_per_iter + u
            cur.append(pl.ds(pl.multiple_of(idx * BAND, BAND), BAND))
            q.append(qs[cur[u], :])
            k2.append(ks[cur[u], :])
            v2.append(vs[cur[u], :])
            first.append(jnp.where(idx % nb == 0, HG, 0))
            if has_prev:
                prev = pl.ds(pl.multiple_of(jnp.maximum(idx - 1, 0) * BAND, BAND), BAND)
                k2[u] = jnp.concatenate([ks[prev, :], k2[u]], axis=0)
                v2[u] = jnp.concatenate([vs[prev, :], v2[u]], axis=0)
        chains = [(u, h) for u in range(blocks_per_iter) for h in range(HG)]
        s = [_dg(jnp.where(hm[h], q[u], jnp.zeros_like(q[u])), k2[u], NT) + bias_scr[first[u] + h, :, 0:nk]
             for u, h in chains]
        m = [jnp.max(x, axis=-1, keepdims=True) for x in s]
        p = [jnp.exp(x - mx) for x, mx in zip(s, m)]
        l = [jnp.sum(x, axis=-1, keepdims=True) for x in p]
        pv = [_dg(p[i].astype(BF16), v2[u]) for i, (u, h) in enumerate(chains)]
        for u in range(blocks_per_iter):
            o_acc = jnp.zeros((BAND, GROUP_W), F32)
            lse_acc = jnp.zeros((BAND, GROUP_W), F32)
            for h in range(HG):
                i = u * HG + h
                o_acc = jnp.where(hm[h], pv[i] / l[i], o_acc)
                lse_acc = jnp.where(hm[h], m[i] + jnp.log(l[i]), lse_acc)
            if dil == 1:
                for half in range(LANE_HALVES):
                    cols = slice(half * 128, (half + 1) * 128)
                    o_scr[group * LANE_HALVES + half, cur[u], :] = o_acc[:, cols]
                    lse_scr[group * LANE_HALVES + half, cur[u], :] = lse_acc[:, cols]
            else:
                os_[cur[u], :] = o_acc
                ls[cur[u], :] = lse_acc
        return carry

    assert (dil * nb) % blocks_per_iter == 0
    lax.fori_loop(0, dil * nb // blocks_per_iter, block_pair, 0)
    for res, nat in ((os_, o_scr), (ls, lse_scr)) if dil > 1 else ():
        for half in range(LANE_HALVES):
            cols = slice(half * 128, (half + 1) * 128)
            slab = group * LANE_HALVES + half
            for r in range(dil):
                src = pl.ds(r * l_res, l_res)
                if outer == 1:
                    nat[slab, _residue_rows(r, l_res, dil), :] = res[src, cols]
                else:
                    tmp[half, pl.ds((r % inner) * l_in + r // inner, l_res, stride=outer), :] = res[src, cols]
            if outer > 1:
                for c in range(inner):
                    nat[slab, _residue_rows(c, l_in, inner), :] = tmp[half, pl.ds(c * l_in, l_in), :]


def _attn_prompt_body(q0_ref, q1_ref, k0_ref, k1_ref, v0_ref, v1_ref, o_ref, o_scr, lse_scr, qs, ks, vs, os_, ls, bias_scr, tmp):
    q_ref, k_ref, v_ref = (q0_ref, q1_ref), (k0_ref, k1_ref), (v0_ref, v1_ref)
    gid = pl.program_id(1)
    n_groups = len(DILATIONS)
    for g in range(n_groups):
        @pl.when(gid == g)
        def _(g=g):
            _attn_group_blocks(q_ref, k_ref, v_ref, o_scr, lse_scr, qs, ks, vs, os_, ls, bias_scr, tmp, g)

    @pl.when(gid == n_groups - 1)
    def _():
        def merge(i, carry):
            rows = pl.ds(pl.multiple_of(i * BAND, BAND), BAND)
            both = lambda ref, g: jnp.concatenate(
                [ref[g * LANE_HALVES + half, rows, :] for half in range(LANE_HALVES)], axis=1)
            lse = [both(lse_scr, g) for g in range(n_groups)]
            m = functools.reduce(jnp.maximum, lse)
            wts = [jnp.exp(x - m) for x in lse]
            num = sum(wts[g] * both(o_scr, g) for g in range(n_groups))
            o_ref[rows, :] = (num / sum(wts)).astype(o_ref.dtype)
            return carry

        lax.fori_loop(0, o_ref.shape[0] // BAND, merge, 0)


def _attn_prompt(qkv):
    b, s, _ = qkv.shape
    n_groups = len(DILATIONS)
    assert s % (DILATIONS[-1] * BAND) == 0
    sec = ATTN_W // 128

    def col(section, half):
        return pl.BlockSpec((None, s, 128), lambda bi, g: (bi, 0, section * sec + g * LANE_HALVES + half))

    return pl.pallas_call(
        _attn_prompt_body,
        grid=(b, n_groups),
        in_specs=[col(section, half) for section in range(3) for half in range(LANE_HALVES)],
        out_specs=pl.BlockSpec((None, s, GROUP_W), lambda bi, g: (bi, 0, 0)),
        out_shape=jax.ShapeDtypeStruct((b, s, GROUP_W), BF16),
        scratch_shapes=([pltpu.VMEM((n_groups * LANE_HALVES, s, 128), F32)] * 2
                        + [pltpu.VMEM((s, GROUP_W), BF16)] * 3 + [pltpu.VMEM((s, GROUP_W), F32)] * 2
                        + [pltpu.VMEM((2 * HG, BAND, 2 * BAND), F32), pltpu.VMEM((LANE_HALVES, s, 128), F32)]),
        compiler_params=_params(("parallel", "arbitrary")),
        name="attn_prompt",
    )(*([qkv] * (3 * LANE_HALVES)))


def _attn_sample_body(qkv_ref, c0_ref, c1_ref, c2_ref, o_ref):
    for job in _attn_sample_jobs(qkv_ref, c0_ref, c1_ref, c2_ref, o_ref):
        job()


def _attn_sample_jobs(qkv_ref, c0_ref, c1_ref, c2_ref, o_ref):
    rows = 8
    row = lax.broadcasted_iota(jnp.int32, (rows, GROUP_W), 0)
    own = lax.broadcasted_iota(jnp.int32, (rows, GROUP_W), 1) // HEAD_DIM == row
    hrow = lax.broadcasted_iota(jnp.int32, (HG, 1), 0)
    lane_head = lax.broadcasted_iota(jnp.int32, (1, GROUP_W), 1) // HEAD_DIM
    c_refs = (c0_ref, c1_ref, c2_ref)
    groups = range(len(DILATIONS))
    st = [dict() for _ in groups]

    def to_column(x_row):
        return jnp.sum(jnp.where(own, x_row, 0.0).T, axis=1, keepdims=True)

    def per_head(col):
        out = jnp.zeros((1, GROUP_W), F32)
        for h in range(HG):
            out = jnp.where(lane_head == h, col[h:h + 1, :], out)
        return out

    def scores():
        for g in groups:
            lb = c_refs[g].shape[2]
            col = lambda sec: qkv_ref[:, sec * ATTN_W + g * GROUP_W:sec * ATTN_W + (g + 1) * GROUP_W]
            q, kn, vn = col(0), col(1), col(2)
            prod = c_refs[g][0] * to_column(q)
            s_b = jnp.sum(prod.reshape(HG, HEAD_DIM, lb), axis=1) * ATTN_SCALE
            s_n = jnp.sum(jnp.where(own, q * kn, 0.0), axis=-1, keepdims=True)[0:HG] * ATTN_SCALE
            st[g].update(vn=vn, s_b=s_b, s_n=s_n)

    def values():
        for g in groups:
            dil = DILATIONS[g]
            lb = c_refs[g].shape[2]
            slope = jnp.zeros((HG, 1), F32)
            for h in range(HG):
                slope = jnp.where(hrow == h, SLOPES[g * HG + h], slope)
            t = lax.broadcasted_iota(jnp.int32, (HG, lb), 1)
            s_b = jnp.where(t % dil == 0, st[g]["s_b"] - slope * (lb - t).astype(F32), NEG_INF)
            m = jnp.maximum(jnp.max(s_b, axis=-1, keepdims=True), st[g]["s_n"])
            p_b = jnp.exp(s_b - m)
            p_n = jnp.exp(st[g]["s_n"] - m)
            p_rows = jnp.broadcast_to(p_b[:, None, :], (HG, HEAD_DIM, lb)).reshape(GROUP_W, lb)
            acc = jnp.sum(p_rows * c_refs[g][1], axis=1, keepdims=True)
            st[g].update(m=m, p_n=p_n, l=jnp.sum(p_b, axis=-1, keepdims=True) + p_n, acc=acc)

    def merge():
        m_all = functools.reduce(jnp.maximum, [s["m"] for s in st])
        sc = [jnp.exp(s["m"] - m_all) for s in st]
        den = sum(c * s["l"] for c, s in zip(sc, st))
        out = jnp.zeros((1, GROUP_W), F32)
        for c, s in zip(sc, st):
            acc_row = jnp.broadcast_to(s["acc"], (GROUP_W, rows)).T[0:1, :]
            out = out + per_head(c / den) * (acc_row + per_head(s["p_n"]) * s["vn"])
        o_ref[...] = out.astype(o_ref.dtype)

    return [scores, values, merge]


def _attn_sample_operands(qkv, caches):
    b = qkv.shape[0]
    args = [qkv.reshape(b, 1, QKV_W)]
    specs = [pl.BlockSpec((None, 1, QKV_W), lambda i: (i, 0, 0))]
    for g, c in enumerate(caches):
        lb = c.shape[1]
        assert lb == BAND * DILATIONS[g]
        args.append(jnp.transpose(c, (0, 2, 3, 4, 1)).reshape(b, 2, GROUP_W, lb))
        specs.append(pl.BlockSpec((None, 2, GROUP_W, lb), lambda i: (i, 0, 0, 0)))
    out_spec = pl.BlockSpec((None, 1, GROUP_W), lambda i: (i, 0, 0))
    return args, specs, out_spec, jax.ShapeDtypeStruct((b, 1, GROUP_W), BF16)


def _attn_sample(qkv, caches):
    args, specs, out_spec, out_shape = _attn_sample_operands(qkv, caches)
    out = pl.pallas_call(
        _attn_sample_body,
        grid=(qkv.shape[0],),
        in_specs=specs,
        out_specs=out_spec,
        out_shape=out_shape,
        compiler_params=_params(("parallel",)),
        name="attn_sample",
    )(*args)
    return out.reshape(-1, GROUP_W)


WKV_PASSES = dict(g=1, s0=1, x=1, inv=1, y=1, s1=1)


def _mm(a, b, dims, passes):
    if passes == 1:
        return _dg(a.astype(BF16), b.astype(BF16), dims)
    ah, al = _split2(a)
    if passes == 2:
        bh = b.astype(BF16)
        return _dg(ah, bh, dims) + _dg(al, bh, dims)
    bh, bl = _split2(b)
    return _dg(ah, bh, dims) + _dg(al, bh, dims) + _dg(ah, bl, dims)


def _wkv_levels(live_rows):
    return int(np.ceil(np.log2(live_rows))) if live_rows > 1 else 0


def _wkv_chunk(operands, states, levels, side=()):
    side = list(side)

    def run_side():
        if side:
            side.pop(0)()
    c = CHUNK
    n2 = 2 * c
    n_seq = len(operands)
    first = lax.broadcasted_iota(jnp.int32, (1, PAIR_W), 1) < RWKV_N

    def stack(x, p):
        x = x[:, p * PAIR_W:(p + 1) * PAIR_W]
        return jnp.concatenate([jnp.where(first, x, 0.0), jnp.where(first, 0.0, x)], axis=0)

    bi = lax.broadcasted_iota(jnp.int32, (n_seq * c, n_seq * c), 0)
    bj = lax.broadcasted_iota(jnp.int32, (n_seq * c, n_seq * c), 1)
    tri_all = ((bi >= bj) & (bi // c == bj // c)).astype(BF16)
    lc_all = _dot_exact_rhs_left(tri_all, jnp.concatenate([ops[3] for ops in operands], axis=0))
    ar, bk, v_s, s0, w_end = [], [], [], [], []
    for j, ((r, kp, v, logw, av, bv), seq_states) in enumerate(zip(operands, states)):
        lc = lc_all[j * c:(j + 1) * c]
        e_in = jnp.exp(lc)
        e_neg = jnp.exp(-lc)
        a_t = av * jnp.exp(lc - logw)
        r_t = r * e_in
        b_t = bv * e_neg
        k_t = kp * e_neg
        for p in range(N_PAIRS):
            ar.append(jnp.concatenate([stack(a_t, p), stack(r_t, p)], axis=0))
            bk.append(jnp.concatenate([stack(b_t, p), stack(k_t, p)], axis=0))
            v_s.append(stack(v, p))
            s0.append(seq_states[p])
            w_end.append(e_in[c - 1:c, p * PAIR_W:(p + 1) * PAIR_W])
    chains = range(len(ar))
    g = [_mm(ar[i], bk[i], NT, WKV_PASSES["g"]) for i in chains]
    ar_s0 = [_mm(ar[i], s0[i], NT, WKV_PASSES["s0"]) for i in chains]
    run_side()
    ri = lax.broadcasted_iota(jnp.int32, (n2, n2), 0)
    ci = lax.broadcasted_iota(jnp.int32, (n2, n2), 1)
    strict = ri > ci
    incl = ri >= ci
    n_ab = [jnp.where(strict, g[i][0:n2, 0:n2], 0.0) for i in chains]
    n_ak = [jnp.where(strict, g[i][0:n2, n2:2 * n2], 0.0) for i in chains]
    m_r = [jnp.where(jnp.concatenate([incl, incl], axis=1), g[i][n2:2 * n2, :], 0.0) for i in chains]
    z = [ar_s0[i][0:n2] + _mm(n_ak[i], v_s[i], NN, WKV_PASSES["x"]) for i in chains]
    pw = n_ab
    for lvl in range(levels):
        if lvl < levels - 1:
            pz = [_mm(pw[i], jnp.concatenate([pw[i], z[i]], axis=1), NN, WKV_PASSES["inv"]) for i in chains]
            pw = [pz[i][:, 0:n2] for i in chains]
            z = [z[i] + pz[i][:, n2:2 * n2] for i in chains]
        else:
            z = [z[i] + _mm(pw[i], z[i], NN, WKV_PASSES["inv"]) for i in chains]
        if lvl % 2 == 1:
            run_side()
    while side:
        run_side()
    uv = [jnp.concatenate([z[i], v_s[i]], axis=0) for i in chains]
    y_s = [ar_s0[i][n2:2 * n2] + _mm(m_r[i], uv[i], NN, WKV_PASSES["y"]) for i in chains]
    s1 = [(s0[i] + _mm(uv[i], bk[i], TN, WKV_PASSES["s1"])) * w_end[i] for i in chains]
    ys = [jnp.concatenate([y_s[j * N_PAIRS + p][0:c] + y_s[j * N_PAIRS + p][c:n2] for p in range(N_PAIRS)], axis=1)
          for j in range(n_seq)]
    new_states = tuple(tuple(s1[j * N_PAIRS:(j + 1) * N_PAIRS]) for j in range(n_seq))
    return ys, new_states


def _dot_exact_rhs_left(lhs_bf16, b):
    out = None
    rem = b
    for _ in range(3):
        part = rem.astype(BF16)
        term = _dg(lhs_bf16, part)
        out = term if out is None else out + term
        rem = rem - part.astype(F32)
    return out


def _rwkv_body(zb_ref, sh_ref, mu_ref, s0_ref,
               w0_ref, a0_ref, kk_ref, ka_ref, rk_ref, gg_ref, gb_ref, wwa_ref, wg_ref,
               out_ref, s_out_ref,
               s_scr, carry_scr, ring0_scr, ring1_scr,
               *, t_valid):
    t = pl.program_id(1)
    n_seq, tc, _ = zb_ref.shape

    @pl.when(t == 0)
    def _():
        s_scr[...] = s0_ref[...]
        carry_scr[...] = sh_ref[...]

    lr = lax.broadcasted_iota(jnp.int32, (2 * PAIR_W, 2 * PAIR_W), 0)
    lc = lax.broadcasted_iota(jnp.int32, (2 * PAIR_W, 2 * PAIR_W), 1)
    seg = (lr // RWKV_N == lc // RWKV_N).astype(BF16)
    half = lax.broadcasted_iota(jnp.int32, (1, 128), 1) < 64
    row = lax.broadcasted_iota(jnp.int32, (CHUNK, 1), 0)
    inv_n = 1.0 / RWKV_N
    levels = _wkv_levels(CHUNK if t_valid is None else min(t_valid, CHUNK))
    short = tc < CHUNK

    def head_sum(x):
        wide = 2 * PAIR_W
        return jnp.concatenate(
            [_dg(x[:, c:c + wide].astype(BF16), seg) for c in range(0, RWKV_W, wide)], axis=1)

    n_chunks = max(tc // CHUNK, 1)
    rings = (ring0_scr, ring1_scr)

    def staging_jobs(ci, ring):
        rows = pl.ds(pl.multiple_of(ci * CHUNK, CHUNK), CHUNK)
        st = [dict() for _ in range(n_seq)]
        seq_rows = lambda x, j: x[j * CHUNK:(j + 1) * CHUNK]

        def mix_and_lora():
            x_wa, x_g = [], []
            for j in range(n_seq):
                if short:
                    z = jnp.where(row == 0, zb_ref[j], 0.0)
                    before = carry_scr[j]
                else:
                    z = zb_ref[j, rows, :]
                    before = zb_ref[j, pl.ds(jnp.maximum(ci * CHUNK - 1, 0), 1), :]
                    before = jnp.where(ci == 0, carry_scr[j], before)
                zp = jnp.where(row == 0, before, pltpu.roll(z, 1, 0))
                zm = z + (zp - z) * mu_ref[...]
                z_wa = zm[:, 3 * RWKV_W:3 * RWKV_W + 128]
                x_wa.append(jnp.where(half, jnp.tanh(z_wa), z_wa).astype(BF16))
                x_g.append(_sigmoid(zm[:, 3 * RWKV_W + 128:SHIFT_W]).astype(BF16))
                st[j].update(r=zm[:, 0:RWKV_W], k=zm[:, RWKV_W:2 * RWKV_W], v=zm[:, 2 * RWKV_W:3 * RWKV_W])
            x_wa = jnp.concatenate(x_wa, axis=0)
            u = _dg(x_wa, wwa_ref[:, 0:RWKV_W])
            la = _dg(x_wa, wwa_ref[:, RWKV_W:2 * RWKV_W])
            gate = _dg(jnp.concatenate(x_g, axis=0), wg_ref[...])
            for j in range(n_seq):
                st[j].update(u=seq_rows(u, j), la=seq_rows(la, j), gate=seq_rows(gate, j))

        def head_sums():
            for j in range(n_seq):
                s = st[j]
                a = _sigmoid(a0_ref[...] + s["la"])
                kk = s["k"] * kk_ref[...]
                kp = s["k"] * (1.0 + (a - 1.0) * ka_ref[...])
                s.update(a=a, kk=kk, kp=kp)
            kk2 = head_sum(jnp.concatenate([s["kk"] * s["kk"] for s in st], axis=0))
            rk = head_sum(jnp.concatenate([s["r"] * s["kp"] * rk_ref[...] for s in st], axis=0))
            for j in range(n_seq):
                st[j].update(kk2=seq_rows(kk2, j), rk=seq_rows(rk, j))

        def finish():
            for j in range(n_seq):
                s = st[j]
                logw = -DECAY_SCALE * _sigmoid(w0_ref[...] + s["u"])
                kk = s["kk"] * lax.rsqrt(jnp.maximum(s["kk2"], 1e-24))
                ops = (s["r"], s["kp"], s["v"], logw, -kk, kk * s["a"], s["rk"] * s["v"])
                if t_valid is not None:
                    live = row + ci * CHUNK < t_valid
                    ops = tuple(jnp.where(live, x, 0.0) for x in ops)
                for i, x in enumerate(ops + (s["gate"],)):
                    ring[j, i] = x

        return mix_and_lora, head_sums, finish

    def chunk(ci, states, ring, other):
        side = staging_jobs(jnp.minimum(ci + 1, n_chunks - 1), other) if n_chunks > 1 else ()
        ys, states = _wkv_chunk([tuple(ring[j, i] for i in range(6)) for j in range(n_seq)], states, levels, side)
        y_all = jnp.concatenate(ys, axis=0)
        d_all = y_all - head_sum(y_all) * inv_n
        var_all = head_sum(d_all * d_all) * inv_n
        yn_all = d_all * lax.rsqrt(var_all + GN_EPS) * gg_ref[...] + gb_ref[...]
        for j in range(n_seq):
            res = (yn_all[j * CHUNK:(j + 1) * CHUNK] + ring[j, 6]) * ring[j, 7]
            if short:
                out_ref[j] = res[0:tc, :].astype(out_ref.dtype)
            else:
                out_ref[j, pl.ds(pl.multiple_of(ci * CHUNK, CHUNK), CHUNK), :] = res.astype(out_ref.dtype)
        return states

    for job in staging_jobs(0, rings[0]):
        job()
    states = tuple(tuple(s_scr[j, p] for p in range(N_PAIRS)) for j in range(n_seq))
    if n_chunks == 1:
        states = chunk(0, states, rings[0], rings[1])
    else:
        assert n_chunks % 2 == 0

        def chunk_pair(i, states):
            states = chunk(2 * i, states, rings[0], rings[1])
            return chunk(2 * i + 1, states, rings[1], rings[0])

        states = lax.fori_loop(0, n_chunks // 2, chunk_pair, states)
    for j in range(n_seq):
        carry_scr[j] = zb_ref[j, tc - 1:tc, :]
        for p in range(N_PAIRS):
            s_scr[j, p] = states[j][p]
            s_out_ref[j, p] = states[j][p]


def _rwkv(zb, shift0, s0_bd, w, tc, n_seq, t_valid=None):
    b, t_len, _ = zb.shape
    full = lambda a: pl.BlockSpec(a.shape, lambda bi, t: (0,) * a.ndim)
    state = pl.BlockSpec((n_seq, N_PAIRS, PAIR_W, PAIR_W), lambda bi, t: (bi, 0, 0, 0))
    consts = [w["mu_shift"]]
    vecs = [w["w0"], w["a0"], w["k_k"], w["k_a"], w["r_k"], w["gn_g"], w["gn_b"], w["wa_lora"], w["g_lora"]]
    in_specs = ([pl.BlockSpec((n_seq, tc, SHIFT_W), lambda bi, t: (bi, t, 0)),
                 pl.BlockSpec((n_seq, 1, SHIFT_W), lambda bi, t: (bi, 0, 0))]
                + [full(c) for c in consts] + [state] + [full(c) for c in vecs])
    vm = lambda shape: pltpu.VMEM(shape, F32)
    assert b % n_seq == 0 and t_len % tc == 0
    assert tc % CHUNK == 0 or (tc == t_len == t_valid == 1), "whole chunks, or single-token sequences"
    scratch = [vm((n_seq, N_PAIRS, PAIR_W, PAIR_W)), vm((n_seq, 1, SHIFT_W))] + [vm((n_seq, 8, CHUNK, RWKV_W))] * 2
    return pl.pallas_call(
        functools.partial(_rwkv_body, t_valid=t_valid),
        grid=(b // n_seq, t_len // tc),
        in_specs=in_specs,
        out_specs=[pl.BlockSpec((n_seq, tc, RWKV_W), lambda bi, t: (bi, t, 0)), state],
        out_shape=[jax.ShapeDtypeStruct((b, t_len, RWKV_W), BF16),
                   jax.ShapeDtypeStruct((b, N_PAIRS, PAIR_W, PAIR_W), F32)],
        scratch_shapes=scratch,
        compiler_params=_params(("parallel", "arbitrary")),
        name="rwkv",
    )(zb, shift0, *consts, s0_bd, *vecs)


def _state_to_pairs(s):
    b = s.shape[0]
    s = s.reshape(b, N_PAIRS, 2, RWKV_N, RWKV_N)
    z = jnp.zeros_like(s[:, :, 0])
    top = jnp.concatenate([s[:, :, 0], z], axis=-1)
    bot = jnp.concatenate([z, s[:, :, 1]], axis=-1)
    return jnp.concatenate([top, bot], axis=-2)


def _pairs_to_state(s_bd):
    b = s_bd.shape[0]
    h0 = s_bd[:, :, :RWKV_N, :RWKV_N]
    h1 = s_bd[:, :, RWKV_N:, RWKV_N:]
    return jnp.stack([h0, h1], axis=2).reshape(b, RWKV_HEADS, RWKV_N, RWKV_N)


def _out_proj_body(x_ref, attn_ref, rwkv_ref, gate_ref, wpa_ref, wpb_ref, wout_ref, n2_ref,
                   wup_ref, wdn_ref, nf_ref, *refs):
    rider = []
    if len(refs) > 1:
        qkv_s_ref, c0_ref, c1_ref, c2_ref, y_ref, attn_s_ref = refs
        rider = _attn_sample_jobs(qkv_s_ref, c0_ref, c1_ref, c2_ref, attn_s_ref)
    else:
        y_ref, = refs
    pa = _dg(attn_ref[...], wpa_ref[...])
    pb = _dg(rwkv_ref[...], wpb_ref[...])
    merged = gate_ref[:, 0:D_MODEL] * pa + gate_ref[:, D_MODEL:GATE_W] * pb
    x1 = x_ref[...] + _dg(merged.astype(BF16), wout_ref[...])
    hm = _rms(x1, n2_ref[...]).astype(BF16)
    acc = x1
    for c in range(0, D_FF, 1024):
        up = jnp.maximum(_dg(hm, wup_ref[:, c:c + 1024]), 0.0)
        if rider:
            rider.pop(0)()
        acc = acc + _dg((up * up).astype(BF16), wdn_ref[c:c + 1024, :])
    y_ref[...] = _rms(acc, nf_ref[...])


def _out_proj(x, attn, rwkv, gate, w, tm, rider=None):
    m = x.shape[0]
    row = lambda wd: pl.BlockSpec((tm, wd), lambda i: (i, 0))
    full = lambda a: pl.BlockSpec(a.shape, lambda i: (0,) * a.ndim)
    consts = [w["w_proj_a"], w["w_proj_b"], w["w_out"], w["norm2_g"], w["w_up"], w["w_down"], w["normf_g"]]
    in_specs = [row(D_MODEL), row(GROUP_W), row(RWKV_W), row(GATE_W)] + [full(c) for c in consts]
    args = [x, attn, rwkv, gate, *consts]
    out_specs = [row(D_MODEL)]
    out_shape = [jax.ShapeDtypeStruct((m, D_MODEL), F32)]
    if rider is not None:
        r_args, r_specs, r_out_spec, r_out_shape = _attn_sample_operands(*rider)
        assert r_out_shape.shape[0] == m // tm
        in_specs += r_specs
        args += r_args
        out_specs.append(r_out_spec)
        out_shape.append(r_out_shape)
    outs = pl.pallas_call(
        _out_proj_body,
        grid=(m // tm,),
        in_specs=in_specs,
        out_specs=out_specs,
        out_shape=out_shape,
        compiler_params=_params(("parallel",)),
        name="out_proj",
    )(*args)
    if rider is None:
        return outs[0]
    return outs[0], outs[1].reshape(-1, GROUP_W)


def _kv_rows(qkv, group, rows):
    b, s, _ = qkv.shape
    k0 = ATTN_W + group * GROUP_W
    v0 = 2 * ATTN_W + group * GROUP_W
    k = qkv[:, s - rows:, k0:k0 + GROUP_W].reshape(b, rows, HG, HEAD_DIM)
    v = qkv[:, s - rows:, v0:v0 + GROUP_W].reshape(b, rows, HG, HEAD_DIM)
    return jnp.stack([k, v], axis=2)


def _layer_weights(l, norm1_g, w_in, b_gate, mu_shift, w0, w_lora_up, a0, a_lora_up, g_lora_up, k_k, k_a,
                   r_k, gn_g, gn_b, w_proj_a, w_proj_b, w_out, norm2_g, w_up, w_down, normf_g):
    row = lambda a: a.reshape(1, -1)
    zero = jnp.zeros_like(w_lora_up[l])
    wa_lora = jnp.concatenate([jnp.concatenate([w_lora_up[l], zero], axis=1),
                               jnp.concatenate([zero, a_lora_up[l]], axis=1)], axis=0)
    return dict(
        norm1_g=row(norm1_g[l]), w_in=w_in[l].astype(BF16), b_gate=row(b_gate[l]), mu_shift=row(mu_shift[l]),
        w0=row(w0[l]), a0=row(a0[l]), k_k=row(k_k[l]), k_a=row(k_a[l]), r_k=row(r_k[l]),
        gn_g=row(gn_g[l]), gn_b=row(gn_b[l]),
        wa_lora=wa_lora.astype(BF16), g_lora=g_lora_up[l].astype(BF16),
        w_proj_a=w_proj_a[l].astype(BF16), w_proj_b=w_proj_b[l].astype(BF16), w_out=w_out[l].astype(BF16),
        norm2_g=row(norm2_g[l]), w_up=w_up[l].astype(BF16), w_down=w_down[l].astype(BF16),
        normf_g=row(normf_g))


def _layer(xp, xs, caches, s0, shift0, w):
    b, s, _ = xp.shape
    bs, t_len, _ = xs.shape
    assert t_len == 1
    xp2 = xp.reshape(b * s, D_MODEL)
    xs2 = xs.reshape(bs, D_MODEL)
    qkv_s, zb_s, gate_s = _in_proj(xs2, w["norm1_g"], w["w_in"], w["b_gate"], bs)
    qkv, zb, gate, *kv_t = _in_proj(xp2, w["norm1_g"], w["w_in"], w["b_gate"], PROMPT_TILE, seq_len=s)
    zb = zb.reshape(b, s, SHIFT_W)
    attn = _attn_prompt(qkv.reshape(b, s, QKV_W))
    rwkv, s_bd = _rwkv(zb, jnp.zeros((b, 1, SHIFT_W), F32), jnp.zeros((b, N_PAIRS, PAIR_W, PAIR_W), F32), w,
                       RWKV_TILE, RWKV_SEQS_PER_STEP)
    attn2, rwkv2 = attn.reshape(b * s, GROUP_W), rwkv.reshape(b * s, RWKV_W)
    if b * s // PROMPT_TILE == bs:
        y_p, attn_s = _out_proj(xp2, attn2, rwkv2, gate, w, PROMPT_TILE, rider=(qkv_s, caches))
    else:
        y_p = _out_proj(xp2, attn2, rwkv2, gate, w, PROMPT_TILE)
        attn_s = _attn_sample(qkv_s, caches)
    rwkv_s, s_bd_s = _rwkv(zb_s[:, None, :], shift0[:, None, :], _state_to_pairs(s0), w, 1, RWKV_SEQS_PER_STEP,
                           t_valid=1)
    y_s = _out_proj(xs2, attn_s, rwkv_s[:, 0], gate_s, w, bs)
    kv_p = [jnp.transpose(t.reshape(b, 2, HG, HEAD_DIM, t.shape[-1]), (0, 4, 1, 2, 3)) for t in kv_t]
    kv_s = [_kv_rows(qkv_s[:, None, :], g, 1) for g in range(len(DILATIONS))]
    prompt = (y_p.reshape(b, s, D_MODEL), kv_p, _pairs_to_state(s_bd), zb[:, -1])
    sample = (y_s.reshape(bs, 1, D_MODEL), kv_s, _pairs_to_state(s_bd_s), zb_s)
    return prompt, sample


def kernel(x_prompt, x_sample, cache_kv_w128, cache_kv_w512, cache_kv_w2048, state_wkv, state_shift, norm1_g, w_in, b_gate, mu_shift, w0, w_lora_up, a0, a_lora_up, g_lora_up, k_k, k_a, r_k, gn_g, gn_b, w_proj_a, w_proj_b, w_out, norm2_g, w_up, w_down, normf_g):
    depth = w_in.shape[0]
    assert depth == 1, "the final norm is fused into the layer's output stage"
    w = _layer_weights(0, norm1_g, w_in, b_gate, mu_shift, w0, w_lora_up, a0, a_lora_up, g_lora_up, k_k, k_a,
                       r_k, gn_g, gn_b, w_proj_a, w_proj_b, w_out, norm2_g, w_up, w_down, normf_g)
    (y_p, kv_p, wkv_p, shift_p), (y_s, kv_s, wkv_s, shift_s) = _layer(
        x_prompt, x_sample, (cache_kv_w128[0], cache_kv_w512[0], cache_kv_w2048[0]), state_wkv[0], state_shift[0], w)
    lead = lambda a: a[None]
    return (y_p, y_s, lead(kv_p[0]), lead(kv_p[1]), lead(kv_p[2]), lead(wkv_p), lead(shift_p),
            lead(kv_s[0]), lead(kv_s[1]), lead(kv_s[2]), lead(wkv_s), lead(shift_s))
```

```python
import functools

import numpy as np
import jax
import jax.numpy as jnp
from jax import lax
from jax.experimental import pallas as pl
from jax.experimental.pallas import tpu as pltpu

F32 = jnp.float32
BF16 = jnp.bfloat16

D_MODEL = 1024
HEAD_DIM = 64
HG = 4
DILATIONS = (1, 4, 16)
BAND = 128
N_ATTN_HEADS = HG * len(DILATIONS)
GROUP_W = HG * HEAD_DIM
LANE_HALVES = GROUP_W // 128
ATTN_W = N_ATTN_HEADS * HEAD_DIM
QKV_W = 3 * ATTN_W
RWKV_N = 64
RWKV_W = 512
RWKV_HEADS = RWKV_W // RWKV_N
PAIR_W = 2 * RWKV_N
N_PAIRS = RWKV_HEADS // 2
LORA_W = 256
SHIFT_W = 3 * RWKV_W + LORA_W
GATE_W = 2 * D_MODEL
IN_W = QKV_W + SHIFT_W + GATE_W
D_FF = 4 * D_MODEL
NORM_EPS = 1e-6
GN_EPS = 64e-5
NEG_INF = -1e30
ATTN_SCALE = HEAD_DIM ** -0.5
CHUNK = 64
RWKV_SEQS_PER_STEP = 4
RWKV_TILE = 256
PROMPT_TILE = 512
DECAY_SCALE = float(np.exp(-0.5))
SLOPES = [float(s) for s in np.exp2(-8.0 * np.arange(1, N_ATTN_HEADS + 1, dtype=np.float32) / N_ATTN_HEADS)]

V7X_VMEM_LIMIT = 56 * 1024 * 1024

NN = (((1,), (0,)), ((), ()))
NT = (((1,), (1,)), ((), ()))
TN = (((0,), (0,)), ((), ()))


def _dg(a, b, dims=NN):
    return lax.dot_general(a, b, dims, preferred_element_type=F32)


def _split2(a):
    hi = a.astype(BF16)
    lo = (a - hi.astype(F32)).astype(BF16)
    return hi, lo


def _dot3(a, b, dims=NN):
    ah, al = _split2(a)
    bh, bl = _split2(b)
    return _dg(ah, bh, dims) + _dg(al, bh, dims) + _dg(ah, bl, dims)


def _dot_exact_rhs(a, b_bf16, passes):
    out = None
    rem = a
    for _ in range(passes):
        part = rem.astype(BF16)
        term = _dg(part, b_bf16)
        out = term if out is None else out + term
        rem = rem - part.astype(F32)
    return out


def _sigmoid(x):
    return 0.5 * jnp.tanh(0.5 * x) + 0.5


def _rms(x, g):
    return x * lax.rsqrt(jnp.mean(x * x, axis=-1, keepdims=True) + NORM_EPS) * g


def _params(sem):
    return pltpu.CompilerParams(dimension_semantics=sem, vmem_limit_bytes=V7X_VMEM_LIMIT)


def _kv_tail_plan(seq_len, tm):
    plan = []
    for dil in DILATIONS:
        rows = min(BAND * dil, seq_len)
        width = min(rows, tm)
        plan.append((rows, width, (seq_len - rows) // tm))
    return plan


def _in_proj_body(x_ref, g_ref, w_ref, bg_ref, qkv_ref, zb_ref, gate_ref, *kv_refs, seq_len):
    tm = x_ref.shape[0]

    def write_tail(g, kv_ref, width):
        for sec in (1, 2):
            c0 = sec * ATTN_W + g * GROUP_W
            kv_ref[sec - 1] = qkv_ref[tm - width:tm, c0:c0 + GROUP_W].T

    plan = _kv_tail_plan(seq_len, tm) if kv_refs else []
    h = _rms(x_ref[...], g_ref[...]).astype(BF16)
    for c in range(0, GATE_W, 1024):
        zg = _dg(h, w_ref[:, QKV_W + SHIFT_W + c:QKV_W + SHIFT_W + c + 1024])
        gate_ref[:, c:c + 1024] = _sigmoid(zg + bg_ref[:, c:c + 1024])
    for c in range(0, QKV_W, 768):
        qkv_ref[:, c:c + 768] = _dg(h, w_ref[:, c:c + 768])
    for g, (kv_ref, (_, width, first)) in enumerate(zip(kv_refs, plan)):
        if first == 0:
            write_tail(g, kv_ref, width)
    for c in range(0, SHIFT_W, 896):
        zb_ref[:, c:c + 896] = _dg(h, w_ref[:, QKV_W + c:QKV_W + c + 896])
    for g, (kv_ref, (_, width, first)) in enumerate(zip(kv_refs, plan)):
        if first > 0:
            tile = pl.program_id(0) % (seq_len // tm)
            pl.when(tile >= first)(functools.partial(write_tail, g, kv_ref, width))


def _in_proj(x, norm_g, w_in_bf16, b_gate, tm, seq_len=None):
    m = x.shape[0]
    row = lambda w: pl.BlockSpec((tm, w), lambda i: (i, 0))
    full = lambda a: pl.BlockSpec(a.shape, lambda i: (0,) * a.ndim)
    out_specs = [row(QKV_W), row(SHIFT_W), row(GATE_W)]
    out_shape = [jax.ShapeDtypeStruct((m, QKV_W), F32),
                 jax.ShapeDtypeStruct((m, SHIFT_W), F32),
                 jax.ShapeDtypeStruct((m, GATE_W), F32)]
    if seq_len is not None:
        tiles = seq_len // tm
        for rows, width, first in _kv_tail_plan(seq_len, tm):
            out_specs.append(pl.BlockSpec(
                (None, 2, GROUP_W, width),
                lambda i, first=first: (i // tiles, 0, 0, jnp.maximum(i % tiles - first, 0))))
            out_shape.append(jax.ShapeDtypeStruct((m // seq_len, 2, GROUP_W, rows), F32))
    return pl.pallas_call(
        functools.partial(_in_proj_body, seq_len=seq_len),
        grid=(m // tm,),
        in_specs=[row(D_MODEL), full(norm_g), full(w_in_bf16), full(b_gate)],
        out_specs=out_specs,
        out_shape=out_shape,
        compiler_params=_params(("arbitrary",)),
        name="in_proj",
    )(x, norm_g, w_in_bf16, b_gate)


def _residue_rows(r, count, dil):
    return pl.ds(r, count) if dil == 1 else pl.ds(r, count, stride=dil)


def _attn_group_blocks(q_ref, k_ref, v_ref, o_scr, lse_scr, qs, ks, vs, os_, ls, bias_scr, tmp, group):
    s_len = qs.shape[0]
    dil = DILATIONS[group]
    l_res = s_len // dil
    nb = l_res // BAND
    inner = 4 if dil > 4 else dil
    outer = dil // inner
    l_in = s_len // inner
    assert dil in (1, inner, inner * outer) and outer <= 4

    def gather(halves, r):
        if outer == 1:
            return jnp.concatenate([h[_residue_rows(r, l_res, dil), :] for h in halves], axis=1)
        c, q4 = r % inner, r // inner
        return jnp.concatenate([tmp[i, pl.ds(c * l_in + q4, l_res, stride=outer), :]
                                for i in range(LANE_HALVES)], axis=1)

    head = lax.broadcasted_iota(jnp.int32, (1, GROUP_W), 1) // HEAD_DIM
    hm = [head == h for h in range(HG)]
    for name, halves in (("q", q_ref), ("k", k_ref), ("v", v_ref)):
        if outer > 1:
            for i, h in enumerate(halves):
                for c in range(inner):
                    tmp[i, pl.ds(c * l_in, l_in), :] = h[_residue_rows(c, l_in, inner), :]
        for r in range(dil):
            dst = pl.ds(r * l_res, l_res)
            x = gather(halves, r)
            if name == "q":
                qs[dst, :] = (x * ATTN_SCALE).astype(BF16)
            elif name == "k":
                ks[dst, :] = x.astype(BF16)
            else:
                vs[dst, :] = x.astype(BF16)
    has_prev = nb > 1
    nk = 2 * BAND if has_prev else BAND
    qi = lax.broadcasted_iota(jnp.int32, (BAND, nk), 0)
    kj = lax.broadcasted_iota(jnp.int32, (BAND, nk), 1)
    delta = (nk - BAND) + qi - kj
    band = (delta >= 0) & (delta <= BAND)
    dist = (delta * dil).astype(F32)
    for h in range(HG):
        alibi = -SLOPES[group * HG + h] * dist
        bias_scr[h, :, 0:nk] = jnp.where(band, alibi, NEG_INF)
        bias_scr[HG + h, :, 0:nk] = jnp.where(band & (kj >= nk - BAND), alibi, NEG_INF)
    blocks_per_iter = 2

    def block_pair(it, carry):
        cur, q, k2, v2, first = [], [], [], [], []
        for u in range(blocks_per_iter):
            idx = it * blocks_per_iter + u
            cur.append(pl.ds(pl.multiple_of(idx * BAND, BAND), BAND))
            q.append(qs[cur[u], :])
            k2.append(ks[cur[u], :])
            v2.append(vs[cur[u], :])
            first.append(jnp.where(idx % nb == 0, HG, 0))
            if has_prev:
                prev = pl.ds(pl.multiple_of(jnp.maximum(idx - 1, 0) * BAND, BAND), BAND)
                k2[u] = jnp.concatenate([ks[prev, :], k2[u]], axis=0)
                v2[u] = jnp.concatenate([vs[prev, :], v2[u]], axis=0)
        chains = [(u, h) for u in range(blocks_per_iter) for h in range(HG)]
        s = [_dg(jnp.where(hm[h], q[u], jnp.zeros_like(q[u])), k2[u], NT) + bias_scr[first[u] + h, :, 0:nk]
             for u, h in chains]
        m = [jnp.max(x, axis=-1, keepdims=True) for x in s]
        p = [jnp.exp(x - mx) for x, mx in zip(s, m)]
        l = [jnp.sum(x, axis=-1, keepdims=True) for x in p]
        pv = [_dg(p[i].astype(BF16), v2[u]) for i, (u, h) in enumerate(chains)]
        for u in range(blocks_per_iter):
            o_acc = jnp.zeros((BAND, GROUP_W), F32)
            lse_acc = jnp.zeros((BAND, GROUP_W), F32)
            for h in range(HG):
                i = u * HG + h
                o_acc = jnp.where(hm[h], pv[i] / l[i], o_acc)
                lse_acc = jnp.where(hm[h], m[i] + jnp.log(l[i]), lse_acc)
            if dil == 1:
                for half in range(LANE_HALVES):
                    cols = slice(half * 128, (half + 1) * 128)
                    o_scr[group * LANE_HALVES + half, cur[u], :] = o_acc[:, cols]
                    lse_scr[group * LANE_HALVES + half, cur[u], :] = lse_acc[:, cols]
            else:
                os_[cur[u], :] = o_acc
                ls[cur[u], :] = lse_acc
        return carry

    assert (dil * nb) % blocks_per_iter == 0
    lax.fori_loop(0, dil * nb // blocks_per_iter, block_pair, 0)
    for res, nat in ((os_, o_scr), (ls, lse_scr)) if dil > 1 else ():
        for half in range(LANE_HALVES):
            cols = slice(half * 128, (half + 1) * 128)
            slab = group * LANE_HALVES + half
            for r in range(dil):
                src = pl.ds(r * l_res, l_res)
                if outer == 1:
                    nat[slab, _residue_rows(r, l_res, dil), :] = res[src, cols]
                else:
                    tmp[half, pl.ds((r % inner) * l_in + r // inner, l_res, stride=outer), :] = res[src, cols]
            if outer > 1:
                for c in range(inner):
                    nat[slab, _residue_rows(c, l_in, inner), :] = tmp[half, pl.ds(c * l_in, l_in), :]


def _attn_prompt_body(q0_ref, q1_ref, k0_ref, k1_ref, v0_ref, v1_ref, o_ref, o_scr, lse_scr, qs, ks, vs, os_, ls, bias_scr, tmp):
    q_ref, k_ref, v_ref = (q0_ref, q1_ref), (k0_ref, k1_ref), (v0_ref, v1_ref)
    gid = pl.program_id(1)
    n_groups = len(DILATIONS)
    for g in range(n_groups):
        @pl.when(gid == g)
        def _(g=g):
            _attn_group_blocks(q_ref, k_ref, v_ref, o_scr, lse_scr, qs, ks, vs, os_, ls, bias_scr, tmp, g)

    @pl.when(gid == n_groups - 1)
    def _():
        def merge(i, carry):
            rows = pl.ds(pl.multiple_of(i * BAND, BAND), BAND)
            both = lambda ref, g: jnp.concatenate(
                [ref[g * LANE_HALVES + half, rows, :] for half in range(LANE_HALVES)], axis=1)
            lse = [both(lse_scr, g) for g in range(n_groups)]
            m = functools.reduce(jnp.maximum, lse)
            wts = [jnp.exp(x - m) for x in lse]
            num = sum(wts[g] * both(o_scr, g) for g in range(n_groups))
            o_ref[rows, :] = (num / sum(wts)).astype(o_ref.dtype)
            return carry

        lax.fori_loop(0, o_ref.shape[0] // BAND, merge, 0)


def _attn_prompt(qkv):
    b, s, _ = qkv.shape
    n_groups = len(DILATIONS)
    assert s % (DILATIONS[-1] * BAND) == 0
    sec = ATTN_W // 128

    def col(section, half):
        return pl.BlockSpec((None, s, 128), lambda bi, g: (bi, 0, section * sec + g * LANE_HALVES + half))

    return pl.pallas_call(
        _attn_prompt_body,
        grid=(b, n_groups),
        in_specs=[col(section, half) for section in range(3) for half in range(LANE_HALVES)],
        out_specs=pl.BlockSpec((None, s, GROUP_W), lambda bi, g: (bi, 0, 0)),
        out_shape=jax.ShapeDtypeStruct((b, s, GROUP_W), BF16),
        scratch_shapes=([pltpu.VMEM((n_groups * LANE_HALVES, s, 128), F32)] * 2
                        + [pltpu.VMEM((s, GROUP_W), BF16)] * 3 + [pltpu.VMEM((s, GROUP_W), F32)] * 2
                        + [pltpu.VMEM((2 * HG, BAND, 2 * BAND), F32), pltpu.VMEM((LANE_HALVES, s, 128), F32)]),
        compiler_params=_params(("parallel", "arbitrary")),
        name="attn_prompt",
    )(*([qkv] * (3 * LANE_HALVES)))


def _attn_sample_body(qkv_ref, c0_ref, c1_ref, c2_ref, o_ref):
    for job in _attn_sample_jobs(qkv_ref, c0_ref, c1_ref, c2_ref, o_ref):
        job()


def _attn_sample_jobs(qkv_ref, c0_ref, c1_ref, c2_ref, o_ref):
    rows = 8
    row = lax.broadcasted_iota(jnp.int32, (rows, GROUP_W), 0)
    own = lax.broadcasted_iota(jnp.int32, (rows, GROUP_W), 1) // HEAD_DIM == row
    hrow = lax.broadcasted_iota(jnp.int32, (HG, 1), 0)
    lane_head = lax.broadcasted_iota(jnp.int32, (1, GROUP_W), 1) // HEAD_DIM
    c_refs = (c0_ref, c1_ref, c2_ref)
    groups = range(len(DILATIONS))
    st = [dict() for _ in groups]

    def to_column(x_row):
        return jnp.sum(jnp.where(own, x_row, 0.0).T, axis=1, keepdims=True)

    def per_head(col):
        out = jnp.zeros((1, GROUP_W), F32)
        for h in range(HG):
            out = jnp.where(lane_head == h, col[h:h + 1, :], out)
        return out

    def scores():
        for g in groups:
            lb = c_refs[g].shape[2]
            col = lambda sec: qkv_ref[:, sec * ATTN_W + g * GROUP_W:sec * ATTN_W + (g + 1) * GROUP_W]
            q, kn, vn = col(0), col(1), col(2)
            prod = c_refs[g][0] * to_column(q)
            s_b = jnp.sum(prod.reshape(HG, HEAD_DIM, lb), axis=1) * ATTN_SCALE
            s_n = jnp.sum(jnp.where(own, q * kn, 0.0), axis=-1, keepdims=True)[0:HG] * ATTN_SCALE
            st[g].update(vn=vn, s_b=s_b, s_n=s_n)

    def values():
        for g in groups:
            dil = DILATIONS[g]
            lb = c_refs[g].shape[2]
            slope = jnp.zeros((HG, 1), F32)
            for h in range(HG):
                slope = jnp.where(hrow == h, SLOPES[g * HG + h], slope)
            t = lax.broadcasted_iota(jnp.int32, (HG, lb), 1)
            s_b = jnp.where(t % dil == 0, st[g]["s_b"] - slope * (lb - t).astype(F32), NEG_INF)
            m = jnp.maximum(jnp.max(s_b, axis=-1, keepdims=True), st[g]["s_n"])
            p_b = jnp.exp(s_b - m)
            p_n = jnp.exp(st[g]["s_n"] - m)
            p_rows = jnp.broadcast_to(p_b[:, None, :], (HG, HEAD_DIM, lb)).reshape(GROUP_W, lb)
            acc = jnp.sum(p_rows * c_refs[g][1], axis=1, keepdims=True)
            st[g].update(m=m, p_n=p_n, l=jnp.sum(p_b, axis=-1, keepdims=True) + p_n, acc=acc)

    def merge():
        m_all = functools.reduce(jnp.maximum, [s["m"] for s in st])
        sc = [jnp.exp(s["m"] - m_all) for s in st]
        den = sum(c * s["l"] for c, s in zip(sc, st))
        out = jnp.zeros((1, GROUP_W), F32)
        for c, s in zip(sc, st):
            acc_row = jnp.broadcast_to(s["acc"], (GROUP_W, rows)).T[0:1, :]
            out = out + per_head(c / den) * (acc_row + per_head(s["p_n"]) * s["vn"])
        o_ref[...] = out.astype(o_ref.dtype)

    return [scores, values, merge]


def _attn_sample_operands(qkv, caches):
    b = qkv.shape[0]
    args = [qkv.reshape(b, 1, QKV_W)]
    specs = [pl.BlockSpec((None, 1, QKV_W), lambda i: (i, 0, 0))]
    for g, c in enumerate(caches):
        lb = c.shape[1]
        assert lb == BAND * DILATIONS[g]
        args.append(jnp.transpose(c, (0, 2, 3, 4, 1)).reshape(b, 2, GROUP_W, lb))
        specs.append(pl.BlockSpec((None, 2, GROUP_W, lb), lambda i: (i, 0, 0, 0)))
    out_spec = pl.BlockSpec((None, 1, GROUP_W), lambda i: (i, 0, 0))
    return args, specs, out_spec, jax.ShapeDtypeStruct((b, 1, GROUP_W), BF16)


def _attn_sample(qkv, caches):
    args, specs, out_spec, out_shape = _attn_sample_operands(qkv, caches)
    out = pl.pallas_call(
        _attn_sample_body,
        grid=(qkv.shape[0],),
        in_specs=specs,
        out_specs=out_spec,
        out_shape=out_shape,
        compiler_params=_params(("parallel",)),
        name="attn_sample",
    )(*args)
    return out.reshape(-1, GROUP_W)


WKV_PASSES = dict(g=1, s0=1, x=1, inv=1, y=1, s1=1)


def _mm(a, b, dims, passes):
    if passes == 1:
        return _dg(a.astype(BF16), b.astype(BF16), dims)
    ah, al = _split2(a)
    if passes == 2:
        bh = b.astype(BF16)
        return _dg(ah, bh, dims) + _dg(al, bh, dims)
    bh, bl = _split2(b)
    return _dg(ah, bh, dims) + _dg(al, bh, dims) + _dg(ah, bl, dims)


def _wkv_levels(live_rows):
    return int(np.ceil(np.log2(live_rows))) if live_rows > 1 else 0


def _wkv_chunk(operands, states, levels, side=()):
    side = list(side)

    def run_side():
        if side:
            side.pop(0)()
    c = CHUNK
    n2 = 2 * c
    n_seq = len(operands)
    first = lax.broadcasted_iota(jnp.int32, (1, PAIR_W), 1) < RWKV_N

    def stack(x, p):
        x = x[:, p * PAIR_W:(p + 1) * PAIR_W]
        return jnp.concatenate([jnp.where(first, x, 0.0), jnp.where(first, 0.0, x)], axis=0)

    bi = lax.broadcasted_iota(jnp.int32, (n_seq * c, n_seq * c), 0)
    bj = lax.broadcasted_iota(jnp.int32, (n_seq * c, n_seq * c), 1)
    tri_all = ((bi >= bj) & (bi // c == bj // c)).astype(BF16)
    lc_all = _dot_exact_rhs_left(tri_all, jnp.concatenate([ops[3] for ops in operands], axis=0))
    ar, bk, v_s, s0, w_end = [], [], [], [], []
    for j, ((r, kp, v, logw, av, bv), seq_states) in enumerate(zip(operands, states)):
        lc = lc_all[j * c:(j + 1) * c]
        e_in = jnp.exp(lc)
        e_neg = jnp.exp(-lc)
        a_t = av * jnp.exp(lc - logw)
        r_t = r * e_in
        b_t = bv * e_neg
        k_t = kp * e_neg
        for p in range(N_PAIRS):
            ar.append(jnp.concatenate([stack(a_t, p), stack(r_t, p)], axis=0))
            bk.append(jnp.concatenate([stack(b_t, p), stack(k_t, p)], axis=0))
            v_s.append(stack(v, p))
            s0.append(seq_states[p])
            w_end.append(e_in[c - 1:c, p * PAIR_W:(p + 1) * PAIR_W])
    chains = range(len(ar))
    g = [_mm(ar[i], bk[i], NT, WKV_PASSES["g"]) for i in chains]
    ar_s0 = [_mm(ar[i], s0[i], NT, WKV_PASSES["s0"]) for i in chains]
    run_side()
    ri = lax.broadcasted_iota(jnp.int32, (n2, n2), 0)
    ci = lax.broadcasted_iota(jnp.int32, (n2, n2), 1)
    strict = ri > ci
    incl = ri >= ci
    n_ab = [jnp.where(strict, g[i][0:n2, 0:n2], 0.0) for i in chains]
    n_ak = [jnp.where(strict, g[i][0:n2, n2:2 * n2], 0.0) for i in chains]
    m_r = [jnp.where(jnp.concatenate([incl, incl], axis=1), g[i][n2:2 * n2, :], 0.0) for i in chains]
    z = [ar_s0[i][0:n2] + _mm(n_ak[i], v_s[i], NN, WKV_PASSES["x"]) for i in chains]
    pw = n_ab
    for lvl in range(levels):
        if lvl < levels - 1:
            pz = [_mm(pw[i], jnp.concatenate([pw[i], z[i]], axis=1), NN, WKV_PASSES["inv"]) for i in chains]
            pw = [pz[i][:, 0:n2] for i in chains]
            z = [z[i] + pz[i][:, n2:2 * n2] for i in chains]
        else:
            z = [z[i] + _mm(pw[i], z[i], NN, WKV_PASSES["inv"]) for i in chains]
        if lvl % 2 == 1:
            run_side()
    while side:
        run_side()
    uv = [jnp.concatenate([z[i], v_s[i]], axis=0) for i in chains]
    y_s = [ar_s0[i][n2:2 * n2] + _mm(m_r[i], uv[i], NN, WKV_PASSES["y"]) for i in chains]
    s1 = [(s0[i] + _mm(uv[i], bk[i], TN, WKV_PASSES["s1"])) * w_end[i] for i in chains]
    ys = [jnp.concatenate([y_s[j * N_PAIRS + p][0:c] + y_s[j * N_PAIRS + p][c:n2] for p in range(N_PAIRS)], axis=1)
          for j in range(n_seq)]
    new_states = tuple(tuple(s1[j * N_PAIRS:(j + 1) * N_PAIRS]) for j in range(n_seq))
    return ys, new_states


def _dot_exact_rhs_left(lhs_bf16, b):
    out = None
    rem = b
    for _ in range(3):
        part = rem.astype(BF16)
        term = _dg(lhs_bf16, part)
        out = term if out is None else out + term
        rem = rem - part.astype(F32)
    return out


def _rwkv_body(zb_ref, mu_ref, w0_ref, a0_ref, kk_ref, ka_ref, rk_ref, gg_ref, gb_ref, wwa_ref, wg_ref, *refs,
               t_valid, fresh):
    if fresh:
        sh_ref = s0_ref = None
    else:
        sh_ref, s0_ref = refs[:2]
        refs = refs[2:]
    out_ref, s_out_ref, sh_out_ref, s_scr, carry_scr, ring0_scr, ring1_scr = refs
    t = pl.program_id(1)
    n_seq, tc, _ = zb_ref.shape

    @pl.when(t == 0)
    def _():
        if fresh:
            s_scr[...] = jnp.zeros(s_scr.shape, F32)
            carry_scr[...] = jnp.zeros(carry_scr.shape, F32)
        else:
            carry_scr[...] = sh_ref[...]
            zero = jnp.zeros((RWKV_N, RWKV_N), F32)
            for j in range(n_seq):
                for p in range(N_PAIRS):
                    top = jnp.concatenate([s0_ref[j, 2 * p], zero], axis=1)
                    bottom = jnp.concatenate([zero, s0_ref[j, 2 * p + 1]], axis=1)
                    s_scr[j, p] = jnp.concatenate([top, bottom], axis=0)

    lr = lax.broadcasted_iota(jnp.int32, (2 * PAIR_W, 2 * PAIR_W), 0)
    lc = lax.broadcasted_iota(jnp.int32, (2 * PAIR_W, 2 * PAIR_W), 1)
    seg = (lr // RWKV_N == lc // RWKV_N).astype(BF16)
    half = lax.broadcasted_iota(jnp.int32, (1, 128), 1) < 64
    row = lax.broadcasted_iota(jnp.int32, (CHUNK, 1), 0)
    inv_n = 1.0 / RWKV_N
    levels = _wkv_levels(CHUNK if t_valid is None else min(t_valid, CHUNK))
    short = tc < CHUNK

    def head_sum(x):
        wide = 2 * PAIR_W
        return jnp.concatenate(
            [_dg(x[:, c:c + wide].astype(BF16), seg) for c in range(0, RWKV_W, wide)], axis=1)

    n_chunks = max(tc // CHUNK, 1)
    rings = (ring0_scr, ring1_scr)

    def staging_jobs(ci, ring):
        rows = pl.ds(pl.multiple_of(ci * CHUNK, CHUNK), CHUNK)
        st = [dict() for _ in range(n_seq)]
        seq_rows = lambda x, j: x[j * CHUNK:(j + 1) * CHUNK]

        def mix_and_lora():
            x_wa, x_g = [], []
            for j in range(n_seq):
                if short:
                    z = jnp.where(row == 0, zb_ref[j], 0.0)
                    before = carry_scr[j]
                else:
                    z = zb_ref[j, rows, :]
                    before = zb_ref[j, pl.ds(jnp.maximum(ci * CHUNK - 1, 0), 1), :]
                    before = jnp.where(ci == 0, carry_scr[j], before)
                zp = jnp.where(row == 0, before, pltpu.roll(z, 1, 0))
                zm = z + (zp - z) * mu_ref[...]
                z_wa = zm[:, 3 * RWKV_W:3 * RWKV_W + 128]
                x_wa.append(jnp.where(half, jnp.tanh(z_wa), z_wa).astype(BF16))
                x_g.append(_sigmoid(zm[:, 3 * RWKV_W + 128:SHIFT_W]).astype(BF16))
                st[j].update(r=zm[:, 0:RWKV_W], k=zm[:, RWKV_W:2 * RWKV_W], v=zm[:, 2 * RWKV_W:3 * RWKV_W])
            x_wa = jnp.concatenate(x_wa, axis=0)
            u = _dg(x_wa, wwa_ref[:, 0:RWKV_W])
            la = _dg(x_wa, wwa_ref[:, RWKV_W:2 * RWKV_W])
            gate = _dg(jnp.concatenate(x_g, axis=0), wg_ref[...])
            for j in range(n_seq):
                st[j].update(u=seq_rows(u, j), la=seq_rows(la, j), gate=seq_rows(gate, j))

        def head_sums():
            for j in range(n_seq):
                s = st[j]
                a = _sigmoid(a0_ref[...] + s["la"])
                kk = s["k"] * kk_ref[...]
                kp = s["k"] * (1.0 + (a - 1.0) * ka_ref[...])
                s.update(a=a, kk=kk, kp=kp)
            kk2 = head_sum(jnp.concatenate([s["kk"] * s["kk"] for s in st], axis=0))
            rk = head_sum(jnp.concatenate([s["r"] * s["kp"] * rk_ref[...] for s in st], axis=0))
            for j in range(n_seq):
                st[j].update(kk2=seq_rows(kk2, j), rk=seq_rows(rk, j))

        def finish():
            for j in range(n_seq):
                s = st[j]
                logw = -DECAY_SCALE * _sigmoid(w0_ref[...] + s["u"])
                kk = s["kk"] * lax.rsqrt(jnp.maximum(s["kk2"], 1e-24))
                ops = (s["r"], s["kp"], s["v"], logw, -kk, kk * s["a"], s["rk"] * s["v"])
                if t_valid is not None:
                    live = row + ci * CHUNK < t_valid
                    ops = tuple(jnp.where(live, x, 0.0) for x in ops)
                for i, x in enumerate(ops + (s["gate"],)):
                    ring[j, i] = x

        return mix_and_lora, head_sums, finish

    def chunk(ci, states, ring, other):
        side = staging_jobs(jnp.minimum(ci + 1, n_chunks - 1), other) if n_chunks > 1 else ()
        ys, states = _wkv_chunk([tuple(ring[j, i] for i in range(6)) for j in range(n_seq)], states, levels, side)
        y_all = jnp.concatenate(ys, axis=0)
        d_all = y_all - head_sum(y_all) * inv_n
        var_all = head_sum(d_all * d_all) * inv_n
        yn_all = d_all * lax.rsqrt(var_all + GN_EPS) * gg_ref[...] + gb_ref[...]
        for j in range(n_seq):
            res = (yn_all[j * CHUNK:(j + 1) * CHUNK] + ring[j, 6]) * ring[j, 7]
            if short:
                out_ref[j] = res[0:tc, :].astype(out_ref.dtype)
            else:
                out_ref[j, pl.ds(pl.multiple_of(ci * CHUNK, CHUNK), CHUNK), :] = res.astype(out_ref.dtype)
        return states

    for job in staging_jobs(0, rings[0]):
        job()
    states = tuple(tuple(s_scr[j, p] for p in range(N_PAIRS)) for j in range(n_seq))
    if n_chunks == 1:
        states = chunk(0, states, rings[0], rings[1])
    else:
        assert n_chunks % 2 == 0

        def chunk_pair(i, states):
            states = chunk(2 * i, states, rings[0], rings[1])
            return chunk(2 * i + 1, states, rings[1], rings[0])

        states = lax.fori_loop(0, n_chunks // 2, chunk_pair, states)
    for j in range(n_seq):
        carry_scr[j] = zb_ref[j, tc - 1:tc, :]
        for p in range(N_PAIRS):
            s_scr[j, p] = states[j][p]

    @pl.when(t == pl.num_programs(1) - 1)
    def _():
        for j in range(n_seq):
            sh_out_ref[j] = zb_ref[j, tc - 1:tc, :]
            for p in range(N_PAIRS):
                for e in range(2):
                    span = slice(e * RWKV_N, (e + 1) * RWKV_N)
                    s_out_ref[j, 2 * p + e] = states[j][p][span, span]


def _rwkv(zb, w, tc, n_seq, state=None, t_valid=None):
    b, t_len, _ = zb.shape
    full = lambda a: pl.BlockSpec(a.shape, lambda bi, t: (0,) * a.ndim)
    per_seq = lambda *dims: pl.BlockSpec((n_seq,) + dims, lambda bi, t: (bi,) + (0,) * len(dims))
    consts = [w["mu_shift"], w["w0"], w["a0"], w["k_k"], w["k_a"], w["r_k"], w["gn_g"], w["gn_b"], w["wa_lora"],
              w["g_lora"]]
    in_specs = [pl.BlockSpec((n_seq, tc, SHIFT_W), lambda bi, t: (bi, t, 0))] + [full(c) for c in consts]
    args = [zb, *consts]
    if state is not None:
        in_specs += [per_seq(1, SHIFT_W), per_seq(RWKV_HEADS, RWKV_N, RWKV_N)]
        args += list(state)
    vm = lambda shape: pltpu.VMEM(shape, F32)
    assert b % n_seq == 0 and t_len % tc == 0
    assert tc % CHUNK == 0 or (tc == t_len == t_valid == 1), "whole chunks, or single-token sequences"
    scratch = [vm((n_seq, N_PAIRS, PAIR_W, PAIR_W)), vm((n_seq, 1, SHIFT_W))] + [vm((n_seq, 8, CHUNK, RWKV_W))] * 2
    return pl.pallas_call(
        functools.partial(_rwkv_body, t_valid=t_valid, fresh=state is None),
        grid=(b // n_seq, t_len // tc),
        in_specs=in_specs,
        out_specs=[pl.BlockSpec((n_seq, tc, RWKV_W), lambda bi, t: (bi, t, 0)),
                   per_seq(RWKV_HEADS, RWKV_N, RWKV_N), per_seq(1, SHIFT_W)],
        out_shape=[jax.ShapeDtypeStruct((b, t_len, RWKV_W), BF16),
                   jax.ShapeDtypeStruct((b, RWKV_HEADS, RWKV_N, RWKV_N), F32),
                   jax.ShapeDtypeStruct((b, 1, SHIFT_W), F32)],
        scratch_shapes=scratch,
        compiler_params=_params(("parallel", "arbitrary")),
        name="rwkv",
    )(*args)


def _out_proj_body(x_ref, attn_ref, rwkv_ref, gate_ref, wpa_ref, wpb_ref, wout_ref, n2_ref,
                   wup_ref, wdn_ref, nf_ref, *refs):
    rider = []
    if len(refs) > 1:
        qkv_s_ref, c0_ref, c1_ref, c2_ref, y_ref, attn_s_ref = refs
        rider = _attn_sample_jobs(qkv_s_ref, c0_ref, c1_ref, c2_ref, attn_s_ref)
    else:
        y_ref, = refs
    pa = _dg(attn_ref[...], wpa_ref[...])
    pb = _dg(rwkv_ref[...], wpb_ref[...])
    merged = gate_ref[:, 0:D_MODEL] * pa + gate_ref[:, D_MODEL:GATE_W] * pb
    x1 = x_ref[...] + _dg(merged.astype(BF16), wout_ref[...])
    hm = _rms(x1, n2_ref[...]).astype(BF16)
    acc = x1
    for c in range(0, D_FF, 1024):
        up = jnp.maximum(_dg(hm, wup_ref[:, c:c + 1024]), 0.0)
        if rider:
            rider.pop(0)()
        acc = acc + _dg((up * up).astype(BF16), wdn_ref[c:c + 1024, :])
    y_ref[...] = _rms(acc, nf_ref[...])


def _out_proj(x, attn, rwkv, gate, w, tm, rider=None):
    m = x.shape[0]
    row = lambda wd: pl.BlockSpec((tm, wd), lambda i: (i, 0))
    full = lambda a: pl.BlockSpec(a.shape, lambda i: (0,) * a.ndim)
    consts = [w["w_proj_a"], w["w_proj_b"], w["w_out"], w["norm2_g"], w["w_up"], w["w_down"], w["normf_g"]]
    in_specs = [row(D_MODEL), row(GROUP_W), row(RWKV_W), row(GATE_W)] + [full(c) for c in consts]
    args = [x, attn, rwkv, gate, *consts]
    out_specs = [row(D_MODEL)]
    out_shape = [jax.ShapeDtypeStruct((m, D_MODEL), F32)]
    if rider is not None:
        r_args, r_specs, r_out_spec, r_out_shape = _attn_sample_operands(*rider)
        assert r_out_shape.shape[0] == m // tm
        in_specs += r_specs
        args += r_args
        out_specs.append(r_out_spec)
        out_shape.append(r_out_shape)
    outs = pl.pallas_call(
        _out_proj_body,
        grid=(m // tm,),
        in_specs=in_specs,
        out_specs=out_specs,
        out_shape=out_shape,
        compiler_params=_params(("parallel",)),
        name="out_proj",
    )(*args)
    if rider is None:
        return outs[0]
    return outs[0], outs[1].reshape(-1, GROUP_W)


def _kv_rows(qkv, group, rows):
    b, s, _ = qkv.shape
    k0 = ATTN_W + group * GROUP_W
    v0 = 2 * ATTN_W + group * GROUP_W
    k = qkv[:, s - rows:, k0:k0 + GROUP_W].reshape(b, rows, HG, HEAD_DIM)
    v = qkv[:, s - rows:, v0:v0 + GROUP_W].reshape(b, rows, HG, HEAD_DIM)
    return jnp.stack([k, v], axis=2)


def _layer_weights(l, norm1_g, w_in, b_gate, mu_shift, w0, w_lora_up, a0, a_lora_up, g_lora_up, k_k, k_a,
                   r_k, gn_g, gn_b, w_proj_a, w_proj_b, w_out, norm2_g, w_up, w_down, normf_g):
    row = lambda a: a.reshape(1, -1)
    zero = jnp.zeros_like(w_lora_up[l])
    wa_lora = jnp.concatenate([jnp.concatenate([w_lora_up[l], zero], axis=1),
                               jnp.concatenate([zero, a_lora_up[l]], axis=1)], axis=0)
    return dict(
        norm1_g=row(norm1_g[l]), w_in=w_in[l].astype(BF16), b_gate=row(b_gate[l]), mu_shift=row(mu_shift[l]),
        w0=row(w0[l]), a0=row(a0[l]), k_k=row(k_k[l]), k_a=row(k_a[l]), r_k=row(r_k[l]),
        gn_g=row(gn_g[l]), gn_b=row(gn_b[l]),
        wa_lora=wa_lora.astype(BF16), g_lora=g_lora_up[l].astype(BF16),
        w_proj_a=w_proj_a[l].astype(BF16), w_proj_b=w_proj_b[l].astype(BF16), w_out=w_out[l].astype(BF16),
        norm2_g=row(norm2_g[l]), w_up=w_up[l].astype(BF16), w_down=w_down[l].astype(BF16),
        normf_g=row(normf_g))


def _layer(xp, xs, caches, s0, shift0, w):
    b, s, _ = xp.shape
    bs, t_len, _ = xs.shape
    assert t_len == 1
    xp2 = xp.reshape(b * s, D_MODEL)
    xs2 = xs.reshape(bs, D_MODEL)
    qkv_s, zb_s, gate_s = _in_proj(xs2, w["norm1_g"], w["w_in"], w["b_gate"], bs)
    qkv, zb, gate, *kv_t = _in_proj(xp2, w["norm1_g"], w["w_in"], w["b_gate"], PROMPT_TILE, seq_len=s)
    zb = zb.reshape(b, s, SHIFT_W)
    attn = _attn_prompt(qkv.reshape(b, s, QKV_W))
    rwkv, wkv_p, shift_p = _rwkv(zb, w, RWKV_TILE, RWKV_SEQS_PER_STEP)
    attn2, rwkv2 = attn.reshape(b * s, GROUP_W), rwkv.reshape(b * s, RWKV_W)
    if b * s // PROMPT_TILE == bs:
        y_p, attn_s = _out_proj(xp2, attn2, rwkv2, gate, w, PROMPT_TILE, rider=(qkv_s, caches))
    else:
        y_p = _out_proj(xp2, attn2, rwkv2, gate, w, PROMPT_TILE)
        attn_s = _attn_sample(qkv_s, caches)
    rwkv_s, wkv_s, _ = _rwkv(zb_s[:, None, :], w, 1, RWKV_SEQS_PER_STEP, state=(shift0[:, None, :], s0), t_valid=1)
    y_s = _out_proj(xs2, attn_s, rwkv_s[:, 0], gate_s, w, bs)
    kv_p = [jnp.transpose(t.reshape(b, 2, HG, HEAD_DIM, t.shape[-1]), (0, 4, 1, 2, 3)) for t in kv_t]
    kv_s = [_kv_rows(qkv_s[:, None, :], g, 1) for g in range(len(DILATIONS))]
    prompt = (y_p.reshape(b, s, D_MODEL), kv_p, wkv_p, shift_p[:, 0])
    sample = (y_s.reshape(bs, 1, D_MODEL), kv_s, wkv_s, zb_s)
    return prompt, sample


def kernel(x_prompt, x_sample, cache_kv_w128, cache_kv_w512, cache_kv_w2048, state_wkv, state_shift, norm1_g, w_in, b_gate, mu_shift, w0, w_lora_up, a0, a_lora_up, g_lora_up, k_k, k_a, r_k, gn_g, gn_b, w_proj_a, w_proj_b, w_out, norm2_g, w_up, w_down, normf_g):
    depth = w_in.shape[0]
    assert depth == 1, "the final norm is fused into the layer's output stage"
    w = _layer_weights(0, norm1_g, w_in, b_gate, mu_shift, w0, w_lora_up, a0, a_lora_up, g_lora_up, k_k, k_a,
                       r_k, gn_g, gn_b, w_proj_a, w_proj_b, w_out, norm2_g, w_up, w_down, normf_g)
    (y_p, kv_p, wkv_p, shift_p), (y_s, kv_s, wkv_s, shift_s) = _layer(
        x_prompt, x_sample, (cache_kv_w128[0], cache_kv_w512[0], cache_kv_w2048[0]), state_wkv[0], state_shift[0], w)
    lead = lambda a: a[None]
    return (y_p, y_s, lead(kv_p[0]), lead(kv_p[1]), lead(kv_p[2]), lead(wkv_p), lead(shift_p),
            lead(kv_s[0]), lead(kv_s[1]), lead(kv_s[2]), lead(wkv_s), lead(shift_s))
```

```python
import functools

import numpy as np
import jax
import jax.numpy as jnp
from jax import lax
from jax.experimental import pallas as pl
from jax.experimental.pallas import tpu as pltpu

F32 = jnp.float32
BF16 = jnp.bfloat16

D_MODEL = 1024
HEAD_DIM = 64
HG = 4
DILATIONS = (1, 4, 16)
BAND = 128
N_ATTN_HEADS = HG * len(DILATIONS)
GROUP_W = HG * HEAD_DIM
LANE_HALVES = GROUP_W // 128
ATTN_W = N_ATTN_HEADS * HEAD_DIM
QKV_W = 3 * ATTN_W
RWKV_N = 64
RWKV_W = 512
RWKV_HEADS = RWKV_W // RWKV_N
PAIR_W = 2 * RWKV_N
N_PAIRS = RWKV_HEADS // 2
LORA_W = 256
SHIFT_W = 3 * RWKV_W + LORA_W
GATE_W = 2 * D_MODEL
IN_W = QKV_W + SHIFT_W + GATE_W
D_FF = 4 * D_MODEL
NORM_EPS = 1e-6
GN_EPS = 64e-5
NEG_INF = -1e30
ATTN_SCALE = HEAD_DIM ** -0.5
CHUNK = 64
RWKV_SEQS_PER_STEP = 4
RWKV_TILE = 256
PROMPT_TILE = 512
DECAY_SCALE = float(np.exp(-0.5))
SLOPES = [float(s) for s in np.exp2(-8.0 * np.arange(1, N_ATTN_HEADS + 1, dtype=np.float32) / N_ATTN_HEADS)]

V7X_VMEM_LIMIT = 56 * 1024 * 1024

NN = (((1,), (0,)), ((), ()))
NT = (((1,), (1,)), ((), ()))
TN = (((0,), (0,)), ((), ()))


def _dg(a, b, dims=NN):
    return lax.dot_general(a, b, dims, preferred_element_type=F32)


def _split2(a):
    hi = a.astype(BF16)
    lo = (a - hi.astype(F32)).astype(BF16)
    return hi, lo


def _dot3(a, b, dims=NN):
    ah, al = _split2(a)
    bh, bl = _split2(b)
    return _dg(ah, bh, dims) + _dg(al, bh, dims) + _dg(ah, bl, dims)


def _dot_exact_rhs(a, b_bf16, passes):
    out = None
    rem = a
    for _ in range(passes):
        part = rem.astype(BF16)
        term = _dg(part, b_bf16)
        out = term if out is None else out + term
        rem = rem - part.astype(F32)
    return out


def _sigmoid(x):
    return 0.5 * jnp.tanh(0.5 * x) + 0.5


def _rms(x, g):
    return x * lax.rsqrt(jnp.mean(x * x, axis=-1, keepdims=True) + NORM_EPS) * g


def _params(sem):
    return pltpu.CompilerParams(dimension_semantics=sem, vmem_limit_bytes=V7X_VMEM_LIMIT)


def _kv_tail_plan(seq_len, tm):
    plan = []
    for dil in DILATIONS:
        rows = min(BAND * dil, seq_len)
        width = min(rows, tm)
        plan.append((rows, width, (seq_len - rows) // tm))
    return plan


def _in_proj_body(x_ref, g_ref, w_ref, bg_ref, qkv_ref, zb_ref, gate_ref, *kv_refs, seq_len):
    tm = x_ref.shape[0]

    def write_tail(g, kv_ref, width):
        for sec in (1, 2):
            c0 = sec * ATTN_W + g * GROUP_W
            kv_ref[sec - 1] = qkv_ref[tm - width:tm, c0:c0 + GROUP_W].T

    plan = _kv_tail_plan(seq_len, tm) if kv_refs else []
    h = _rms(x_ref[...], g_ref[...]).astype(BF16)
    for c in range(0, GATE_W, 1024):
        zg = _dg(h, w_ref[:, QKV_W + SHIFT_W + c:QKV_W + SHIFT_W + c + 1024])
        gate_ref[:, c:c + 1024] = _sigmoid(zg + bg_ref[:, c:c + 1024])
    for c in range(0, QKV_W, 768):
        qkv_ref[:, c:c + 768] = _dg(h, w_ref[:, c:c + 768])
    for g, (kv_ref, (_, width, first)) in enumerate(zip(kv_refs, plan)):
        if first == 0:
            write_tail(g, kv_ref, width)
    for c in range(0, SHIFT_W, 896):
        zb_ref[:, c:c + 896] = _dg(h, w_ref[:, QKV_W + c:QKV_W + c + 896])
    for g, (kv_ref, (_, width, first)) in enumerate(zip(kv_refs, plan)):
        if first > 0:
            tile = pl.program_id(0) % (seq_len // tm)
            pl.when(tile >= first)(functools.partial(write_tail, g, kv_ref, width))


def _in_proj(x, norm_g, w_in_bf16, b_gate, tm, seq_len=None):
    m = x.shape[0]
    row = lambda w: pl.BlockSpec((tm, w), lambda i: (i, 0))
    full = lambda a: pl.BlockSpec(a.shape, lambda i: (0,) * a.ndim)
    out_specs = [row(QKV_W), row(SHIFT_W), row(GATE_W)]
    out_shape = [jax.ShapeDtypeStruct((m, QKV_W), F32),
                 jax.ShapeDtypeStruct((m, SHIFT_W), F32),
                 jax.ShapeDtypeStruct((m, GATE_W), F32)]
    if seq_len is not None:
        tiles = seq_len // tm
        for rows, width, first in _kv_tail_plan(seq_len, tm):
            out_specs.append(pl.BlockSpec(
                (None, 2, GROUP_W, width),
                lambda i, first=first: (i // tiles, 0, 0, jnp.maximum(i % tiles - first, 0))))
            out_shape.append(jax.ShapeDtypeStruct((m // seq_len, 2, GROUP_W, rows), F32))
    return pl.pallas_call(
        functools.partial(_in_proj_body, seq_len=seq_len),
        grid=(m // tm,),
        in_specs=[row(D_MODEL), full(norm_g), full(w_in_bf16), full(b_gate)],
        out_specs=out_specs,
        out_shape=out_shape,
        compiler_params=_params(("arbitrary",)),
        name="in_proj",
    )(x, norm_g, w_in_bf16, b_gate)


def _residue_rows(r, count, dil):
    return pl.ds(r, count) if dil == 1 else pl.ds(r, count, stride=dil)


def _attn_group_blocks(q_ref, k_ref, v_ref, o_scr, lse_scr, qs, ks, vs, os_, ls, bias_scr, tmp, group):
    s_len = qs.shape[0]
    dil = DILATIONS[group]
    l_res = s_len // dil
    nb = l_res // BAND
    inner = 4 if dil > 4 else dil
    outer = dil // inner
    l_in = s_len // inner
    assert dil in (1, inner, inner * outer) and outer <= 4

    def gather(halves, r):
        if outer == 1:
            return jnp.concatenate([h[_residue_rows(r, l_res, dil), :] for h in halves], axis=1)
        c, q4 = r % inner, r // inner
        return jnp.concatenate([tmp[i, pl.ds(c * l_in + q4, l_res, stride=outer), :]
                                for i in range(LANE_HALVES)], axis=1)

    head = lax.broadcasted_iota(jnp.int32, (1, GROUP_W), 1) // HEAD_DIM
    hm = [head == h for h in range(HG)]
    for name, halves in (("q", q_ref), ("k", k_ref), ("v", v_ref)):
        if outer > 1:
            for i, h in enumerate(halves):
                for c in range(inner):
                    tmp[i, pl.ds(c * l_in, l_in), :] = h[_residue_rows(c, l_in, inner), :]
        for r in range(dil):
            dst = pl.ds(r * l_res, l_res)
            x = gather(halves, r)
            if name == "q":
                qs[dst, :] = (x * ATTN_SCALE).astype(BF16)
            elif name == "k":
                ks[dst, :] = x.astype(BF16)
            else:
                vs[dst, :] = x.astype(BF16)
    has_prev = nb > 1
    nk = 2 * BAND if has_prev else BAND
    qi = lax.broadcasted_iota(jnp.int32, (BAND, nk), 0)
    kj = lax.broadcasted_iota(jnp.int32, (BAND, nk), 1)
    delta = (nk - BAND) + qi - kj
    band = (delta >= 0) & (delta <= BAND)
    dist = (delta * dil).astype(F32)
    for h in range(HG):
        alibi = -SLOPES[group * HG + h] * dist
        bias_scr[h, :, 0:nk] = jnp.where(band, alibi, NEG_INF)
        bias_scr[HG + h, :, 0:nk] = jnp.where(band & (kj >= nk - BAND), alibi, NEG_INF)
    blocks_per_iter = 4

    def block_pair(it, carry):
        cur, q, k2, v2, first = [], [], [], [], []
        for u in range(blocks_per_iter):
            idx = it * blocks_per_iter + u
            cur.append(pl.ds(pl.multiple_of(idx * BAND, BAND), BAND))
            q.append(qs[cur[u], :])
            k2.append(ks[cur[u], :])
            v2.append(vs[cur[u], :])
            first.append(jnp.where(idx % nb == 0, HG, 0))
            if has_prev:
                prev = pl.ds(pl.multiple_of(jnp.maximum(idx - 1, 0) * BAND, BAND), BAND)
                k2[u] = jnp.concatenate([ks[prev, :], k2[u]], axis=0)
                v2[u] = jnp.concatenate([vs[prev, :], v2[u]], axis=0)
        chains = [(u, h) for u in range(blocks_per_iter) for h in range(HG)]
        s = [_dg(jnp.where(hm[h], q[u], jnp.zeros_like(q[u])), k2[u], NT) + bias_scr[first[u] + h, :, 0:nk]
             for u, h in chains]
        m = [jnp.max(x, axis=-1, keepdims=True) for x in s]
        p = [jnp.exp(x - mx) for x, mx in zip(s, m)]
        l = [jnp.sum(x, axis=-1, keepdims=True) for x in p]
        pv = [_dg(p[i].astype(BF16), v2[u]) for i, (u, h) in enumerate(chains)]
        for u in range(blocks_per_iter):
            o_acc = jnp.zeros((BAND, GROUP_W), F32)
            lse_acc = jnp.zeros((BAND, GROUP_W), F32)
            for h in range(HG):
                i = u * HG + h
                o_acc = jnp.where(hm[h], pv[i] / l[i], o_acc)
                lse_acc = jnp.where(hm[h], m[i] + jnp.log(l[i]), lse_acc)
            if dil == 1:
                for half in range(LANE_HALVES):
                    cols = slice(half * 128, (half + 1) * 128)
                    o_scr[group * LANE_HALVES + half, cur[u], :] = o_acc[:, cols]
                    lse_scr[group * LANE_HALVES + half, cur[u], :] = lse_acc[:, cols]
            else:
                os_[cur[u], :] = o_acc
                ls[cur[u], :] = lse_acc
        return carry

    assert (dil * nb) % blocks_per_iter == 0
    lax.fori_loop(0, dil * nb // blocks_per_iter, block_pair, 0)
    for res, nat in ((os_, o_scr), (ls, lse_scr)) if dil > 1 else ():
        for half in range(LANE_HALVES):
            cols = slice(half * 128, (half + 1) * 128)
            slab = group * LANE_HALVES + half
            for r in range(dil):
                src = pl.ds(r * l_res, l_res)
                if outer == 1:
                    nat[slab, _residue_rows(r, l_res, dil), :] = res[src, cols]
                else:
                    tmp[half, pl.ds((r % inner) * l_in + r // inner, l_res, stride=outer), :] = res[src, cols]
            if outer > 1:
                for c in range(inner):
                    nat[slab, _residue_rows(c, l_in, inner), :] = tmp[half, pl.ds(c * l_in, l_in), :]


def _attn_prompt_body(q0_ref, q1_ref, k0_ref, k1_ref, v0_ref, v1_ref, o_ref, o_scr, lse_scr, qs, ks, vs, os_, ls, bias_scr, tmp):
    q_ref, k_ref, v_ref = (q0_ref, q1_ref), (k0_ref, k1_ref), (v0_ref, v1_ref)
    gid = pl.program_id(1)
    n_groups = len(DILATIONS)
    for g in range(n_groups):
        @pl.when(gid == g)
        def _(g=g):
            _attn_group_blocks(q_ref, k_ref, v_ref, o_scr, lse_scr, qs, ks, vs, os_, ls, bias_scr, tmp, g)

    @pl.when(gid == n_groups - 1)
    def _():
        def merge(i, carry):
            rows = pl.ds(pl.multiple_of(i * BAND, BAND), BAND)
            both = lambda ref, g: jnp.concatenate(
                [ref[g * LANE_HALVES + half, rows, :] for half in range(LANE_HALVES)], axis=1)
            lse = [both(lse_scr, g) for g in range(n_groups)]
            m = functools.reduce(jnp.maximum, lse)
            wts = [jnp.exp(x - m) for x in lse]
            num = sum(wts[g] * both(o_scr, g) for g in range(n_groups))
            o_ref[rows, :] = (num / sum(wts)).astype(o_ref.dtype)
            return carry

        lax.fori_loop(0, o_ref.shape[0] // BAND, merge, 0)


def _attn_prompt(qkv):
    b, s, _ = qkv.shape
    n_groups = len(DILATIONS)
    assert s % (DILATIONS[-1] * BAND) == 0
    sec = ATTN_W // 128

    def col(section, half):
        return pl.BlockSpec((None, s, 128), lambda bi, g: (bi, 0, section * sec + g * LANE_HALVES + half))

    return pl.pallas_call(
        _attn_prompt_body,
        grid=(b, n_groups),
        in_specs=[col(section, half) for section in range(3) for half in range(LANE_HALVES)],
        out_specs=pl.BlockSpec((None, s, GROUP_W), lambda bi, g: (bi, 0, 0)),
        out_shape=jax.ShapeDtypeStruct((b, s, GROUP_W), BF16),
        scratch_shapes=([pltpu.VMEM((n_groups * LANE_HALVES, s, 128), F32)] * 2
                        + [pltpu.VMEM((s, GROUP_W), BF16)] * 3 + [pltpu.VMEM((s, GROUP_W), F32)] * 2
                        + [pltpu.VMEM((2 * HG, BAND, 2 * BAND), F32), pltpu.VMEM((LANE_HALVES, s, 128), F32)]),
        compiler_params=_params(("parallel", "arbitrary")),
        name="attn_prompt",
    )(*([qkv] * (3 * LANE_HALVES)))


def _attn_sample_body(qkv_ref, c0_ref, c1_ref, c2_ref, o_ref):
    for job in _attn_sample_jobs(qkv_ref, c0_ref, c1_ref, c2_ref, o_ref):
        job()


def _attn_sample_jobs(qkv_ref, c0_ref, c1_ref, c2_ref, o_ref):
    rows = 8
    row = lax.broadcasted_iota(jnp.int32, (rows, GROUP_W), 0)
    own = lax.broadcasted_iota(jnp.int32, (rows, GROUP_W), 1) // HEAD_DIM == row
    hrow = lax.broadcasted_iota(jnp.int32, (HG, 1), 0)
    lane_head = lax.broadcasted_iota(jnp.int32, (1, GROUP_W), 1) // HEAD_DIM
    c_refs = (c0_ref, c1_ref, c2_ref)
    groups = range(len(DILATIONS))
    st = [dict() for _ in groups]

    def to_column(x_row):
        return jnp.sum(jnp.where(own, x_row, 0.0).T, axis=1, keepdims=True)

    def per_head(col):
        out = jnp.zeros((1, GROUP_W), F32)
        for h in range(HG):
            out = jnp.where(lane_head == h, col[h:h + 1, :], out)
        return out

    def scores():
        for g in groups:
            lb = c_refs[g].shape[2]
            col = lambda sec: qkv_ref[:, sec * ATTN_W + g * GROUP_W:sec * ATTN_W + (g + 1) * GROUP_W]
            q, kn, vn = col(0), col(1), col(2)
            prod = c_refs[g][0] * to_column(q)
            s_b = jnp.sum(prod.reshape(HG, HEAD_DIM, lb), axis=1) * ATTN_SCALE
            s_n = jnp.sum(jnp.where(own, q * kn, 0.0), axis=-1, keepdims=True)[0:HG] * ATTN_SCALE
            st[g].update(vn=vn, s_b=s_b, s_n=s_n)

    def values():
        for g in groups:
            dil = DILATIONS[g]
            lb = c_refs[g].shape[2]
            slope = jnp.zeros((HG, 1), F32)
            for h in range(HG):
                slope = jnp.where(hrow == h, SLOPES[g * HG + h], slope)
            t = lax.broadcasted_iota(jnp.int32, (HG, lb), 1)
            s_b = jnp.where(t % dil == 0, st[g]["s_b"] - slope * (lb - t).astype(F32), NEG_INF)
            m = jnp.maximum(jnp.max(s_b, axis=-1, keepdims=True), st[g]["s_n"])
            p_b = jnp.exp(s_b - m)
            p_n = jnp.exp(st[g]["s_n"] - m)
            p_rows = jnp.broadcast_to(p_b[:, None, :], (HG, HEAD_DIM, lb)).reshape(GROUP_W, lb)
            acc = jnp.sum(p_rows * c_refs[g][1], axis=1, keepdims=True)
            st[g].update(m=m, p_n=p_n, l=jnp.sum(p_b, axis=-1, keepdims=True) + p_n, acc=acc)

    def merge():
        m_all = functools.reduce(jnp.maximum, [s["m"] for s in st])
        sc = [jnp.exp(s["m"] - m_all) for s in st]
        den = sum(c * s["l"] for c, s in zip(sc, st))
        out = jnp.zeros((1, GROUP_W), F32)
        for c, s in zip(sc, st):
            acc_row = jnp.broadcast_to(s["acc"], (GROUP_W, rows)).T[0:1, :]
            out = out + per_head(c / den) * (acc_row + per_head(s["p_n"]) * s["vn"])
        o_ref[...] = out.astype(o_ref.dtype)

    return [scores, values, merge]


def _attn_sample_operands(qkv, caches):
    b = qkv.shape[0]
    args = [qkv.reshape(b, 1, QKV_W)]
    specs = [pl.BlockSpec((None, 1, QKV_W), lambda i: (i, 0, 0))]
    for g, c in enumerate(caches):
        lb = c.shape[1]
        assert lb == BAND * DILATIONS[g]
        args.append(jnp.transpose(c, (0, 2, 3, 4, 1)).reshape(b, 2, GROUP_W, lb))
        specs.append(pl.BlockSpec((None, 2, GROUP_W, lb), lambda i: (i, 0, 0, 0)))
    out_spec = pl.BlockSpec((None, 1, GROUP_W), lambda i: (i, 0, 0))
    return args, specs, out_spec, jax.ShapeDtypeStruct((b, 1, GROUP_W), BF16)


def _attn_sample(qkv, caches):
    args, specs, out_spec, out_shape = _attn_sample_operands(qkv, caches)
    out = pl.pallas_call(
        _attn_sample_body,
        grid=(qkv.shape[0],),
        in_specs=specs,
        out_specs=out_spec,
        out_shape=out_shape,
        compiler_params=_params(("parallel",)),
        name="attn_sample",
    )(*args)
    return out.reshape(-1, GROUP_W)


WKV_PASSES = dict(g=1, s0=1, x=1, inv=1, y=1, s1=1)


def _mm(a, b, dims, passes):
    if passes == 1:
        return _dg(a.astype(BF16), b.astype(BF16), dims)
    ah, al = _split2(a)
    if passes == 2:
        bh = b.astype(BF16)
        return _dg(ah, bh, dims) + _dg(al, bh, dims)
    bh, bl = _split2(b)
    return _dg(ah, bh, dims) + _dg(al, bh, dims) + _dg(ah, bl, dims)


def _wkv_levels(live_rows):
    return int(np.ceil(np.log2(live_rows))) if live_rows > 1 else 0


def _wkv_chunk(operands, states, levels, side=()):
    side = list(side)

    def run_side():
        if side:
            side.pop(0)()
    c = CHUNK
    n2 = 2 * c
    n_seq = len(operands)
    first = lax.broadcasted_iota(jnp.int32, (1, PAIR_W), 1) < RWKV_N

    def stack(x, p):
        x = x[:, p * PAIR_W:(p + 1) * PAIR_W]
        return jnp.concatenate([jnp.where(first, x, 0.0), jnp.where(first, 0.0, x)], axis=0)

    bi = lax.broadcasted_iota(jnp.int32, (n_seq * c, n_seq * c), 0)
    bj = lax.broadcasted_iota(jnp.int32, (n_seq * c, n_seq * c), 1)
    tri_all = ((bi >= bj) & (bi // c == bj // c)).astype(BF16)
    lc_all = _dot_exact_rhs_left(tri_all, jnp.concatenate([ops[3] for ops in operands], axis=0))
    ar, bk, v_s, s0, w_end = [], [], [], [], []
    for j, ((r, kp, v, logw, av, bv), seq_states) in enumerate(zip(operands, states)):
        lc = lc_all[j * c:(j + 1) * c]
        e_in = jnp.exp(lc)
        e_neg = jnp.exp(-lc)
        a_t = av * jnp.exp(lc - logw)
        r_t = r * e_in
        b_t = bv * e_neg
        k_t = kp * e_neg
        for p in range(N_PAIRS):
            ar.append(jnp.concatenate([stack(a_t, p), stack(r_t, p)], axis=0))
            bk.append(jnp.concatenate([stack(b_t, p), stack(k_t, p)], axis=0))
            v_s.append(stack(v, p))
            s0.append(seq_states[p])
            w_end.append(e_in[c - 1:c, p * PAIR_W:(p + 1) * PAIR_W])
    chains = range(len(ar))
    g = [_mm(ar[i], bk[i], NT, WKV_PASSES["g"]) for i in chains]
    ar_s0 = [_mm(ar[i], s0[i], NT, WKV_PASSES["s0"]) for i in chains]
    run_side()
    ri = lax.broadcasted_iota(jnp.int32, (n2, n2), 0)
    ci = lax.broadcasted_iota(jnp.int32, (n2, n2), 1)
    strict = ri > ci
    incl = ri >= ci
    n_ab = [jnp.where(strict, g[i][0:n2, 0:n2], 0.0) for i in chains]
    n_ak = [jnp.where(strict, g[i][0:n2, n2:2 * n2], 0.0) for i in chains]
    m_r = [jnp.where(jnp.concatenate([incl, incl], axis=1), g[i][n2:2 * n2, :], 0.0) for i in chains]
    z = [ar_s0[i][0:n2] + _mm(n_ak[i], v_s[i], NN, WKV_PASSES["x"]) for i in chains]
    pw = n_ab
    for lvl in range(levels):
        if lvl < levels - 1:
            pz = [_mm(pw[i], jnp.concatenate([pw[i], z[i]], axis=1), NN, WKV_PASSES["inv"]) for i in chains]
            pw = [pz[i][:, 0:n2] for i in chains]
            z = [z[i] + pz[i][:, n2:2 * n2] for i in chains]
        else:
            z = [z[i] + _mm(pw[i], z[i], NN, WKV_PASSES["inv"]) for i in chains]
        if lvl % 2 == 1:
            run_side()
    while side:
        run_side()
    uv = [jnp.concatenate([z[i], v_s[i]], axis=0) for i in chains]
    y_s = [ar_s0[i][n2:2 * n2] + _mm(m_r[i], uv[i], NN, WKV_PASSES["y"]) for i in chains]
    s1 = [(s0[i] + _mm(uv[i], bk[i], TN, WKV_PASSES["s1"])) * w_end[i] for i in chains]
    ys = [jnp.concatenate([y_s[j * N_PAIRS + p][0:c] + y_s[j * N_PAIRS + p][c:n2] for p in range(N_PAIRS)], axis=1)
          for j in range(n_seq)]
    new_states = tuple(tuple(s1[j * N_PAIRS:(j + 1) * N_PAIRS]) for j in range(n_seq))
    return ys, new_states


def _dot_exact_rhs_left(lhs_bf16, b):
    out = None
    rem = b
    for _ in range(3):
        part = rem.astype(BF16)
        term = _dg(lhs_bf16, part)
        out = term if out is None else out + term
        rem = rem - part.astype(F32)
    return out


def _rwkv_body(zb_ref, mu_ref, w0_ref, a0_ref, kk_ref, ka_ref, rk_ref, gg_ref, gb_ref, wwa_ref, wg_ref, *refs,
               t_valid, fresh):
    if fresh:
        sh_ref = s0_ref = None
    else:
        sh_ref, s0_ref = refs[:2]
        refs = refs[2:]
    out_ref, s_out_ref, sh_out_ref, s_scr, carry_scr, ring0_scr, ring1_scr = refs
    t = pl.program_id(1)
    n_seq, tc, _ = zb_ref.shape

    @pl.when(t == 0)
    def _():
        if fresh:
            s_scr[...] = jnp.zeros(s_scr.shape, F32)
            carry_scr[...] = jnp.zeros(carry_scr.shape, F32)
        else:
            carry_scr[...] = sh_ref[...]
            zero = jnp.zeros((RWKV_N, RWKV_N), F32)
            for j in range(n_seq):
                for p in range(N_PAIRS):
                    top = jnp.concatenate([s0_ref[j, 2 * p], zero], axis=1)
                    bottom = jnp.concatenate([zero, s0_ref[j, 2 * p + 1]], axis=1)
                    s_scr[j, p] = jnp.concatenate([top, bottom], axis=0)

    lr = lax.broadcasted_iota(jnp.int32, (2 * PAIR_W, 2 * PAIR_W), 0)
    lc = lax.broadcasted_iota(jnp.int32, (2 * PAIR_W, 2 * PAIR_W), 1)
    seg = (lr // RWKV_N == lc // RWKV_N).astype(BF16)
    half = lax.broadcasted_iota(jnp.int32, (1, 128), 1) < 64
    row = lax.broadcasted_iota(jnp.int32, (CHUNK, 1), 0)
    inv_n = 1.0 / RWKV_N
    levels = _wkv_levels(CHUNK if t_valid is None else min(t_valid, CHUNK))
    short = tc < CHUNK

    def head_sum(x):
        wide = 2 * PAIR_W
        return jnp.concatenate(
            [_dg(x[:, c:c + wide].astype(BF16), seg) for c in range(0, RWKV_W, wide)], axis=1)

    n_chunks = max(tc // CHUNK, 1)
    rings = (ring0_scr, ring1_scr)

    def staging_jobs(ci, ring):
        rows = pl.ds(pl.multiple_of(ci * CHUNK, CHUNK), CHUNK)
        st = [dict() for _ in range(n_seq)]
        seq_rows = lambda x, j: x[j * CHUNK:(j + 1) * CHUNK]

        def mix_and_lora():
            x_wa, x_g = [], []
            for j in range(n_seq):
                if short:
                    z = jnp.where(row == 0, zb_ref[j], 0.0)
                    before = carry_scr[j]
                else:
                    z = zb_ref[j, rows, :]
                    before = zb_ref[j, pl.ds(jnp.maximum(ci * CHUNK - 1, 0), 1), :]
                    before = jnp.where(ci == 0, carry_scr[j], before)
                zp = jnp.where(row == 0, before, pltpu.roll(z, 1, 0))
                zm = z + (zp - z) * mu_ref[...]
                z_wa = zm[:, 3 * RWKV_W:3 * RWKV_W + 128]
                x_wa.append(jnp.where(half, jnp.tanh(z_wa), z_wa).astype(BF16))
                x_g.append(_sigmoid(zm[:, 3 * RWKV_W + 128:SHIFT_W]).astype(BF16))
                st[j].update(r=zm[:, 0:RWKV_W], k=zm[:, RWKV_W:2 * RWKV_W], v=zm[:, 2 * RWKV_W:3 * RWKV_W])
            x_wa = jnp.concatenate(x_wa, axis=0)
            u = _dg(x_wa, wwa_ref[:, 0:RWKV_W])
            la = _dg(x_wa, wwa_ref[:, RWKV_W:2 * RWKV_W])
            gate = _dg(jnp.concatenate(x_g, axis=0), wg_ref[...])
            for j in range(n_seq):
                st[j].update(u=seq_rows(u, j), la=seq_rows(la, j), gate=seq_rows(gate, j))

        def head_sums():
            for j in range(n_seq):
                s = st[j]
                a = _sigmoid(a0_ref[...] + s["la"])
                kk = s["k"] * kk_ref[...]
                kp = s["k"] * (1.0 + (a - 1.0) * ka_ref[...])
                s.update(a=a, kk=kk, kp=kp)
            kk2 = head_sum(jnp.concatenate([s["kk"] * s["kk"] for s in st], axis=0))
            rk = head_sum(jnp.concatenate([s["r"] * s["kp"] * rk_ref[...] for s in st], axis=0))
            for j in range(n_seq):
                st[j].update(kk2=seq_rows(kk2, j), rk=seq_rows(rk, j))

        def finish():
            for j in range(n_seq):
                s = st[j]
                logw = -DECAY_SCALE * _sigmoid(w0_ref[...] + s["u"])
                kk = s["kk"] * lax.rsqrt(jnp.maximum(s["kk2"], 1e-24))
                ops = (s["r"], s["kp"], s["v"], logw, -kk, kk * s["a"], s["rk"] * s["v"])
                if t_valid is not None:
                    live = row + ci * CHUNK < t_valid
                    ops = tuple(jnp.where(live, x, 0.0) for x in ops)
                for i, x in enumerate(ops + (s["gate"],)):
                    ring[j, i] = x

        return mix_and_lora, head_sums, finish

    def chunk(ci, states, ring, other):
        side = staging_jobs(jnp.minimum(ci + 1, n_chunks - 1), other) if n_chunks > 1 else ()
        ys, states = _wkv_chunk([tuple(ring[j, i] for i in range(6)) for j in range(n_seq)], states, levels, side)
        y_all = jnp.concatenate(ys, axis=0)
        d_all = y_all - head_sum(y_all) * inv_n
        var_all = head_sum(d_all * d_all) * inv_n
        yn_all = d_all * lax.rsqrt(var_all + GN_EPS) * gg_ref[...] + gb_ref[...]
        for j in range(n_seq):
            res = (yn_all[j * CHUNK:(j + 1) * CHUNK] + ring[j, 6]) * ring[j, 7]
            if short:
                out_ref[j] = res[0:tc, :].astype(out_ref.dtype)
            else:
                out_ref[j, pl.ds(pl.multiple_of(ci * CHUNK, CHUNK), CHUNK), :] = res.astype(out_ref.dtype)
        return states

    for job in staging_jobs(0, rings[0]):
        job()
    states = tuple(tuple(s_scr[j, p] for p in range(N_PAIRS)) for j in range(n_seq))
    if n_chunks == 1:
        states = chunk(0, states, rings[0], rings[1])
    else:
        assert n_chunks % 2 == 0

        def chunk_pair(i, states):
            states = chunk(2 * i, states, rings[0], rings[1])
            return chunk(2 * i + 1, states, rings[1], rings[0])

        states = lax.fori_loop(0, n_chunks // 2, chunk_pair, states)
    for j in range(n_seq):
        carry_scr[j] = zb_ref[j, tc - 1:tc, :]
        for p in range(N_PAIRS):
            s_scr[j, p] = states[j][p]

    @pl.when(t == pl.num_programs(1) - 1)
    def _():
        for j in range(n_seq):
            sh_out_ref[j] = zb_ref[j, tc - 1:tc, :]
            for p in range(N_PAIRS):
                for e in range(2):
                    span = slice(e * RWKV_N, (e + 1) * RWKV_N)
                    s_out_ref[j, 2 * p + e] = states[j][p][span, span]


def _rwkv(zb, w, tc, n_seq, state=None, t_valid=None):
    b, t_len, _ = zb.shape
    full = lambda a: pl.BlockSpec(a.shape, lambda bi, t: (0,) * a.ndim)
    per_seq = lambda *dims: pl.BlockSpec((n_seq,) + dims, lambda bi, t: (bi,) + (0,) * len(dims))
    consts = [w["mu_shift"], w["w0"], w["a0"], w["k_k"], w["k_a"], w["r_k"], w["gn_g"], w["gn_b"], w["wa_lora"],
              w["g_lora"]]
    in_specs = [pl.BlockSpec((n_seq, tc, SHIFT_W), lambda bi, t: (bi, t, 0))] + [full(c) for c in consts]
    args = [zb, *consts]
    if state is not None:
        in_specs += [per_seq(1, SHIFT_W), per_seq(RWKV_HEADS, RWKV_N, RWKV_N)]
        args += list(state)
    vm = lambda shape: pltpu.VMEM(shape, F32)
    assert b % n_seq == 0 and t_len % tc == 0
    assert tc % CHUNK == 0 or (tc == t_len == t_valid == 1), "whole chunks, or single-token sequences"
    scratch = [vm((n_seq, N_PAIRS, PAIR_W, PAIR_W)), vm((n_seq, 1, SHIFT_W))] + [vm((n_seq, 8, CHUNK, RWKV_W))] * 2
    return pl.pallas_call(
        functools.partial(_rwkv_body, t_valid=t_valid, fresh=state is None),
        grid=(b // n_seq, t_len // tc),
        in_specs=in_specs,
        out_specs=[pl.BlockSpec((n_seq, tc, RWKV_W), lambda bi, t: (bi, t, 0)),
                   per_seq(RWKV_HEADS, RWKV_N, RWKV_N), per_seq(1, SHIFT_W)],
        out_shape=[jax.ShapeDtypeStruct((b, t_len, RWKV_W), BF16),
                   jax.ShapeDtypeStruct((b, RWKV_HEADS, RWKV_N, RWKV_N), F32),
                   jax.ShapeDtypeStruct((b, 1, SHIFT_W), F32)],
        scratch_shapes=scratch,
        compiler_params=_params(("parallel", "arbitrary")),
        name="rwkv",
    )(*args)


def _out_proj_body(x_ref, attn_ref, rwkv_ref, gate_ref, wpa_ref, wpb_ref, wout_ref, n2_ref,
                   wup_ref, wdn_ref, nf_ref, *refs):
    rider = []
    if len(refs) > 1:
        qkv_s_ref, c0_ref, c1_ref, c2_ref, y_ref, attn_s_ref = refs
        rider = _attn_sample_jobs(qkv_s_ref, c0_ref, c1_ref, c2_ref, attn_s_ref)
    else:
        y_ref, = refs
    pa = _dg(attn_ref[...], wpa_ref[...])
    pb = _dg(rwkv_ref[...], wpb_ref[...])
    merged = gate_ref[:, 0:D_MODEL] * pa + gate_ref[:, D_MODEL:GATE_W] * pb
    x1 = x_ref[...] + _dg(merged.astype(BF16), wout_ref[...])
    hm = _rms(x1, n2_ref[...]).astype(BF16)
    acc = x1
    for c in range(0, D_FF, 1024):
        up = jnp.maximum(_dg(hm, wup_ref[:, c:c + 1024]), 0.0)
        if rider:
            rider.pop(0)()
        acc = acc + _dg((up * up).astype(BF16), wdn_ref[c:c + 1024, :])
    y_ref[...] = _rms(acc, nf_ref[...])


def _out_proj(x, attn, rwkv, gate, w, tm, rider=None):
    m = x.shape[0]
    row = lambda wd: pl.BlockSpec((tm, wd), lambda i: (i, 0))
    full = lambda a: pl.BlockSpec(a.shape, lambda i: (0,) * a.ndim)
    consts = [w["w_proj_a"], w["w_proj_b"], w["w_out"], w["norm2_g"], w["w_up"], w["w_down"], w["normf_g"]]
    in_specs = [row(D_MODEL), row(GROUP_W), row(RWKV_W), row(GATE_W)] + [full(c) for c in consts]
    args = [x, attn, rwkv, gate, *consts]
    out_specs = [row(D_MODEL)]
    out_shape = [jax.ShapeDtypeStruct((m, D_MODEL), F32)]
    if rider is not None:
        r_args, r_specs, r_out_spec, r_out_shape = _attn_sample_operands(*rider)
        assert r_out_shape.shape[0] == m // tm
        in_specs += r_specs
        args += r_args
        out_specs.append(r_out_spec)
        out_shape.append(r_out_shape)
    outs = pl.pallas_call(
        _out_proj_body,
        grid=(m // tm,),
        in_specs=in_specs,
        out_specs=out_specs,
        out_shape=out_shape,
        compiler_params=_params(("parallel",)),
        name="out_proj",
    )(*args)
    if rider is None:
        return outs[0]
    return outs[0], outs[1].reshape(-1, GROUP_W)


def _kv_rows(qkv, group, rows):
    b, s, _ = qkv.shape
    k0 = ATTN_W + group * GROUP_W
    v0 = 2 * ATTN_W + group * GROUP_W
    k = qkv[:, s - rows:, k0:k0 + GROUP_W].reshape(b, rows, HG, HEAD_DIM)
    v = qkv[:, s - rows:, v0:v0 + GROUP_W].reshape(b, rows, HG, HEAD_DIM)
    return jnp.stack([k, v], axis=2)


def _layer_weights(l, norm1_g, w_in, b_gate, mu_shift, w0, w_lora_up, a0, a_lora_up, g_lora_up, k_k, k_a,
                   r_k, gn_g, gn_b, w_proj_a, w_proj_b, w_out, norm2_g, w_up, w_down, normf_g):
    row = lambda a: a.reshape(1, -1)
    zero = jnp.zeros_like(w_lora_up[l])
    wa_lora = jnp.concatenate([jnp.concatenate([w_lora_up[l], zero], axis=1),
                               jnp.concatenate([zero, a_lora_up[l]], axis=1)], axis=0)
    return dict(
        norm1_g=row(norm1_g[l]), w_in=w_in[l].astype(BF16), b_gate=row(b_gate[l]), mu_shift=row(mu_shift[l]),
        w0=row(w0[l]), a0=row(a0[l]), k_k=row(k_k[l]), k_a=row(k_a[l]), r_k=row(r_k[l]),
        gn_g=row(gn_g[l]), gn_b=row(gn_b[l]),
        wa_lora=wa_lora.astype(BF16), g_lora=g_lora_up[l].astype(BF16),
        w_proj_a=w_proj_a[l].astype(BF16), w_proj_b=w_proj_b[l].astype(BF16), w_out=w_out[l].astype(BF16),
        norm2_g=row(norm2_g[l]), w_up=w_up[l].astype(BF16), w_down=w_down[l].astype(BF16),
        normf_g=row(normf_g))


def _layer(xp, xs, caches, s0, shift0, w):
    b, s, _ = xp.shape
    bs, t_len, _ = xs.shape
    assert t_len == 1
    xp2 = xp.reshape(b * s, D_MODEL)
    xs2 = xs.reshape(bs, D_MODEL)
    qkv_s, zb_s, gate_s = _in_proj(xs2, w["norm1_g"], w["w_in"], w["b_gate"], bs)
    qkv, zb, gate, *kv_t = _in_proj(xp2, w["norm1_g"], w["w_in"], w["b_gate"], PROMPT_TILE, seq_len=s)
    zb = zb.reshape(b, s, SHIFT_W)
    attn = _attn_prompt(qkv.reshape(b, s, QKV_W))
    rwkv, wkv_p, shift_p = _rwkv(zb, w, RWKV_TILE, RWKV_SEQS_PER_STEP)
    attn2, rwkv2 = attn.reshape(b * s, GROUP_W), rwkv.reshape(b * s, RWKV_W)
    if b * s // PROMPT_TILE == bs:
        y_p, attn_s = _out_proj(xp2, attn2, rwkv2, gate, w, PROMPT_TILE, rider=(qkv_s, caches))
    else:
        y_p = _out_proj(xp2, attn2, rwkv2, gate, w, PROMPT_TILE)
        attn_s = _attn_sample(qkv_s, caches)
    rwkv_s, wkv_s, _ = _rwkv(zb_s[:, None, :], w, 1, RWKV_SEQS_PER_STEP, state=(shift0[:, None, :], s0), t_valid=1)
    y_s = _out_proj(xs2, attn_s, rwkv_s[:, 0], gate_s, w, bs)
    kv_p = [jnp.transpose(t.reshape(b, 2, HG, HEAD_DIM, t.shape[-1]), (0, 4, 1, 2, 3)) for t in kv_t]
    kv_s = [_kv_rows(qkv_s[:, None, :], g, 1) for g in range(len(DILATIONS))]
    prompt = (y_p.reshape(b, s, D_MODEL), kv_p, wkv_p, shift_p[:, 0])
    sample = (y_s.reshape(bs, 1, D_MODEL), kv_s, wkv_s, zb_s)
    return prompt, sample


def kernel(x_prompt, x_sample, cache_kv_w128, cache_kv_w512, cache_kv_w2048, state_wkv, state_shift, norm1_g, w_in, b_gate, mu_shift, w0, w_lora_up, a0, a_lora_up, g_lora_up, k_k, k_a, r_k, gn_g, gn_b, w_proj_a, w_proj_b, w_out, norm2_g, w_up, w_down, normf_g):
    depth = w_in.shape[0]
    assert depth == 1, "the final norm is fused into the layer's output stage"
    w = _layer_weights(0, norm1_g, w_in, b_gate, mu_shift, w0, w_lora_up, a0, a_lora_up, g_lora_up, k_k, k_a,
                       r_k, gn_g, gn_b, w_proj_a, w_proj_b, w_out, norm2_g, w_up, w_down, normf_g)
    (y_p, kv_p, wkv_p, shift_p), (y_s, kv_s, wkv_s, shift_s) = _layer(
        x_prompt, x_sample, (cache_kv_w128[0], cache_kv_w512[0], cache_kv_w2048[0]), state_wkv[0], state_shift[0], w)
    lead = lambda a: a[None]
    return (y_p, y_s, lead(kv_p[0]), lead(kv_p[1]), lead(kv_p[2]), lead(wkv_p), lead(shift_p),
            lead(kv_s[0]), lead(kv_s[1]), lead(kv_s[2]), lead(wkv_s), lead(shift_s))
```

```python
import functools

import numpy as np
import jax
import jax.numpy as jnp
from jax import lax
from jax.experimental import pallas as pl
from jax.experimental.pallas import tpu as pltpu

F32 = jnp.float32
BF16 = jnp.bfloat16

D_MODEL = 1024
HEAD_DIM = 64
HG = 4
DILATIONS = (1, 4, 16)
BAND = 128
N_ATTN_HEADS = HG * len(DILATIONS)
GROUP_W = HG * HEAD_DIM
LANE_HALVES = GROUP_W // 128
ATTN_W = N_ATTN_HEADS * HEAD_DIM
QKV_W = 3 * ATTN_W
RWKV_N = 64
RWKV_W = 512
RWKV_HEADS = RWKV_W // RWKV_N
PAIR_W = 2 * RWKV_N
N_PAIRS = RWKV_HEADS // 2
LORA_W = 256
SHIFT_W = 3 * RWKV_W + LORA_W
GATE_W = 2 * D_MODEL
IN_W = QKV_W + SHIFT_W + GATE_W
D_FF = 4 * D_MODEL
NORM_EPS = 1e-6
GN_EPS = 64e-5
NEG_INF = -1e30
ATTN_SCALE = HEAD_DIM ** -0.5
CHUNK = 64
RWKV_SEQS_PER_STEP = 4
RWKV_TILE = 256
PROMPT_TILE = 512
DECAY_SCALE = float(np.exp(-0.5))
SLOPES = [float(s) for s in np.exp2(-8.0 * np.arange(1, N_ATTN_HEADS + 1, dtype=np.float32) / N_ATTN_HEADS)]

V7X_VMEM_LIMIT = 56 * 1024 * 1024

NN = (((1,), (0,)), ((), ()))
NT = (((1,), (1,)), ((), ()))
TN = (((0,), (0,)), ((), ()))


def _dg(a, b, dims=NN):
    return lax.dot_general(a, b, dims, preferred_element_type=F32)


def _split2(a):
    hi = a.astype(BF16)
    lo = (a - hi.astype(F32)).astype(BF16)
    return hi, lo


def _dot3(a, b, dims=NN):
    ah, al = _split2(a)
    bh, bl = _split2(b)
    return _dg(ah, bh, dims) + _dg(al, bh, dims) + _dg(ah, bl, dims)


def _dot_exact_rhs(a, b_bf16, passes):
    out = None
    rem = a
    for _ in range(passes):
        part = rem.astype(BF16)
        term = _dg(part, b_bf16)
        out = term if out is None else out + term
        rem = rem - part.astype(F32)
    return out


def _sigmoid(x):
    return 0.5 * jnp.tanh(0.5 * x) + 0.5


def _rms(x, g):
    return x * lax.rsqrt(jnp.mean(x * x, axis=-1, keepdims=True) + NORM_EPS) * g


def _params(sem):
    return pltpu.CompilerParams(dimension_semantics=sem, vmem_limit_bytes=V7X_VMEM_LIMIT)


def _kv_tail_plan(seq_len, tm):
    plan = []
    for dil in DILATIONS:
        rows = min(BAND * dil, seq_len)
        width = min(rows, tm)
        plan.append((rows, width, (seq_len - rows) // tm))
    return plan


def _in_proj_body(x_ref, g_ref, w_ref, bg_ref, qkv_ref, zb_ref, gate_ref, *kv_refs, seq_len):
    tm = x_ref.shape[0]

    def write_tail(g, kv_ref, width):
        for sec in (1, 2):
            c0 = sec * ATTN_W + g * GROUP_W
            kv_ref[sec - 1] = qkv_ref[tm - width:tm, c0:c0 + GROUP_W].T

    plan = _kv_tail_plan(seq_len, tm) if kv_refs else []
    h = _rms(x_ref[...], g_ref[...]).astype(BF16)
    for c in range(0, GATE_W, 1024):
        zg = _dg(h, w_ref[:, QKV_W + SHIFT_W + c:QKV_W + SHIFT_W + c + 1024])
        gate_ref[:, c:c + 1024] = _sigmoid(zg + bg_ref[:, c:c + 1024])
    for c in range(0, QKV_W, 768):
        qkv_ref[:, c:c + 768] = _dg(h, w_ref[:, c:c + 768])
    for g, (kv_ref, (_, width, first)) in enumerate(zip(kv_refs, plan)):
        if first == 0:
            write_tail(g, kv_ref, width)
    for c in range(0, SHIFT_W, 896):
        zb_ref[:, c:c + 896] = _dg(h, w_ref[:, QKV_W + c:QKV_W + c + 896])
    for g, (kv_ref, (_, width, first)) in enumerate(zip(kv_refs, plan)):
        if first > 0:
            tile = pl.program_id(0) % (seq_len // tm)
            pl.when(tile >= first)(functools.partial(write_tail, g, kv_ref, width))


def _in_proj(x, norm_g, w_in_bf16, b_gate, tm, seq_len=None):
    m = x.shape[0]
    row = lambda w: pl.BlockSpec((tm, w), lambda i: (i, 0))
    full = lambda a: pl.BlockSpec(a.shape, lambda i: (0,) * a.ndim)
    out_specs = [row(QKV_W), row(SHIFT_W), row(GATE_W)]
    out_shape = [jax.ShapeDtypeStruct((m, QKV_W), F32),
                 jax.ShapeDtypeStruct((m, SHIFT_W), F32),
                 jax.ShapeDtypeStruct((m, GATE_W), F32)]
    if seq_len is not None:
        tiles = seq_len // tm
        for rows, width, first in _kv_tail_plan(seq_len, tm):
            out_specs.append(pl.BlockSpec(
                (None, 2, GROUP_W, width),
                lambda i, first=first: (i // tiles, 0, 0, jnp.maximum(i % tiles - first, 0))))
            out_shape.append(jax.ShapeDtypeStruct((m // seq_len, 2, GROUP_W, rows), F32))
    return pl.pallas_call(
        functools.partial(_in_proj_body, seq_len=seq_len),
        grid=(m // tm,),
        in_specs=[row(D_MODEL), full(norm_g), full(w_in_bf16), full(b_gate)],
        out_specs=out_specs,
        out_shape=out_shape,
        compiler_params=_params(("arbitrary",)),
        name="in_proj",
    )(x, norm_g, w_in_bf16, b_gate)


def _residue_rows(r, count, dil):
    return pl.ds(r, count) if dil == 1 else pl.ds(r, count, stride=dil)


def _attn_group_blocks(q_ref, k_ref, v_ref, o_scr, lse_scr, qs, ks, vs, os_, ls, bias_scr, tmp, group):
    s_len = qs.shape[0]
    dil = DILATIONS[group]
    l_res = s_len // dil
    nb = l_res // BAND
    inner = 4 if dil > 4 else dil
    outer = dil // inner
    l_in = s_len // inner
    assert dil in (1, inner, inner * outer) and outer <= 4

    def gather(halves, r):
        if outer == 1:
            return jnp.concatenate([h[_residue_rows(r, l_res, dil), :] for h in halves], axis=1)
        c, q4 = r % inner, r // inner
        return jnp.concatenate([tmp[i, pl.ds(c * l_in + q4, l_res, stride=outer), :]
                                for i in range(LANE_HALVES)], axis=1)

    head = lax.broadcasted_iota(jnp.int32, (1, GROUP_W), 1) // HEAD_DIM
    hm = [head == h for h in range(HG)]
    for name, halves in (("q", q_ref), ("k", k_ref), ("v", v_ref)):
        if outer > 1:
            for i, h in enumerate(halves):
                for c in range(inner):
                    tmp[i, pl.ds(c * l_in, l_in), :] = h[_residue_rows(c, l_in, inner), :]
        for r in range(dil):
            dst = pl.ds(r * l_res, l_res)
            x = gather(halves, r)
            if name == "q":
                qs[dst, :] = (x * ATTN_SCALE).astype(BF16)
            elif name == "k":
                ks[dst, :] = x.astype(BF16)
            else:
                vs[dst, :] = x.astype(BF16)
    has_prev = nb > 1
    nk = 2 * BAND if has_prev else BAND
    qi = lax.broadcasted_iota(jnp.int32, (BAND, nk), 0)
    kj = lax.broadcasted_iota(jnp.int32, (BAND, nk), 1)
    delta = (nk - BAND) + qi - kj
    band = (delta >= 0) & (delta <= BAND)
    dist = (delta * dil).astype(F32)
    for h in range(HG):
        alibi = -SLOPES[group * HG + h] * dist
        bias_scr[h, :, 0:nk] = jnp.where(band, alibi, NEG_INF)
        bias_scr[HG + h, :, 0:nk] = jnp.where(band & (kj >= nk - BAND), alibi, NEG_INF)
    blocks_per_iter = 8

    def block_pair(it, carry):
        cur, q, k2, v2, first = [], [], [], [], []
        for u in range(blocks_per_iter):
            idx = it * blocks_per_iter + u
            cur.append(pl.ds(pl.multiple_of(idx * BAND, BAND), BAND))
            q.append(qs[cur[u], :])
            k2.append(ks[cur[u], :])
            v2.append(vs[cur[u], :])
            first.append(jnp.where(idx % nb == 0, HG, 0))
            if has_prev:
                prev = pl.ds(pl.multiple_of(jnp.maximum(idx - 1, 0) * BAND, BAND), BAND)
                k2[u] = jnp.concatenate([ks[prev, :], k2[u]], axis=0)
                v2[u] = jnp.concatenate([vs[prev, :], v2[u]], axis=0)
        chains = [(u, h) for u in range(blocks_per_iter) for h in range(HG)]
        s = [_dg(jnp.where(hm[h], q[u], jnp.zeros_like(q[u])), k2[u], NT) + bias_scr[first[u] + h, :, 0:nk]
             for u, h in chains]
        m = [jnp.max(x, axis=-1, keepdims=True) for x in s]
        p = [jnp.exp(x - mx) for x, mx in zip(s, m)]
        l = [jnp.sum(x, axis=-1, keepdims=True) for x in p]
        pv = [_dg(p[i].astype(BF16), v2[u]) for i, (u, h) in enumerate(chains)]
        for u in range(blocks_per_iter):
            o_acc = jnp.zeros((BAND, GROUP_W), F32)
            lse_acc = jnp.zeros((BAND, GROUP_W), F32)
            for h in range(HG):
                i = u * HG + h
                o_acc = jnp.where(hm[h], pv[i] / l[i], o_acc)
                lse_acc = jnp.where(hm[h], m[i] + jnp.log(l[i]), lse_acc)
            if dil == 1:
                for half in range(LANE_HALVES):
                    cols = slice(half * 128, (half + 1) * 128)
                    o_scr[group * LANE_HALVES + half, cur[u], :] = o_acc[:, cols]
                    lse_scr[group * LANE_HALVES + half, cur[u], :] = lse_acc[:, cols]
            else:
                os_[cur[u], :] = o_acc
                ls[cur[u], :] = lse_acc
        return carry

    assert (dil * nb) % blocks_per_iter == 0
    lax.fori_loop(0, dil * nb // blocks_per_iter, block_pair, 0)
    for res, nat in ((os_, o_scr), (ls, lse_scr)) if dil > 1 else ():
        for half in range(LANE_HALVES):
            cols = slice(half * 128, (half + 1) * 128)
            slab = group * LANE_HALVES + half
            for r in range(dil):
                src = pl.ds(r * l_res, l_res)
                if outer == 1:
                    nat[slab, _residue_rows(r, l_res, dil), :] = res[src, cols]
                else:
                    tmp[half, pl.ds((r % inner) * l_in + r // inner, l_res, stride=outer), :] = res[src, cols]
            if outer > 1:
                for c in range(inner):
                    nat[slab, _residue_rows(c, l_in, inner), :] = tmp[half, pl.ds(c * l_in, l_in), :]


def _attn_prompt_body(q0_ref, q1_ref, k0_ref, k1_ref, v0_ref, v1_ref, o_ref, o_scr, lse_scr, qs, ks, vs, os_, ls, bias_scr, tmp):
    q_ref, k_ref, v_ref = (q0_ref, q1_ref), (k0_ref, k1_ref), (v0_ref, v1_ref)
    gid = pl.program_id(1)
    n_groups = len(DILATIONS)
    for g in range(n_groups):
        @pl.when(gid == g)
        def _(g=g):
            _attn_group_blocks(q_ref, k_ref, v_ref, o_scr, lse_scr, qs, ks, vs, os_, ls, bias_scr, tmp, g)

    @pl.when(gid == n_groups - 1)
    def _():
        def merge(i, carry):
            rows = pl.ds(pl.multiple_of(i * BAND, BAND), BAND)
            both = lambda ref, g: jnp.concatenate(
                [ref[g * LANE_HALVES + half, rows, :] for half in range(LANE_HALVES)], axis=1)
            lse = [both(lse_scr, g) for g in range(n_groups)]
            m = functools.reduce(jnp.maximum, lse)
            wts = [jnp.exp(x - m) for x in lse]
            num = sum(wts[g] * both(o_scr, g) for g in range(n_groups))
            o_ref[rows, :] = (num / sum(wts)).astype(o_ref.dtype)
            return carry

        lax.fori_loop(0, o_ref.shape[0] // BAND, merge, 0)


def _attn_prompt(qkv):
    b, s, _ = qkv.shape
    n_groups = len(DILATIONS)
    assert s % (DILATIONS[-1] * BAND) == 0
    sec = ATTN_W // 128

    def col(section, half):
        return pl.BlockSpec((None, s, 128), lambda bi, g: (bi, 0, section * sec + g * LANE_HALVES + half))

    return pl.pallas_call(
        _attn_prompt_body,
        grid=(b, n_groups),
        in_specs=[col(section, half) for section in range(3) for half in range(LANE_HALVES)],
        out_specs=pl.BlockSpec((None, s, GROUP_W), lambda bi, g: (bi, 0, 0)),
        out_shape=jax.ShapeDtypeStruct((b, s, GROUP_W), BF16),
        scratch_shapes=([pltpu.VMEM((n_groups * LANE_HALVES, s, 128), F32)] * 2
                        + [pltpu.VMEM((s, GROUP_W), BF16)] * 3 + [pltpu.VMEM((s, GROUP_W), F32)] * 2
                        + [pltpu.VMEM((2 * HG, BAND, 2 * BAND), F32), pltpu.VMEM((LANE_HALVES, s, 128), F32)]),
        compiler_params=_params(("parallel", "arbitrary")),
        name="attn_prompt",
    )(*([qkv] * (3 * LANE_HALVES)))


def _attn_sample_body(qkv_ref, c0_ref, c1_ref, c2_ref, o_ref):
    for job in _attn_sample_jobs(qkv_ref, c0_ref, c1_ref, c2_ref, o_ref):
        job()


def _attn_sample_jobs(qkv_ref, c0_ref, c1_ref, c2_ref, o_ref):
    rows = 8
    row = lax.broadcasted_iota(jnp.int32, (rows, GROUP_W), 0)
    own = lax.broadcasted_iota(jnp.int32, (rows, GROUP_W), 1) // HEAD_DIM == row
    hrow = lax.broadcasted_iota(jnp.int32, (HG, 1), 0)
    lane_head = lax.broadcasted_iota(jnp.int32, (1, GROUP_W), 1) // HEAD_DIM
    c_refs = (c0_ref, c1_ref, c2_ref)
    groups = range(len(DILATIONS))
    st = [dict() for _ in groups]

    def to_column(x_row):
        return jnp.sum(jnp.where(own, x_row, 0.0).T, axis=1, keepdims=True)

    def per_head(col):
        out = jnp.zeros((1, GROUP_W), F32)
        for h in range(HG):
            out = jnp.where(lane_head == h, col[h:h + 1, :], out)
        return out

    def scores():
        for g in groups:
            lb = c_refs[g].shape[2]
            col = lambda sec: qkv_ref[:, sec * ATTN_W + g * GROUP_W:sec * ATTN_W + (g + 1) * GROUP_W]
            q, kn, vn = col(0), col(1), col(2)
            prod = c_refs[g][0] * to_column(q)
            s_b = jnp.sum(prod.reshape(HG, HEAD_DIM, lb), axis=1) * ATTN_SCALE
            s_n = jnp.sum(jnp.where(own, q * kn, 0.0), axis=-1, keepdims=True)[0:HG] * ATTN_SCALE
            st[g].update(vn=vn, s_b=s_b, s_n=s_n)

    def values():
        for g in groups:
            dil = DILATIONS[g]
            lb = c_refs[g].shape[2]
            slope = jnp.zeros((HG, 1), F32)
            for h in range(HG):
                slope = jnp.where(hrow == h, SLOPES[g * HG + h], slope)
            t = lax.broadcasted_iota(jnp.int32, (HG, lb), 1)
            s_b = jnp.where(t % dil == 0, st[g]["s_b"] - slope * (lb - t).astype(F32), NEG_INF)
            m = jnp.maximum(jnp.max(s_b, axis=-1, keepdims=True), st[g]["s_n"])
            p_b = jnp.exp(s_b - m)
            p_n = jnp.exp(st[g]["s_n"] - m)
            p_rows = jnp.broadcast_to(p_b[:, None, :], (HG, HEAD_DIM, lb)).reshape(GROUP_W, lb)
            acc = jnp.sum(p_rows * c_refs[g][1], axis=1, keepdims=True)
            st[g].update(m=m, p_n=p_n, l=jnp.sum(p_b, axis=-1, keepdims=True) + p_n, acc=acc)

    def merge():
        m_all = functools.reduce(jnp.maximum, [s["m"] for s in st])
        sc = [jnp.exp(s["m"] - m_all) for s in st]
        den = sum(c * s["l"] for c, s in zip(sc, st))
        out = jnp.zeros((1, GROUP_W), F32)
        for c, s in zip(sc, st):
            acc_row = jnp.broadcast_to(s["acc"], (GROUP_W, rows)).T[0:1, :]
            out = out + per_head(c / den) * (acc_row + per_head(s["p_n"]) * s["vn"])
        o_ref[...] = out.astype(o_ref.dtype)

    return [scores, values, merge]


def _attn_sample_operands(qkv, caches):
    b = qkv.shape[0]
    args = [qkv.reshape(b, 1, QKV_W)]
    specs = [pl.BlockSpec((None, 1, QKV_W), lambda i: (i, 0, 0))]
    for g, c in enumerate(caches):
        lb = c.shape[1]
        assert lb == BAND * DILATIONS[g]
        args.append(jnp.transpose(c, (0, 2, 3, 4, 1)).reshape(b, 2, GROUP_W, lb))
        specs.append(pl.BlockSpec((None, 2, GROUP_W, lb), lambda i: (i, 0, 0, 0)))
    out_spec = pl.BlockSpec((None, 1, GROUP_W), lambda i: (i, 0, 0))
    return args, specs, out_spec, jax.ShapeDtypeStruct((b, 1, GROUP_W), BF16)


def _attn_sample(qkv, caches):
    args, specs, out_spec, out_shape = _attn_sample_operands(qkv, caches)
    out = pl.pallas_call(
        _attn_sample_body,
        grid=(qkv.shape[0],),
        in_specs=specs,
        out_specs=out_spec,
        out_shape=out_shape,
        compiler_params=_params(("parallel",)),
        name="attn_sample",
    )(*args)
    return out.reshape(-1, GROUP_W)


WKV_PASSES = dict(g=1, s0=1, x=1, inv=1, y=1, s1=1)


def _mm(a, b, dims, passes):
    if passes == 1:
        return _dg(a.astype(BF16), b.astype(BF16), dims)
    ah, al = _split2(a)
    if passes == 2:
        bh = b.astype(BF16)
        return _dg(ah, bh, dims) + _dg(al, bh, dims)
    bh, bl = _split2(b)
    return _dg(ah, bh, dims) + _dg(al, bh, dims) + _dg(ah, bl, dims)


def _wkv_levels(live_rows):
    return int(np.ceil(np.log2(live_rows))) if live_rows > 1 else 0


def _wkv_chunk(operands, states, levels, side=()):
    side = list(side)

    def run_side():
        if side:
            side.pop(0)()
    c = CHUNK
    n2 = 2 * c
    n_seq = len(operands)
    first = lax.broadcasted_iota(jnp.int32, (1, PAIR_W), 1) < RWKV_N

    def stack(x, p):
        x = x[:, p * PAIR_W:(p + 1) * PAIR_W]
        return jnp.concatenate([jnp.where(first, x, 0.0), jnp.where(first, 0.0, x)], axis=0)

    bi = lax.broadcasted_iota(jnp.int32, (n_seq * c, n_seq * c), 0)
    bj = lax.broadcasted_iota(jnp.int32, (n_seq * c, n_seq * c), 1)
    tri_all = ((bi >= bj) & (bi // c == bj // c)).astype(BF16)
    lc_all = _dot_exact_rhs_left(tri_all, jnp.concatenate([ops[3] for ops in operands], axis=0))
    ar, bk, v_s, s0, w_end = [], [], [], [], []
    for j, ((r, kp, v, logw, av, bv), seq_states) in enumerate(zip(operands, states)):
        lc = lc_all[j * c:(j + 1) * c]
        e_in = jnp.exp(lc)
        e_neg = jnp.exp(-lc)
        a_t = av * jnp.exp(lc - logw)
        r_t = r * e_in
        b_t = bv * e_neg
        k_t = kp * e_neg
        for p in range(N_PAIRS):
            ar.append(jnp.concatenate([stack(a_t, p), stack(r_t, p)], axis=0))
            bk.append(jnp.concatenate([stack(b_t, p), stack(k_t, p)], axis=0))
            v_s.append(stack(v, p))
            s0.append(seq_states[p])
            w_end.append(e_in[c - 1:c, p * PAIR_W:(p + 1) * PAIR_W])
    chains = range(len(ar))
    g = [_mm(ar[i], bk[i], NT, WKV_PASSES["g"]) for i in chains]
    ar_s0 = [_mm(ar[i], s0[i], NT, WKV_PASSES["s0"]) for i in chains]
    run_side()
    ri = lax.broadcasted_iota(jnp.int32, (n2, n2), 0)
    ci = lax.broadcasted_iota(jnp.int32, (n2, n2), 1)
    strict = ri > ci
    incl = ri >= ci
    n_ab = [jnp.where(strict, g[i][0:n2, 0:n2], 0.0) for i in chains]
    n_ak = [jnp.where(strict, g[i][0:n2, n2:2 * n2], 0.0) for i in chains]
    m_r = [jnp.where(jnp.concatenate([incl, incl], axis=1), g[i][n2:2 * n2, :], 0.0) for i in chains]
    z = [ar_s0[i][0:n2] + _mm(n_ak[i], v_s[i], NN, WKV_PASSES["x"]) for i in chains]
    pw = n_ab
    for lvl in range(levels):
        if lvl < levels - 1:
            pz = [_mm(pw[i], jnp.concatenate([pw[i], z[i]], axis=1), NN, WKV_PASSES["inv"]) for i in chains]
            pw = [pz[i][:, 0:n2] for i in chains]
            z = [z[i] + pz[i][:, n2:2 * n2] for i in chains]
        else:
            z = [z[i] + _mm(pw[i], z[i], NN, WKV_PASSES["inv"]) for i in chains]
        if lvl % 2 == 1:
            run_side()
    while side:
        run_side()
    uv = [jnp.concatenate([z[i], v_s[i]], axis=0) for i in chains]
    y_s = [ar_s0[i][n2:2 * n2] + _mm(m_r[i], uv[i], NN, WKV_PASSES["y"]) for i in chains]
    s1 = [(s0[i] + _mm(uv[i], bk[i], TN, WKV_PASSES["s1"])) * w_end[i] for i in chains]
    ys = [jnp.concatenate([y_s[j * N_PAIRS + p][0:c] + y_s[j * N_PAIRS + p][c:n2] for p in range(N_PAIRS)], axis=1)
          for j in range(n_seq)]
    new_states = tuple(tuple(s1[j * N_PAIRS:(j + 1) * N_PAIRS]) for j in range(n_seq))
    return ys, new_states


def _dot_exact_rhs_left(lhs_bf16, b):
    out = None
    rem = b
    for _ in range(3):
        part = rem.astype(BF16)
        term = _dg(lhs_bf16, part)
        out = term if out is None else out + term
        rem = rem - part.astype(F32)
    return out


def _rwkv_body(zb_ref, mu_ref, w0_ref, a0_ref, kk_ref, ka_ref, rk_ref, gg_ref, gb_ref, wwa_ref, wg_ref, *refs,
               t_valid, fresh):
    if fresh:
        sh_ref = s0_ref = None
    else:
        sh_ref, s0_ref = refs[:2]
        refs = refs[2:]
    out_ref, s_out_ref, sh_out_ref, s_scr, carry_scr, ring0_scr, ring1_scr = refs
    t = pl.program_id(1)
    n_seq, tc, _ = zb_ref.shape

    @pl.when(t == 0)
    def _():
        if fresh:
            s_scr[...] = jnp.zeros(s_scr.shape, F32)
            carry_scr[...] = jnp.zeros(carry_scr.shape, F32)
        else:
            carry_scr[...] = sh_ref[...]
            zero = jnp.zeros((RWKV_N, RWKV_N), F32)
            for j in range(n_seq):
                for p in range(N_PAIRS):
                    top = jnp.concatenate([s0_ref[j, 2 * p], zero], axis=1)
                    bottom = jnp.concatenate([zero, s0_ref[j, 2 * p + 1]], axis=1)
                    s_scr[j, p] = jnp.concatenate([top, bottom], axis=0)

    lr = lax.broadcasted_iota(jnp.int32, (2 * PAIR_W, 2 * PAIR_W), 0)
    lc = lax.broadcasted_iota(jnp.int32, (2 * PAIR_W, 2 * PAIR_W), 1)
    seg = (lr // RWKV_N == lc // RWKV_N).astype(BF16)
    half = lax.broadcasted_iota(jnp.int32, (1, 128), 1) < 64
    row = lax.broadcasted_iota(jnp.int32, (CHUNK, 1), 0)
    inv_n = 1.0 / RWKV_N
    levels = _wkv_levels(CHUNK if t_valid is None else min(t_valid, CHUNK))
    short = tc < CHUNK

    def head_sum(x):
        wide = 2 * PAIR_W
        return jnp.concatenate(
            [_dg(x[:, c:c + wide].astype(BF16), seg) for c in range(0, RWKV_W, wide)], axis=1)

    n_chunks = max(tc // CHUNK, 1)
    rings = (ring0_scr, ring1_scr)

    def staging_jobs(ci, ring):
        rows = pl.ds(pl.multiple_of(ci * CHUNK, CHUNK), CHUNK)
        st = [dict() for _ in range(n_seq)]
        seq_rows = lambda x, j: x[j * CHUNK:(j + 1) * CHUNK]

        def mix_and_lora():
            x_wa, x_g = [], []
            for j in range(n_seq):
                if short:
                    z = jnp.where(row == 0, zb_ref[j], 0.0)
                    before = carry_scr[j]
                else:
                    z = zb_ref[j, rows, :]
                    before = zb_ref[j, pl.ds(jnp.maximum(ci * CHUNK - 1, 0), 1), :]
                    before = jnp.where(ci == 0, carry_scr[j], before)
                zp = jnp.where(row == 0, before, pltpu.roll(z, 1, 0))
                zm = z + (zp - z) * mu_ref[...]
                z_wa = zm[:, 3 * RWKV_W:3 * RWKV_W + 128]
                x_wa.append(jnp.where(half, jnp.tanh(z_wa), z_wa).astype(BF16))
                x_g.append(_sigmoid(zm[:, 3 * RWKV_W + 128:SHIFT_W]).astype(BF16))
                st[j].update(r=zm[:, 0:RWKV_W], k=zm[:, RWKV_W:2 * RWKV_W], v=zm[:, 2 * RWKV_W:3 * RWKV_W])
            x_wa = jnp.concatenate(x_wa, axis=0)
            u = _dg(x_wa, wwa_ref[:, 0:RWKV_W])
            la = _dg(x_wa, wwa_ref[:, RWKV_W:2 * RWKV_W])
            gate = _dg(jnp.concatenate(x_g, axis=0), wg_ref[...])
            for j in range(n_seq):
                st[j].update(u=seq_rows(u, j), la=seq_rows(la, j), gate=seq_rows(gate, j))

        def head_sums():
            for j in range(n_seq):
                s = st[j]
                a = _sigmoid(a0_ref[...] + s["la"])
                kk = s["k"] * kk_ref[...]
                kp = s["k"] * (1.0 + (a - 1.0) * ka_ref[...])
                s.update(a=a, kk=kk, kp=kp)
            kk2 = head_sum(jnp.concatenate([s["kk"] * s["kk"] for s in st], axis=0))
            rk = head_sum(jnp.concatenate([s["r"] * s["kp"] * rk_ref[...] for s in st], axis=0))
            for j in range(n_seq):
                st[j].update(kk2=seq_rows(kk2, j), rk=seq_rows(rk, j))

        def finish():
            for j in range(n_seq):
                s = st[j]
                logw = -DECAY_SCALE * _sigmoid(w0_ref[...] + s["u"])
                kk = s["kk"] * lax.rsqrt(jnp.maximum(s["kk2"], 1e-24))
                ops = (s["r"], s["kp"], s["v"], logw, -kk, kk * s["a"], s["rk"] * s["v"])
                if t_valid is not None:
                    live = row + ci * CHUNK < t_valid
                    ops = tuple(jnp.where(live, x, 0.0) for x in ops)
                for i, x in enumerate(ops + (s["gate"],)):
                    ring[j, i] = x

        return mix_and_lora, head_sums, finish

    def chunk(ci, states, ring, other):
        side = staging_jobs(jnp.minimum(ci + 1, n_chunks - 1), other) if n_chunks > 1 else ()
        ys, states = _wkv_chunk([tuple(ring[j, i] for i in range(6)) for j in range(n_seq)], states, levels, side)
        y_all = jnp.concatenate(ys, axis=0)
        d_all = y_all - head_sum(y_all) * inv_n
        var_all = head_sum(d_all * d_all) * inv_n
        yn_all = d_all * lax.rsqrt(var_all + GN_EPS) * gg_ref[...] + gb_ref[...]
        for j in range(n_seq):
            res = (yn_all[j * CHUNK:(j + 1) * CHUNK] + ring[j, 6]) * ring[j, 7]
            if short:
                out_ref[j] = res[0:tc, :].astype(out_ref.dtype)
            else:
                out_ref[j, pl.ds(pl.multiple_of(ci * CHUNK, CHUNK), CHUNK), :] = res.astype(out_ref.dtype)
        return states

    for job in staging_jobs(0, rings[0]):
        job()
    states = tuple(tuple(s_scr[j, p] for p in range(N_PAIRS)) for j in range(n_seq))
    if n_chunks == 1:
        states = chunk(0, states, rings[0], rings[1])
    else:
        assert n_chunks % 2 == 0

        def chunk_pair(i, states):
            states = chunk(2 * i, states, rings[0], rings[1])
            return chunk(2 * i + 1, states, rings[1], rings[0])

        states = lax.fori_loop(0, n_chunks // 2, chunk_pair, states)
    for j in range(n_seq):
        carry_scr[j] = zb_ref[j, tc - 1:tc, :]
        for p in range(N_PAIRS):
            s_scr[j, p] = states[j][p]

    @pl.when(t == pl.num_programs(1) - 1)
    def _():
        for j in range(n_seq):
            sh_out_ref[j] = zb_ref[j, tc - 1:tc, :]
            for p in range(N_PAIRS):
                for e in range(2):
                    span = slice(e * RWKV_N, (e + 1) * RWKV_N)
                    s_out_ref[j, 2 * p + e] = states[j][p][span, span]


def _rwkv(zb, w, tc, n_seq, state=None, t_valid=None):
    b, t_len, _ = zb.shape
    full = lambda a: pl.BlockSpec(a.shape, lambda bi, t: (0,) * a.ndim)
    per_seq = lambda *dims: pl.BlockSpec((n_seq,) + dims, lambda bi, t: (bi,) + (0,) * len(dims))
    consts = [w["mu_shift"], w["w0"], w["a0"], w["k_k"], w["k_a"], w["r_k"], w["gn_g"], w["gn_b"], w["wa_lora"],
              w["g_lora"]]
    in_specs = [pl.BlockSpec((n_seq, tc, SHIFT_W), lambda bi, t: (bi, t, 0))] + [full(c) for c in consts]
    args = [zb, *consts]
    if state is not None:
        in_specs += [per_seq(1, SHIFT_W), per_seq(RWKV_HEADS, RWKV_N, RWKV_N)]
        args += list(state)
    vm = lambda shape: pltpu.VMEM(shape, F32)
    assert b % n_seq == 0 and t_len % tc == 0
    assert tc % CHUNK == 0 or (tc == t_len == t_valid == 1), "whole chunks, or single-token sequences"
    scratch = [vm((n_seq, N_PAIRS, PAIR_W, PAIR_W)), vm((n_seq, 1, SHIFT_W))] + [vm((n_seq, 8, CHUNK, RWKV_W))] * 2
    return pl.pallas_call(
        functools.partial(_rwkv_body, t_valid=t_valid, fresh=state is None),
        grid=(b // n_seq, t_len // tc),
        in_specs=in_specs,
        out_specs=[pl.BlockSpec((n_seq, tc, RWKV_W), lambda bi, t: (bi, t, 0)),
                   per_seq(RWKV_HEADS, RWKV_N, RWKV_N), per_seq(1, SHIFT_W)],
        out_shape=[jax.ShapeDtypeStruct((b, t_len, RWKV_W), BF16),
                   jax.ShapeDtypeStruct((b, RWKV_HEADS, RWKV_N, RWKV_N), F32),
                   jax.ShapeDtypeStruct((b, 1, SHIFT_W), F32)],
        scratch_shapes=scratch,
        compiler_params=_params(("parallel", "arbitrary")),
        name="rwkv",
    )(*args)


def _out_proj_body(x_ref, attn_ref, rwkv_ref, gate_ref, wpa_ref, wpb_ref, wout_ref, n2_ref,
                   wup_ref, wdn_ref, nf_ref, *refs):
    rider = []
    if len(refs) > 1:
        qkv_s_ref, c0_ref, c1_ref, c2_ref, y_ref, attn_s_ref = refs
        rider = _attn_sample_jobs(qkv_s_ref, c0_ref, c1_ref, c2_ref, attn_s_ref)
    else:
        y_ref, = refs
    pa = _dg(attn_ref[...], wpa_ref[...])
    pb = _dg(rwkv_ref[...], wpb_ref[...])
    merged = gate_ref[:, 0:D_MODEL] * pa + gate_ref[:, D_MODEL:GATE_W] * pb
    x1 = x_ref[...] + _dg(merged.astype(BF16), wout_ref[...])
    hm = _rms(x1, n2_ref[...]).astype(BF16)
    acc = x1
    for c in range(0, D_FF, 1024):
        up = jnp.maximum(_dg(hm, wup_ref[:, c:c + 1024]), 0.0)
        if rider:
            rider.pop(0)()
        acc = acc + _dg((up * up).astype(BF16), wdn_ref[c:c + 1024, :])
    y_ref[...] = _rms(acc, nf_ref[...])


def _out_proj(x, attn, rwkv, gate, w, tm, rider=None):
    m = x.shape[0]
    row = lambda wd: pl.BlockSpec((tm, wd), lambda i: (i, 0))
    full = lambda a: pl.BlockSpec(a.shape, lambda i: (0,) * a.ndim)
    consts = [w["w_proj_a"], w["w_proj_b"], w["w_out"], w["norm2_g"], w["w_up"], w["w_down"], w["normf_g"]]
    in_specs = [row(D_MODEL), row(GROUP_W), row(RWKV_W), row(GATE_W)] + [full(c) for c in consts]
    args = [x, attn, rwkv, gate, *consts]
    out_specs = [row(D_MODEL)]
    out_shape = [jax.ShapeDtypeStruct((m, D_MODEL), F32)]
    if rider is not None:
        r_args, r_specs, r_out_spec, r_out_shape = _attn_sample_operands(*rider)
        assert r_out_shape.shape[0] == m // tm
        in_specs += r_specs
        args += r_args
        out_specs.append(r_out_spec)
        out_shape.append(r_out_shape)
    outs = pl.pallas_call(
        _out_proj_body,
        grid=(m // tm,),
        in_specs=in_specs,
        out_specs=out_specs,
        out_shape=out_shape,
        compiler_params=_params(("parallel",)),
        name="out_proj",
    )(*args)
    if rider is None:
        return outs[0]
    return outs[0], outs[1].reshape(-1, GROUP_W)


def _kv_rows(qkv, group, rows):
    b, s, _ = qkv.shape
    k0 = ATTN_W + group * GROUP_W
    v0 = 2 * ATTN_W + group * GROUP_W
    k = qkv[:, s - rows:, k0:k0 + GROUP_W].reshape(b, rows, HG, HEAD_DIM)
    v = qkv[:, s - rows:, v0:v0 + GROUP_W].reshape(b, rows, HG, HEAD_DIM)
    return jnp.stack([k, v], axis=2)


def _layer_weights(l, norm1_g, w_in, b_gate, mu_shift, w0, w_lora_up, a0, a_lora_up, g_lora_up, k_k, k_a,
                   r_k, gn_g, gn_b, w_proj_a, w_proj_b, w_out, norm2_g, w_up, w_down, normf_g):
    row = lambda a: a.reshape(1, -1)
    zero = jnp.zeros_like(w_lora_up[l])
    wa_lora = jnp.concatenate([jnp.concatenate([w_lora_up[l], zero], axis=1),
                               jnp.concatenate([zero, a_lora_up[l]], axis=1)], axis=0)
    return dict(
        norm1_g=row(norm1_g[l]), w_in=w_in[l].astype(BF16), b_gate=row(b_gate[l]), mu_shift=row(mu_shift[l]),
        w0=row(w0[l]), a0=row(a0[l]), k_k=row(k_k[l]), k_a=row(k_a[l]), r_k=row(r_k[l]),
        gn_g=row(gn_g[l]), gn_b=row(gn_b[l]),
        wa_lora=wa_lora.astype(BF16), g_lora=g_lora_up[l].astype(BF16),
        w_proj_a=w_proj_a[l].astype(BF16), w_proj_b=w_proj_b[l].astype(BF16), w_out=w_out[l].astype(BF16),
        norm2_g=row(norm2_g[l]), w_up=w_up[l].astype(BF16), w_down=w_down[l].astype(BF16),
        normf_g=row(normf_g))


def _layer(xp, xs, caches, s0, shift0, w):
    b, s, _ = xp.shape
    bs, t_len, _ = xs.shape
    assert t_len == 1
    xp2 = xp.reshape(b * s, D_MODEL)
    xs2 = xs.reshape(bs, D_MODEL)
    qkv_s, zb_s, gate_s = _in_proj(xs2, w["norm1_g"], w["w_in"], w["b_gate"], bs)
    qkv, zb, gate, *kv_t = _in_proj(xp2, w["norm1_g"], w["w_in"], w["b_gate"], PROMPT_TILE, seq_len=s)
    zb = zb.reshape(b, s, SHIFT_W)
    attn = _attn_prompt(qkv.reshape(b, s, QKV_W))
    rwkv, wkv_p, shift_p = _rwkv(zb, w, RWKV_TILE, RWKV_SEQS_PER_STEP)
    attn2, rwkv2 = attn.reshape(b * s, GROUP_W), rwkv.reshape(b * s, RWKV_W)
    if b * s // PROMPT_TILE == bs:
        y_p, attn_s = _out_proj(xp2, attn2, rwkv2, gate, w, PROMPT_TILE, rider=(qkv_s, caches))
    else:
        y_p = _out_proj(xp2, attn2, rwkv2, gate, w, PROMPT_TILE)
        attn_s = _attn_sample(qkv_s, caches)
    rwkv_s, wkv_s, _ = _rwkv(zb_s[:, None, :], w, 1, RWKV_SEQS_PER_STEP, state=(shift0[:, None, :], s0), t_valid=1)
    y_s = _out_proj(xs2, attn_s, rwkv_s[:, 0], gate_s, w, bs)
    kv_p = [jnp.transpose(t.reshape(b, 2, HG, HEAD_DIM, t.shape[-1]), (0, 4, 1, 2, 3)) for t in kv_t]
    kv_s = [_kv_rows(qkv_s[:, None, :], g, 1) for g in range(len(DILATIONS))]
    prompt = (y_p.reshape(b, s, D_MODEL), kv_p, wkv_p, shift_p[:, 0])
    sample = (y_s.reshape(bs, 1, D_MODEL), kv_s, wkv_s, zb_s)
    return prompt, sample


def kernel(x_prompt, x_sample, cache_kv_w128, cache_kv_w512, cache_kv_w2048, state_wkv, state_shift, norm1_g, w_in, b_gate, mu_shift, w0, w_lora_up, a0, a_lora_up, g_lora_up, k_k, k_a, r_k, gn_g, gn_b, w_proj_a, w_proj_b, w_out, norm2_g, w_up, w_down, normf_g):
    depth = w_in.shape[0]
    assert depth == 1, "the final norm is fused into the layer's output stage"
    w = _layer_weights(0, norm1_g, w_in, b_gate, mu_shift, w0, w_lora_up, a0, a_lora_up, g_lora_up, k_k, k_a,
                       r_k, gn_g, gn_b, w_proj_a, w_proj_b, w_out, norm2_g, w_up, w_down, normf_g)
    (y_p, kv_p, wkv_p, shift_p), (y_s, kv_s, wkv_s, shift_s) = _layer(
        x_prompt, x_sample, (cache_kv_w128[0], cache_kv_w512[0], cache_kv_w2048[0]), state_wkv[0], state_shift[0], w)
    lead = lambda a: a[None]
    return (y_p, y_s, lead(kv_p[0]), lead(kv_p[1]), lead(kv_p[2]), lead(wkv_p), lead(shift_p),
            lead(kv_s[0]), lead(kv_s[1]), lead(kv_s[2]), lead(wkv_s), lead(shift_s))
```

```python
import functools

import numpy as np
import jax
import jax.numpy as jnp
from jax import lax
from jax.experimental import pallas as pl
from jax.experimental.pallas import tpu as pltpu

F32 = jnp.float32
BF16 = jnp.bfloat16

D_MODEL = 1024
HEAD_DIM = 64
HG = 4
DILATIONS = (1, 4, 16)
BAND = 128
N_ATTN_HEADS = HG * len(DILATIONS)
GROUP_W = HG * HEAD_DIM
LANE_HALVES = GROUP_W // 128
ATTN_W = N_ATTN_HEADS * HEAD_DIM
QKV_W = 3 * ATTN_W
RWKV_N = 64
RWKV_W = 512
RWKV_HEADS = RWKV_W // RWKV_N
PAIR_W = 2 * RWKV_N
N_PAIRS = RWKV_HEADS // 2
LORA_W = 256
SHIFT_W = 3 * RWKV_W + LORA_W
GATE_W = 2 * D_MODEL
IN_W = QKV_W + SHIFT_W + GATE_W
D_FF = 4 * D_MODEL
NORM_EPS = 1e-6
GN_EPS = 64e-5
NEG_INF = -1e30
ATTN_SCALE = HEAD_DIM ** -0.5
CHUNK = 64
RWKV_SEQS_PER_STEP = 4
RWKV_TILE = 256
PROMPT_TILE = 512
DECAY_SCALE = float(np.exp(-0.5))
SLOPES = [float(s) for s in np.exp2(-8.0 * np.arange(1, N_ATTN_HEADS + 1, dtype=np.float32) / N_ATTN_HEADS)]

V7X_VMEM_LIMIT = 56 * 1024 * 1024

NN = (((1,), (0,)), ((), ()))
NT = (((1,), (1,)), ((), ()))
TN = (((0,), (0,)), ((), ()))


def _dg(a, b, dims=NN):
    return lax.dot_general(a, b, dims, preferred_element_type=F32)


def _split2(a):
    hi = a.astype(BF16)
    lo = (a - hi.astype(F32)).astype(BF16)
    return hi, lo


def _dot3(a, b, dims=NN):
    ah, al = _split2(a)
    bh, bl = _split2(b)
    return _dg(ah, bh, dims) + _dg(al, bh, dims) + _dg(ah, bl, dims)


def _dot_exact_rhs(a, b_bf16, passes):
    out = None
    rem = a
    for _ in range(passes):
        part = rem.astype(BF16)
        term = _dg(part, b_bf16)
        out = term if out is None else out + term
        rem = rem - part.astype(F32)
    return out


def _sigmoid(x):
    return 0.5 * jnp.tanh(0.5 * x) + 0.5


def _rms(x, g):
    return x * lax.rsqrt(jnp.mean(x * x, axis=-1, keepdims=True) + NORM_EPS) * g


def _params(sem):
    return pltpu.CompilerParams(dimension_semantics=sem, vmem_limit_bytes=V7X_VMEM_LIMIT)


def _kv_tail_plan(seq_len, tm):
    plan = []
    for dil in DILATIONS:
        rows = min(BAND * dil, seq_len)
        width = min(rows, tm)
        plan.append((rows, width, (seq_len - rows) // tm))
    return plan


def _in_proj_body(x_ref, g_ref, w_ref, bg_ref, qkv_ref, zb_ref, gate_ref, *kv_refs, seq_len):
    tm = x_ref.shape[0]

    def write_tail(g, kv_ref, width):
        for sec in (1, 2):
            c0 = sec * ATTN_W + g * GROUP_W
            kv_ref[sec - 1] = qkv_ref[tm - width:tm, c0:c0 + GROUP_W].T

    plan = _kv_tail_plan(seq_len, tm) if kv_refs else []
    h = _rms(x_ref[...], g_ref[...]).astype(BF16)
    for c in range(0, GATE_W, 1024):
        zg = _dg(h, w_ref[:, QKV_W + SHIFT_W + c:QKV_W + SHIFT_W + c + 1024])
        gate_ref[:, c:c + 1024] = _sigmoid(zg + bg_ref[:, c:c + 1024])
    for c in range(0, QKV_W, 768):
        qkv_ref[:, c:c + 768] = _dg(h, w_ref[:, c:c + 768])
    for g, (kv_ref, (_, width, first)) in enumerate(zip(kv_refs, plan)):
        if first == 0:
            write_tail(g, kv_ref, width)
    for c in range(0, SHIFT_W, 896):
        zb_ref[:, c:c + 896] = _dg(h, w_ref[:, QKV_W + c:QKV_W + c + 896])
    for g, (kv_ref, (_, width, first)) in enumerate(zip(kv_refs, plan)):
        if first > 0:
            tile = pl.program_id(0) % (seq_len // tm)
            pl.when(tile >= first)(functools.partial(write_tail, g, kv_ref, width))


def _in_proj(x, norm_g, w_in_bf16, b_gate, tm, seq_len=None):
    m = x.shape[0]
    row = lambda w: pl.BlockSpec((tm, w), lambda i: (i, 0))
    full = lambda a: pl.BlockSpec(a.shape, lambda i: (0,) * a.ndim)
    out_specs = [row(QKV_W), row(SHIFT_W), row(GATE_W)]
    out_shape = [jax.ShapeDtypeStruct((m, QKV_W), F32),
                 jax.ShapeDtypeStruct((m, SHIFT_W), F32),
                 jax.ShapeDtypeStruct((m, GATE_W), F32)]
    if seq_len is not None:
        tiles = seq_len // tm
        for rows, width, first in _kv_tail_plan(seq_len, tm):
            out_specs.append(pl.BlockSpec(
                (None, 2, GROUP_W, width),
                lambda i, first=first: (i // tiles, 0, 0, jnp.maximum(i % tiles - first, 0))))
            out_shape.append(jax.ShapeDtypeStruct((m // seq_len, 2, GROUP_W, rows), F32))
    return pl.pallas_call(
        functools.partial(_in_proj_body, seq_len=seq_len),
        grid=(m // tm,),
        in_specs=[row(D_MODEL), full(norm_g), full(w_in_bf16), full(b_gate)],
        out_specs=out_specs,
        out_shape=out_shape,
        compiler_params=_params(("arbitrary",)),
        name="in_proj",
    )(x, norm_g, w_in_bf16, b_gate)


def _residue_rows(r, count, dil):
    return pl.ds(r, count) if dil == 1 else pl.ds(r, count, stride=dil)


def _attn_group_blocks(q_ref, k_ref, v_ref, o_scr, lse_scr, qs, ks, vs, os_, ls, bias_scr, tmp, group):
    s_len = qs.shape[0]
    dil = DILATIONS[group]
    l_res = s_len // dil
    nb = l_res // BAND
    inner = 4 if dil > 4 else dil
    outer = dil // inner
    l_in = s_len // inner
    assert dil in (1, inner, inner * outer) and outer <= 4

    def gather(halves, r):
        if outer == 1:
            return jnp.concatenate([h[_residue_rows(r, l_res, dil), :] for h in halves], axis=1)
        c, q4 = r % inner, r // inner
        return jnp.concatenate([tmp[i, pl.ds(c * l_in + q4, l_res, stride=outer), :]
                                for i in range(LANE_HALVES)], axis=1)

    head = lax.broadcasted_iota(jnp.int32, (1, GROUP_W), 1) // HEAD_DIM
    hm = [head == h for h in range(HG)]
    for name, halves in (("q", q_ref), ("k", k_ref), ("v", v_ref)):
        if outer > 1:
            for i, h in enumerate(halves):
                for c in range(inner):
                    tmp[i, pl.ds(c * l_in, l_in), :] = h[_residue_rows(c, l_in, inner), :]
        for r in range(dil):
            dst = pl.ds(r * l_res, l_res)
            x = gather(halves, r)
            if name == "q":
                qs[dst, :] = (x * ATTN_SCALE).astype(BF16)
            elif name == "k":
                ks[dst, :] = x.astype(BF16)
            else:
                vs[dst, :] = x.astype(BF16)
    has_prev = nb > 1
    nk = 2 * BAND if has_prev else BAND
    qi = lax.broadcasted_iota(jnp.int32, (BAND, nk), 0)
    kj = lax.broadcasted_iota(jnp.int32, (BAND, nk), 1)
    delta = (nk - BAND) + qi - kj
    band = (delta >= 0) & (delta <= BAND)
    dist = (delta * dil).astype(F32)
    for h in range(HG):
        alibi = -SLOPES[group * HG + h] * dist
        bias_scr[h, :, 0:nk] = jnp.where(band, alibi, NEG_INF)
        bias_scr[HG + h, :, 0:nk] = jnp.where(band & (kj >= nk - BAND), alibi, NEG_INF)
    blocks_per_iter = 16

    def block_pair(it, carry):
        cur, q, k2, v2, first = [], [], [], [], []
        for u in range(blocks_per_iter):
            idx = it * blocks_per_iter + u
            cur.append(pl.ds(pl.multiple_of(idx * BAND, BAND), BAND))
            q.append(qs[cur[u], :])
            k2.append(ks[cur[u], :])
            v2.append(vs[cur[u], :])
            first.append(jnp.where(idx % nb == 0, HG, 0))
            if has_prev:
                prev = pl.ds(pl.multiple_of(jnp.maximum(idx - 1, 0) * BAND, BAND), BAND)
                k2[u] = jnp.concatenate([ks[prev, :], k2[u]], axis=0)
                v2[u] = jnp.concatenate([vs[prev, :], v2[u]], axis=0)
        chains = [(u, h) for u in range(blocks_per_iter) for h in range(HG)]
        s = [_dg(jnp.where(hm[h], q[u], jnp.zeros_like(q[u])), k2[u], NT) + bias_scr[first[u] + h, :, 0:nk]
             for u, h in chains]
        m = [jnp.max(x, axis=-1, keepdims=True) for x in s]
        p = [jnp.exp(x - mx) for x, mx in zip(s, m)]
        l = [jnp.sum(x, axis=-1, keepdims=True) for x in p]
        pv = [_dg(p[i].astype(BF16), v2[u]) for i, (u, h) in enumerate(chains)]
        for u in range(blocks_per_iter):
            o_acc = jnp.zeros((BAND, GROUP_W), F32)
            lse_acc = jnp.zeros((BAND, GROUP_W), F32)
            for h in range(HG):
                i = u * HG + h
                o_acc = jnp.where(hm[h], pv[i] / l[i], o_acc)
                lse_acc = jnp.where(hm[h], m[i] + jnp.log(l[i]), lse_acc)
            if dil == 1:
                for half in range(LANE_HALVES):
                    cols = slice(half * 128, (half + 1) * 128)
                    o_scr[group * LANE_HALVES + half, cur[u], :] = o_acc[:, cols]
                    lse_scr[group * LANE_HALVES + half, cur[u], :] = lse_acc[:, cols]
            else:
                os_[cur[u], :] = o_acc
                ls[cur[u], :] = lse_acc
        return carry

    assert (dil * nb) % blocks_per_iter == 0
    lax.fori_loop(0, dil * nb // blocks_per_iter, block_pair, 0)
    for res, nat in ((os_, o_scr), (ls, lse_scr)) if dil > 1 else ():
        for half in range(LANE_HALVES):
            cols = slice(half * 128, (half + 1) * 128)
            slab = group * LANE_HALVES + half
            for r in range(dil):
                src = pl.ds(r * l_res, l_res)
                if outer == 1:
                    nat[slab, _residue_rows(r, l_res, dil), :] = res[src, cols]
                else:
                    tmp[half, pl.ds((r % inner) * l_in + r // inner, l_res, stride=outer), :] = res[src, cols]
            if outer > 1:
                for c in range(inner):
                    nat[slab, _residue_rows(c, l_in, inner), :] = tmp[half, pl.ds(c * l_in, l_in), :]


def _attn_prompt_body(q0_ref, q1_ref, k0_ref, k1_ref, v0_ref, v1_ref, o_ref, o_scr, lse_scr, qs, ks, vs, os_, ls, bias_scr, tmp):
    q_ref, k_ref, v_ref = (q0_ref, q1_ref), (k0_ref, k1_ref), (v0_ref, v1_ref)
    gid = pl.program_id(1)
    n_groups = len(DILATIONS)
    for g in range(n_groups):
        @pl.when(gid == g)
        def _(g=g):
            _attn_group_blocks(q_ref, k_ref, v_ref, o_scr, lse_scr, qs, ks, vs, os_, ls, bias_scr, tmp, g)

    @pl.when(gid == n_groups - 1)
    def _():
        def merge(i, carry):
            rows = pl.ds(pl.multiple_of(i * BAND, BAND), BAND)
            both = lambda ref, g: jnp.concatenate(
                [ref[g * LANE_HALVES + half, rows, :] for half in range(LANE_HALVES)], axis=1)
            lse = [both(lse_scr, g) for g in range(n_groups)]
            m = functools.reduce(jnp.maximum, lse)
            wts = [jnp.exp(x - m) for x in lse]
            num = sum(wts[g] * both(o_scr, g) for g in range(n_groups))
            o_ref[rows, :] = (num / sum(wts)).astype(o_ref.dtype)
            return carry

        lax.fori_loop(0, o_ref.shape[0] // BAND, merge, 0)


def _attn_prompt(qkv):
    b, s, _ = qkv.shape
    n_groups = len(DILATIONS)
    assert s % (DILATIONS[-1] * BAND) == 0
    sec = ATTN_W // 128

    def col(section, half):
        return pl.BlockSpec((None, s, 128), lambda bi, g: (bi, 0, section * sec + g * LANE_HALVES + half))

    return pl.pallas_call(
        _attn_prompt_body,
        grid=(b, n_groups),
        in_specs=[col(section, half) for section in range(3) for half in range(LANE_HALVES)],
        out_specs=pl.BlockSpec((None, s, GROUP_W), lambda bi, g: (bi, 0, 0)),
        out_shape=jax.ShapeDtypeStruct((b, s, GROUP_W), BF16),
        scratch_shapes=([pltpu.VMEM((n_groups * LANE_HALVES, s, 128), F32)] * 2
                        + [pltpu.VMEM((s, GROUP_W), BF16)] * 3 + [pltpu.VMEM((s, GROUP_W), F32)] * 2
                        + [pltpu.VMEM((2 * HG, BAND, 2 * BAND), F32), pltpu.VMEM((LANE_HALVES, s, 128), F32)]),
        compiler_params=_params(("parallel", "arbitrary")),
        name="attn_prompt",
    )(*([qkv] * (3 * LANE_HALVES)))


def _attn_sample_body(qkv_ref, c0_ref, c1_ref, c2_ref, o_ref):
    for job in _attn_sample_jobs(qkv_ref, c0_ref, c1_ref, c2_ref, o_ref):
        job()


def _attn_sample_jobs(qkv_ref, c0_ref, c1_ref, c2_ref, o_ref):
    rows = 8
    row = lax.broadcasted_iota(jnp.int32, (rows, GROUP_W), 0)
    own = lax.broadcasted_iota(jnp.int32, (rows, GROUP_W), 1) // HEAD_DIM == row
    hrow = lax.broadcasted_iota(jnp.int32, (HG, 1), 0)
    lane_head = lax.broadcasted_iota(jnp.int32, (1, GROUP_W), 1) // HEAD_DIM
    c_refs = (c0_ref, c1_ref, c2_ref)
    groups = range(len(DILATIONS))
    st = [dict() for _ in groups]

    def to_column(x_row):
        return jnp.sum(jnp.where(own, x_row, 0.0).T, axis=1, keepdims=True)

    def per_head(col):
        out = jnp.zeros((1, GROUP_W), F32)
        for h in range(HG):
            out = jnp.where(lane_head == h, col[h:h + 1, :], out)
        return out

    def scores():
        for g in groups:
            lb = c_refs[g].shape[2]
            col = lambda sec: qkv_ref[:, sec * ATTN_W + g * GROUP_W:sec * ATTN_W + (g + 1) * GROUP_W]
            q, kn, vn = col(0), col(1), col(2)
            prod = c_refs[g][0] * to_column(q)
            s_b = jnp.sum(prod.reshape(HG, HEAD_DIM, lb), axis=1) * ATTN_SCALE
            s_n = jnp.sum(jnp.where(own, q * kn, 0.0), axis=-1, keepdims=True)[0:HG] * ATTN_SCALE
            st[g].update(vn=vn, s_b=s_b, s_n=s_n)

    def values():
        for g in groups:
            dil = DILATIONS[g]
            lb = c_refs[g].shape[2]
            slope = jnp.zeros((HG, 1), F32)
            for h in range(HG):
                slope = jnp.where(hrow == h, SLOPES[g * HG + h], slope)
            t = lax.broadcasted_iota(jnp.int32, (HG, lb), 1)
            s_b = jnp.where(t % dil == 0, st[g]["s_b"] - slope * (lb - t).astype(F32), NEG_INF)
            m = jnp.maximum(jnp.max(s_b, axis=-1, keepdims=True), st[g]["s_n"])
            p_b = jnp.exp(s_b - m)
            p_n = jnp.exp(st[g]["s_n"] - m)
            p_rows = jnp.broadcast_to(p_b[:, None, :], (HG, HEAD_DIM, lb)).reshape(GROUP_W, lb)
            acc = jnp.sum(p_rows * c_refs[g][1], axis=1, keepdims=True)
            st[g].update(m=m, p_n=p_n, l=jnp.sum(p_b, axis=-1, keepdims=True) + p_n, acc=acc)

    def merge():
        m_all = functools.reduce(jnp.maximum, [s["m"] for s in st])
        sc = [jnp.exp(s["m"] - m_all) for s in st]
        den = sum(c * s["l"] for c, s in zip(sc, st))
        out = jnp.zeros((1, GROUP_W), F32)
        for c, s in zip(sc, st):
            acc_row = jnp.broadcast_to(s["acc"], (GROUP_W, rows)).T[0:1, :]
            out = out + per_head(c / den) * (acc_row + per_head(s["p_n"]) * s["vn"])
        o_ref[...] = out.astype(o_ref.dtype)

    return [scores, values, merge]


def _attn_sample_operands(qkv, caches):
    b = qkv.shape[0]
    args = [qkv.reshape(b, 1, QKV_W)]
    specs = [pl.BlockSpec((None, 1, QKV_W), lambda i: (i, 0, 0))]
    for g, c in enumerate(caches):
        lb = c.shape[1]
        assert lb == BAND * DILATIONS[g]
        args.append(jnp.transpose(c, (0, 2, 3, 4, 1)).reshape(b, 2, GROUP_W, lb))
        specs.append(pl.BlockSpec((None, 2, GROUP_W, lb), lambda i: (i, 0, 0, 0)))
    out_spec = pl.BlockSpec((None, 1, GROUP_W), lambda i: (i, 0, 0))
    return args, specs, out_spec, jax.ShapeDtypeStruct((b, 1, GROUP_W), BF16)


def _attn_sample(qkv, caches):
    args, specs, out_spec, out_shape = _attn_sample_operands(qkv, caches)
    out = pl.pallas_call(
        _attn_sample_body,
        grid=(qkv.shape[0],),
        in_specs=specs,
        out_specs=out_spec,
        out_shape=out_shape,
        compiler_params=_params(("parallel",)),
        name="attn_sample",
    )(*args)
    return out.reshape(-1, GROUP_W)


WKV_PASSES = dict(g=1, s0=1, x=1, inv=1, y=1, s1=1)


def _mm(a, b, dims, passes):
    if passes == 1:
        return _dg(a.astype(BF16), b.astype(BF16), dims)
    ah, al = _split2(a)
    if passes == 2:
        bh = b.astype(BF16)
        return _dg(ah, bh, dims) + _dg(al, bh, dims)
    bh, bl = _split2(b)
    return _dg(ah, bh, dims) + _dg(al, bh, dims) + _dg(ah, bl, dims)


def _wkv_levels(live_rows):
    return int(np.ceil(np.log2(live_rows))) if live_rows > 1 else 0


def _wkv_chunk(operands, states, levels, side=()):
    side = list(side)

    def run_side():
        if side:
            side.pop(0)()
    c = CHUNK
    n2 = 2 * c
    n_seq = len(operands)
    first = lax.broadcasted_iota(jnp.int32, (1, PAIR_W), 1) < RWKV_N

    def stack(x, p):
        x = x[:, p * PAIR_W:(p + 1) * PAIR_W]
        return jnp.concatenate([jnp.where(first, x, 0.0), jnp.where(first, 0.0, x)], axis=0)

    bi = lax.broadcasted_iota(jnp.int32, (n_seq * c, n_seq * c), 0)
    bj = lax.broadcasted_iota(jnp.int32, (n_seq * c, n_seq * c), 1)
    tri_all = ((bi >= bj) & (bi // c == bj // c)).astype(BF16)
    lc_all = _dot_exact_rhs_left(tri_all, jnp.concatenate([ops[3] for ops in operands], axis=0))
    ar, bk, v_s, s0, w_end = [], [], [], [], []
    for j, ((r, kp, v, logw, av, bv), seq_states) in enumerate(zip(operands, states)):
        lc = lc_all[j * c:(j + 1) * c]
        e_in = jnp.exp(lc)
        e_neg = jnp.exp(-lc)
        a_t = av * jnp.exp(lc - logw)
        r_t = r * e_in
        b_t = bv * e_neg
        k_t = kp * e_neg
        for p in range(N_PAIRS):
            ar.append(jnp.concatenate([stack(a_t, p), stack(r_t, p)], axis=0))
            bk.append(jnp.concatenate([stack(b_t, p), stack(k_t, p)], axis=0))
            v_s.append(stack(v, p))
            s0.append(seq_states[p])
            w_end.append(e_in[c - 1:c, p * PAIR_W:(p + 1) * PAIR_W])
    chains = range(len(ar))
    g = [_mm(ar[i], bk[i], NT, WKV_PASSES["g"]) for i in chains]
    ar_s0 = [_mm(ar[i], s0[i], NT, WKV_PASSES["s0"]) for i in chains]
    run_side()
    ri = lax.broadcasted_iota(jnp.int32, (n2, n2), 0)
    ci = lax.broadcasted_iota(jnp.int32, (n2, n2), 1)
    strict = ri > ci
    incl = ri >= ci
    n_ab = [jnp.where(strict, g[i][0:n2, 0:n2], 0.0) for i in chains]
    n_ak = [jnp.where(strict, g[i][0:n2, n2:2 * n2], 0.0) for i in chains]
    m_r = [jnp.where(jnp.concatenate([incl, incl], axis=1), g[i][n2:2 * n2, :], 0.0) for i in chains]
    z = [ar_s0[i][0:n2] + _mm(n_ak[i], v_s[i], NN, WKV_PASSES["x"]) for i in chains]
    pw = n_ab
    for lvl in range(levels):
        if lvl < levels - 1:
            pz = [_mm(pw[i], jnp.concatenate([pw[i], z[i]], axis=1), NN, WKV_PASSES["inv"]) for i in chains]
            pw = [pz[i][:, 0:n2] for i in chains]
            z = [z[i] + pz[i][:, n2:2 * n2] for i in chains]
        else:
            z = [z[i] + _mm(pw[i], z[i], NN, WKV_PASSES["inv"]) for i in chains]
        if lvl % 2 == 1:
            run_side()
    while side:
        run_side()
    uv = [jnp.concatenate([z[i], v_s[i]], axis=0) for i in chains]
    y_s = [ar_s0[i][n2:2 * n2] + _mm(m_r[i], uv[i], NN, WKV_PASSES["y"]) for i in chains]
    s1 = [(s0[i] + _mm(uv[i], bk[i], TN, WKV_PASSES["s1"])) * w_end[i] for i in chains]
    ys = [jnp.concatenate([y_s[j * N_PAIRS + p][0:c] + y_s[j * N_PAIRS + p][c:n2] for p in range(N_PAIRS)], axis=1)
          for j in range(n_seq)]
    new_states = tuple(tuple(s1[j * N_PAIRS:(j + 1) * N_PAIRS]) for j in range(n_seq))
    return ys, new_states


def _dot_exact_rhs_left(lhs_bf16, b):
    out = None
    rem = b
    for _ in range(3):
        part = rem.astype(BF16)
        term = _dg(lhs_bf16, part)
        out = term if out is None else out + term
        rem = rem - part.astype(F32)
    return out


def _rwkv_body(zb_ref, mu_ref, w0_ref, a0_ref, kk_ref, ka_ref, rk_ref, gg_ref, gb_ref, wwa_ref, wg_ref, *refs,
               t_valid, fresh):
    if fresh:
        sh_ref = s0_ref = None
    else:
        sh_ref, s0_ref = refs[:2]
        refs = refs[2:]
    out_ref, s_out_ref, sh_out_ref, s_scr, carry_scr, ring0_scr, ring1_scr = refs
    t = pl.program_id(1)
    n_seq, tc, _ = zb_ref.shape

    @pl.when(t == 0)
    def _():
        if fresh:
            s_scr[...] = jnp.zeros(s_scr.shape, F32)
            carry_scr[...] = jnp.zeros(carry_scr.shape, F32)
        else:
            carry_scr[...] = sh_ref[...]
            zero = jnp.zeros((RWKV_N, RWKV_N), F32)
            for j in range(n_seq):
                for p in range(N_PAIRS):
                    top = jnp.concatenate([s0_ref[j, 2 * p], zero], axis=1)
                    bottom = jnp.concatenate([zero, s0_ref[j, 2 * p + 1]], axis=1)
                    s_scr[j, p] = jnp.concatenate([top, bottom], axis=0)

    lr = lax.broadcasted_iota(jnp.int32, (2 * PAIR_W, 2 * PAIR_W), 0)
    lc = lax.broadcasted_iota(jnp.int32, (2 * PAIR_W, 2 * PAIR_W), 1)
    seg = (lr // RWKV_N == lc // RWKV_N).astype(BF16)
    half = lax.broadcasted_iota(jnp.int32, (1, 128), 1) < 64
    row = lax.broadcasted_iota(jnp.int32, (CHUNK, 1), 0)
    inv_n = 1.0 / RWKV_N
    levels = _wkv_levels(CHUNK if t_valid is None else min(t_valid, CHUNK))
    short = tc < CHUNK

    def head_sum(x):
        wide = 2 * PAIR_W
        return jnp.concatenate(
            [_dg(x[:, c:c + wide].astype(BF16), seg) for c in range(0, RWKV_W, wide)], axis=1)

    n_chunks = max(tc // CHUNK, 1)
    rings = (ring0_scr, ring1_scr)

    def staging_jobs(ci, ring):
        rows = pl.ds(pl.multiple_of(ci * CHUNK, CHUNK), CHUNK)
        st = [dict() for _ in range(n_seq)]
        seq_rows = lambda x, j: x[j * CHUNK:(j + 1) * CHUNK]

        def mix_and_lora():
            x_wa, x_g = [], []
            for j in range(n_seq):
                if short:
                    z = jnp.where(row == 0, zb_ref[j], 0.0)
                    before = carry_scr[j]
                else:
                    z = zb_ref[j, rows, :]
                    before = zb_ref[j, pl.ds(jnp.maximum(ci * CHUNK - 1, 0), 1), :]
                    before = jnp.where(ci == 0, carry_scr[j], before)
                zp = jnp.where(row == 0, before, pltpu.roll(z, 1, 0))
                zm = z + (zp - z) * mu_ref[...]
                z_wa = zm[:, 3 * RWKV_W:3 * RWKV_W + 128]
                x_wa.append(jnp.where(half, jnp.tanh(z_wa), z_wa).astype(BF16))
                x_g.append(_sigmoid(zm[:, 3 * RWKV_W + 128:SHIFT_W]).astype(BF16))
                st[j].update(r=zm[:, 0:RWKV_W], k=zm[:, RWKV_W:2 * RWKV_W], v=zm[:, 2 * RWKV_W:3 * RWKV_W])
            x_wa = jnp.concatenate(x_wa, axis=0)
            u = _dg(x_wa, wwa_ref[:, 0:RWKV_W])
            la = _dg(x_wa, wwa_ref[:, RWKV_W:2 * RWKV_W])
            gate = _dg(jnp.concatenate(x_g, axis=0), wg_ref[...])
            for j in range(n_seq):
                st[j].update(u=seq_rows(u, j), la=seq_rows(la, j), gate=seq_rows(gate, j))

        def head_sums():
            for j in range(n_seq):
                s = st[j]
                a = _sigmoid(a0_ref[...] + s["la"])
                kk = s["k"] * kk_ref[...]
                kp = s["k"] * (1.0 + (a - 1.0) * ka_ref[...])
                s.update(a=a, kk=kk, kp=kp)
            kk2 = head_sum(jnp.concatenate([s["kk"] * s["kk"] for s in st], axis=0))
            rk = head_sum(jnp.concatenate([s["r"] * s["kp"] * rk_ref[...] for s in st], axis=0))
            for j in range(n_seq):
                st[j].update(kk2=seq_rows(kk2, j), rk=seq_rows(rk, j))

        def finish():
            for j in range(n_seq):
                s = st[j]
                logw = -DECAY_SCALE * _sigmoid(w0_ref[...] + s["u"])
                kk = s["kk"] * lax.rsqrt(jnp.maximum(s["kk2"], 1e-24))
                ops = (s["r"], s["kp"], s["v"], logw, -kk, kk * s["a"], s["rk"] * s["v"])
                if t_valid is not None:
                    live = row + ci * CHUNK < t_valid
                    ops = tuple(jnp.where(live, x, 0.0) for x in ops)
                for i, x in enumerate(ops + (s["gate"],)):
                    ring[j, i] = x

        return mix_and_lora, head_sums, finish

    def chunk(ci, states, ring, other):
        side = staging_jobs(jnp.minimum(ci + 1, n_chunks - 1), other) if n_chunks > 1 else ()
        ys, states = _wkv_chunk([tuple(ring[j, i] for i in range(6)) for j in range(n_seq)], states, levels, side)
        y_all = jnp.concatenate(ys, axis=0)
        d_all = y_all - head_sum(y_all) * inv_n
        var_all = head_sum(d_all * d_all) * inv_n
        yn_all = d_all * lax.rsqrt(var_all + GN_EPS) * gg_ref[...] + gb_ref[...]
        for j in range(n_seq):
            res = (yn_all[j * CHUNK:(j + 1) * CHUNK] + ring[j, 6]) * ring[j, 7]
            if short:
                out_ref[j] = res[0:tc, :].astype(out_ref.dtype)
            else:
                out_ref[j, pl.ds(pl.multiple_of(ci * CHUNK, CHUNK), CHUNK), :] = res.astype(out_ref.dtype)
        return states

    for job in staging_jobs(0, rings[0]):
        job()
    states = tuple(tuple(s_scr[j, p] for p in range(N_PAIRS)) for j in range(n_seq))
    if n_chunks == 1:
        states = chunk(0, states, rings[0], rings[1])
    else:
        assert n_chunks % 2 == 0

        def chunk_pair(i, states):
            states = chunk(2 * i, states, rings[0], rings[1])
            return chunk(2 * i + 1, states, rings[1], rings[0])

        states = lax.fori_loop(0, n_chunks // 2, chunk_pair, states)
    for j in range(n_seq):
        carry_scr[j] = zb_ref[j, tc - 1:tc, :]
        for p in range(N_PAIRS):
            s_scr[j, p] = states[j][p]

    @pl.when(t == pl.num_programs(1) - 1)
    def _():
        for j in range(n_seq):
            sh_out_ref[j] = zb_ref[j, tc - 1:tc, :]
            for p in range(N_PAIRS):
                for e in range(2):
                    span = slice(e * RWKV_N, (e + 1) * RWKV_N)
                    s_out_ref[j, 2 * p + e] = states[j][p][span, span]


def _rwkv(zb, w, tc, n_seq, state=None, t_valid=None):
    b, t_len, _ = zb.shape
    full = lambda a: pl.BlockSpec(a.shape, lambda bi, t: (0,) * a.ndim)
    per_seq = lambda *dims: pl.BlockSpec((n_seq,) + dims, lambda bi, t: (bi,) + (0,) * len(dims))
    consts = [w["mu_shift"], w["w0"], w["a0"], w["k_k"], w["k_a"], w["r_k"], w["gn_g"], w["gn_b"], w["wa_lora"],
              w["g_lora"]]
    in_specs = [pl.BlockSpec((n_seq, tc, SHIFT_W), lambda bi, t: (bi, t, 0))] + [full(c) for c in consts]
    args = [zb, *consts]
    if state is not None:
        in_specs += [per_seq(1, SHIFT_W), per_seq(RWKV_HEADS, RWKV_N, RWKV_N)]
        args += list(state)
    vm = lambda shape: pltpu.VMEM(shape, F32)
    assert b % n_seq == 0 and t_len % tc == 0
    assert tc % CHUNK == 0 or (tc == t_len == t_valid == 1), "whole chunks, or single-token sequences"
    scratch = [vm((n_seq, N_PAIRS, PAIR_W, PAIR_W)), vm((n_seq, 1, SHIFT_W))] + [vm((n_seq, 8, CHUNK, RWKV_W))] * 2
    return pl.pallas_call(
        functools.partial(_rwkv_body, t_valid=t_valid, fresh=state is None),
        grid=(b // n_seq, t_len // tc),
        in_specs=in_specs,
        out_specs=[pl.BlockSpec((n_seq, tc, RWKV_W), lambda bi, t: (bi, t, 0)),
                   per_seq(RWKV_HEADS, RWKV_N, RWKV_N), per_seq(1, SHIFT_W)],
        out_shape=[jax.ShapeDtypeStruct((b, t_len, RWKV_W), BF16),
                   jax.ShapeDtypeStruct((b, RWKV_HEADS, RWKV_N, RWKV_N), F32),
                   jax.ShapeDtypeStruct((b, 1, SHIFT_W), F32)],
        scratch_shapes=scratch,
        compiler_params=_params(("parallel", "arbitrary")),
        name="rwkv",
    )(*args)


def _out_proj_body(x_ref, attn_ref, rwkv_ref, gate_ref, wpa_ref, wpb_ref, wout_ref, n2_ref,
                   wup_ref, wdn_ref, nf_ref, *refs):
    rider = []
    if len(refs) > 1:
        qkv_s_ref, c0_ref, c1_ref, c2_ref, y_ref, attn_s_ref = refs
        rider = _attn_sample_jobs(qkv_s_ref, c0_ref, c1_ref, c2_ref, attn_s_ref)
    else:
        y_ref, = refs
    pa = _dg(attn_ref[...], wpa_ref[...])
    pb = _dg(rwkv_ref[...], wpb_ref[...])
    merged = gate_ref[:, 0:D_MODEL] * pa + gate_ref[:, D_MODEL:GATE_W] * pb
    x1 = x_ref[...] + _dg(merged.astype(BF16), wout_ref[...])
    hm = _rms(x1, n2_ref[...]).astype(BF16)
    acc = x1
    for c in range(0, D_FF, 1024):
        up = jnp.maximum(_dg(hm, wup_ref[:, c:c + 1024]), 0.0)
        if rider:
            rider.pop(0)()
        acc = acc + _dg((up * up).astype(BF16), wdn_ref[c:c + 1024, :])
    y_ref[...] = _rms(acc, nf_ref[...])


def _out_proj(x, attn, rwkv, gate, w, tm, rider=None):
    m = x.shape[0]
    row = lambda wd: pl.BlockSpec((tm, wd), lambda i: (i, 0))
    full = lambda a: pl.BlockSpec(a.shape, lambda i: (0,) * a.ndim)
    consts = [w["w_proj_a"], w["w_proj_b"], w["w_out"], w["norm2_g"], w["w_up"], w["w_down"], w["normf_g"]]
    in_specs = [row(D_MODEL), row(GROUP_W), row(RWKV_W), row(GATE_W)] + [full(c) for c in consts]
    args = [x, attn, rwkv, gate, *consts]
    out_specs = [row(D_MODEL)]
    out_shape = [jax.ShapeDtypeStruct((m, D_MODEL), F32)]
    if rider is not None:
        r_args, r_specs, r_out_spec, r_out_shape = _attn_sample_operands(*rider)
        assert r_out_shape.shape[0] == m // tm
        in_specs += r_specs
        args += r_args
        out_specs.append(r_out_spec)
        out_shape.append(r_out_shape)
    outs = pl.pallas_call(
        _out_proj_body,
        grid=(m // tm,),
        in_specs=in_specs,
        out_specs=out_specs,
        out_shape=out_shape,
        compiler_params=_params(("parallel",)),
        name="out_proj",
    )(*args)
    if rider is None:
        return outs[0]
    return outs[0], outs[1].reshape(-1, GROUP_W)


def _kv_rows(qkv, group, rows):
    b, s, _ = qkv.shape
    k0 = ATTN_W + group * GROUP_W
    v0 = 2 * ATTN_W + group * GROUP_W
    k = qkv[:, s - rows:, k0:k0 + GROUP_W].reshape(b, rows, HG, HEAD_DIM)
    v = qkv[:, s - rows:, v0:v0 + GROUP_W].reshape(b, rows, HG, HEAD_DIM)
    return jnp.stack([k, v], axis=2)


def _layer_weights(l, norm1_g, w_in, b_gate, mu_shift, w0, w_lora_up, a0, a_lora_up, g_lora_up, k_k, k_a,
                   r_k, gn_g, gn_b, w_proj_a, w_proj_b, w_out, norm2_g, w_up, w_down, normf_g):
    row = lambda a: a.reshape(1, -1)
    zero = jnp.zeros_like(w_lora_up[l])
    wa_lora = jnp.concatenate([jnp.concatenate([w_lora_up[l], zero], axis=1),
                               jnp.concatenate([zero, a_lora_up[l]], axis=1)], axis=0)
    return dict(
        norm1_g=row(norm1_g[l]), w_in=w_in[l].astype(BF16), b_gate=row(b_gate[l]), mu_shift=row(mu_shift[l]),
        w0=row(w0[l]), a0=row(a0[l]), k_k=row(k_k[l]), k_a=row(k_a[l]), r_k=row(r_k[l]),
        gn_g=row(gn_g[l]), gn_b=row(gn_b[l]),
        wa_lora=wa_lora.astype(BF16), g_lora=g_lora_up[l].astype(BF16),
        w_proj_a=w_proj_a[l].astype(BF16), w_proj_b=w_proj_b[l].astype(BF16), w_out=w_out[l].astype(BF16),
        norm2_g=row(norm2_g[l]), w_up=w_up[l].astype(BF16), w_down=w_down[l].astype(BF16),
        normf_g=row(normf_g))


def _layer(xp, xs, caches, s0, shift0, w):
    b, s, _ = xp.shape
    bs, t_len, _ = xs.shape
    assert t_len == 1
    xp2 = xp.reshape(b * s, D_MODEL)
    xs2 = xs.reshape(bs, D_MODEL)
    qkv_s, zb_s, gate_s = _in_proj(xs2, w["norm1_g"], w["w_in"], w["b_gate"], bs)
    qkv, zb, gate, *kv_t = _in_proj(xp2, w["norm1_g"], w["w_in"], w["b_gate"], PROMPT_TILE, seq_len=s)
    zb = zb.reshape(b, s, SHIFT_W)
    attn = _attn_prompt(qkv.reshape(b, s, QKV_W))
    rwkv, wkv_p, shift_p = _rwkv(zb, w, RWKV_TILE, RWKV_SEQS_PER_STEP)
    attn2, rwkv2 = attn.reshape(b * s, GROUP_W), rwkv.reshape(b * s, RWKV_W)
    if b * s // PROMPT_TILE == bs:
        y_p, attn_s = _out_proj(xp2, attn2, rwkv2, gate, w, PROMPT_TILE, rider=(qkv_s, caches))
    else:
        y_p = _out_proj(xp2, attn2, rwkv2, gate, w, PROMPT_TILE)
        attn_s = _attn_sample(qkv_s, caches)
    rwkv_s, wkv_s, _ = _rwkv(zb_s[:, None, :], w, 1, RWKV_SEQS_PER_STEP, state=(shift0[:, None, :], s0), t_valid=1)
    y_s = _out_proj(xs2, attn_s, rwkv_s[:, 0], gate_s, w, bs)
    kv_p = [jnp.transpose(t.reshape(b, 2, HG, HEAD_DIM, t.shape[-1]), (0, 4, 1, 2, 3)) for t in kv_t]
    kv_s = [_kv_rows(qkv_s[:, None, :], g, 1) for g in range(len(DILATIONS))]
    prompt = (y_p.reshape(b, s, D_MODEL), kv_p, wkv_p, shift_p[:, 0])
    sample = (y_s.reshape(bs, 1, D_MODEL), kv_s, wkv_s, zb_s)
    return prompt, sample


def kernel(x_prompt, x_sample, cache_kv_w128, cache_kv_w512, cache_kv_w2048, state_wkv, state_shift, norm1_g, w_in, b_gate, mu_shift, w0, w_lora_up, a0, a_lora_up, g_lora_up, k_k, k_a, r_k, gn_g, gn_b, w_proj_a, w_proj_b, w_out, norm2_g, w_up, w_down, normf_g):
    depth = w_in.shape[0]
    assert depth == 1, "the final norm is fused into the layer's output stage"
    w = _layer_weights(0, norm1_g, w_in, b_gate, mu_shift, w0, w_lora_up, a0, a_lora_up, g_lora_up, k_k, k_a,
                       r_k, gn_g, gn_b, w_proj_a, w_proj_b, w_out, norm2_g, w_up, w_down, normf_g)
    (y_p, kv_p, wkv_p, shift_p), (y_s, kv_s, wkv_s, shift_s) = _layer(
        x_prompt, x_sample, (cache_kv_w128[0], cache_kv_w512[0], cache_kv_w2048[0]), state_wkv[0], state_shift[0], w)
    lead = lambda a: a[None]
    return (y_p, y_s, lead(kv_p[0]), lead(kv_p[1]), lead(kv_p[2]), lead(wkv_p), lead(shift_p),
            lead(kv_s[0]), lead(kv_s[1]), lead(kv_s[2]), lead(wkv_s), lead(shift_s))
```

```python
import functools

import numpy as np
import jax
import jax.numpy as jnp
from jax import lax
from jax.experimental import pallas as pl
from jax.experimental.pallas import tpu as pltpu

F32 = jnp.float32
BF16 = jnp.bfloat16

D_MODEL = 1024
HEAD_DIM = 64
HG = 4
DILATIONS = (1, 4, 16)
BAND = 128
N_ATTN_HEADS = HG * len(DILATIONS)
GROUP_W = HG * HEAD_DIM
LANE_HALVES = GROUP_W // 128
ATTN_W = N_ATTN_HEADS * HEAD_DIM
QKV_W = 3 * ATTN_W
RWKV_N = 64
RWKV_W = 512
RWKV_HEADS = RWKV_W // RWKV_N
PAIR_W = 2 * RWKV_N
N_PAIRS = RWKV_HEADS // 2
LORA_W = 256
SHIFT_W = 3 * RWKV_W + LORA_W
GATE_W = 2 * D_MODEL
IN_W = QKV_W + SHIFT_W + GATE_W
D_FF = 4 * D_MODEL
NORM_EPS = 1e-6
GN_EPS = 64e-5
NEG_INF = -1e30
ATTN_SCALE = HEAD_DIM ** -0.5
CHUNK = 64
RWKV_SEQS_PER_STEP = 8
RWKV_TILE = 128
PROMPT_TILE = 512
DECAY_SCALE = float(np.exp(-0.5))
SLOPES = [float(s) for s in np.exp2(-8.0 * np.arange(1, N_ATTN_HEADS + 1, dtype=np.float32) / N_ATTN_HEADS)]

V7X_VMEM_LIMIT = 56 * 1024 * 1024

NN = (((1,), (0,)), ((), ()))
NT = (((1,), (1,)), ((), ()))
TN = (((0,), (0,)), ((), ()))


def _dg(a, b, dims=NN):
    return lax.dot_general(a, b, dims, preferred_element_type=F32)


def _split2(a):
    hi = a.astype(BF16)
    lo = (a - hi.astype(F32)).astype(BF16)
    return hi, lo


def _dot3(a, b, dims=NN):
    ah, al = _split2(a)
    bh, bl = _split2(b)
    return _dg(ah, bh, dims) + _dg(al, bh, dims) + _dg(ah, bl, dims)


def _dot_exact_rhs(a, b_bf16, passes):
    out = None
    rem = a
    for _ in range(passes):
        part = rem.astype(BF16)
        term = _dg(part, b_bf16)
        out = term if out is None else out + term
        rem = rem - part.astype(F32)
    return out


def _sigmoid(x):
    return 0.5 * jnp.tanh(0.5 * x) + 0.5


def _rms(x, g):
    return x * lax.rsqrt(jnp.mean(x * x, axis=-1, keepdims=True) + NORM_EPS) * g


def _params(sem):
    return pltpu.CompilerParams(dimension_semantics=sem, vmem_limit_bytes=V7X_VMEM_LIMIT)


def _kv_tail_plan(seq_len, tm):
    plan = []
    for dil in DILATIONS:
        rows = min(BAND * dil, seq_len)
        width = min(rows, tm)
        plan.append((rows, width, (seq_len - rows) // tm))
    return plan


def _in_proj_body(x_ref, g_ref, w_ref, bg_ref, qkv_ref, zb_ref, gate_ref, *kv_refs, seq_len):
    tm = x_ref.shape[0]

    def write_tail(g, kv_ref, width):
        for sec in (1, 2):
            c0 = sec * ATTN_W + g * GROUP_W
            kv_ref[sec - 1] = qkv_ref[tm - width:tm, c0:c0 + GROUP_W].T

    plan = _kv_tail_plan(seq_len, tm) if kv_refs else []
    h = _rms(x_ref[...], g_ref[...]).astype(BF16)
    for c in range(0, GATE_W, 1024):
        zg = _dg(h, w_ref[:, QKV_W + SHIFT_W + c:QKV_W + SHIFT_W + c + 1024])
        gate_ref[:, c:c + 1024] = _sigmoid(zg + bg_ref[:, c:c + 1024])
    for c in range(0, QKV_W, 768):
        qkv_ref[:, c:c + 768] = _dg(h, w_ref[:, c:c + 768])
    for g, (kv_ref, (_, width, first)) in enumerate(zip(kv_refs, plan)):
        if first == 0:
            write_tail(g, kv_ref, width)
    for c in range(0, SHIFT_W, 896):
        zb_ref[:, c:c + 896] = _dg(h, w_ref[:, QKV_W + c:QKV_W + c + 896])
    for g, (kv_ref, (_, width, first)) in enumerate(zip(kv_refs, plan)):
        if first > 0:
            tile = pl.program_id(0) % (seq_len // tm)
            pl.when(tile >= first)(functools.partial(write_tail, g, kv_ref, width))


def _in_proj(x, norm_g, w_in_bf16, b_gate, tm, seq_len=None):
    m = x.shape[0]
    row = lambda w: pl.BlockSpec((tm, w), lambda i: (i, 0))
    full = lambda a: pl.BlockSpec(a.shape, lambda i: (0,) * a.ndim)
    out_specs = [row(QKV_W), row(SHIFT_W), row(GATE_W)]
    out_shape = [jax.ShapeDtypeStruct((m, QKV_W), F32),
                 jax.ShapeDtypeStruct((m, SHIFT_W), F32),
                 jax.ShapeDtypeStruct((m, GATE_W), F32)]
    if seq_len is not None:
        tiles = seq_len // tm
        for rows, width, first in _kv_tail_plan(seq_len, tm):
            out_specs.append(pl.BlockSpec(
                (None, 2, GROUP_W, width),
                lambda i, first=first: (i // tiles, 0, 0, jnp.maximum(i % tiles - first, 0))))
            out_shape.append(jax.ShapeDtypeStruct((m // seq_len, 2, GROUP_W, rows), F32))
    return pl.pallas_call(
        functools.partial(_in_proj_body, seq_len=seq_len),
        grid=(m // tm,),
        in_specs=[row(D_MODEL), full(norm_g), full(w_in_bf16), full(b_gate)],
        out_specs=out_specs,
        out_shape=out_shape,
        compiler_params=_params(("arbitrary",)),
        name="in_proj",
    )(x, norm_g, w_in_bf16, b_gate)


def _residue_rows(r, count, dil):
    return pl.ds(r, count) if dil == 1 else pl.ds(r, count, stride=dil)


def _attn_group_blocks(q_ref, k_ref, v_ref, o_scr, lse_scr, qs, ks, vs, os_, ls, bias_scr, tmp, group):
    s_len = qs.shape[0]
    dil = DILATIONS[group]
    l_res = s_len // dil
    nb = l_res // BAND
    inner = 4 if dil > 4 else dil
    outer = dil // inner
    l_in = s_len // inner
    assert dil in (1, inner, inner * outer) and outer <= 4

    def gather(halves, r):
        if outer == 1:
            return jnp.concatenate([h[_residue_rows(r, l_res, dil), :] for h in halves], axis=1)
        c, q4 = r % inner, r // inner
        return jnp.concatenate([tmp[i, pl.ds(c * l_in + q4, l_res, stride=outer), :]
                                for i in range(LANE_HALVES)], axis=1)

    head = lax.broadcasted_iota(jnp.int32, (1, GROUP_W), 1) // HEAD_DIM
    hm = [head == h for h in range(HG)]
    for name, halves in (("q", q_ref), ("k", k_ref), ("v", v_ref)):
        if outer > 1:
            for i, h in enumerate(halves):
                for c in range(inner):
                    tmp[i, pl.ds(c * l_in, l_in), :] = h[_residue_rows(c, l_in, inner), :]
        for r in range(dil):
            dst = pl.ds(r * l_res, l_res)
            x = gather(halves, r)
            if name == "q":
                qs[dst, :] = (x * ATTN_SCALE).astype(BF16)
            elif name == "k":
                ks[dst, :] = x.astype(BF16)
            else:
                vs[dst, :] = x.astype(BF16)
    has_prev = nb > 1
    nk = 2 * BAND if has_prev else BAND
    qi = lax.broadcasted_iota(jnp.int32, (BAND, nk), 0)
    kj = lax.broadcasted_iota(jnp.int32, (BAND, nk), 1)
    delta = (nk - BAND) + qi - kj
    band = (delta >= 0) & (delta <= BAND)
    dist = (delta * dil).astype(F32)
    for h in range(HG):
        alibi = -SLOPES[group * HG + h] * dist
        bias_scr[h, :, 0:nk] = jnp.where(band, alibi, NEG_INF)
        bias_scr[HG + h, :, 0:nk] = jnp.where(band & (kj >= nk - BAND), alibi, NEG_INF)
    blocks_per_iter = 16

    def block_pair(it, carry):
        cur, q, k2, v2, first = [], [], [], [], []
        for u in range(blocks_per_iter):
            idx = it * blocks_per_iter + u
            cur.append(pl.ds(pl.multiple_of(idx * BAND, BAND), BAND))
            q.append(qs[cur[u], :])
            k2.append(ks[cur[u], :])
            v2.append(vs[cur[u], :])
            first.append(jnp.where(idx % nb == 0, HG, 0))
            if has_prev:
                prev = pl.ds(pl.multiple_of(jnp.maximum(idx - 1, 0) * BAND, BAND), BAND)
                k2[u] = jnp.concatenate([ks[prev, :], k2[u]], axis=0)
                v2[u] = jnp.concatenate([vs[prev, :], v2[u]], axis=0)
        chains = [(u, h) for u in range(blocks_per_iter) for h in range(HG)]
        s = [_dg(jnp.where(hm[h], q[u], jnp.zeros_like(q[u])), k2[u], NT) + bias_scr[first[u] + h, :, 0:nk]
             for u, h in chains]
        m = [jnp.max(x, axis=-1, keepdims=True) for x in s]
        p = [jnp.exp(x - mx) for x, mx in zip(s, m)]
        l = [jnp.sum(x, axis=-1, keepdims=True) for x in p]
        pv = [_dg(p[i].astype(BF16), v2[u]) for i, (u, h) in enumerate(chains)]
        for u in range(blocks_per_iter):
            o_acc = jnp.zeros((BAND, GROUP_W), F32)
            lse_acc = jnp.zeros((BAND, GROUP_W), F32)
            for h in range(HG):
                i = u * HG + h
                o_acc = jnp.where(hm[h], pv[i] / l[i], o_acc)
                lse_acc = jnp.where(hm[h], m[i] + jnp.log(l[i]), lse_acc)
            if dil == 1:
                for half in range(LANE_HALVES):
                    cols = slice(half * 128, (half + 1) * 128)
                    o_scr[group * LANE_HALVES + half, cur[u], :] = o_acc[:, cols]
                    lse_scr[group * LANE_HALVES + half, cur[u], :] = lse_acc[:, cols]
            else:
                os_[cur[u], :] = o_acc
                ls[cur[u], :] = lse_acc
        return carry

    assert (dil * nb) % blocks_per_iter == 0
    lax.fori_loop(0, dil * nb // blocks_per_iter, block_pair, 0)
    for res, nat in ((os_, o_scr), (ls, lse_scr)) if dil > 1 else ():
        for half in range(LANE_HALVES):
            cols = slice(half * 128, (half + 1) * 128)
            slab = group * LANE_HALVES + half
            for r in range(dil):
                src = pl.ds(r * l_res, l_res)
                if outer == 1:
                    nat[slab, _residue_rows(r, l_res, dil), :] = res[src, cols]
                else:
                    tmp[half, pl.ds((r % inner) * l_in + r // inner, l_res, stride=outer), :] = res[src, cols]
            if outer > 1:
                for c in range(inner):
                    nat[slab, _residue_rows(c, l_in, inner), :] = tmp[half, pl.ds(c * l_in, l_in), :]


def _attn_prompt_body(q0_ref, q1_ref, k0_ref, k1_ref, v0_ref, v1_ref, o_ref, o_scr, lse_scr, qs, ks, vs, os_, ls, bias_scr, tmp):
    q_ref, k_ref, v_ref = (q0_ref, q1_ref), (k0_ref, k1_ref), (v0_ref, v1_ref)
    gid = pl.program_id(1)
    n_groups = len(DILATIONS)
    for g in range(n_groups):
        @pl.when(gid == g)
        def _(g=g):
            _attn_group_blocks(q_ref, k_ref, v_ref, o_scr, lse_scr, qs, ks, vs, os_, ls, bias_scr, tmp, g)

    @pl.when(gid == n_groups - 1)
    def _():
        def merge(i, carry):
            rows = pl.ds(pl.multiple_of(i * BAND, BAND), BAND)
            both = lambda ref, g: jnp.concatenate(
                [ref[g * LANE_HALVES + half, rows, :] for half in range(LANE_HALVES)], axis=1)
            lse = [both(lse_scr, g) for g in range(n_groups)]
            m = functools.reduce(jnp.maximum, lse)
            wts = [jnp.exp(x - m) for x in lse]
            num = sum(wts[g] * both(o_scr, g) for g in range(n_groups))
            o_ref[rows, :] = (num / sum(wts)).astype(o_ref.dtype)
            return carry

        lax.fori_loop(0, o_ref.shape[0] // BAND, merge, 0)


def _attn_prompt(qkv):
    b, s, _ = qkv.shape
    n_groups = len(DILATIONS)
    assert s % (DILATIONS[-1] * BAND) == 0
    sec = ATTN_W // 128

    def col(section, half):
        return pl.BlockSpec((None, s, 128), lambda bi, g: (bi, 0, section * sec + g * LANE_HALVES + half))

    return pl.pallas_call(
        _attn_prompt_body,
        grid=(b, n_groups),
        in_specs=[col(section, half) for section in range(3) for half in range(LANE_HALVES)],
        out_specs=pl.BlockSpec((None, s, GROUP_W), lambda bi, g: (bi, 0, 0)),
        out_shape=jax.ShapeDtypeStruct((b, s, GROUP_W), BF16),
        scratch_shapes=([pltpu.VMEM((n_groups * LANE_HALVES, s, 128), F32)] * 2
                        + [pltpu.VMEM((s, GROUP_W), BF16)] * 3 + [pltpu.VMEM((s, GROUP_W), F32)] * 2
                        + [pltpu.VMEM((2 * HG, BAND, 2 * BAND), F32), pltpu.VMEM((LANE_HALVES, s, 128), F32)]),
        compiler_params=_params(("parallel", "arbitrary")),
        name="attn_prompt",
    )(*([qkv] * (3 * LANE_HALVES)))


def _attn_sample_body(qkv_ref, c0_ref, c1_ref, c2_ref, o_ref):
    for job in _attn_sample_jobs(qkv_ref, c0_ref, c1_ref, c2_ref, o_ref):
        job()


def _attn_sample_jobs(qkv_ref, c0_ref, c1_ref, c2_ref, o_ref):
    rows = 8
    row = lax.broadcasted_iota(jnp.int32, (rows, GROUP_W), 0)
    own = lax.broadcasted_iota(jnp.int32, (rows, GROUP_W), 1) // HEAD_DIM == row
    hrow = lax.broadcasted_iota(jnp.int32, (HG, 1), 0)
    lane_head = lax.broadcasted_iota(jnp.int32, (1, GROUP_W), 1) // HEAD_DIM
    c_refs = (c0_ref, c1_ref, c2_ref)
    groups = range(len(DILATIONS))
    st = [dict() for _ in groups]

    def to_column(x_row):
        return jnp.sum(jnp.where(own, x_row, 0.0).T, axis=1, keepdims=True)

    def per_head(col):
        out = jnp.zeros((1, GROUP_W), F32)
        for h in range(HG):
            out = jnp.where(lane_head == h, col[h:h + 1, :], out)
        return out

    def scores():
        for g in groups:
            lb = c_refs[g].shape[2]
            col = lambda sec: qkv_ref[:, sec * ATTN_W + g * GROUP_W:sec * ATTN_W + (g + 1) * GROUP_W]
            q, kn, vn = col(0), col(1), col(2)
            prod = c_refs[g][0] * to_column(q)
            s_b = jnp.sum(prod.reshape(HG, HEAD_DIM, lb), axis=1) * ATTN_SCALE
            s_n = jnp.sum(jnp.where(own, q * kn, 0.0), axis=-1, keepdims=True)[0:HG] * ATTN_SCALE
            st[g].update(vn=vn, s_b=s_b, s_n=s_n)

    def values():
        for g in groups:
            dil = DILATIONS[g]
            lb = c_refs[g].shape[2]
            slope = jnp.zeros((HG, 1), F32)
            for h in range(HG):
                slope = jnp.where(hrow == h, SLOPES[g * HG + h], slope)
            t = lax.broadcasted_iota(jnp.int32, (HG, lb), 1)
            s_b = jnp.where(t % dil == 0, st[g]["s_b"] - slope * (lb - t).astype(F32), NEG_INF)
            m = jnp.maximum(jnp.max(s_b, axis=-1, keepdims=True), st[g]["s_n"])
            p_b = jnp.exp(s_b - m)
            p_n = jnp.exp(st[g]["s_n"] - m)
            p_rows = jnp.broadcast_to(p_b[:, None, :], (HG, HEAD_DIM, lb)).reshape(GROUP_W, lb)
            acc = jnp.sum(p_rows * c_refs[g][1], axis=1, keepdims=True)
            st[g].update(m=m, p_n=p_n, l=jnp.sum(p_b, axis=-1, keepdims=True) + p_n, acc=acc)

    def merge():
        m_all = functools.reduce(jnp.maximum, [s["m"] for s in st])
        sc = [jnp.exp(s["m"] - m_all) for s in st]
        den = sum(c * s["l"] for c, s in zip(sc, st))
        out = jnp.zeros((1, GROUP_W), F32)
        for c, s in zip(sc, st):
            acc_row = jnp.broadcast_to(s["acc"], (GROUP_W, rows)).T[0:1, :]
            out = out + per_head(c / den) * (acc_row + per_head(s["p_n"]) * s["vn"])
        o_ref[...] = out.astype(o_ref.dtype)

    return [scores, values, merge]


def _attn_sample_operands(qkv, caches):
    b = qkv.shape[0]
    args = [qkv.reshape(b, 1, QKV_W)]
    specs = [pl.BlockSpec((None, 1, QKV_W), lambda i: (i, 0, 0))]
    for g, c in enumerate(caches):
        lb = c.shape[1]
        assert lb == BAND * DILATIONS[g]
        args.append(jnp.transpose(c, (0, 2, 3, 4, 1)).reshape(b, 2, GROUP_W, lb))
        specs.append(pl.BlockSpec((None, 2, GROUP_W, lb), lambda i: (i, 0, 0, 0)))
    out_spec = pl.BlockSpec((None, 1, GROUP_W), lambda i: (i, 0, 0))
    return args, specs, out_spec, jax.ShapeDtypeStruct((b, 1, GROUP_W), BF16)


def _attn_sample(qkv, caches):
    args, specs, out_spec, out_shape = _attn_sample_operands(qkv, caches)
    out = pl.pallas_call(
        _attn_sample_body,
        grid=(qkv.shape[0],),
        in_specs=specs,
        out_specs=out_spec,
        out_shape=out_shape,
        compiler_params=_params(("parallel",)),
        name="attn_sample",
    )(*args)
    return out.reshape(-1, GROUP_W)


WKV_PASSES = dict(g=1, s0=1, x=1, inv=1, y=1, s1=1)


def _mm(a, b, dims, passes):
    if passes == 1:
        return _dg(a.astype(BF16), b.astype(BF16), dims)
    ah, al = _split2(a)
    if passes == 2:
        bh = b.astype(BF16)
        return _dg(ah, bh, dims) + _dg(al, bh, dims)
    bh, bl = _split2(b)
    return _dg(ah, bh, dims) + _dg(al, bh, dims) + _dg(ah, bl, dims)


def _wkv_levels(live_rows):
    return int(np.ceil(np.log2(live_rows))) if live_rows > 1 else 0


def _wkv_chunk(operands, states, levels, side=()):
    side = list(side)

    def run_side():
        if side:
            side.pop(0)()
    c = CHUNK
    n2 = 2 * c
    n_seq = len(operands)
    first = lax.broadcasted_iota(jnp.int32, (1, PAIR_W), 1) < RWKV_N

    def stack(x, p):
        x = x[:, p * PAIR_W:(p + 1) * PAIR_W]
        return jnp.concatenate([jnp.where(first, x, 0.0), jnp.where(first, 0.0, x)], axis=0)

    bi = lax.broadcasted_iota(jnp.int32, (n_seq * c, n_seq * c), 0)
    bj = lax.broadcasted_iota(jnp.int32, (n_seq * c, n_seq * c), 1)
    tri_all = ((bi >= bj) & (bi // c == bj // c)).astype(BF16)
    lc_all = _dot_exact_rhs_left(tri_all, jnp.concatenate([ops[3] for ops in operands], axis=0))
    ar, bk, v_s, s0, w_end = [], [], [], [], []
    for j, ((r, kp, v, logw, av, bv), seq_states) in enumerate(zip(operands, states)):
        lc = lc_all[j * c:(j + 1) * c]
        e_in = jnp.exp(lc)
        e_neg = jnp.exp(-lc)
        a_t = av * jnp.exp(lc - logw)
        r_t = r * e_in
        b_t = bv * e_neg
        k_t = kp * e_neg
        for p in range(N_PAIRS):
            ar.append(jnp.concatenate([stack(a_t, p), stack(r_t, p)], axis=0))
            bk.append(jnp.concatenate([stack(b_t, p), stack(k_t, p)], axis=0))
            v_s.append(stack(v, p))
            s0.append(seq_states[p])
            w_end.append(e_in[c - 1:c, p * PAIR_W:(p + 1) * PAIR_W])
    chains = range(len(ar))
    g = [_mm(ar[i], bk[i], NT, WKV_PASSES["g"]) for i in chains]
    ar_s0 = [_mm(ar[i], s0[i], NT, WKV_PASSES["s0"]) for i in chains]
    run_side()
    ri = lax.broadcasted_iota(jnp.int32, (n2, n2), 0)
    ci = lax.broadcasted_iota(jnp.int32, (n2, n2), 1)
    strict = ri > ci
    incl = ri >= ci
    n_ab = [jnp.where(strict, g[i][0:n2, 0:n2], 0.0) for i in chains]
    n_ak = [jnp.where(strict, g[i][0:n2, n2:2 * n2], 0.0) for i in chains]
    m_r = [jnp.where(jnp.concatenate([incl, incl], axis=1), g[i][n2:2 * n2, :], 0.0) for i in chains]
    z = [ar_s0[i][0:n2] + _mm(n_ak[i], v_s[i], NN, WKV_PASSES["x"]) for i in chains]
    pw = n_ab
    for lvl in range(levels):
        if lvl < levels - 1:
            pz = [_mm(pw[i], jnp.concatenate([pw[i], z[i]], axis=1), NN, WKV_PASSES["inv"]) for i in chains]
            pw = [pz[i][:, 0:n2] for i in chains]
            z = [z[i] + pz[i][:, n2:2 * n2] for i in chains]
        else:
            z = [z[i] + _mm(pw[i], z[i], NN, WKV_PASSES["inv"]) for i in chains]
        if lvl % 2 == 1:
            run_side()
    while side:
        run_side()
    uv = [jnp.concatenate([z[i], v_s[i]], axis=0) for i in chains]
    y_s = [ar_s0[i][n2:2 * n2] + _mm(m_r[i], uv[i], NN, WKV_PASSES["y"]) for i in chains]
    s1 = [(s0[i] + _mm(uv[i], bk[i], TN, WKV_PASSES["s1"])) * w_end[i] for i in chains]
    ys = [jnp.concatenate([y_s[j * N_PAIRS + p][0:c] + y_s[j * N_PAIRS + p][c:n2] for p in range(N_PAIRS)], axis=1)
          for j in range(n_seq)]
    new_states = tuple(tuple(s1[j * N_PAIRS:(j + 1) * N_PAIRS]) for j in range(n_seq))
    return ys, new_states


def _dot_exact_rhs_left(lhs_bf16, b):
    out = None
    rem = b
    for _ in range(3):
        part = rem.astype(BF16)
        term = _dg(lhs_bf16, part)
        out = term if out is None else out + term
        rem = rem - part.astype(F32)
    return out


def _rwkv_body(zb_ref, mu_ref, w0_ref, a0_ref, kk_ref, ka_ref, rk_ref, gg_ref, gb_ref, wwa_ref, wg_ref, *refs,
               t_valid, fresh):
    if fresh:
        sh_ref = s0_ref = None
    else:
        sh_ref, s0_ref = refs[:2]
        refs = refs[2:]
    out_ref, s_out_ref, sh_out_ref, s_scr, carry_scr, ring0_scr, ring1_scr = refs
    t = pl.program_id(1)
    n_seq, tc, _ = zb_ref.shape

    @pl.when(t == 0)
    def _():
        if fresh:
            s_scr[...] = jnp.zeros(s_scr.shape, F32)
            carry_scr[...] = jnp.zeros(carry_scr.shape, F32)
        else:
            carry_scr[...] = sh_ref[...]
            zero = jnp.zeros((RWKV_N, RWKV_N), F32)
            for j in range(n_seq):
                for p in range(N_PAIRS):
                    top = jnp.concatenate([s0_ref[j, 2 * p], zero], axis=1)
                    bottom = jnp.concatenate([zero, s0_ref[j, 2 * p + 1]], axis=1)
                    s_scr[j, p] = jnp.concatenate([top, bottom], axis=0)

    lr = lax.broadcasted_iota(jnp.int32, (2 * PAIR_W, 2 * PAIR_W), 0)
    lc = lax.broadcasted_iota(jnp.int32, (2 * PAIR_W, 2 * PAIR_W), 1)
    seg = (lr // RWKV_N == lc // RWKV_N).astype(BF16)
    half = lax.broadcasted_iota(jnp.int32, (1, 128), 1) < 64
    row = lax.broadcasted_iota(jnp.int32, (CHUNK, 1), 0)
    inv_n = 1.0 / RWKV_N
    levels = _wkv_levels(CHUNK if t_valid is None else min(t_valid, CHUNK))
    short = tc < CHUNK

    def head_sum(x):
        wide = 2 * PAIR_W
        return jnp.concatenate(
            [_dg(x[:, c:c + wide].astype(BF16), seg) for c in range(0, RWKV_W, wide)], axis=1)

    n_chunks = max(tc // CHUNK, 1)
    rings = (ring0_scr, ring1_scr)

    def staging_jobs(ci, ring):
        rows = pl.ds(pl.multiple_of(ci * CHUNK, CHUNK), CHUNK)
        st = [dict() for _ in range(n_seq)]
        seq_rows = lambda x, j: x[j * CHUNK:(j + 1) * CHUNK]

        def mix_and_lora():
            x_wa, x_g = [], []
            for j in range(n_seq):
                if short:
                    z = jnp.where(row == 0, zb_ref[j], 0.0)
                    before = carry_scr[j]
                else:
                    z = zb_ref[j, rows, :]
                    before = zb_ref[j, pl.ds(jnp.maximum(ci * CHUNK - 1, 0), 1), :]
                    before = jnp.where(ci == 0, carry_scr[j], before)
                zp = jnp.where(row == 0, before, pltpu.roll(z, 1, 0))
                zm = z + (zp - z) * mu_ref[...]
                z_wa = zm[:, 3 * RWKV_W:3 * RWKV_W + 128]
                x_wa.append(jnp.where(half, jnp.tanh(z_wa), z_wa).astype(BF16))
                x_g.append(_sigmoid(zm[:, 3 * RWKV_W + 128:SHIFT_W]).astype(BF16))
                st[j].update(r=zm[:, 0:RWKV_W], k=zm[:, RWKV_W:2 * RWKV_W], v=zm[:, 2 * RWKV_W:3 * RWKV_W])
            x_wa = jnp.concatenate(x_wa, axis=0)
            u = _dg(x_wa, wwa_ref[:, 0:RWKV_W])
            la = _dg(x_wa, wwa_ref[:, RWKV_W:2 * RWKV_W])
            gate = _dg(jnp.concatenate(x_g, axis=0), wg_ref[...])
            for j in range(n_seq):
                st[j].update(u=seq_rows(u, j), la=seq_rows(la, j), gate=seq_rows(gate, j))

        def head_sums():
            for j in range(n_seq):
                s = st[j]
                a = _sigmoid(a0_ref[...] + s["la"])
                kk = s["k"] * kk_ref[...]
                kp = s["k"] * (1.0 + (a - 1.0) * ka_ref[...])
                s.update(a=a, kk=kk, kp=kp)
            kk2 = head_sum(jnp.concatenate([s["kk"] * s["kk"] for s in st], axis=0))
            rk = head_sum(jnp.concatenate([s["r"] * s["kp"] * rk_ref[...] for s in st], axis=0))
            for j in range(n_seq):
                st[j].update(kk2=seq_rows(kk2, j), rk=seq_rows(rk, j))

        def finish():
            for j in range(n_seq):
                s = st[j]
                logw = -DECAY_SCALE * _sigmoid(w0_ref[...] + s["u"])
                kk = s["kk"] * lax.rsqrt(jnp.maximum(s["kk2"], 1e-24))
                ops = (s["r"], s["kp"], s["v"], logw, -kk, kk * s["a"], s["rk"] * s["v"])
                if t_valid is not None:
                    live = row + ci * CHUNK < t_valid
                    ops = tuple(jnp.where(live, x, 0.0) for x in ops)
                for i, x in enumerate(ops + (s["gate"],)):
                    ring[j, i] = x

        return mix_and_lora, head_sums, finish

    def chunk(ci, states, ring, other):
        side = staging_jobs(jnp.minimum(ci + 1, n_chunks - 1), other) if n_chunks > 1 else ()
        ys, states = _wkv_chunk([tuple(ring[j, i] for i in range(6)) for j in range(n_seq)], states, levels, side)
        y_all = jnp.concatenate(ys, axis=0)
        d_all = y_all - head_sum(y_all) * inv_n
        var_all = head_sum(d_all * d_all) * inv_n
        yn_all = d_all * lax.rsqrt(var_all + GN_EPS) * gg_ref[...] + gb_ref[...]
        for j in range(n_seq):
            res = (yn_all[j * CHUNK:(j + 1) * CHUNK] + ring[j, 6]) * ring[j, 7]
            if short:
                out_ref[j] = res[0:tc, :].astype(out_ref.dtype)
            else:
                out_ref[j, pl.ds(pl.multiple_of(ci * CHUNK, CHUNK), CHUNK), :] = res.astype(out_ref.dtype)
        return states

    for job in staging_jobs(0, rings[0]):
        job()
    states = tuple(tuple(s_scr[j, p] for p in range(N_PAIRS)) for j in range(n_seq))
    if n_chunks == 1:
        states = chunk(0, states, rings[0], rings[1])
    else:
        assert n_chunks % 2 == 0

        def chunk_pair(i, states):
            states = chunk(2 * i, states, rings[0], rings[1])
            return chunk(2 * i + 1, states, rings[1], rings[0])

        states = lax.fori_loop(0, n_chunks // 2, chunk_pair, states)
    for j in range(n_seq):
        carry_scr[j] = zb_ref[j, tc - 1:tc, :]
        for p in range(N_PAIRS):
            s_scr[j, p] = states[j][p]

    @pl.when(t == pl.num_programs(1) - 1)
    def _():
        for j in range(n_seq):
            sh_out_ref[j] = zb_ref[j, tc - 1:tc, :]
            for p in range(N_PAIRS):
                for e in range(2):
                    span = slice(e * RWKV_N, (e + 1) * RWKV_N)
                    s_out_ref[j, 2 * p + e] = states[j][p][span, span]


def _rwkv(zb, w, tc, n_seq, state=None, t_valid=None):
    b, t_len, _ = zb.shape
    full = lambda a: pl.BlockSpec(a.shape, lambda bi, t: (0,) * a.ndim)
    per_seq = lambda *dims: pl.BlockSpec((n_seq,) + dims, lambda bi, t: (bi,) + (0,) * len(dims))
    consts = [w["mu_shift"], w["w0"], w["a0"], w["k_k"], w["k_a"], w["r_k"], w["gn_g"], w["gn_b"], w["wa_lora"],
              w["g_lora"]]
    in_specs = [pl.BlockSpec((n_seq, tc, SHIFT_W), lambda bi, t: (bi, t, 0))] + [full(c) for c in consts]
    args = [zb, *consts]
    if state is not None:
        in_specs += [per_seq(1, SHIFT_W), per_seq(RWKV_HEADS, RWKV_N, RWKV_N)]
        args += list(state)
    vm = lambda shape: pltpu.VMEM(shape, F32)
    assert b % n_seq == 0 and t_len % tc == 0
    assert tc % CHUNK == 0 or (tc == t_len == t_valid == 1), "whole chunks, or single-token sequences"
    scratch = [vm((n_seq, N_PAIRS, PAIR_W, PAIR_W)), vm((n_seq, 1, SHIFT_W))] + [vm((n_seq, 8, CHUNK, RWKV_W))] * 2
    return pl.pallas_call(
        functools.partial(_rwkv_body, t_valid=t_valid, fresh=state is None),
        grid=(b // n_seq, t_len // tc),
        in_specs=in_specs,
        out_specs=[pl.BlockSpec((n_seq, tc, RWKV_W), lambda bi, t: (bi, t, 0)),
                   per_seq(RWKV_HEADS, RWKV_N, RWKV_N), per_seq(1, SHIFT_W)],
        out_shape=[jax.ShapeDtypeStruct((b, t_len, RWKV_W), BF16),
                   jax.ShapeDtypeStruct((b, RWKV_HEADS, RWKV_N, RWKV_N), F32),
                   jax.ShapeDtypeStruct((b, 1, SHIFT_W), F32)],
        scratch_shapes=scratch,
        compiler_params=_params(("parallel", "arbitrary")),
        name="rwkv",
    )(*args)


def _out_proj_body(x_ref, attn_ref, rwkv_ref, gate_ref, wpa_ref, wpb_ref, wout_ref, n2_ref,
                   wup_ref, wdn_ref, nf_ref, *refs):
    rider = []
    if len(refs) > 1:
        qkv_s_ref, c0_ref, c1_ref, c2_ref, y_ref, attn_s_ref = refs
        rider = _attn_sample_jobs(qkv_s_ref, c0_ref, c1_ref, c2_ref, attn_s_ref)
    else:
        y_ref, = refs
    pa = _dg(attn_ref[...], wpa_ref[...])
    pb = _dg(rwkv_ref[...], wpb_ref[...])
    merged = gate_ref[:, 0:D_MODEL] * pa + gate_ref[:, D_MODEL:GATE_W] * pb
    x1 = x_ref[...] + _dg(merged.astype(BF16), wout_ref[...])
    hm = _rms(x1, n2_ref[...]).astype(BF16)
    acc = x1
    for c in range(0, D_FF, 1024):
        up = jnp.maximum(_dg(hm, wup_ref[:, c:c + 1024]), 0.0)
        if rider:
            rider.pop(0)()
        acc = acc + _dg((up * up).astype(BF16), wdn_ref[c:c + 1024, :])
    y_ref[...] = _rms(acc, nf_ref[...])


def _out_proj(x, attn, rwkv, gate, w, tm, rider=None):
    m = x.shape[0]
    row = lambda wd: pl.BlockSpec((tm, wd), lambda i: (i, 0))
    full = lambda a: pl.BlockSpec(a.shape, lambda i: (0,) * a.ndim)
    consts = [w["w_proj_a"], w["w_proj_b"], w["w_out"], w["norm2_g"], w["w_up"], w["w_down"], w["normf_g"]]
    in_specs = [row(D_MODEL), row(GROUP_W), row(RWKV_W), row(GATE_W)] + [full(c) for c in consts]
    args = [x, attn, rwkv, gate, *consts]
    out_specs = [row(D_MODEL)]
    out_shape = [jax.ShapeDtypeStruct((m, D_MODEL), F32)]
    if rider is not None:
        r_args, r_specs, r_out_spec, r_out_shape = _attn_sample_operands(*rider)
        assert r_out_shape.shape[0] == m // tm
        in_specs += r_specs
        args += r_args
        out_specs.append(r_out_spec)
        out_shape.append(r_out_shape)
    outs = pl.pallas_call(
        _out_proj_body,
        grid=(m // tm,),
        in_specs=in_specs,
        out_specs=out_specs,
        out_shape=out_shape,
        compiler_params=_params(("parallel",)),
        name="out_proj",
    )(*args)
    if rider is None:
        return outs[0]
    return outs[0], outs[1].reshape(-1, GROUP_W)


def _kv_rows(qkv, group, rows):
    b, s, _ = qkv.shape
    k0 = ATTN_W + group * GROUP_W
    v0 = 2 * ATTN_W + group * GROUP_W
    k = qkv[:, s - rows:, k0:k0 + GROUP_W].reshape(b, rows, HG, HEAD_DIM)
    v = qkv[:, s - rows:, v0:v0 + GROUP_W].reshape(b, rows, HG, HEAD_DIM)
    return jnp.stack([k, v], axis=2)


def _layer_weights(l, norm1_g, w_in, b_gate, mu_shift, w0, w_lora_up, a0, a_lora_up, g_lora_up, k_k, k_a,
                   r_k, gn_g, gn_b, w_proj_a, w_proj_b, w_out, norm2_g, w_up, w_down, normf_g):
    row = lambda a: a.reshape(1, -1)
    zero = jnp.zeros_like(w_lora_up[l])
    wa_lora = jnp.concatenate([jnp.concatenate([w_lora_up[l], zero], axis=1),
                               jnp.concatenate([zero, a_lora_up[l]], axis=1)], axis=0)
    return dict(
        norm1_g=row(norm1_g[l]), w_in=w_in[l].astype(BF16), b_gate=row(b_gate[l]), mu_shift=row(mu_shift[l]),
        w0=row(w0[l]), a0=row(a0[l]), k_k=row(k_k[l]), k_a=row(k_a[l]), r_k=row(r_k[l]),
        gn_g=row(gn_g[l]), gn_b=row(gn_b[l]),
        wa_lora=wa_lora.astype(BF16), g_lora=g_lora_up[l].astype(BF16),
        w_proj_a=w_proj_a[l].astype(BF16), w_proj_b=w_proj_b[l].astype(BF16), w_out=w_out[l].astype(BF16),
        norm2_g=row(norm2_g[l]), w_up=w_up[l].astype(BF16), w_down=w_down[l].astype(BF16),
        normf_g=row(normf_g))


def _layer(xp, xs, caches, s0, shift0, w):
    b, s, _ = xp.shape
    bs, t_len, _ = xs.shape
    assert t_len == 1
    xp2 = xp.reshape(b * s, D_MODEL)
    xs2 = xs.reshape(bs, D_MODEL)
    qkv_s, zb_s, gate_s = _in_proj(xs2, w["norm1_g"], w["w_in"], w["b_gate"], bs)
    qkv, zb, gate, *kv_t = _in_proj(xp2, w["norm1_g"], w["w_in"], w["b_gate"], PROMPT_TILE, seq_len=s)
    zb = zb.reshape(b, s, SHIFT_W)
    attn = _attn_prompt(qkv.reshape(b, s, QKV_W))
    rwkv, wkv_p, shift_p = _rwkv(zb, w, RWKV_TILE, RWKV_SEQS_PER_STEP)
    attn2, rwkv2 = attn.reshape(b * s, GROUP_W), rwkv.reshape(b * s, RWKV_W)
    if b * s // PROMPT_TILE == bs:
        y_p, attn_s = _out_proj(xp2, attn2, rwkv2, gate, w, PROMPT_TILE, rider=(qkv_s, caches))
    else:
        y_p = _out_proj(xp2, attn2, rwkv2, gate, w, PROMPT_TILE)
        attn_s = _attn_sample(qkv_s, caches)
    rwkv_s, wkv_s, _ = _rwkv(zb_s[:, None, :], w, 1, RWKV_SEQS_PER_STEP, state=(shift0[:, None, :], s0), t_valid=1)
    y_s = _out_proj(xs2, attn_s, rwkv_s[:, 0], gate_s, w, bs)
    kv_p = [jnp.transpose(t.reshape(b, 2, HG, HEAD_DIM, t.shape[-1]), (0, 4, 1, 2, 3)) for t in kv_t]
    kv_s = [_kv_rows(qkv_s[:, None, :], g, 1) for g in range(len(DILATIONS))]
    prompt = (y_p.reshape(b, s, D_MODEL), kv_p, wkv_p, shift_p[:, 0])
    sample = (y_s.reshape(bs, 1, D_MODEL), kv_s, wkv_s, zb_s)
    return prompt, sample


def kernel(x_prompt, x_sample, cache_kv_w128, cache_kv_w512, cache_kv_w2048, state_wkv, state_shift, norm1_g, w_in, b_gate, mu_shift, w0, w_lora_up, a0, a_lora_up, g_lora_up, k_k, k_a, r_k, gn_g, gn_b, w_proj_a, w_proj_b, w_out, norm2_g, w_up, w_down, normf_g):
    depth = w_in.shape[0]
    assert depth == 1, "the final norm is fused into the layer's output stage"
    w = _layer_weights(0, norm1_g, w_in, b_gate, mu_shift, w0, w_lora_up, a0, a_lora_up, g_lora_up, k_k, k_a,
                       r_k, gn_g, gn_b, w_proj_a, w_proj_b, w_out, norm2_g, w_up, w_down, normf_g)
    (y_p, kv_p, wkv_p, shift_p), (y_s, kv_s, wkv_s, shift_s) = _layer(
        x_prompt, x_sample, (cache_kv_w128[0], cache_kv_w512[0], cache_kv_w2048[0]), state_wkv[0], state_shift[0], w)
    lead = lambda a: a[None]
    return (y_p, y_s, lead(kv_p[0]), lead(kv_p[1]), lead(kv_p[2]), lead(wkv_p), lead(shift_p),
            lead(kv_s[0]), lead(kv_s[1]), lead(kv_s[2]), lead(wkv_s), lead(shift_s))
```

```python
import functools

import numpy as np
import jax
import jax.numpy as jnp
from jax import lax
from jax.experimental import pallas as pl
from jax.experimental.pallas import tpu as pltpu

F32 = jnp.float32
BF16 = jnp.bfloat16

D_MODEL = 1024
HEAD_DIM = 64
HG = 4
DILATIONS = (1, 4, 16)
BAND = 128
N_ATTN_HEADS = HG * len(DILATIONS)
GROUP_W = HG * HEAD_DIM
LANE_HALVES = GROUP_W // 128
ATTN_W = N_ATTN_HEADS * HEAD_DIM
QKV_W = 3 * ATTN_W
RWKV_N = 64
RWKV_W = 512
RWKV_HEADS = RWKV_W // RWKV_N
PAIR_W = 2 * RWKV_N
N_PAIRS = RWKV_HEADS // 2
LORA_W = 256
SHIFT_W = 3 * RWKV_W + LORA_W
GATE_W = 2 * D_MODEL
IN_W = QKV_W + SHIFT_W + GATE_W
D_FF = 4 * D_MODEL
NORM_EPS = 1e-6
GN_EPS = 64e-5
NEG_INF = -1e30
ATTN_SCALE = HEAD_DIM ** -0.5
CHUNK = 64
RWKV_SEQS_PER_STEP = 4
RWKV_TILE = 256
RWKV_UNROLLED_CHUNKS = 4
PROMPT_TILE = 512
DECAY_SCALE = float(np.exp(-0.5))
SLOPES = [float(s) for s in np.exp2(-8.0 * np.arange(1, N_ATTN_HEADS + 1, dtype=np.float32) / N_ATTN_HEADS)]

V7X_VMEM_LIMIT = 56 * 1024 * 1024

NN = (((1,), (0,)), ((), ()))
NT = (((1,), (1,)), ((), ()))
TN = (((0,), (0,)), ((), ()))


def _dg(a, b, dims=NN):
    return lax.dot_general(a, b, dims, preferred_element_type=F32)


def _split2(a):
    hi = a.astype(BF16)
    lo = (a - hi.astype(F32)).astype(BF16)
    return hi, lo


def _dot3(a, b, dims=NN):
    ah, al = _split2(a)
    bh, bl = _split2(b)
    return _dg(ah, bh, dims) + _dg(al, bh, dims) + _dg(ah, bl, dims)


def _dot_exact_rhs(a, b_bf16, passes):
    out = None
    rem = a
    for _ in range(passes):
        part = rem.astype(BF16)
        term = _dg(part, b_bf16)
        out = term if out is None else out + term
        rem = rem - part.astype(F32)
    return out


def _sigmoid(x):
    return 0.5 * jnp.tanh(0.5 * x) + 0.5


def _rms(x, g):
    return x * lax.rsqrt(jnp.mean(x * x, axis=-1, keepdims=True) + NORM_EPS) * g


def _params(sem):
    return pltpu.CompilerParams(dimension_semantics=sem, vmem_limit_bytes=V7X_VMEM_LIMIT)


def _kv_tail_plan(seq_len, tm):
    plan = []
    for dil in DILATIONS:
        rows = min(BAND * dil, seq_len)
        width = min(rows, tm)
        plan.append((rows, width, (seq_len - rows) // tm))
    return plan


def _in_proj_body(x_ref, g_ref, w_ref, bg_ref, qkv_ref, zb_ref, gate_ref, *kv_refs, seq_len):
    tm = x_ref.shape[0]

    def write_tail(g, kv_ref, width):
        for sec in (1, 2):
            c0 = sec * ATTN_W + g * GROUP_W
            kv_ref[sec - 1] = qkv_ref[tm - width:tm, c0:c0 + GROUP_W].T

    plan = _kv_tail_plan(seq_len, tm) if kv_refs else []
    h = _rms(x_ref[...], g_ref[...]).astype(BF16)
    for c in range(0, GATE_W, 1024):
        zg = _dg(h, w_ref[:, QKV_W + SHIFT_W + c:QKV_W + SHIFT_W + c + 1024])
        gate_ref[:, c:c + 1024] = _sigmoid(zg + bg_ref[:, c:c + 1024])
    for c in range(0, QKV_W, 768):
        qkv_ref[:, c:c + 768] = _dg(h, w_ref[:, c:c + 768])
    for g, (kv_ref, (_, width, first)) in enumerate(zip(kv_refs, plan)):
        if first == 0:
            write_tail(g, kv_ref, width)
    for c in range(0, SHIFT_W, 896):
        zb_ref[:, c:c + 896] = _dg(h, w_ref[:, QKV_W + c:QKV_W + c + 896])
    for g, (kv_ref, (_, width, first)) in enumerate(zip(kv_refs, plan)):
        if first > 0:
            tile = pl.program_id(0) % (seq_len // tm)
            pl.when(tile >= first)(functools.partial(write_tail, g, kv_ref, width))


def _in_proj(x, norm_g, w_in_bf16, b_gate, tm, seq_len=None):
    m = x.shape[0]
    row = lambda w: pl.BlockSpec((tm, w), lambda i: (i, 0))
    full = lambda a: pl.BlockSpec(a.shape, lambda i: (0,) * a.ndim)
    out_specs = [row(QKV_W), row(SHIFT_W), row(GATE_W)]
    out_shape = [jax.ShapeDtypeStruct((m, QKV_W), F32),
                 jax.ShapeDtypeStruct((m, SHIFT_W), F32),
                 jax.ShapeDtypeStruct((m, GATE_W), F32)]
    if seq_len is not None:
        tiles = seq_len // tm
        for rows, width, first in _kv_tail_plan(seq_len, tm):
            out_specs.append(pl.BlockSpec(
                (None, 2, GROUP_W, width),
                lambda i, first=first: (i // tiles, 0, 0, jnp.maximum(i % tiles - first, 0))))
            out_shape.append(jax.ShapeDtypeStruct((m // seq_len, 2, GROUP_W, rows), F32))
    return pl.pallas_call(
        functools.partial(_in_proj_body, seq_len=seq_len),
        grid=(m // tm,),
        in_specs=[row(D_MODEL), full(norm_g), full(w_in_bf16), full(b_gate)],
        out_specs=out_specs,
        out_shape=out_shape,
        compiler_params=_params(("arbitrary",)),
        name="in_proj",
    )(x, norm_g, w_in_bf16, b_gate)


def _residue_rows(r, count, dil):
    return pl.ds(r, count) if dil == 1 else pl.ds(r, count, stride=dil)


def _attn_group_blocks(q_ref, k_ref, v_ref, o_scr, lse_scr, qs, ks, vs, os_, ls, bias_scr, tmp, group):
    s_len = qs.shape[0]
    dil = DILATIONS[group]
    l_res = s_len // dil
    nb = l_res // BAND
    inner = 4 if dil > 4 else dil
    outer = dil // inner
    l_in = s_len // inner
    assert dil in (1, inner, inner * outer) and outer <= 4

    def gather(halves, r):
        if outer == 1:
            return jnp.concatenate([h[_residue_rows(r, l_res, dil), :] for h in halves], axis=1)
        c, q4 = r % inner, r // inner
        return jnp.concatenate([tmp[i, pl.ds(c * l_in + q4, l_res, stride=outer), :]
                                for i in range(LANE_HALVES)], axis=1)

    head = lax.broadcasted_iota(jnp.int32, (1, GROUP_W), 1) // HEAD_DIM
    hm = [head == h for h in range(HG)]
    for name, halves in (("q", q_ref), ("k", k_ref), ("v", v_ref)):
        if outer > 1:
            for i, h in enumerate(halves):
                for c in range(inner):
                    tmp[i, pl.ds(c * l_in, l_in), :] = h[_residue_rows(c, l_in, inner), :]
        for r in range(dil):
            dst = pl.ds(r * l_res, l_res)
            x = gather(halves, r)
            if name == "q":
                qs[dst, :] = (x * ATTN_SCALE).astype(BF16)
            elif name == "k":
                ks[dst, :] = x.astype(BF16)
            else:
                vs[dst, :] = x.astype(BF16)
    has_prev = nb > 1
    nk = 2 * BAND if has_prev else BAND
    qi = lax.broadcasted_iota(jnp.int32, (BAND, nk), 0)
    kj = lax.broadcasted_iota(jnp.int32, (BAND, nk), 1)
    delta = (nk - BAND) + qi - kj
    band = (delta >= 0) & (delta <= BAND)
    dist = (delta * dil).astype(F32)
    for h in range(HG):
        alibi = -SLOPES[group * HG + h] * dist
        bias_scr[h, :, 0:nk] = jnp.where(band, alibi, NEG_INF)
        bias_scr[HG + h, :, 0:nk] = jnp.where(band & (kj >= nk - BAND), alibi, NEG_INF)
    blocks_per_iter = 16

    def block_pair(it, carry):
        cur, q, k2, v2, first = [], [], [], [], []
        for u in range(blocks_per_iter):
            idx = it * blocks_per_iter + u
            cur.append(pl.ds(pl.multiple_of(idx * BAND, BAND), BAND))
            q.append(qs[cur[u], :])
            k2.append(ks[cur[u], :])
            v2.append(vs[cur[u], :])
            first.append(jnp.where(idx % nb == 0, HG, 0))
            if has_prev:
                prev = pl.ds(pl.multiple_of(jnp.maximum(idx - 1, 0) * BAND, BAND), BAND)
                k2[u] = jnp.concatenate([ks[prev, :], k2[u]], axis=0)
                v2[u] = jnp.concatenate([vs[prev, :], v2[u]], axis=0)
        chains = [(u, h) for u in range(blocks_per_iter) for h in range(HG)]
        s = [_dg(jnp.where(hm[h], q[u], jnp.zeros_like(q[u])), k2[u], NT) + bias_scr[first[u] + h, :, 0:nk]
             for u, h in chains]
        m = [jnp.max(x, axis=-1, keepdims=True) for x in s]
        p = [jnp.exp(x - mx) for x, mx in zip(s, m)]
        l = [jnp.sum(x, axis=-1, keepdims=True) for x in p]
        pv = [_dg(p[i].astype(BF16), v2[u]) for i, (u, h) in enumerate(chains)]
        for u in range(blocks_per_iter):
            o_acc = jnp.zeros((BAND, GROUP_W), F32)
            lse_acc = jnp.zeros((BAND, GROUP_W), F32)
            for h in range(HG):
                i = u * HG + h
                o_acc = jnp.where(hm[h], pv[i] / l[i], o_acc)
                lse_acc = jnp.where(hm[h], m[i] + jnp.log(l[i]), lse_acc)
            if dil == 1:
                for half in range(LANE_HALVES):
                    cols = slice(half * 128, (half + 1) * 128)
                    o_scr[group * LANE_HALVES + half, cur[u], :] = o_acc[:, cols]
                    lse_scr[group * LANE_HALVES + half, cur[u], :] = lse_acc[:, cols]
            else:
                os_[cur[u], :] = o_acc
                ls[cur[u], :] = lse_acc
        return carry

    assert (dil * nb) % blocks_per_iter == 0
    lax.fori_loop(0, dil * nb // blocks_per_iter, block_pair, 0)
    for res, nat in ((os_, o_scr), (ls, lse_scr)) if dil > 1 else ():
        for half in range(LANE_HALVES):
            cols = slice(half * 128, (half + 1) * 128)
            slab = group * LANE_HALVES + half
            for r in range(dil):
                src = pl.ds(r * l_res, l_res)
                if outer == 1:
                    nat[slab, _residue_rows(r, l_res, dil), :] = res[src, cols]
                else:
                    tmp[half, pl.ds((r % inner) * l_in + r // inner, l_res, stride=outer), :] = res[src, cols]
            if outer > 1:
                for c in range(inner):
                    nat[slab, _residue_rows(c, l_in, inner), :] = tmp[half, pl.ds(c * l_in, l_in), :]


def _attn_prompt_body(q0_ref, q1_ref, k0_ref, k1_ref, v0_ref, v1_ref, o_ref, o_scr, lse_scr, qs, ks, vs, os_, ls, bias_scr, tmp):
    q_ref, k_ref, v_ref = (q0_ref, q1_ref), (k0_ref, k1_ref), (v0_ref, v1_ref)
    gid = pl.program_id(1)
    n_groups = len(DILATIONS)
    for g in range(n_groups):
        @pl.when(gid == g)
        def _(g=g):
            _attn_group_blocks(q_ref, k_ref, v_ref, o_scr, lse_scr, qs, ks, vs, os_, ls, bias_scr, tmp, g)

    @pl.when(gid == n_groups - 1)
    def _():
        def merge(i, carry):
            rows = pl.ds(pl.multiple_of(i * BAND, BAND), BAND)
            both = lambda ref, g: jnp.concatenate(
                [ref[g * LANE_HALVES + half, rows, :] for half in range(LANE_HALVES)], axis=1)
            lse = [both(lse_scr, g) for g in range(n_groups)]
            m = functools.reduce(jnp.maximum, lse)
            wts = [jnp.exp(x - m) for x in lse]
            num = sum(wts[g] * both(o_scr, g) for g in range(n_groups))
            o_ref[rows, :] = (num / sum(wts)).astype(o_ref.dtype)
            return carry

        lax.fori_loop(0, o_ref.shape[0] // BAND, merge, 0)


def _attn_prompt(qkv):
    b, s, _ = qkv.shape
    n_groups = len(DILATIONS)
    assert s % (DILATIONS[-1] * BAND) == 0
    sec = ATTN_W // 128

    def col(section, half):
        return pl.BlockSpec((None, s, 128), lambda bi, g: (bi, 0, section * sec + g * LANE_HALVES + half))

    return pl.pallas_call(
        _attn_prompt_body,
        grid=(b, n_groups),
        in_specs=[col(section, half) for section in range(3) for half in range(LANE_HALVES)],
        out_specs=pl.BlockSpec((None, s, GROUP_W), lambda bi, g: (bi, 0, 0)),
        out_shape=jax.ShapeDtypeStruct((b, s, GROUP_W), BF16),
        scratch_shapes=([pltpu.VMEM((n_groups * LANE_HALVES, s, 128), F32)] * 2
                        + [pltpu.VMEM((s, GROUP_W), BF16)] * 3 + [pltpu.VMEM((s, GROUP_W), F32)] * 2
                        + [pltpu.VMEM((2 * HG, BAND, 2 * BAND), F32), pltpu.VMEM((LANE_HALVES, s, 128), F32)]),
        compiler_params=_params(("parallel", "arbitrary")),
        name="attn_prompt",
    )(*([qkv] * (3 * LANE_HALVES)))


def _attn_sample_body(qkv_ref, c0_ref, c1_ref, c2_ref, o_ref):
    for job in _attn_sample_jobs(qkv_ref, c0_ref, c1_ref, c2_ref, o_ref):
        job()


def _attn_sample_jobs(qkv_ref, c0_ref, c1_ref, c2_ref, o_ref):
    rows = 8
    row = lax.broadcasted_iota(jnp.int32, (rows, GROUP_W), 0)
    own = lax.broadcasted_iota(jnp.int32, (rows, GROUP_W), 1) // HEAD_DIM == row
    hrow = lax.broadcasted_iota(jnp.int32, (HG, 1), 0)
    lane_head = lax.broadcasted_iota(jnp.int32, (1, GROUP_W), 1) // HEAD_DIM
    c_refs = (c0_ref, c1_ref, c2_ref)
    groups = range(len(DILATIONS))
    st = [dict() for _ in groups]

    def to_column(x_row):
        return jnp.sum(jnp.where(own, x_row, 0.0).T, axis=1, keepdims=True)

    def per_head(col):
        out = jnp.zeros((1, GROUP_W), F32)
        for h in range(HG):
            out = jnp.where(lane_head == h, col[h:h + 1, :], out)
        return out

    def scores():
        for g in groups:
            lb = c_refs[g].shape[2]
            col = lambda sec: qkv_ref[:, sec * ATTN_W + g * GROUP_W:sec * ATTN_W + (g + 1) * GROUP_W]
            q, kn, vn = col(0), col(1), col(2)
            prod = c_refs[g][0] * to_column(q)
            s_b = jnp.sum(prod.reshape(HG, HEAD_DIM, lb), axis=1) * ATTN_SCALE
            s_n = jnp.sum(jnp.where(own, q * kn, 0.0), axis=-1, keepdims=True)[0:HG] * ATTN_SCALE
            st[g].update(vn=vn, s_b=s_b, s_n=s_n)

    def values():
        for g in groups:
            dil = DILATIONS[g]
            lb = c_refs[g].shape[2]
            slope = jnp.zeros((HG, 1), F32)
            for h in range(HG):
                slope = jnp.where(hrow == h, SLOPES[g * HG + h], slope)
            t = lax.broadcasted_iota(jnp.int32, (HG, lb), 1)
            s_b = jnp.where(t % dil == 0, st[g]["s_b"] - slope * (lb - t).astype(F32), NEG_INF)
            m = jnp.maximum(jnp.max(s_b, axis=-1, keepdims=True), st[g]["s_n"])
            p_b = jnp.exp(s_b - m)
            p_n = jnp.exp(st[g]["s_n"] - m)
            p_rows = jnp.broadcast_to(p_b[:, None, :], (HG, HEAD_DIM, lb)).reshape(GROUP_W, lb)
            acc = jnp.sum(p_rows * c_refs[g][1], axis=1, keepdims=True)
            st[g].update(m=m, p_n=p_n, l=jnp.sum(p_b, axis=-1, keepdims=True) + p_n, acc=acc)

    def merge():
        m_all = functools.reduce(jnp.maximum, [s["m"] for s in st])
        sc = [jnp.exp(s["m"] - m_all) for s in st]
        den = sum(c * s["l"] for c, s in zip(sc, st))
        out = jnp.zeros((1, GROUP_W), F32)
        for c, s in zip(sc, st):
            acc_row = jnp.broadcast_to(s["acc"], (GROUP_W, rows)).T[0:1, :]
            out = out + per_head(c / den) * (acc_row + per_head(s["p_n"]) * s["vn"])
        o_ref[...] = out.astype(o_ref.dtype)

    return [scores, values, merge]


def _attn_sample_operands(qkv, caches):
    b = qkv.shape[0]
    args = [qkv.reshape(b, 1, QKV_W)]
    specs = [pl.BlockSpec((None, 1, QKV_W), lambda i: (i, 0, 0))]
    for g, c in enumerate(caches):
        lb = c.shape[1]
        assert lb == BAND * DILATIONS[g]
        args.append(jnp.transpose(c, (0, 2, 3, 4, 1)).reshape(b, 2, GROUP_W, lb))
        specs.append(pl.BlockSpec((None, 2, GROUP_W, lb), lambda i: (i, 0, 0, 0)))
    out_spec = pl.BlockSpec((None, 1, GROUP_W), lambda i: (i, 0, 0))
    return args, specs, out_spec, jax.ShapeDtypeStruct((b, 1, GROUP_W), BF16)


def _attn_sample(qkv, caches):
    args, specs, out_spec, out_shape = _attn_sample_operands(qkv, caches)
    out = pl.pallas_call(
        _attn_sample_body,
        grid=(qkv.shape[0],),
        in_specs=specs,
        out_specs=out_spec,
        out_shape=out_shape,
        compiler_params=_params(("parallel",)),
        name="attn_sample",
    )(*args)
    return out.reshape(-1, GROUP_W)


WKV_PASSES = dict(g=1, s0=1, x=1, inv=1, y=1, s1=1)


def _mm(a, b, dims, passes):
    if passes == 1:
        return _dg(a.astype(BF16), b.astype(BF16), dims)
    ah, al = _split2(a)
    if passes == 2:
        bh = b.astype(BF16)
        return _dg(ah, bh, dims) + _dg(al, bh, dims)
    bh, bl = _split2(b)
    return _dg(ah, bh, dims) + _dg(al, bh, dims) + _dg(ah, bl, dims)


def _wkv_levels(live_rows):
    return int(np.ceil(np.log2(live_rows))) if live_rows > 1 else 0


def _wkv_chunk(operands, states, levels, side=()):
    side = list(side)

    def run_side():
        if side:
            side.pop(0)()
    c = CHUNK
    n2 = 2 * c
    n_seq = len(operands)
    first = lax.broadcasted_iota(jnp.int32, (1, PAIR_W), 1) < RWKV_N

    def stack(x, p):
        x = x[:, p * PAIR_W:(p + 1) * PAIR_W]
        return jnp.concatenate([jnp.where(first, x, 0.0), jnp.where(first, 0.0, x)], axis=0)

    bi = lax.broadcasted_iota(jnp.int32, (n_seq * c, n_seq * c), 0)
    bj = lax.broadcasted_iota(jnp.int32, (n_seq * c, n_seq * c), 1)
    tri_all = ((bi >= bj) & (bi // c == bj // c)).astype(BF16)
    lc_all = _dot_exact_rhs_left(tri_all, jnp.concatenate([ops[3] for ops in operands], axis=0))
    ar, bk, v_s, s0, w_end = [], [], [], [], []
    for j, ((r, kp, v, logw, av, bv), seq_states) in enumerate(zip(operands, states)):
        lc = lc_all[j * c:(j + 1) * c]
        e_in = jnp.exp(lc)
        e_neg = jnp.exp(-lc)
        a_t = av * jnp.exp(lc - logw)
        r_t = r * e_in
        b_t = bv * e_neg
        k_t = kp * e_neg
        for p in range(N_PAIRS):
            ar.append(jnp.concatenate([stack(a_t, p), stack(r_t, p)], axis=0))
            bk.append(jnp.concatenate([stack(b_t, p), stack(k_t, p)], axis=0))
            v_s.append(stack(v, p))
            s0.append(seq_states[p])
            w_end.append(e_in[c - 1:c, p * PAIR_W:(p + 1) * PAIR_W])
    chains = range(len(ar))
    g = [_mm(ar[i], bk[i], NT, WKV_PASSES["g"]) for i in chains]
    ar_s0 = [_mm(ar[i], s0[i], NT, WKV_PASSES["s0"]) for i in chains]
    run_side()
    ri = lax.broadcasted_iota(jnp.int32, (n2, n2), 0)
    ci = lax.broadcasted_iota(jnp.int32, (n2, n2), 1)
    strict = ri > ci
    incl = ri >= ci
    n_ab = [jnp.where(strict, g[i][0:n2, 0:n2], 0.0) for i in chains]
    n_ak = [jnp.where(strict, g[i][0:n2, n2:2 * n2], 0.0) for i in chains]
    m_r = [jnp.where(jnp.concatenate([incl, incl], axis=1), g[i][n2:2 * n2, :], 0.0) for i in chains]
    z = [ar_s0[i][0:n2] + _mm(n_ak[i], v_s[i], NN, WKV_PASSES["x"]) for i in chains]
    pw = n_ab
    for lvl in range(levels):
        if lvl < levels - 1:
            pz = [_mm(pw[i], jnp.concatenate([pw[i], z[i]], axis=1), NN, WKV_PASSES["inv"]) for i in chains]
            pw = [pz[i][:, 0:n2] for i in chains]
            z = [z[i] + pz[i][:, n2:2 * n2] for i in chains]
        else:
            z = [z[i] + _mm(pw[i], z[i], NN, WKV_PASSES["inv"]) for i in chains]
        if lvl % 2 == 1:
            run_side()
    while side:
        run_side()
    uv = [jnp.concatenate([z[i], v_s[i]], axis=0) for i in chains]
    y_s = [ar_s0[i][n2:2 * n2] + _mm(m_r[i], uv[i], NN, WKV_PASSES["y"]) for i in chains]
    s1 = [(s0[i] + _mm(uv[i], bk[i], TN, WKV_PASSES["s1"])) * w_end[i] for i in chains]
    ys = [jnp.concatenate([y_s[j * N_PAIRS + p][0:c] + y_s[j * N_PAIRS + p][c:n2] for p in range(N_PAIRS)], axis=1)
          for j in range(n_seq)]
    new_states = tuple(tuple(s1[j * N_PAIRS:(j + 1) * N_PAIRS]) for j in range(n_seq))
    return ys, new_states


def _dot_exact_rhs_left(lhs_bf16, b):
    out = None
    rem = b
    for _ in range(3):
        part = rem.astype(BF16)
        term = _dg(lhs_bf16, part)
        out = term if out is None else out + term
        rem = rem - part.astype(F32)
    return out


def _rwkv_body(zb_ref, mu_ref, w0_ref, a0_ref, kk_ref, ka_ref, rk_ref, gg_ref, gb_ref, wwa_ref, wg_ref, *refs,
               t_valid, fresh):
    if fresh:
        sh_ref = s0_ref = None
    else:
        sh_ref, s0_ref = refs[:2]
        refs = refs[2:]
    out_ref, s_out_ref, sh_out_ref, s_scr, carry_scr, ring0_scr, ring1_scr = refs
    t = pl.program_id(1)
    n_seq, tc, _ = zb_ref.shape

    @pl.when(t == 0)
    def _():
        if fresh:
            s_scr[...] = jnp.zeros(s_scr.shape, F32)
            carry_scr[...] = jnp.zeros(carry_scr.shape, F32)
        else:
            carry_scr[...] = sh_ref[...]
            zero = jnp.zeros((RWKV_N, RWKV_N), F32)
            for j in range(n_seq):
                for p in range(N_PAIRS):
                    top = jnp.concatenate([s0_ref[j, 2 * p], zero], axis=1)
                    bottom = jnp.concatenate([zero, s0_ref[j, 2 * p + 1]], axis=1)
                    s_scr[j, p] = jnp.concatenate([top, bottom], axis=0)

    lr = lax.broadcasted_iota(jnp.int32, (2 * PAIR_W, 2 * PAIR_W), 0)
    lc = lax.broadcasted_iota(jnp.int32, (2 * PAIR_W, 2 * PAIR_W), 1)
    seg = (lr // RWKV_N == lc // RWKV_N).astype(BF16)
    half = lax.broadcasted_iota(jnp.int32, (1, 128), 1) < 64
    row = lax.broadcasted_iota(jnp.int32, (CHUNK, 1), 0)
    inv_n = 1.0 / RWKV_N
    levels = _wkv_levels(CHUNK if t_valid is None else min(t_valid, CHUNK))
    short = tc < CHUNK

    def head_sum(x):
        wide = 2 * PAIR_W
        return jnp.concatenate(
            [_dg(x[:, c:c + wide].astype(BF16), seg) for c in range(0, RWKV_W, wide)], axis=1)

    n_chunks = max(tc // CHUNK, 1)
    rings = (ring0_scr, ring1_scr)

    def staging_jobs(ci, ring):
        rows = pl.ds(pl.multiple_of(ci * CHUNK, CHUNK), CHUNK)
        st = [dict() for _ in range(n_seq)]
        seq_rows = lambda x, j: x[j * CHUNK:(j + 1) * CHUNK]

        def mix_and_lora():
            x_wa, x_g = [], []
            for j in range(n_seq):
                if short:
                    z = jnp.where(row == 0, zb_ref[j], 0.0)
                    before = carry_scr[j]
                else:
                    z = zb_ref[j, rows, :]
                    before = zb_ref[j, pl.ds(jnp.maximum(ci * CHUNK - 1, 0), 1), :]
                    before = jnp.where(ci == 0, carry_scr[j], before)
                zp = jnp.where(row == 0, before, pltpu.roll(z, 1, 0))
                zm = z + (zp - z) * mu_ref[...]
                z_wa = zm[:, 3 * RWKV_W:3 * RWKV_W + 128]
                x_wa.append(jnp.where(half, jnp.tanh(z_wa), z_wa).astype(BF16))
                x_g.append(_sigmoid(zm[:, 3 * RWKV_W + 128:SHIFT_W]).astype(BF16))
                st[j].update(r=zm[:, 0:RWKV_W], k=zm[:, RWKV_W:2 * RWKV_W], v=zm[:, 2 * RWKV_W:3 * RWKV_W])
            x_wa = jnp.concatenate(x_wa, axis=0)
            u = _dg(x_wa, wwa_ref[:, 0:RWKV_W])
            la = _dg(x_wa, wwa_ref[:, RWKV_W:2 * RWKV_W])
            gate = _dg(jnp.concatenate(x_g, axis=0), wg_ref[...])
            for j in range(n_seq):
                st[j].update(u=seq_rows(u, j), la=seq_rows(la, j), gate=seq_rows(gate, j))

        def head_sums():
            for j in range(n_seq):
                s = st[j]
                a = _sigmoid(a0_ref[...] + s["la"])
                kk = s["k"] * kk_ref[...]
                kp = s["k"] * (1.0 + (a - 1.0) * ka_ref[...])
                s.update(a=a, kk=kk, kp=kp)
            kk2 = head_sum(jnp.concatenate([s["kk"] * s["kk"] for s in st], axis=0))
            rk = head_sum(jnp.concatenate([s["r"] * s["kp"] * rk_ref[...] for s in st], axis=0))
            for j in range(n_seq):
                st[j].update(kk2=seq_rows(kk2, j), rk=seq_rows(rk, j))

        def finish():
            for j in range(n_seq):
                s = st[j]
                logw = -DECAY_SCALE * _sigmoid(w0_ref[...] + s["u"])
                kk = s["kk"] * lax.rsqrt(jnp.maximum(s["kk2"], 1e-24))
                ops = (s["r"], s["kp"], s["v"], logw, -kk, kk * s["a"], s["rk"] * s["v"])
                if t_valid is not None:
                    live = row + ci * CHUNK < t_valid
                    ops = tuple(jnp.where(live, x, 0.0) for x in ops)
                for i, x in enumerate(ops + (s["gate"],)):
                    ring[j, i] = x

        return mix_and_lora, head_sums, finish

    def chunk(ci, states, ring, other):
        side = staging_jobs(jnp.minimum(ci + 1, n_chunks - 1), other) if n_chunks > 1 else ()
        ys, states = _wkv_chunk([tuple(ring[j, i] for i in range(6)) for j in range(n_seq)], states, levels, side)
        y_all = jnp.concatenate(ys, axis=0)
        d_all = y_all - head_sum(y_all) * inv_n
        var_all = head_sum(d_all * d_all) * inv_n
        yn_all = d_all * lax.rsqrt(var_all + GN_EPS) * gg_ref[...] + gb_ref[...]
        for j in range(n_seq):
            res = (yn_all[j * CHUNK:(j + 1) * CHUNK] + ring[j, 6]) * ring[j, 7]
            if short:
                out_ref[j] = res[0:tc, :].astype(out_ref.dtype)
            else:
                out_ref[j, pl.ds(pl.multiple_of(ci * CHUNK, CHUNK), CHUNK), :] = res.astype(out_ref.dtype)
        return states

    for job in staging_jobs(0, rings[0]):
        job()
    states = tuple(tuple(s_scr[j, p] for p in range(N_PAIRS)) for j in range(n_seq))
    if n_chunks == 1:
        states = chunk(0, states, rings[0], rings[1])
    else:
        assert n_chunks % 2 == 0

        def chunk_pair(i, states):
            states = chunk(2 * i, states, rings[0], rings[1])
            return chunk(2 * i + 1, states, rings[1], rings[0])

        if n_chunks <= RWKV_UNROLLED_CHUNKS:
            for i in range(n_chunks // 2):
                states = chunk_pair(i, states)
        else:
            states = lax.fori_loop(0, n_chunks // 2, chunk_pair, states)
    for j in range(n_seq):
        carry_scr[j] = zb_ref[j, tc - 1:tc, :]
        for p in range(N_PAIRS):
            s_scr[j, p] = states[j][p]

    @pl.when(t == pl.num_programs(1) - 1)
    def _():
        for j in range(n_seq):
            sh_out_ref[j] = zb_ref[j, tc - 1:tc, :]
            for p in range(N_PAIRS):
                for e in range(2):
                    span = slice(e * RWKV_N, (e + 1) * RWKV_N)
                    s_out_ref[j, 2 * p + e] = states[j][p][span, span]


def _rwkv(zb, w, tc, n_seq, state=None, t_valid=None):
    b, t_len, _ = zb.shape
    full = lambda a: pl.BlockSpec(a.shape, lambda bi, t: (0,) * a.ndim)
    per_seq = lambda *dims: pl.BlockSpec((n_seq,) + dims, lambda bi, t: (bi,) + (0,) * len(dims))
    consts = [w["mu_shift"], w["w0"], w["a0"], w["k_k"], w["k_a"], w["r_k"], w["gn_g"], w["gn_b"], w["wa_lora"],
              w["g_lora"]]
    in_specs = [pl.BlockSpec((n_seq, tc, SHIFT_W), lambda bi, t: (bi, t, 0))] + [full(c) for c in consts]
    args = [zb, *consts]
    if state is not None:
        in_specs += [per_seq(1, SHIFT_W), per_seq(RWKV_HEADS, RWKV_N, RWKV_N)]
        args += list(state)
    vm = lambda shape: pltpu.VMEM(shape, F32)
    assert b % n_seq == 0 and t_len % tc == 0
    assert tc % CHUNK == 0 or (tc == t_len == t_valid == 1), "whole chunks, or single-token sequences"
    scratch = [vm((n_seq, N_PAIRS, PAIR_W, PAIR_W)), vm((n_seq, 1, SHIFT_W))] + [vm((n_seq, 8, CHUNK, RWKV_W))] * 2
    return pl.pallas_call(
        functools.partial(_rwkv_body, t_valid=t_valid, fresh=state is None),
        grid=(b // n_seq, t_len // tc),
        in_specs=in_specs,
        out_specs=[pl.BlockSpec((n_seq, tc, RWKV_W), lambda bi, t: (bi, t, 0)),
                   per_seq(RWKV_HEADS, RWKV_N, RWKV_N), per_seq(1, SHIFT_W)],
        out_shape=[jax.ShapeDtypeStruct((b, t_len, RWKV_W), BF16),
                   jax.ShapeDtypeStruct((b, RWKV_HEADS, RWKV_N, RWKV_N), F32),
                   jax.ShapeDtypeStruct((b, 1, SHIFT_W), F32)],
        scratch_shapes=scratch,
        compiler_params=_params(("parallel", "arbitrary")),
        name="rwkv",
    )(*args)


def _out_proj_body(x_ref, attn_ref, rwkv_ref, gate_ref, wpa_ref, wpb_ref, wout_ref, n2_ref,
                   wup_ref, wdn_ref, nf_ref, *refs):
    rider = []
    if len(refs) > 1:
        qkv_s_ref, c0_ref, c1_ref, c2_ref, y_ref, attn_s_ref = refs
        rider = _attn_sample_jobs(qkv_s_ref, c0_ref, c1_ref, c2_ref, attn_s_ref)
    else:
        y_ref, = refs
    pa = _dg(attn_ref[...], wpa_ref[...])
    pb = _dg(rwkv_ref[...], wpb_ref[...])
    merged = gate_ref[:, 0:D_MODEL] * pa + gate_ref[:, D_MODEL:GATE_W] * pb
    x1 = x_ref[...] + _dg(merged.astype(BF16), wout_ref[...])
    hm = _rms(x1, n2_ref[...]).astype(BF16)
    acc = x1
    for c in range(0, D_FF, 1024):
        up = jnp.maximum(_dg(hm, wup_ref[:, c:c + 1024]), 0.0)
        if rider:
            rider.pop(0)()
        acc = acc + _dg((up * up).astype(BF16), wdn_ref[c:c + 1024, :])
    y_ref[...] = _rms(acc, nf_ref[...])


def _out_proj(x, attn, rwkv, gate, w, tm, rider=None):
    m = x.shape[0]
    row = lambda wd: pl.BlockSpec((tm, wd), lambda i: (i, 0))
    full = lambda a: pl.BlockSpec(a.shape, lambda i: (0,) * a.ndim)
    consts = [w["w_proj_a"], w["w_proj_b"], w["w_out"], w["norm2_g"], w["w_up"], w["w_down"], w["normf_g"]]
    in_specs = [row(D_MODEL), row(GROUP_W), row(RWKV_W), row(GATE_W)] + [full(c) for c in consts]
    args = [x, attn, rwkv, gate, *consts]
    out_specs = [row(D_MODEL)]
    out_shape = [jax.ShapeDtypeStruct((m, D_MODEL), F32)]
    if rider is not None:
        r_args, r_specs, r_out_spec, r_out_shape = _attn_sample_operands(*rider)
        assert r_out_shape.shape[0] == m // tm
        in_specs += r_specs
        args += r_args
        out_specs.append(r_out_spec)
        out_shape.append(r_out_shape)
    outs = pl.pallas_call(
        _out_proj_body,
        grid=(m // tm,),
        in_specs=in_specs,
        out_specs=out_specs,
        out_shape=out_shape,
        compiler_params=_params(("parallel",)),
        name="out_proj",
    )(*args)
    if rider is None:
        return outs[0]
    return outs[0], outs[1].reshape(-1, GROUP_W)


def _kv_rows(qkv, group, rows):
    b, s, _ = qkv.shape
    k0 = ATTN_W + group * GROUP_W
    v0 = 2 * ATTN_W + group * GROUP_W
    k = qkv[:, s - rows:, k0:k0 + GROUP_W].reshape(b, rows, HG, HEAD_DIM)
    v = qkv[:, s - rows:, v0:v0 + GROUP_W].reshape(b, rows, HG, HEAD_DIM)
    return jnp.stack([k, v], axis=2)


def _layer_weights(l, norm1_g, w_in, b_gate, mu_shift, w0, w_lora_up, a0, a_lora_up, g_lora_up, k_k, k_a,
                   r_k, gn_g, gn_b, w_proj_a, w_proj_b, w_out, norm2_g, w_up, w_down, normf_g):
    row = lambda a: a.reshape(1, -1)
    zero = jnp.zeros_like(w_lora_up[l])
    wa_lora = jnp.concatenate([jnp.concatenate([w_lora_up[l], zero], axis=1),
                               jnp.concatenate([zero, a_lora_up[l]], axis=1)], axis=0)
    return dict(
        norm1_g=row(norm1_g[l]), w_in=w_in[l].astype(BF16), b_gate=row(b_gate[l]), mu_shift=row(mu_shift[l]),
        w0=row(w0[l]), a0=row(a0[l]), k_k=row(k_k[l]), k_a=row(k_a[l]), r_k=row(r_k[l]),
        gn_g=row(gn_g[l]), gn_b=row(gn_b[l]),
        wa_lora=wa_lora.astype(BF16), g_lora=g_lora_up[l].astype(BF16),
        w_proj_a=w_proj_a[l].astype(BF16), w_proj_b=w_proj_b[l].astype(BF16), w_out=w_out[l].astype(BF16),
        norm2_g=row(norm2_g[l]), w_up=w_up[l].astype(BF16), w_down=w_down[l].astype(BF16),
        normf_g=row(normf_g))


def _layer(xp, xs, caches, s0, shift0, w):
    b, s, _ = xp.shape
    bs, t_len, _ = xs.shape
    assert t_len == 1
    xp2 = xp.reshape(b * s, D_MODEL)
    xs2 = xs.reshape(bs, D_MODEL)
    qkv_s, zb_s, gate_s = _in_proj(xs2, w["norm1_g"], w["w_in"], w["b_gate"], bs)
    qkv, zb, gate, *kv_t = _in_proj(xp2, w["norm1_g"], w["w_in"], w["b_gate"], PROMPT_TILE, seq_len=s)
    zb = zb.reshape(b, s, SHIFT_W)
    attn = _attn_prompt(qkv.reshape(b, s, QKV_W))
    rwkv, wkv_p, shift_p = _rwkv(zb, w, RWKV_TILE, RWKV_SEQS_PER_STEP)
    attn2, rwkv2 = attn.reshape(b * s, GROUP_W), rwkv.reshape(b * s, RWKV_W)
    if b * s // PROMPT_TILE == bs:
        y_p, attn_s = _out_proj(xp2, attn2, rwkv2, gate, w, PROMPT_TILE, rider=(qkv_s, caches))
    else:
        y_p = _out_proj(xp2, attn2, rwkv2, gate, w, PROMPT_TILE)
        attn_s = _attn_sample(qkv_s, caches)
    rwkv_s, wkv_s, _ = _rwkv(zb_s[:, None, :], w, 1, RWKV_SEQS_PER_STEP, state=(shift0[:, None, :], s0), t_valid=1)
    y_s = _out_proj(xs2, attn_s, rwkv_s[:, 0], gate_s, w, bs)
    kv_p = [jnp.transpose(t.reshape(b, 2, HG, HEAD_DIM, t.shape[-1]), (0, 4, 1, 2, 3)) for t in kv_t]
    kv_s = [_kv_rows(qkv_s[:, None, :], g, 1) for g in range(len(DILATIONS))]
    prompt = (y_p.reshape(b, s, D_MODEL), kv_p, wkv_p, shift_p[:, 0])
    sample = (y_s.reshape(bs, 1, D_MODEL), kv_s, wkv_s, zb_s)
    return prompt, sample


def kernel(x_prompt, x_sample, cache_kv_w128, cache_kv_w512, cache_kv_w2048, state_wkv, state_shift, norm1_g, w_in, b_gate, mu_shift, w0, w_lora_up, a0, a_lora_up, g_lora_up, k_k, k_a, r_k, gn_g, gn_b, w_proj_a, w_proj_b, w_out, norm2_g, w_up, w_down, normf_g):
    depth = w_in.shape[0]
    assert depth == 1, "the final norm is fused into the layer's output stage"
    w = _layer_weights(0, norm1_g, w_in, b_gate, mu_shift, w0, w_lora_up, a0, a_lora_up, g_lora_up, k_k, k_a,
                       r_k, gn_g, gn_b, w_proj_a, w_proj_b, w_out, norm2_g, w_up, w_down, normf_g)
    (y_p, kv_p, wkv_p, shift_p), (y_s, kv_s, wkv_s, shift_s) = _layer(
        x_prompt, x_sample, (cache_kv_w128[0], cache_kv_w512[0], cache_kv_w2048[0]), state_wkv[0], state_shift[0], w)
    lead = lambda a: a[None]
    return (y_p, y_s, lead(kv_p[0]), lead(kv_p[1]), lead(kv_p[2]), lead(wkv_p), lead(shift_p),
            lead(kv_s[0]), lead(kv_s[1]), lead(kv_s[2]), lead(wkv_s), lead(shift_s))
```

```python
import functools

import numpy as np
import jax
import jax.numpy as jnp
from jax import lax
from jax.experimental import pallas as pl
from jax.experimental.pallas import tpu as pltpu

F32 = jnp.float32
BF16 = jnp.bfloat16

D_MODEL = 1024
HEAD_DIM = 64
HG = 4
DILATIONS = (1, 4, 16)
BAND = 128
N_ATTN_HEADS = HG * len(DILATIONS)
GROUP_W = HG * HEAD_DIM
LANE_HALVES = GROUP_W // 128
ATTN_W = N_ATTN_HEADS * HEAD_DIM
QKV_W = 3 * ATTN_W
RWKV_N = 64
RWKV_W = 512
RWKV_HEADS = RWKV_W // RWKV_N
PAIR_W = 2 * RWKV_N
N_PAIRS = RWKV_HEADS // 2
LORA_W = 256
SHIFT_W = 3 * RWKV_W + LORA_W
GATE_W = 2 * D_MODEL
D_FF = 4 * D_MODEL
NORM_EPS = 1e-6
GN_EPS = 64e-5
NEG_INF = -1e30
ATTN_SCALE = HEAD_DIM ** -0.5
CHUNK = 64
RWKV_SEQS_PER_STEP = 4
RWKV_TILE = 256
RWKV_UNROLLED_CHUNKS = 4
PROMPT_TILE = 512
DECAY_SCALE = float(np.exp(-0.5))
SLOPES = [float(s) for s in np.exp2(-8.0 * np.arange(1, N_ATTN_HEADS + 1, dtype=np.float32) / N_ATTN_HEADS)]

V7X_VMEM_LIMIT = 56 * 1024 * 1024

NN = (((1,), (0,)), ((), ()))
NT = (((1,), (1,)), ((), ()))
TN = (((0,), (0,)), ((), ()))


def _dg(a, b, dims=NN):
    return lax.dot_general(a, b, dims, preferred_element_type=F32)


def _split2(a):
    hi = a.astype(BF16)
    lo = (a - hi.astype(F32)).astype(BF16)
    return hi, lo


def _sigmoid(x):
    return 0.5 * jnp.tanh(0.5 * x) + 0.5


def _rms(x, g):
    return x * lax.rsqrt(jnp.mean(x * x, axis=-1, keepdims=True) + NORM_EPS) * g


def _params(sem):
    return pltpu.CompilerParams(dimension_semantics=sem, vmem_limit_bytes=V7X_VMEM_LIMIT)


def _kv_tail_plan(seq_len, tm):
    plan = []
    for dil in DILATIONS:
        rows = min(BAND * dil, seq_len)
        width = min(rows, tm)
        plan.append((rows, width, (seq_len - rows) // tm))
    return plan


def _in_proj_body(x_ref, g_ref, w_ref, bg_ref, qkv_ref, zb_ref, gate_ref, *kv_refs, seq_len):
    tm = x_ref.shape[0]

    def write_tail(g, kv_ref, width):
        for sec in (1, 2):
            c0 = sec * ATTN_W + g * GROUP_W
            kv_ref[sec - 1] = qkv_ref[tm - width:tm, c0:c0 + GROUP_W].T

    plan = _kv_tail_plan(seq_len, tm) if kv_refs else []
    h = _rms(x_ref[...], g_ref[...]).astype(BF16)
    for c in range(0, GATE_W, 1024):
        zg = _dg(h, w_ref[:, QKV_W + SHIFT_W + c:QKV_W + SHIFT_W + c + 1024])
        gate_ref[:, c:c + 1024] = _sigmoid(zg + bg_ref[:, c:c + 1024])
    for c in range(0, QKV_W, 768):
        qkv_ref[:, c:c + 768] = _dg(h, w_ref[:, c:c + 768])
    for g, (kv_ref, (_, width, first)) in enumerate(zip(kv_refs, plan)):
        if first == 0:
            write_tail(g, kv_ref, width)
    for c in range(0, SHIFT_W, 896):
        zb_ref[:, c:c + 896] = _dg(h, w_ref[:, QKV_W + c:QKV_W + c + 896])
    for g, (kv_ref, (_, width, first)) in enumerate(zip(kv_refs, plan)):
        if first > 0:
            tile = pl.program_id(0) % (seq_len // tm)
            pl.when(tile >= first)(functools.partial(write_tail, g, kv_ref, width))


def _in_proj(x, norm_g, w_in_bf16, b_gate, tm, seq_len=None):
    m = x.shape[0]
    row = lambda w: pl.BlockSpec((tm, w), lambda i: (i, 0))
    full = lambda a: pl.BlockSpec(a.shape, lambda i: (0,) * a.ndim)
    out_specs = [row(QKV_W), row(SHIFT_W), row(GATE_W)]
    out_shape = [jax.ShapeDtypeStruct((m, QKV_W), F32),
                 jax.ShapeDtypeStruct((m, SHIFT_W), F32),
                 jax.ShapeDtypeStruct((m, GATE_W), F32)]
    if seq_len is not None:
        tiles = seq_len // tm
        for rows, width, first in _kv_tail_plan(seq_len, tm):
            out_specs.append(pl.BlockSpec(
                (None, 2, GROUP_W, width),
                lambda i, first=first: (i // tiles, 0, 0, jnp.maximum(i % tiles - first, 0))))
            out_shape.append(jax.ShapeDtypeStruct((m // seq_len, 2, GROUP_W, rows), F32))
    return pl.pallas_call(
        functools.partial(_in_proj_body, seq_len=seq_len),
        grid=(m // tm,),
        in_specs=[row(D_MODEL), full(norm_g), full(w_in_bf16), full(b_gate)],
        out_specs=out_specs,
        out_shape=out_shape,
        compiler_params=_params(("arbitrary",)),
        name="in_proj",
    )(x, norm_g, w_in_bf16, b_gate)


def _residue_rows(r, count, dil):
    return pl.ds(r, count) if dil == 1 else pl.ds(r, count, stride=dil)


def _attn_group_blocks(q_ref, k_ref, v_ref, o_scr, lse_scr, qs, ks, vs, os_, ls, bias_scr, tmp, group):
    s_len = qs.shape[0]
    dil = DILATIONS[group]
    l_res = s_len // dil
    nb = l_res // BAND
    inner = 4 if dil > 4 else dil
    outer = dil // inner
    l_in = s_len // inner
    assert dil in (1, inner, inner * outer) and outer <= 4

    def gather(halves, r):
        if outer == 1:
            return jnp.concatenate([h[_residue_rows(r, l_res, dil), :] for h in halves], axis=1)
        c, q4 = r % inner, r // inner
        return jnp.concatenate([tmp[i, pl.ds(c * l_in + q4, l_res, stride=outer), :]
                                for i in range(LANE_HALVES)], axis=1)

    head = lax.broadcasted_iota(jnp.int32, (1, GROUP_W), 1) // HEAD_DIM
    hm = [head == h for h in range(HG)]
    for name, halves in (("q", q_ref), ("k", k_ref), ("v", v_ref)):
        if outer > 1:
            for i, h in enumerate(halves):
                for c in range(inner):
                    tmp[i, pl.ds(c * l_in, l_in), :] = h[_residue_rows(c, l_in, inner), :]
        for r in range(dil):
            dst = pl.ds(r * l_res, l_res)
            x = gather(halves, r)
            if name == "q":
                qs[dst, :] = (x * ATTN_SCALE).astype(BF16)
            elif name == "k":
                ks[dst, :] = x.astype(BF16)
            else:
                vs[dst, :] = x.astype(BF16)
    has_prev = nb > 1
    nk = 2 * BAND if has_prev else BAND
    qi = lax.broadcasted_iota(jnp.int32, (BAND, nk), 0)
    kj = lax.broadcasted_iota(jnp.int32, (BAND, nk), 1)
    delta = (nk - BAND) + qi - kj
    band = (delta >= 0) & (delta <= BAND)
    dist = (delta * dil).astype(F32)
    for h in range(HG):
        alibi = -SLOPES[group * HG + h] * dist
        bias_scr[h, :, 0:nk] = jnp.where(band, alibi, NEG_INF)
        bias_scr[HG + h, :, 0:nk] = jnp.where(band & (kj >= nk - BAND), alibi, NEG_INF)
    blocks_per_iter = 16

    def block_pair(it, carry):
        cur, q, k2, v2, first = [], [], [], [], []
        for u in range(blocks_per_iter):
            idx = it * blocks_per_iter + u
            cur.append(pl.ds(pl.multiple_of(idx * BAND, BAND), BAND))
            q.append(qs[cur[u], :])
            k2.append(ks[cur[u], :])
            v2.append(vs[cur[u], :])
            first.append(jnp.where(idx % nb == 0, HG, 0))
            if has_prev:
                prev = pl.ds(pl.multiple_of(jnp.maximum(idx - 1, 0) * BAND, BAND), BAND)
                k2[u] = jnp.concatenate([ks[prev, :], k2[u]], axis=0)
                v2[u] = jnp.concatenate([vs[prev, :], v2[u]], axis=0)
        chains = [(u, h) for u in range(blocks_per_iter) for h in range(HG)]
        s = [_dg(jnp.where(hm[h], q[u], jnp.zeros_like(q[u])), k2[u], NT) + bias_scr[first[u] + h, :, 0:nk]
             for u, h in chains]
        m = [jnp.max(x, axis=-1, keepdims=True) for x in s]
        p = [jnp.exp(x - mx) for x, mx in zip(s, m)]
        l = [jnp.sum(x, axis=-1, keepdims=True) for x in p]
        pv = [_dg(p[i].astype(BF16), v2[u]) for i, (u, h) in enumerate(chains)]
        for u in range(blocks_per_iter):
            o_acc = jnp.zeros((BAND, GROUP_W), F32)
            lse_acc = jnp.zeros((BAND, GROUP_W), F32)
            for h in range(HG):
                i = u * HG + h
                o_acc = jnp.where(hm[h], pv[i] / l[i], o_acc)
                lse_acc = jnp.where(hm[h], m[i] + jnp.log(l[i]), lse_acc)
            if dil == 1:
                for half in range(LANE_HALVES):
                    cols = slice(half * 128, (half + 1) * 128)
                    o_scr[group * LANE_HALVES + half, cur[u], :] = o_acc[:, cols]
                    lse_scr[group * LANE_HALVES + half, cur[u], :] = lse_acc[:, cols]
            else:
                os_[cur[u], :] = o_acc
                ls[cur[u], :] = lse_acc
        return carry

    assert (dil * nb) % blocks_per_iter == 0
    lax.fori_loop(0, dil * nb // blocks_per_iter, block_pair, 0)
    for res, nat in ((os_, o_scr), (ls, lse_scr)) if dil > 1 else ():
        for half in range(LANE_HALVES):
            cols = slice(half * 128, (half + 1) * 128)
            slab = group * LANE_HALVES + half
            for r in range(dil):
                src = pl.ds(r * l_res, l_res)
                if outer == 1:
                    nat[slab, _residue_rows(r, l_res, dil), :] = res[src, cols]
                else:
                    tmp[half, pl.ds((r % inner) * l_in + r // inner, l_res, stride=outer), :] = res[src, cols]
            if outer > 1:
                for c in range(inner):
                    nat[slab, _residue_rows(c, l_in, inner), :] = tmp[half, pl.ds(c * l_in, l_in), :]


def _attn_prompt_body(q0_ref, q1_ref, k0_ref, k1_ref, v0_ref, v1_ref, o_ref, o_scr, lse_scr, qs, ks, vs, os_, ls, bias_scr, tmp):
    q_ref, k_ref, v_ref = (q0_ref, q1_ref), (k0_ref, k1_ref), (v0_ref, v1_ref)
    gid = pl.program_id(1)
    n_groups = len(DILATIONS)
    for g in range(n_groups):
        @pl.when(gid == g)
        def _(g=g):
            _attn_group_blocks(q_ref, k_ref, v_ref, o_scr, lse_scr, qs, ks, vs, os_, ls, bias_scr, tmp, g)

    @pl.when(gid == n_groups - 1)
    def _():
        def merge(i, carry):
            rows = pl.ds(pl.multiple_of(i * BAND, BAND), BAND)
            both = lambda ref, g: jnp.concatenate(
                [ref[g * LANE_HALVES + half, rows, :] for half in range(LANE_HALVES)], axis=1)
            lse = [both(lse_scr, g) for g in range(n_groups)]
            m = functools.reduce(jnp.maximum, lse)
            wts = [jnp.exp(x - m) for x in lse]
            num = sum(wts[g] * both(o_scr, g) for g in range(n_groups))
            o_ref[rows, :] = (num / sum(wts)).astype(o_ref.dtype)
            return carry

        lax.fori_loop(0, o_ref.shape[0] // BAND, merge, 0)


def _attn_prompt(qkv):
    b, s, _ = qkv.shape
    n_groups = len(DILATIONS)
    assert s % (DILATIONS[-1] * BAND) == 0
    sec = ATTN_W // 128

    def col(section, half):
        return pl.BlockSpec((None, s, 128), lambda bi, g: (bi, 0, section * sec + g * LANE_HALVES + half))

    return pl.pallas_call(
        _attn_prompt_body,
        grid=(b, n_groups),
        in_specs=[col(section, half) for section in range(3) for half in range(LANE_HALVES)],
        out_specs=pl.BlockSpec((None, s, GROUP_W), lambda bi, g: (bi, 0, 0)),
        out_shape=jax.ShapeDtypeStruct((b, s, GROUP_W), BF16),
        scratch_shapes=([pltpu.VMEM((n_groups * LANE_HALVES, s, 128), F32)] * 2
                        + [pltpu.VMEM((s, GROUP_W), BF16)] * 3 + [pltpu.VMEM((s, GROUP_W), F32)] * 2
                        + [pltpu.VMEM((2 * HG, BAND, 2 * BAND), F32), pltpu.VMEM((LANE_HALVES, s, 128), F32)]),
        compiler_params=_params(("parallel", "arbitrary")),
        name="attn_prompt",
    )(*([qkv] * (3 * LANE_HALVES)))


def _attn_sample_body(qkv_ref, c0_ref, c1_ref, c2_ref, o_ref):
    for job in _attn_sample_jobs(qkv_ref, c0_ref, c1_ref, c2_ref, o_ref):
        job()


def _attn_sample_jobs(qkv_ref, c0_ref, c1_ref, c2_ref, o_ref):
    rows = 8
    row = lax.broadcasted_iota(jnp.int32, (rows, GROUP_W), 0)
    own = lax.broadcasted_iota(jnp.int32, (rows, GROUP_W), 1) // HEAD_DIM == row
    hrow = lax.broadcasted_iota(jnp.int32, (HG, 1), 0)
    lane_head = lax.broadcasted_iota(jnp.int32, (1, GROUP_W), 1) // HEAD_DIM
    c_refs = (c0_ref, c1_ref, c2_ref)
    groups = range(len(DILATIONS))
    st = [dict() for _ in groups]

    def to_column(x_row):
        return jnp.sum(jnp.where(own, x_row, 0.0).T, axis=1, keepdims=True)

    def per_head(col):
        out = jnp.zeros((1, GROUP_W), F32)
        for h in range(HG):
            out = jnp.where(lane_head == h, col[h:h + 1, :], out)
        return out

    def scores():
        for g in groups:
            lb = c_refs[g].shape[2]
            col = lambda sec: qkv_ref[:, sec * ATTN_W + g * GROUP_W:sec * ATTN_W + (g + 1) * GROUP_W]
            q, kn, vn = col(0), col(1), col(2)
            prod = c_refs[g][0] * to_column(q)
            s_b = jnp.sum(prod.reshape(HG, HEAD_DIM, lb), axis=1) * ATTN_SCALE
            s_n = jnp.sum(jnp.where(own, q * kn, 0.0), axis=-1, keepdims=True)[0:HG] * ATTN_SCALE
            st[g].update(vn=vn, s_b=s_b, s_n=s_n)

    def values():
        for g in groups:
            dil = DILATIONS[g]
            lb = c_refs[g].shape[2]
            slope = jnp.zeros((HG, 1), F32)
            for h in range(HG):
                slope = jnp.where(hrow == h, SLOPES[g * HG + h], slope)
            t = lax.broadcasted_iota(jnp.int32, (HG, lb), 1)
            s_b = jnp.where(t % dil == 0, st[g]["s_b"] - slope * (lb - t).astype(F32), NEG_INF)
            m = jnp.maximum(jnp.max(s_b, axis=-1, keepdims=True), st[g]["s_n"])
            p_b = jnp.exp(s_b - m)
            p_n = jnp.exp(st[g]["s_n"] - m)
            p_rows = jnp.broadcast_to(p_b[:, None, :], (HG, HEAD_DIM, lb)).reshape(GROUP_W, lb)
            acc = jnp.sum(p_rows * c_refs[g][1], axis=1, keepdims=True)
            st[g].update(m=m, p_n=p_n, l=jnp.sum(p_b, axis=-1, keepdims=True) + p_n, acc=acc)

    def merge():
        m_all = functools.reduce(jnp.maximum, [s["m"] for s in st])
        sc = [jnp.exp(s["m"] - m_all) for s in st]
        den = sum(c * s["l"] for c, s in zip(sc, st))
        out = jnp.zeros((1, GROUP_W), F32)
        for c, s in zip(sc, st):
            acc_row = jnp.broadcast_to(s["acc"], (GROUP_W, rows)).T[0:1, :]
            out = out + per_head(c / den) * (acc_row + per_head(s["p_n"]) * s["vn"])
        o_ref[...] = out.astype(o_ref.dtype)

    return [scores, values, merge]


def _attn_sample_operands(qkv, caches):
    b = qkv.shape[0]
    args = [qkv.reshape(b, 1, QKV_W)]
    specs = [pl.BlockSpec((None, 1, QKV_W), lambda i: (i, 0, 0))]
    for g, c in enumerate(caches):
        lb = c.shape[1]
        assert lb == BAND * DILATIONS[g]
        args.append(jnp.transpose(c, (0, 2, 3, 4, 1)).reshape(b, 2, GROUP_W, lb))
        specs.append(pl.BlockSpec((None, 2, GROUP_W, lb), lambda i: (i, 0, 0, 0)))
    out_spec = pl.BlockSpec((None, 1, GROUP_W), lambda i: (i, 0, 0))
    return args, specs, out_spec, jax.ShapeDtypeStruct((b, 1, GROUP_W), BF16)


def _attn_sample(qkv, caches):
    args, specs, out_spec, out_shape = _attn_sample_operands(qkv, caches)
    out = pl.pallas_call(
        _attn_sample_body,
        grid=(qkv.shape[0],),
        in_specs=specs,
        out_specs=out_spec,
        out_shape=out_shape,
        compiler_params=_params(("parallel",)),
        name="attn_sample",
    )(*args)
    return out.reshape(-1, GROUP_W)


WKV_PASSES = dict(g=1, s0=1, x=1, inv=1, y=1, s1=1)


def _mm(a, b, dims, passes):
    if passes == 1:
        return _dg(a.astype(BF16), b.astype(BF16), dims)
    ah, al = _split2(a)
    if passes == 2:
        bh = b.astype(BF16)
        return _dg(ah, bh, dims) + _dg(al, bh, dims)
    bh, bl = _split2(b)
    return _dg(ah, bh, dims) + _dg(al, bh, dims) + _dg(ah, bl, dims)


def _wkv_levels(live_rows):
    return int(np.ceil(np.log2(live_rows))) if live_rows > 1 else 0


def _wkv_chunk(operands, states, levels, side=()):
    side = list(side)

    def run_side():
        if side:
            side.pop(0)()
    c = CHUNK
    n2 = 2 * c
    n_seq = len(operands)
    first = lax.broadcasted_iota(jnp.int32, (1, PAIR_W), 1) < RWKV_N

    def stack(x, p):
        x = x[:, p * PAIR_W:(p + 1) * PAIR_W]
        return jnp.concatenate([jnp.where(first, x, 0.0), jnp.where(first, 0.0, x)], axis=0)

    bi = lax.broadcasted_iota(jnp.int32, (n_seq * c, n_seq * c), 0)
    bj = lax.broadcasted_iota(jnp.int32, (n_seq * c, n_seq * c), 1)
    tri_all = ((bi >= bj) & (bi // c == bj // c)).astype(BF16)
    lc_all = _dot_exact_rhs_left(tri_all, jnp.concatenate([ops[3] for ops in operands], axis=0))
    ar, bk, v_s, s0, w_end = [], [], [], [], []
    for j, ((r, kp, v, logw, av, bv), seq_states) in enumerate(zip(operands, states)):
        lc = lc_all[j * c:(j + 1) * c]
        e_in = jnp.exp(lc)
        e_neg = jnp.exp(-lc)
        a_t = av * jnp.exp(lc - logw)
        r_t = r * e_in
        b_t = bv * e_neg
        k_t = kp * e_neg
        for p in range(N_PAIRS):
            ar.append(jnp.concatenate([stack(a_t, p), stack(r_t, p)], axis=0))
            bk.append(jnp.concatenate([stack(b_t, p), stack(k_t, p)], axis=0))
            v_s.append(stack(v, p))
            s0.append(seq_states[p])
            w_end.append(e_in[c - 1:c, p * PAIR_W:(p + 1) * PAIR_W])
    chains = range(len(ar))
    g = [_mm(ar[i], bk[i], NT, WKV_PASSES["g"]) for i in chains]
    ar_s0 = [_mm(ar[i], s0[i], NT, WKV_PASSES["s0"]) for i in chains]
    run_side()
    ri = lax.broadcasted_iota(jnp.int32, (n2, n2), 0)
    ci = lax.broadcasted_iota(jnp.int32, (n2, n2), 1)
    strict = ri > ci
    incl = ri >= ci
    n_ab = [jnp.where(strict, g[i][0:n2, 0:n2], 0.0) for i in chains]
    n_ak = [jnp.where(strict, g[i][0:n2, n2:2 * n2], 0.0) for i in chains]
    m_r = [jnp.where(jnp.concatenate([incl, incl], axis=1), g[i][n2:2 * n2, :], 0.0) for i in chains]
    z = [ar_s0[i][0:n2] + _mm(n_ak[i], v_s[i], NN, WKV_PASSES["x"]) for i in chains]
    pw = n_ab
    for lvl in range(levels):
        if lvl < levels - 1:
            pz = [_mm(pw[i], jnp.concatenate([pw[i], z[i]], axis=1), NN, WKV_PASSES["inv"]) for i in chains]
            pw = [pz[i][:, 0:n2] for i in chains]
            z = [z[i] + pz[i][:, n2:2 * n2] for i in chains]
        else:
            z = [z[i] + _mm(pw[i], z[i], NN, WKV_PASSES["inv"]) for i in chains]
        if lvl % 2 == 1:
            run_side()
    while side:
        run_side()
    uv = [jnp.concatenate([z[i], v_s[i]], axis=0) for i in chains]
    y_s = [ar_s0[i][n2:2 * n2] + _mm(m_r[i], uv[i], NN, WKV_PASSES["y"]) for i in chains]
    s1 = [(s0[i] + _mm(uv[i], bk[i], TN, WKV_PASSES["s1"])) * w_end[i] for i in chains]
    ys = [jnp.concatenate([y_s[j * N_PAIRS + p][0:c] + y_s[j * N_PAIRS + p][c:n2] for p in range(N_PAIRS)], axis=1)
          for j in range(n_seq)]
    new_states = tuple(tuple(s1[j * N_PAIRS:(j + 1) * N_PAIRS]) for j in range(n_seq))
    return ys, new_states


def _dot_exact_rhs_left(lhs_bf16, b):
    out = None
    rem = b
    for _ in range(3):
        part = rem.astype(BF16)
        term = _dg(lhs_bf16, part)
        out = term if out is None else out + term
        rem = rem - part.astype(F32)
    return out


def _rwkv_body(zb_ref, mu_ref, w0_ref, a0_ref, kk_ref, ka_ref, rk_ref, gg_ref, gb_ref, wwa_ref, wg_ref, *refs,
               t_valid, fresh):
    if fresh:
        sh_ref = s0_ref = None
    else:
        sh_ref, s0_ref = refs[:2]
        refs = refs[2:]
    out_ref, s_out_ref, sh_out_ref, s_scr, carry_scr, ring0_scr, ring1_scr = refs
    t = pl.program_id(1)
    n_seq, tc, _ = zb_ref.shape

    @pl.when(t == 0)
    def _():
        if fresh:
            s_scr[...] = jnp.zeros(s_scr.shape, F32)
            carry_scr[...] = jnp.zeros(carry_scr.shape, F32)
        else:
            carry_scr[...] = sh_ref[...]
            zero = jnp.zeros((RWKV_N, RWKV_N), F32)
            for j in range(n_seq):
                for p in range(N_PAIRS):
                    top = jnp.concatenate([s0_ref[j, 2 * p], zero], axis=1)
                    bottom = jnp.concatenate([zero, s0_ref[j, 2 * p + 1]], axis=1)
                    s_scr[j, p] = jnp.concatenate([top, bottom], axis=0)

    lr = lax.broadcasted_iota(jnp.int32, (2 * PAIR_W, 2 * PAIR_W), 0)
    lc = lax.broadcasted_iota(jnp.int32, (2 * PAIR_W, 2 * PAIR_W), 1)
    seg = (lr // RWKV_N == lc // RWKV_N).astype(BF16)
    half = lax.broadcasted_iota(jnp.int32, (1, 128), 1) < 64
    row = lax.broadcasted_iota(jnp.int32, (CHUNK, 1), 0)
    inv_n = 1.0 / RWKV_N
    levels = _wkv_levels(CHUNK if t_valid is None else min(t_valid, CHUNK))
    short = tc < CHUNK

    def head_sum(x):
        wide = 2 * PAIR_W
        return jnp.concatenate(
            [_dg(x[:, c:c + wide].astype(BF16), seg) for c in range(0, RWKV_W, wide)], axis=1)

    n_chunks = max(tc // CHUNK, 1)
    rings = (ring0_scr, ring1_scr)

    def staging_jobs(ci, ring):
        rows = pl.ds(pl.multiple_of(ci * CHUNK, CHUNK), CHUNK)
        st = [dict() for _ in range(n_seq)]
        seq_rows = lambda x, j: x[j * CHUNK:(j + 1) * CHUNK]

        def mix_and_lora():
            x_wa, x_g = [], []
            for j in range(n_seq):
                if short:
                    z = jnp.where(row == 0, zb_ref[j], 0.0)
                    before = carry_scr[j]
                else:
                    z = zb_ref[j, rows, :]
                    before = zb_ref[j, pl.ds(jnp.maximum(ci * CHUNK - 1, 0), 1), :]
                    before = jnp.where(ci == 0, carry_scr[j], before)
                zp = jnp.where(row == 0, before, pltpu.roll(z, 1, 0))
                zm = z + (zp - z) * mu_ref[...]
                z_wa = zm[:, 3 * RWKV_W:3 * RWKV_W + 128]
                x_wa.append(jnp.where(half, jnp.tanh(z_wa), z_wa).astype(BF16))
                x_g.append(_sigmoid(zm[:, 3 * RWKV_W + 128:SHIFT_W]).astype(BF16))
                st[j].update(r=zm[:, 0:RWKV_W], k=zm[:, RWKV_W:2 * RWKV_W], v=zm[:, 2 * RWKV_W:3 * RWKV_W])
            x_wa = jnp.concatenate(x_wa, axis=0)
            u = _dg(x_wa, wwa_ref[:, 0:RWKV_W])
            la = _dg(x_wa, wwa_ref[:, RWKV_W:2 * RWKV_W])
            gate = _dg(jnp.concatenate(x_g, axis=0), wg_ref[...])
            for j in range(n_seq):
                st[j].update(u=seq_rows(u, j), la=seq_rows(la, j), gate=seq_rows(gate, j))

        def head_sums():
            for j in range(n_seq):
                s = st[j]
                a = _sigmoid(a0_ref[...] + s["la"])
                kk = s["k"] * kk_ref[...]
                kp = s["k"] * (1.0 + (a - 1.0) * ka_ref[...])
                s.update(a=a, kk=kk, kp=kp)
            kk2 = head_sum(jnp.concatenate([s["kk"] * s["kk"] for s in st], axis=0))
            rk = head_sum(jnp.concatenate([s["r"] * s["kp"] * rk_ref[...] for s in st], axis=0))
            for j in range(n_seq):
                st[j].update(kk2=seq_rows(kk2, j), rk=seq_rows(rk, j))

        def finish():
            for j in range(n_seq):
                s = st[j]
                logw = -DECAY_SCALE * _sigmoid(w0_ref[...] + s["u"])
                kk = s["kk"] * lax.rsqrt(jnp.maximum(s["kk2"], 1e-24))
                ops = (s["r"], s["kp"], s["v"], logw, -kk, kk * s["a"], s["rk"] * s["v"])
                if t_valid is not None:
                    live = row + ci * CHUNK < t_valid
                    ops = tuple(jnp.where(live, x, 0.0) for x in ops)
                for i, x in enumerate(ops + (s["gate"],)):
                    ring[j, i] = x

        return mix_and_lora, head_sums, finish

    def chunk(ci, states, ring, other):
        side = staging_jobs(jnp.minimum(ci + 1, n_chunks - 1), other) if n_chunks > 1 else ()
        ys, states = _wkv_chunk([tuple(ring[j, i] for i in range(6)) for j in range(n_seq)], states, levels, side)
        y_all = jnp.concatenate(ys, axis=0)
        d_all = y_all - head_sum(y_all) * inv_n
        var_all = head_sum(d_all * d_all) * inv_n
        yn_all = d_all * lax.rsqrt(var_all + GN_EPS) * gg_ref[...] + gb_ref[...]
        for j in range(n_seq):
            res = (yn_all[j * CHUNK:(j + 1) * CHUNK] + ring[j, 6]) * ring[j, 7]
            if short:
                out_ref[j] = res[0:tc, :].astype(out_ref.dtype)
            else:
                out_ref[j, pl.ds(pl.multiple_of(ci * CHUNK, CHUNK), CHUNK), :] = res.astype(out_ref.dtype)
        return states

    for job in staging_jobs(0, rings[0]):
        job()
    states = tuple(tuple(s_scr[j, p] for p in range(N_PAIRS)) for j in range(n_seq))
    if n_chunks == 1:
        states = chunk(0, states, rings[0], rings[1])
    else:
        assert n_chunks % 2 == 0

        def chunk_pair(i, states):
            states = chunk(2 * i, states, rings[0], rings[1])
            return chunk(2 * i + 1, states, rings[1], rings[0])

        if n_chunks <= RWKV_UNROLLED_CHUNKS:
            for i in range(n_chunks // 2):
                states = chunk_pair(i, states)
        else:
            states = lax.fori_loop(0, n_chunks // 2, chunk_pair, states)
    for j in range(n_seq):
        carry_scr[j] = zb_ref[j, tc - 1:tc, :]
        for p in range(N_PAIRS):
            s_scr[j, p] = states[j][p]

    @pl.when(t == pl.num_programs(1) - 1)
    def _():
        for j in range(n_seq):
            sh_out_ref[j] = zb_ref[j, tc - 1:tc, :]
            for p in range(N_PAIRS):
                for e in range(2):
                    span = slice(e * RWKV_N, (e + 1) * RWKV_N)
                    s_out_ref[j, 2 * p + e] = states[j][p][span, span]


def _rwkv(zb, w, tc, n_seq, state=None, t_valid=None):
    b, t_len, _ = zb.shape
    full = lambda a: pl.BlockSpec(a.shape, lambda bi, t: (0,) * a.ndim)
    per_seq = lambda *dims: pl.BlockSpec((n_seq,) + dims, lambda bi, t: (bi,) + (0,) * len(dims))
    consts = [w["mu_shift"], w["w0"], w["a0"], w["k_k"], w["k_a"], w["r_k"], w["gn_g"], w["gn_b"], w["wa_lora"],
              w["g_lora"]]
    in_specs = [pl.BlockSpec((n_seq, tc, SHIFT_W), lambda bi, t: (bi, t, 0))] + [full(c) for c in consts]
    args = [zb, *consts]
    if state is not None:
        in_specs += [per_seq(1, SHIFT_W), per_seq(RWKV_HEADS, RWKV_N, RWKV_N)]
        args += list(state)
    vm = lambda shape: pltpu.VMEM(shape, F32)
    assert b % n_seq == 0 and t_len % tc == 0
    assert tc % CHUNK == 0 or (tc == t_len == t_valid == 1), "whole chunks, or single-token sequences"
    scratch = [vm((n_seq, N_PAIRS, PAIR_W, PAIR_W)), vm((n_seq, 1, SHIFT_W))] + [vm((n_seq, 8, CHUNK, RWKV_W))] * 2
    return pl.pallas_call(
        functools.partial(_rwkv_body, t_valid=t_valid, fresh=state is None),
        grid=(b // n_seq, t_len // tc),
        in_specs=in_specs,
        out_specs=[pl.BlockSpec((n_seq, tc, RWKV_W), lambda bi, t: (bi, t, 0)),
                   per_seq(RWKV_HEADS, RWKV_N, RWKV_N), per_seq(1, SHIFT_W)],
        out_shape=[jax.ShapeDtypeStruct((b, t_len, RWKV_W), BF16),
                   jax.ShapeDtypeStruct((b, RWKV_HEADS, RWKV_N, RWKV_N), F32),
                   jax.ShapeDtypeStruct((b, 1, SHIFT_W), F32)],
        scratch_shapes=scratch,
        compiler_params=_params(("parallel", "arbitrary")),
        name="rwkv",
    )(*args)


def _out_proj_body(x_ref, attn_ref, rwkv_ref, gate_ref, wpa_ref, wpb_ref, wout_ref, n2_ref,
                   wup_ref, wdn_ref, nf_ref, *refs):
    rider = []
    if len(refs) > 1:
        qkv_s_ref, c0_ref, c1_ref, c2_ref, y_ref, attn_s_ref = refs
        rider = _attn_sample_jobs(qkv_s_ref, c0_ref, c1_ref, c2_ref, attn_s_ref)
    else:
        y_ref, = refs
    pa = _dg(attn_ref[...], wpa_ref[...])
    pb = _dg(rwkv_ref[...], wpb_ref[...])
    merged = gate_ref[:, 0:D_MODEL] * pa + gate_ref[:, D_MODEL:GATE_W] * pb
    x1 = x_ref[...] + _dg(merged.astype(BF16), wout_ref[...])
    hm = _rms(x1, n2_ref[...]).astype(BF16)
    acc = x1
    for c in range(0, D_FF, 1024):
        up = jnp.maximum(_dg(hm, wup_ref[:, c:c + 1024]), 0.0)
        if rider:
            rider.pop(0)()
        acc = acc + _dg((up * up).astype(BF16), wdn_ref[c:c + 1024, :])
    y_ref[...] = _rms(acc, nf_ref[...])


def _out_proj(x, attn, rwkv, gate, w, tm, rider=None):
    m = x.shape[0]
    row = lambda wd: pl.BlockSpec((tm, wd), lambda i: (i, 0))
    full = lambda a: pl.BlockSpec(a.shape, lambda i: (0,) * a.ndim)
    consts = [w["w_proj_a"], w["w_proj_b"], w["w_out"], w["norm2_g"], w["w_up"], w["w_down"], w["normf_g"]]
    in_specs = [row(D_MODEL), row(GROUP_W), row(RWKV_W), row(GATE_W)] + [full(c) for c in consts]
    args = [x, attn, rwkv, gate, *consts]
    out_specs = [row(D_MODEL)]
    out_shape = [jax.ShapeDtypeStruct((m, D_MODEL), F32)]
    if rider is not None:
        r_args, r_specs, r_out_spec, r_out_shape = _attn_sample_operands(*rider)
        assert r_out_shape.shape[0] == m // tm
        in_specs += r_specs
        args += r_args
        out_specs.append(r_out_spec)
        out_shape.append(r_out_shape)
    outs = pl.pallas_call(
        _out_proj_body,
        grid=(m // tm,),
        in_specs=in_specs,
        out_specs=out_specs,
        out_shape=out_shape,
        compiler_params=_params(("parallel",)),
        name="out_proj",
    )(*args)
    if rider is None:
        return outs[0]
    return outs[0], outs[1].reshape(-1, GROUP_W)


def _kv_rows(qkv, group, rows):
    b, s, _ = qkv.shape
    k0 = ATTN_W + group * GROUP_W
    v0 = 2 * ATTN_W + group * GROUP_W
    k = qkv[:, s - rows:, k0:k0 + GROUP_W].reshape(b, rows, HG, HEAD_DIM)
    v = qkv[:, s - rows:, v0:v0 + GROUP_W].reshape(b, rows, HG, HEAD_DIM)
    return jnp.stack([k, v], axis=2)


def _layer_weights(l, norm1_g, w_in, b_gate, mu_shift, w0, w_lora_up, a0, a_lora_up, g_lora_up, k_k, k_a,
                   r_k, gn_g, gn_b, w_proj_a, w_proj_b, w_out, norm2_g, w_up, w_down, normf_g):
    row = lambda a: a.reshape(1, -1)
    zero = jnp.zeros_like(w_lora_up[l])
    wa_lora = jnp.concatenate([jnp.concatenate([w_lora_up[l], zero], axis=1),
                               jnp.concatenate([zero, a_lora_up[l]], axis=1)], axis=0)
    return dict(
        norm1_g=row(norm1_g[l]), w_in=w_in[l].astype(BF16), b_gate=row(b_gate[l]), mu_shift=row(mu_shift[l]),
        w0=row(w0[l]), a0=row(a0[l]), k_k=row(k_k[l]), k_a=row(k_a[l]), r_k=row(r_k[l]),
        gn_g=row(gn_g[l]), gn_b=row(gn_b[l]),
        wa_lora=wa_lora.astype(BF16), g_lora=g_lora_up[l].astype(BF16),
        w_proj_a=w_proj_a[l].astype(BF16), w_proj_b=w_proj_b[l].astype(BF16), w_out=w_out[l].astype(BF16),
        norm2_g=row(norm2_g[l]), w_up=w_up[l].astype(BF16), w_down=w_down[l].astype(BF16),
        normf_g=row(normf_g))


def _layer(xp, xs, caches, s0, shift0, w):
    b, s, _ = xp.shape
    bs, t_len, _ = xs.shape
    assert t_len == 1
    xp2 = xp.reshape(b * s, D_MODEL)
    xs2 = xs.reshape(bs, D_MODEL)
    qkv_s, zb_s, gate_s = _in_proj(xs2, w["norm1_g"], w["w_in"], w["b_gate"], bs)
    qkv, zb, gate, *kv_t = _in_proj(xp2, w["norm1_g"], w["w_in"], w["b_gate"], PROMPT_TILE, seq_len=s)
    zb = zb.reshape(b, s, SHIFT_W)
    attn = _attn_prompt(qkv.reshape(b, s, QKV_W))
    rwkv, wkv_p, shift_p = _rwkv(zb, w, RWKV_TILE, RWKV_SEQS_PER_STEP)
    attn2, rwkv2 = attn.reshape(b * s, GROUP_W), rwkv.reshape(b * s, RWKV_W)
    if b * s // PROMPT_TILE == bs:
        y_p, attn_s = _out_proj(xp2, attn2, rwkv2, gate, w, PROMPT_TILE, rider=(qkv_s, caches))
    else:
        y_p = _out_proj(xp2, attn2, rwkv2, gate, w, PROMPT_TILE)
        attn_s = _attn_sample(qkv_s, caches)
    rwkv_s, wkv_s, _ = _rwkv(zb_s[:, None, :], w, 1, RWKV_SEQS_PER_STEP, state=(shift0[:, None, :], s0), t_valid=1)
    y_s = _out_proj(xs2, attn_s, rwkv_s[:, 0], gate_s, w, bs)
    kv_p = [jnp.transpose(t.reshape(b, 2, HG, HEAD_DIM, t.shape[-1]), (0, 4, 1, 2, 3)) for t in kv_t]
    kv_s = [_kv_rows(qkv_s[:, None, :], g, 1) for g in range(len(DILATIONS))]
    prompt = (y_p.reshape(b, s, D_MODEL), kv_p, wkv_p, shift_p[:, 0])
    sample = (y_s.reshape(bs, 1, D_MODEL), kv_s, wkv_s, zb_s)
    return prompt, sample


def kernel(x_prompt, x_sample, cache_kv_w128, cache_kv_w512, cache_kv_w2048, state_wkv, state_shift, norm1_g, w_in, b_gate, mu_shift, w0, w_lora_up, a0, a_lora_up, g_lora_up, k_k, k_a, r_k, gn_g, gn_b, w_proj_a, w_proj_b, w_out, norm2_g, w_up, w_down, normf_g):
    depth = w_in.shape[0]
    assert depth == 1, "the final norm is fused into the layer's output stage"
    w = _layer_weights(0, norm1_g, w_in, b_gate, mu_shift, w0, w_lora_up, a0, a_lora_up, g_lora_up, k_k, k_a,
                       r_k, gn_g, gn_b, w_proj_a, w_proj_b, w_out, norm2_g, w_up, w_down, normf_g)
    (y_p, kv_p, wkv_p, shift_p), (y_s, kv_s, wkv_s, shift_s) = _layer(
        x_prompt, x_sample, (cache_kv_w128[0], cache_kv_w512[0], cache_kv_w2048[0]), state_wkv[0], state_shift[0], w)
    lead = lambda a: a[None]
    return (y_p, y_s, lead(kv_p[0]), lead(kv_p[1]), lead(kv_p[2]), lead(wkv_p), lead(shift_p),
            lead(kv_s[0]), lead(kv_s[1]), lead(kv_s[2]), lead(wkv_s), lead(shift_s))
```

```python
import functools

import numpy as np
import jax
import jax.numpy as jnp
from jax import lax
from jax.experimental import pallas as pl
from jax.experimental.pallas import tpu as pltpu

F32 = jnp.float32
BF16 = jnp.bfloat16

D_MODEL = 1024
HEAD_DIM = 64
HG = 4
DILATIONS = (1, 4, 16)
BAND = 128
N_ATTN_HEADS = HG * len(DILATIONS)
GROUP_W = HG * HEAD_DIM
LANE_HALVES = GROUP_W // 128
ATTN_W = N_ATTN_HEADS * HEAD_DIM
QKV_W = 3 * ATTN_W
RWKV_N = 64
RWKV_W = 512
RWKV_HEADS = RWKV_W // RWKV_N
PAIR_W = 2 * RWKV_N
N_PAIRS = RWKV_HEADS // 2
LORA_W = 256
SHIFT_W = 3 * RWKV_W + LORA_W
GATE_W = 2 * D_MODEL
D_FF = 4 * D_MODEL
NORM_EPS = 1e-6
GN_EPS = 64e-5
NEG_INF = -1e30
ATTN_SCALE = HEAD_DIM ** -0.5
CHUNK = 64
RWKV_SEQS_PER_STEP = 4
RWKV_TILE = 256
RWKV_UNROLLED_CHUNKS = 4
PROMPT_TILE = 512
DECAY_SCALE = float(np.exp(-0.5))
SLOPES = [float(s) for s in np.exp2(-8.0 * np.arange(1, N_ATTN_HEADS + 1, dtype=np.float32) / N_ATTN_HEADS)]

V7X_VMEM_LIMIT = 56 * 1024 * 1024

NN = (((1,), (0,)), ((), ()))
NT = (((1,), (1,)), ((), ()))
TN = (((0,), (0,)), ((), ()))


def _dg(a, b, dims=NN):
    return lax.dot_general(a, b, dims, preferred_element_type=F32)


def _split2(a):
    hi = a.astype(BF16)
    lo = (a - hi.astype(F32)).astype(BF16)
    return hi, lo


def _sigmoid(x):
    return 0.5 * jnp.tanh(0.5 * x) + 0.5


def _rms(x, g):
    return x * lax.rsqrt(jnp.mean(x * x, axis=-1, keepdims=True) + NORM_EPS) * g


def _params(sem):
    return pltpu.CompilerParams(dimension_semantics=sem, vmem_limit_bytes=V7X_VMEM_LIMIT)


def _kv_tail_plan(seq_len, tm):
    plan = []
    for dil in DILATIONS:
        rows = min(BAND * dil, seq_len)
        width = min(rows, tm)
        plan.append((rows, width, (seq_len - rows) // tm))
    return plan


def _in_proj_body(x_ref, g_ref, w_ref, bg_ref, qkv_ref, zb_ref, gate_ref, *kv_refs, seq_len):
    tm = x_ref.shape[0]

    def write_tail(g, kv_ref, width):
        for sec in (1, 2):
            c0 = sec * ATTN_W + g * GROUP_W
            kv_ref[sec - 1] = qkv_ref[tm - width:tm, c0:c0 + GROUP_W].T

    plan = _kv_tail_plan(seq_len, tm) if kv_refs else []
    h = _rms(x_ref[...], g_ref[...]).astype(BF16)
    for c in range(0, GATE_W, 1024):
        zg = _dg(h, w_ref[:, QKV_W + SHIFT_W + c:QKV_W + SHIFT_W + c + 1024])
        gate_ref[:, c:c + 1024] = _sigmoid(zg + bg_ref[:, c:c + 1024]).astype(gate_ref.dtype)
    for c in range(0, QKV_W, 768):
        qkv_ref[:, c:c + 768] = _dg(h, w_ref[:, c:c + 768])
    for g, (kv_ref, (_, width, first)) in enumerate(zip(kv_refs, plan)):
        if first == 0:
            write_tail(g, kv_ref, width)
    for c in range(0, SHIFT_W, 896):
        zb_ref[:, c:c + 896] = _dg(h, w_ref[:, QKV_W + c:QKV_W + c + 896])
    for g, (kv_ref, (_, width, first)) in enumerate(zip(kv_refs, plan)):
        if first > 0:
            tile = pl.program_id(0) % (seq_len // tm)
            pl.when(tile >= first)(functools.partial(write_tail, g, kv_ref, width))


def _in_proj(x, norm_g, w_in_bf16, b_gate, tm, seq_len=None):
    m = x.shape[0]
    row = lambda w: pl.BlockSpec((tm, w), lambda i: (i, 0))
    full = lambda a: pl.BlockSpec(a.shape, lambda i: (0,) * a.ndim)
    out_specs = [row(QKV_W), row(SHIFT_W), row(GATE_W)]
    out_shape = [jax.ShapeDtypeStruct((m, QKV_W), F32),
                 jax.ShapeDtypeStruct((m, SHIFT_W), F32),
                 jax.ShapeDtypeStruct((m, GATE_W), BF16)]
    if seq_len is not None:
        tiles = seq_len // tm
        for rows, width, first in _kv_tail_plan(seq_len, tm):
            out_specs.append(pl.BlockSpec(
                (None, 2, GROUP_W, width),
                lambda i, first=first: (i // tiles, 0, 0, jnp.maximum(i % tiles - first, 0))))
            out_shape.append(jax.ShapeDtypeStruct((m // seq_len, 2, GROUP_W, rows), F32))
    return pl.pallas_call(
        functools.partial(_in_proj_body, seq_len=seq_len),
        grid=(m // tm,),
        in_specs=[row(D_MODEL), full(norm_g), full(w_in_bf16), full(b_gate)],
        out_specs=out_specs,
        out_shape=out_shape,
        compiler_params=_params(("arbitrary",)),
        name="in_proj",
    )(x, norm_g, w_in_bf16, b_gate)


def _residue_rows(r, count, dil):
    return pl.ds(r, count) if dil == 1 else pl.ds(r, count, stride=dil)


def _attn_group_blocks(q_ref, k_ref, v_ref, o_scr, lse_scr, qs, ks, vs, os_, ls, bias_scr, tmp, group):
    s_len = qs.shape[0]
    dil = DILATIONS[group]
    l_res = s_len // dil
    nb = l_res // BAND
    inner = 4 if dil > 4 else dil
    outer = dil // inner
    l_in = s_len // inner
    assert dil in (1, inner, inner * outer) and outer <= 4

    def gather(halves, r):
        if outer == 1:
            return jnp.concatenate([h[_residue_rows(r, l_res, dil), :] for h in halves], axis=1)
        c, q4 = r % inner, r // inner
        return jnp.concatenate([tmp[i, pl.ds(c * l_in + q4, l_res, stride=outer), :]
                                for i in range(LANE_HALVES)], axis=1)

    head = lax.broadcasted_iota(jnp.int32, (1, GROUP_W), 1) // HEAD_DIM
    hm = [head == h for h in range(HG)]
    for name, halves in (("q", q_ref), ("k", k_ref), ("v", v_ref)):
        if outer > 1:
            for i, h in enumerate(halves):
                for c in range(inner):
                    tmp[i, pl.ds(c * l_in, l_in), :] = h[_residue_rows(c, l_in, inner), :]
        for r in range(dil):
            dst = pl.ds(r * l_res, l_res)
            x = gather(halves, r)
            if name == "q":
                qs[dst, :] = (x * ATTN_SCALE).astype(BF16)
            elif name == "k":
                ks[dst, :] = x.astype(BF16)
            else:
                vs[dst, :] = x.astype(BF16)
    has_prev = nb > 1
    nk = 2 * BAND if has_prev else BAND
    qi = lax.broadcasted_iota(jnp.int32, (BAND, nk), 0)
    kj = lax.broadcasted_iota(jnp.int32, (BAND, nk), 1)
    delta = (nk - BAND) + qi - kj
    band = (delta >= 0) & (delta <= BAND)
    dist = (delta * dil).astype(F32)
    for h in range(HG):
        alibi = -SLOPES[group * HG + h] * dist
        bias_scr[h, :, 0:nk] = jnp.where(band, alibi, NEG_INF)
        bias_scr[HG + h, :, 0:nk] = jnp.where(band & (kj >= nk - BAND), alibi, NEG_INF)
    blocks_per_iter = 16

    def block_pair(it, carry):
        cur, q, k2, v2, first = [], [], [], [], []
        for u in range(blocks_per_iter):
            idx = it * blocks_per_iter + u
            cur.append(pl.ds(pl.multiple_of(idx * BAND, BAND), BAND))
            q.append(qs[cur[u], :])
            k2.append(ks[cur[u], :])
            v2.append(vs[cur[u], :])
            first.append(jnp.where(idx % nb == 0, HG, 0))
            if has_prev:
                prev = pl.ds(pl.multiple_of(jnp.maximum(idx - 1, 0) * BAND, BAND), BAND)
                k2[u] = jnp.concatenate([ks[prev, :], k2[u]], axis=0)
                v2[u] = jnp.concatenate([vs[prev, :], v2[u]], axis=0)
        chains = [(u, h) for u in range(blocks_per_iter) for h in range(HG)]
        s = [_dg(jnp.where(hm[h], q[u], jnp.zeros_like(q[u])), k2[u], NT) + bias_scr[first[u] + h, :, 0:nk]
             for u, h in chains]
        m = [jnp.max(x, axis=-1, keepdims=True) for x in s]
        p = [jnp.exp(x - mx) for x, mx in zip(s, m)]
        l = [jnp.sum(x, axis=-1, keepdims=True) for x in p]
        pv = [_dg(p[i].astype(BF16), v2[u]) for i, (u, h) in enumerate(chains)]
        for u in range(blocks_per_iter):
            o_acc = jnp.zeros((BAND, GROUP_W), F32)
            lse_acc = jnp.zeros((BAND, GROUP_W), F32)
            for h in range(HG):
                i = u * HG + h
                o_acc = jnp.where(hm[h], pv[i] / l[i], o_acc)
                lse_acc = jnp.where(hm[h], m[i] + jnp.log(l[i]), lse_acc)
            if dil == 1:
                for half in range(LANE_HALVES):
                    cols = slice(half * 128, (half + 1) * 128)
                    o_scr[group * LANE_HALVES + half, cur[u], :] = o_acc[:, cols]
                    lse_scr[group * LANE_HALVES + half, cur[u], :] = lse_acc[:, cols]
            else:
                os_[cur[u], :] = o_acc
                ls[cur[u], :] = lse_acc
        return carry

    assert (dil * nb) % blocks_per_iter == 0
    lax.fori_loop(0, dil * nb // blocks_per_iter, block_pair, 0)
    for res, nat in ((os_, o_scr), (ls, lse_scr)) if dil > 1 else ():
        for half in range(LANE_HALVES):
            cols = slice(half * 128, (half + 1) * 128)
            slab = group * LANE_HALVES + half
            for r in range(dil):
                src = pl.ds(r * l_res, l_res)
                if outer == 1:
                    nat[slab, _residue_rows(r, l_res, dil), :] = res[src, cols]
                else:
                    tmp[half, pl.ds((r % inner) * l_in + r // inner, l_res, stride=outer), :] = res[src, cols]
            if outer > 1:
                for c in range(inner):
                    nat[slab, _residue_rows(c, l_in, inner), :] = tmp[half, pl.ds(c * l_in, l_in), :]


def _attn_prompt_body(q0_ref, q1_ref, k0_ref, k1_ref, v0_ref, v1_ref, o_ref, o_scr, lse_scr, qs, ks, vs, os_, ls, bias_scr, tmp):
    q_ref, k_ref, v_ref = (q0_ref, q1_ref), (k0_ref, k1_ref), (v0_ref, v1_ref)
    gid = pl.program_id(1)
    n_groups = len(DILATIONS)
    for g in range(n_groups):
        @pl.when(gid == g)
        def _(g=g):
            _attn_group_blocks(q_ref, k_ref, v_ref, o_scr, lse_scr, qs, ks, vs, os_, ls, bias_scr, tmp, g)

    @pl.when(gid == n_groups - 1)
    def _():
        def merge(i, carry):
            rows = pl.ds(pl.multiple_of(i * BAND, BAND), BAND)
            both = lambda ref, g: jnp.concatenate(
                [ref[g * LANE_HALVES + half, rows, :] for half in range(LANE_HALVES)], axis=1)
            lse = [both(lse_scr, g) for g in range(n_groups)]
            m = functools.reduce(jnp.maximum, lse)
            wts = [jnp.exp(x - m) for x in lse]
            num = sum(wts[g] * both(o_scr, g) for g in range(n_groups))
            o_ref[rows, :] = (num / sum(wts)).astype(o_ref.dtype)
            return carry

        lax.fori_loop(0, o_ref.shape[0] // BAND, merge, 0)


def _attn_prompt(qkv):
    b, s, _ = qkv.shape
    n_groups = len(DILATIONS)
    assert s % (DILATIONS[-1] * BAND) == 0
    sec = ATTN_W // 128

    def col(section, half):
        return pl.BlockSpec((None, s, 128), lambda bi, g: (bi, 0, section * sec + g * LANE_HALVES + half))

    return pl.pallas_call(
        _attn_prompt_body,
        grid=(b, n_groups),
        in_specs=[col(section, half) for section in range(3) for half in range(LANE_HALVES)],
        out_specs=pl.BlockSpec((None, s, GROUP_W), lambda bi, g: (bi, 0, 0)),
        out_shape=jax.ShapeDtypeStruct((b, s, GROUP_W), BF16),
        scratch_shapes=([pltpu.VMEM((n_groups * LANE_HALVES, s, 128), F32)] * 2
                        + [pltpu.VMEM((s, GROUP_W), BF16)] * 3 + [pltpu.VMEM((s, GROUP_W), F32)] * 2
                        + [pltpu.VMEM((2 * HG, BAND, 2 * BAND), F32), pltpu.VMEM((LANE_HALVES, s, 128), F32)]),
        compiler_params=_params(("parallel", "arbitrary")),
        name="attn_prompt",
    )(*([qkv] * (3 * LANE_HALVES)))


def _attn_sample_body(qkv_ref, c0_ref, c1_ref, c2_ref, o_ref):
    for job in _attn_sample_jobs(qkv_ref, c0_ref, c1_ref, c2_ref, o_ref):
        job()


def _attn_sample_jobs(qkv_ref, c0_ref, c1_ref, c2_ref, o_ref):
    rows = 8
    row = lax.broadcasted_iota(jnp.int32, (rows, GROUP_W), 0)
    own = lax.broadcasted_iota(jnp.int32, (rows, GROUP_W), 1) // HEAD_DIM == row
    hrow = lax.broadcasted_iota(jnp.int32, (HG, 1), 0)
    lane_head = lax.broadcasted_iota(jnp.int32, (1, GROUP_W), 1) // HEAD_DIM
    c_refs = (c0_ref, c1_ref, c2_ref)
    groups = range(len(DILATIONS))
    st = [dict() for _ in groups]

    def to_column(x_row):
        return jnp.sum(jnp.where(own, x_row, 0.0).T, axis=1, keepdims=True)

    def per_head(col):
        out = jnp.zeros((1, GROUP_W), F32)
        for h in range(HG):
            out = jnp.where(lane_head == h, col[h:h + 1, :], out)
        return out

    def scores():
        for g in groups:
            lb = c_refs[g].shape[2]
            col = lambda sec: qkv_ref[:, sec * ATTN_W + g * GROUP_W:sec * ATTN_W + (g + 1) * GROUP_W]
            q, kn, vn = col(0), col(1), col(2)
            prod = c_refs[g][0] * to_column(q)
            s_b = jnp.sum(prod.reshape(HG, HEAD_DIM, lb), axis=1) * ATTN_SCALE
            s_n = jnp.sum(jnp.where(own, q * kn, 0.0), axis=-1, keepdims=True)[0:HG] * ATTN_SCALE
            st[g].update(vn=vn, s_b=s_b, s_n=s_n)

    def values():
        for g in groups:
            dil = DILATIONS[g]
            lb = c_refs[g].shape[2]
            slope = jnp.zeros((HG, 1), F32)
            for h in range(HG):
                slope = jnp.where(hrow == h, SLOPES[g * HG + h], slope)
            t = lax.broadcasted_iota(jnp.int32, (HG, lb), 1)
            s_b = jnp.where(t % dil == 0, st[g]["s_b"] - slope * (lb - t).astype(F32), NEG_INF)
            m = jnp.maximum(jnp.max(s_b, axis=-1, keepdims=True), st[g]["s_n"])
            p_b = jnp.exp(s_b - m)
            p_n = jnp.exp(st[g]["s_n"] - m)
            p_rows = jnp.broadcast_to(p_b[:, None, :], (HG, HEAD_DIM, lb)).reshape(GROUP_W, lb)
            acc = jnp.sum(p_rows * c_refs[g][1], axis=1, keepdims=True)
            st[g].update(m=m, p_n=p_n, l=jnp.sum(p_b, axis=-1, keepdims=True) + p_n, acc=acc)

    def merge():
        m_all = functools.reduce(jnp.maximum, [s["m"] for s in st])
        sc = [jnp.exp(s["m"] - m_all) for s in st]
        den = sum(c * s["l"] for c, s in zip(sc, st))
        out = jnp.zeros((1, GROUP_W), F32)
        for c, s in zip(sc, st):
            acc_row = jnp.broadcast_to(s["acc"], (GROUP_W, rows)).T[0:1, :]
            out = out + per_head(c / den) * (acc_row + per_head(s["p_n"]) * s["vn"])
        o_ref[...] = out.astype(o_ref.dtype)

    return [scores, values, merge]


def _attn_sample_operands(qkv, caches):
    b = qkv.shape[0]
    args = [qkv.reshape(b, 1, QKV_W)]
    specs = [pl.BlockSpec((None, 1, QKV_W), lambda i: (i, 0, 0))]
    for g, c in enumerate(caches):
        lb = c.shape[1]
        assert lb == BAND * DILATIONS[g]
        args.append(jnp.transpose(c, (0, 2, 3, 4, 1)).reshape(b, 2, GROUP_W, lb))
        specs.append(pl.BlockSpec((None, 2, GROUP_W, lb), lambda i: (i, 0, 0, 0)))
    out_spec = pl.BlockSpec((None, 1, GROUP_W), lambda i: (i, 0, 0))
    return args, specs, out_spec, jax.ShapeDtypeStruct((b, 1, GROUP_W), BF16)


def _attn_sample(qkv, caches):
    args, specs, out_spec, out_shape = _attn_sample_operands(qkv, caches)
    out = pl.pallas_call(
        _attn_sample_body,
        grid=(qkv.shape[0],),
        in_specs=specs,
        out_specs=out_spec,
        out_shape=out_shape,
        compiler_params=_params(("parallel",)),
        name="attn_sample",
    )(*args)
    return out.reshape(-1, GROUP_W)


WKV_PASSES = dict(g=1, s0=1, x=1, inv=1, y=1, s1=1)


def _mm(a, b, dims, passes):
    if passes == 1:
        return _dg(a.astype(BF16), b.astype(BF16), dims)
    ah, al = _split2(a)
    if passes == 2:
        bh = b.astype(BF16)
        return _dg(ah, bh, dims) + _dg(al, bh, dims)
    bh, bl = _split2(b)
    return _dg(ah, bh, dims) + _dg(al, bh, dims) + _dg(ah, bl, dims)


def _wkv_levels(live_rows):
    return int(np.ceil(np.log2(live_rows))) if live_rows > 1 else 0


def _wkv_chunk(operands, states, levels, side=()):
    side = list(side)

    def run_side():
        if side:
            side.pop(0)()
    c = CHUNK
    n2 = 2 * c
    n_seq = len(operands)
    first = lax.broadcasted_iota(jnp.int32, (1, PAIR_W), 1) < RWKV_N

    def stack(x, p):
        x = x[:, p * PAIR_W:(p + 1) * PAIR_W]
        return jnp.concatenate([jnp.where(first, x, 0.0), jnp.where(first, 0.0, x)], axis=0)

    bi = lax.broadcasted_iota(jnp.int32, (n_seq * c, n_seq * c), 0)
    bj = lax.broadcasted_iota(jnp.int32, (n_seq * c, n_seq * c), 1)
    tri_all = ((bi >= bj) & (bi // c == bj // c)).astype(BF16)
    lc_all = _dot_exact_rhs_left(tri_all, jnp.concatenate([ops[3] for ops in operands], axis=0))
    ar, bk, v_s, s0, w_end = [], [], [], [], []
    for j, ((r, kp, v, logw, av, bv), seq_states) in enumerate(zip(operands, states)):
        lc = lc_all[j * c:(j + 1) * c]
        e_in = jnp.exp(lc)
        e_neg = jnp.exp(-lc)
        a_t = av * jnp.exp(lc - logw)
        r_t = r * e_in
        b_t = bv * e_neg
        k_t = kp * e_neg
        for p in range(N_PAIRS):
            ar.append(jnp.concatenate([stack(a_t, p), stack(r_t, p)], axis=0))
            bk.append(jnp.concatenate([stack(b_t, p), stack(k_t, p)], axis=0))
            v_s.append(stack(v, p))
            s0.append(seq_states[p])
            w_end.append(e_in[c - 1:c, p * PAIR_W:(p + 1) * PAIR_W])
    chains = range(len(ar))
    g = [_mm(ar[i], bk[i], NT, WKV_PASSES["g"]) for i in chains]
    ar_s0 = [_mm(ar[i], s0[i], NT, WKV_PASSES["s0"]) for i in chains]
    run_side()
    ri = lax.broadcasted_iota(jnp.int32, (n2, n2), 0)
    ci = lax.broadcasted_iota(jnp.int32, (n2, n2), 1)
    strict = ri > ci
    incl = ri >= ci
    n_ab = [jnp.where(strict, g[i][0:n2, 0:n2], 0.0) for i in chains]
    n_ak = [jnp.where(strict, g[i][0:n2, n2:2 * n2], 0.0) for i in chains]
    m_r = [jnp.where(jnp.concatenate([incl, incl], axis=1), g[i][n2:2 * n2, :], 0.0) for i in chains]
    z = [ar_s0[i][0:n2] + _mm(n_ak[i], v_s[i], NN, WKV_PASSES["x"]) for i in chains]
    pw = n_ab
    for lvl in range(levels):
        if lvl < levels - 1:
            pz = [_mm(pw[i], jnp.concatenate([pw[i], z[i]], axis=1), NN, WKV_PASSES["inv"]) for i in chains]
            pw = [pz[i][:, 0:n2] for i in chains]
            z = [z[i] + pz[i][:, n2:2 * n2] for i in chains]
        else:
            z = [z[i] + _mm(pw[i], z[i], NN, WKV_PASSES["inv"]) for i in chains]
        if lvl % 2 == 1:
            run_side()
    while side:
        run_side()
    uv = [jnp.concatenate([z[i], v_s[i]], axis=0) for i in chains]
    y_s = [ar_s0[i][n2:2 * n2] + _mm(m_r[i], uv[i], NN, WKV_PASSES["y"]) for i in chains]
    s1 = [(s0[i] + _mm(uv[i], bk[i], TN, WKV_PASSES["s1"])) * w_end[i] for i in chains]
    ys = [jnp.concatenate([y_s[j * N_PAIRS + p][0:c] + y_s[j * N_PAIRS + p][c:n2] for p in range(N_PAIRS)], axis=1)
          for j in range(n_seq)]
    new_states = tuple(tuple(s1[j * N_PAIRS:(j + 1) * N_PAIRS]) for j in range(n_seq))
    return ys, new_states


def _dot_exact_rhs_left(lhs_bf16, b):
    out = None
    rem = b
    for _ in range(3):
        part = rem.astype(BF16)
        term = _dg(lhs_bf16, part)
        out = term if out is None else out + term
        rem = rem - part.astype(F32)
    return out


def _rwkv_body(zb_ref, mu_ref, w0_ref, a0_ref, kk_ref, ka_ref, rk_ref, gg_ref, gb_ref, wwa_ref, wg_ref, *refs,
               t_valid, fresh):
    if fresh:
        sh_ref = s0_ref = None
    else:
        sh_ref, s0_ref = refs[:2]
        refs = refs[2:]
    out_ref, s_out_ref, sh_out_ref, s_scr, carry_scr, ring0_scr, ring1_scr = refs
    t = pl.program_id(1)
    n_seq, tc, _ = zb_ref.shape

    @pl.when(t == 0)
    def _():
        if fresh:
            s_scr[...] = jnp.zeros(s_scr.shape, F32)
            carry_scr[...] = jnp.zeros(carry_scr.shape, F32)
        else:
            carry_scr[...] = sh_ref[...]
            zero = jnp.zeros((RWKV_N, RWKV_N), F32)
            for j in range(n_seq):
                for p in range(N_PAIRS):
                    top = jnp.concatenate([s0_ref[j, 2 * p], zero], axis=1)
                    bottom = jnp.concatenate([zero, s0_ref[j, 2 * p + 1]], axis=1)
                    s_scr[j, p] = jnp.concatenate([top, bottom], axis=0)

    lr = lax.broadcasted_iota(jnp.int32, (2 * PAIR_W, 2 * PAIR_W), 0)
    lc = lax.broadcasted_iota(jnp.int32, (2 * PAIR_W, 2 * PAIR_W), 1)
    seg = (lr // RWKV_N == lc // RWKV_N).astype(BF16)
    half = lax.broadcasted_iota(jnp.int32, (1, 128), 1) < 64
    row = lax.broadcasted_iota(jnp.int32, (CHUNK, 1), 0)
    inv_n = 1.0 / RWKV_N
    levels = _wkv_levels(CHUNK if t_valid is None else min(t_valid, CHUNK))
    short = tc < CHUNK

    def head_sum(x):
        wide = 2 * PAIR_W
        return jnp.concatenate(
            [_dg(x[:, c:c + wide].astype(BF16), seg) for c in range(0, RWKV_W, wide)], axis=1)

    n_chunks = max(tc // CHUNK, 1)
    rings = (ring0_scr, ring1_scr)

    def staging_jobs(ci, ring):
        rows = pl.ds(pl.multiple_of(ci * CHUNK, CHUNK), CHUNK)
        st = [dict() for _ in range(n_seq)]
        seq_rows = lambda x, j: x[j * CHUNK:(j + 1) * CHUNK]

        def mix_and_lora():
            x_wa, x_g = [], []
            for j in range(n_seq):
                if short:
                    z = jnp.where(row == 0, zb_ref[j], 0.0)
                    before = carry_scr[j]
                else:
                    z = zb_ref[j, rows, :]
                    before = zb_ref[j, pl.ds(jnp.maximum(ci * CHUNK - 1, 0), 1), :]
                    before = jnp.where(ci == 0, carry_scr[j], before)
                zp = jnp.where(row == 0, before, pltpu.roll(z, 1, 0))
                zm = z + (zp - z) * mu_ref[...]
                z_wa = zm[:, 3 * RWKV_W:3 * RWKV_W + 128]
                x_wa.append(jnp.where(half, jnp.tanh(z_wa), z_wa).astype(BF16))
                x_g.append(_sigmoid(zm[:, 3 * RWKV_W + 128:SHIFT_W]).astype(BF16))
                st[j].update(r=zm[:, 0:RWKV_W], k=zm[:, RWKV_W:2 * RWKV_W], v=zm[:, 2 * RWKV_W:3 * RWKV_W])
            x_wa = jnp.concatenate(x_wa, axis=0)
            u = _dg(x_wa, wwa_ref[:, 0:RWKV_W])
            la = _dg(x_wa, wwa_ref[:, RWKV_W:2 * RWKV_W])
            gate = _dg(jnp.concatenate(x_g, axis=0), wg_ref[...])
            for j in range(n_seq):
                st[j].update(u=seq_rows(u, j), la=seq_rows(la, j), gate=seq_rows(gate, j))

        def head_sums():
            for j in range(n_seq):
                s = st[j]
                a = _sigmoid(a0_ref[...] + s["la"])
                kk = s["k"] * kk_ref[...]
                kp = s["k"] * (1.0 + (a - 1.0) * ka_ref[...])
                s.update(a=a, kk=kk, kp=kp)
            kk2 = head_sum(jnp.concatenate([s["kk"] * s["kk"] for s in st], axis=0))
            rk = head_sum(jnp.concatenate([s["r"] * s["kp"] * rk_ref[...] for s in st], axis=0))
            for j in range(n_seq):
                st[j].update(kk2=seq_rows(kk2, j), rk=seq_rows(rk, j))

        def finish():
            for j in range(n_seq):
                s = st[j]
                logw = -DECAY_SCALE * _sigmoid(w0_ref[...] + s["u"])
                kk = s["kk"] * lax.rsqrt(jnp.maximum(s["kk2"], 1e-24))
                ops = (s["r"], s["kp"], s["v"], logw, -kk, kk * s["a"], s["rk"] * s["v"])
                if t_valid is not None:
                    live = row + ci * CHUNK < t_valid
                    ops = tuple(jnp.where(live, x, 0.0) for x in ops)
                for i, x in enumerate(ops + (s["gate"],)):
                    ring[j, i] = x

        return mix_and_lora, head_sums, finish

    def chunk(ci, states, ring, other):
        side = staging_jobs(jnp.minimum(ci + 1, n_chunks - 1), other) if n_chunks > 1 else ()
        ys, states = _wkv_chunk([tuple(ring[j, i] for i in range(6)) for j in range(n_seq)], states, levels, side)
        y_all = jnp.concatenate(ys, axis=0)
        d_all = y_all - head_sum(y_all) * inv_n
        var_all = head_sum(d_all * d_all) * inv_n
        yn_all = d_all * lax.rsqrt(var_all + GN_EPS) * gg_ref[...] + gb_ref[...]
        for j in range(n_seq):
            res = (yn_all[j * CHUNK:(j + 1) * CHUNK] + ring[j, 6]) * ring[j, 7]
            if short:
                out_ref[j] = res[0:tc, :].astype(out_ref.dtype)
            else:
                out_ref[j, pl.ds(pl.multiple_of(ci * CHUNK, CHUNK), CHUNK), :] = res.astype(out_ref.dtype)
        return states

    for job in staging_jobs(0, rings[0]):
        job()
    states = tuple(tuple(s_scr[j, p] for p in range(N_PAIRS)) for j in range(n_seq))
    if n_chunks == 1:
        states = chunk(0, states, rings[0], rings[1])
    else:
        assert n_chunks % 2 == 0

        def chunk_pair(i, states):
            states = chunk(2 * i, states, rings[0], rings[1])
            return chunk(2 * i + 1, states, rings[1], rings[0])

        if n_chunks <= RWKV_UNROLLED_CHUNKS:
            for i in range(n_chunks // 2):
                states = chunk_pair(i, states)
        else:
            states = lax.fori_loop(0, n_chunks // 2, chunk_pair, states)
    for j in range(n_seq):
        carry_scr[j] = zb_ref[j, tc - 1:tc, :]
        for p in range(N_PAIRS):
            s_scr[j, p] = states[j][p]

    @pl.when(t == pl.num_programs(1) - 1)
    def _():
        for j in range(n_seq):
            sh_out_ref[j] = zb_ref[j, tc - 1:tc, :]
            for p in range(N_PAIRS):
                for e in range(2):
                    span = slice(e * RWKV_N, (e + 1) * RWKV_N)
                    s_out_ref[j, 2 * p + e] = states[j][p][span, span]


def _rwkv(zb, w, tc, n_seq, state=None, t_valid=None):
    b, t_len, _ = zb.shape
    full = lambda a: pl.BlockSpec(a.shape, lambda bi, t: (0,) * a.ndim)
    per_seq = lambda *dims: pl.BlockSpec((n_seq,) + dims, lambda bi, t: (bi,) + (0,) * len(dims))
    consts = [w["mu_shift"], w["w0"], w["a0"], w["k_k"], w["k_a"], w["r_k"], w["gn_g"], w["gn_b"], w["wa_lora"],
              w["g_lora"]]
    in_specs = [pl.BlockSpec((n_seq, tc, SHIFT_W), lambda bi, t: (bi, t, 0))] + [full(c) for c in consts]
    args = [zb, *consts]
    if state is not None:
        in_specs += [per_seq(1, SHIFT_W), per_seq(RWKV_HEADS, RWKV_N, RWKV_N)]
        args += list(state)
    vm = lambda shape: pltpu.VMEM(shape, F32)
    assert b % n_seq == 0 and t_len % tc == 0
    assert tc % CHUNK == 0 or (tc == t_len == t_valid == 1), "whole chunks, or single-token sequences"
    scratch = [vm((n_seq, N_PAIRS, PAIR_W, PAIR_W)), vm((n_seq, 1, SHIFT_W))] + [vm((n_seq, 8, CHUNK, RWKV_W))] * 2
    return pl.pallas_call(
        functools.partial(_rwkv_body, t_valid=t_valid, fresh=state is None),
        grid=(b // n_seq, t_len // tc),
        in_specs=in_specs,
        out_specs=[pl.BlockSpec((n_seq, tc, RWKV_W), lambda bi, t: (bi, t, 0)),
                   per_seq(RWKV_HEADS, RWKV_N, RWKV_N), per_seq(1, SHIFT_W)],
        out_shape=[jax.ShapeDtypeStruct((b, t_len, RWKV_W), BF16),
                   jax.ShapeDtypeStruct((b, RWKV_HEADS, RWKV_N, RWKV_N), F32),
                   jax.ShapeDtypeStruct((b, 1, SHIFT_W), F32)],
        scratch_shapes=scratch,
        compiler_params=_params(("parallel", "arbitrary")),
        name="rwkv",
    )(*args)


def _out_proj_body(x_ref, attn_ref, rwkv_ref, gate_ref, wpa_ref, wpb_ref, wout_ref, n2_ref,
                   wup_ref, wdn_ref, nf_ref, *refs):
    rider = []
    if len(refs) > 1:
        qkv_s_ref, c0_ref, c1_ref, c2_ref, y_ref, attn_s_ref = refs
        rider = _attn_sample_jobs(qkv_s_ref, c0_ref, c1_ref, c2_ref, attn_s_ref)
    else:
        y_ref, = refs
    pa = _dg(attn_ref[...], wpa_ref[...])
    pb = _dg(rwkv_ref[...], wpb_ref[...])
    merged = gate_ref[:, 0:D_MODEL].astype(F32) * pa + gate_ref[:, D_MODEL:GATE_W].astype(F32) * pb
    x1 = x_ref[...] + _dg(merged.astype(BF16), wout_ref[...])
    hm = _rms(x1, n2_ref[...]).astype(BF16)
    acc = x1
    for c in range(0, D_FF, 1024):
        up = jnp.maximum(_dg(hm, wup_ref[:, c:c + 1024]), 0.0)
        if rider:
            rider.pop(0)()
        acc = acc + _dg((up * up).astype(BF16), wdn_ref[c:c + 1024, :])
    y_ref[...] = _rms(acc, nf_ref[...])


def _out_proj(x, attn, rwkv, gate, w, tm, rider=None):
    m = x.shape[0]
    row = lambda wd: pl.BlockSpec((tm, wd), lambda i: (i, 0))
    full = lambda a: pl.BlockSpec(a.shape, lambda i: (0,) * a.ndim)
    consts = [w["w_proj_a"], w["w_proj_b"], w["w_out"], w["norm2_g"], w["w_up"], w["w_down"], w["normf_g"]]
    in_specs = [row(D_MODEL), row(GROUP_W), row(RWKV_W), row(GATE_W)] + [full(c) for c in consts]
    args = [x, attn, rwkv, gate, *consts]
    out_specs = [row(D_MODEL)]
    out_shape = [jax.ShapeDtypeStruct((m, D_MODEL), F32)]
    if rider is not None:
        r_args, r_specs, r_out_spec, r_out_shape = _attn_sample_operands(*rider)
        assert r_out_shape.shape[0] == m // tm
        in_specs += r_specs
        args += r_args
        out_specs.append(r_out_spec)
        out_shape.append(r_out_shape)
    outs = pl.pallas_call(
        _out_proj_body,
        grid=(m // tm,),
        in_specs=in_specs,
        out_specs=out_specs,
        out_shape=out_shape,
        compiler_params=_params(("parallel",)),
        name="out_proj",
    )(*args)
    if rider is None:
        return outs[0]
    return outs[0], outs[1].reshape(-1, GROUP_W)


def _kv_rows(qkv, group, rows):
    b, s, _ = qkv.shape
    k0 = ATTN_W + group * GROUP_W
    v0 = 2 * ATTN_W + group * GROUP_W
    k = qkv[:, s - rows:, k0:k0 + GROUP_W].reshape(b, rows, HG, HEAD_DIM)
    v = qkv[:, s - rows:, v0:v0 + GROUP_W].reshape(b, rows, HG, HEAD_DIM)
    return jnp.stack([k, v], axis=2)


def _layer_weights(l, norm1_g, w_in, b_gate, mu_shift, w0, w_lora_up, a0, a_lora_up, g_lora_up, k_k, k_a,
                   r_k, gn_g, gn_b, w_proj_a, w_proj_b, w_out, norm2_g, w_up, w_down, normf_g):
    row = lambda a: a.reshape(1, -1)
    zero = jnp.zeros_like(w_lora_up[l])
    wa_lora = jnp.concatenate([jnp.concatenate([w_lora_up[l], zero], axis=1),
                               jnp.concatenate([zero, a_lora_up[l]], axis=1)], axis=0)
    return dict(
        norm1_g=row(norm1_g[l]), w_in=w_in[l].astype(BF16), b_gate=row(b_gate[l]), mu_shift=row(mu_shift[l]),
        w0=row(w0[l]), a0=row(a0[l]), k_k=row(k_k[l]), k_a=row(k_a[l]), r_k=row(r_k[l]),
        gn_g=row(gn_g[l]), gn_b=row(gn_b[l]),
        wa_lora=wa_lora.astype(BF16), g_lora=g_lora_up[l].astype(BF16),
        w_proj_a=w_proj_a[l].astype(BF16), w_proj_b=w_proj_b[l].astype(BF16), w_out=w_out[l].astype(BF16),
        norm2_g=row(norm2_g[l]), w_up=w_up[l].astype(BF16), w_down=w_down[l].astype(BF16),
        normf_g=row(normf_g))


def _layer(xp, xs, caches, s0, shift0, w):
    b, s, _ = xp.shape
    bs, t_len, _ = xs.shape
    assert t_len == 1
    xp2 = xp.reshape(b * s, D_MODEL)
    xs2 = xs.reshape(bs, D_MODEL)
    qkv_s, zb_s, gate_s = _in_proj(xs2, w["norm1_g"], w["w_in"], w["b_gate"], bs)
    qkv, zb, gate, *kv_t = _in_proj(xp2, w["norm1_g"], w["w_in"], w["b_gate"], PROMPT_TILE, seq_len=s)
    zb = zb.reshape(b, s, SHIFT_W)
    attn = _attn_prompt(qkv.reshape(b, s, QKV_W))
    rwkv, wkv_p, shift_p = _rwkv(zb, w, RWKV_TILE, RWKV_SEQS_PER_STEP)
    attn2, rwkv2 = attn.reshape(b * s, GROUP_W), rwkv.reshape(b * s, RWKV_W)
    if b * s // PROMPT_TILE == bs:
        y_p, attn_s = _out_proj(xp2, attn2, rwkv2, gate, w, PROMPT_TILE, rider=(qkv_s, caches))
    else:
        y_p = _out_proj(xp2, attn2, rwkv2, gate, w, PROMPT_TILE)
        attn_s = _attn_sample(qkv_s, caches)
    rwkv_s, wkv_s, _ = _rwkv(zb_s[:, None, :], w, 1, RWKV_SEQS_PER_STEP, state=(shift0[:, None, :], s0), t_valid=1)
    y_s = _out_proj(xs2, attn_s, rwkv_s[:, 0], gate_s, w, bs)
    kv_p = [jnp.transpose(t.reshape(b, 2, HG, HEAD_DIM, t.shape[-1]), (0, 4, 1, 2, 3)) for t in kv_t]
    kv_s = [_kv_rows(qkv_s[:, None, :], g, 1) for g in range(len(DILATIONS))]
    prompt = (y_p.reshape(b, s, D_MODEL), kv_p, wkv_p, shift_p[:, 0])
    sample = (y_s.reshape(bs, 1, D_MODEL), kv_s, wkv_s, zb_s)
    return prompt, sample


def kernel(x_prompt, x_sample, cache_kv_w128, cache_kv_w512, cache_kv_w2048, state_wkv, state_shift, norm1_g, w_in, b_gate, mu_shift, w0, w_lora_up, a0, a_lora_up, g_lora_up, k_k, k_a, r_k, gn_g, gn_b, w_proj_a, w_proj_b, w_out, norm2_g, w_up, w_down, normf_g):
    depth = w_in.shape[0]
    assert depth == 1, "the final norm is fused into the layer's output stage"
    w = _layer_weights(0, norm1_g, w_in, b_gate, mu_shift, w0, w_lora_up, a0, a_lora_up, g_lora_up, k_k, k_a,
                       r_k, gn_g, gn_b, w_proj_a, w_proj_b, w_out, norm2_g, w_up, w_down, normf_g)
    (y_p, kv_p, wkv_p, shift_p), (y_s, kv_s, wkv_s, shift_s) = _layer(
        x_prompt, x_sample, (cache_kv_w128[0], cache_kv_w512[0], cache_kv_w2048[0]), state_wkv[0], state_shift[0], w)
    lead = lambda a: a[None]
    return (y_p, y_s, lead(kv_p[0]), lead(kv_p[1]), lead(kv_p[2]), lead(wkv_p), lead(shift_p),
            lead(kv_s[0]), lead(kv_s[1]), lead(kv_s[2]), lead(wkv_s), lead(shift_s))
```
